```python
import math
import jax
import jax.numpy as jnp
from jax import lax
import numpy as np

D_MODEL = 1024
BATCH = 16
SEQ = 2048
DEPTH = 2

GRID_W = 64
CTX_LEN = 256
N_EVEN = (DEPTH + 1) // 2
N_ODD = DEPTH // 2
HEAD_DIM = 64
A_WIDTH = D_MODEL // 2
B_Q_HEADS = (D_MODEL - A_WIDTH) // HEAD_DIM
B_KV_HEADS = max(1, B_Q_HEADS // 4)
HY_ORDER = 2
HY_BANDS = 16
HY_EMB = 1 + 2 * HY_BANDS
HY_HIDDEN = 64
HY_TARGET = 1e-2
HY_MAX_DECAY = math.log(1.0 / HY_TARGET) / 0.3
HY_MIN_DECAY = math.log(1.0 / HY_TARGET) / 1.5
SHORT_CONV = 3
MLA_HEADS = 8
MLA_NOPE = 64
MLA_ROPE = 32
MLA_V = 64
MLA_Q_RANK = D_MODEL // 4
MLA_KV_RANK = D_MODEL // 8
S5_WIDTH = D_MODEL - MLA_HEADS * MLA_V
S5_GROUP = 16
S5_GROUPS = S5_WIDTH // S5_GROUP
S5_STATE = 64
S5_DT_MIN = 1e-3
S5_DT_MAX = 1e-1
N_GROUPS = 4
EXPERTS_PER_GROUP = 8
N_EXPERTS = N_GROUPS * EXPERTS_PER_GROUP
TOP_K_INNER = 2
D_EXPERT = D_MODEL // 4
ROPE_THETA = 10000.0
Q_BLOCK = 128
EPS = 1e-6
EVEN_IN = 3 * A_WIDTH + (B_Q_HEADS + 2 * B_KV_HEADS) * HEAD_DIM
ODD_IN = MLA_Q_RANK + MLA_KV_RANK + MLA_ROPE + S5_WIDTH
F32 = jnp.float32

kernel_name = 'hybrid_diffusion_hyena_gqa_mla_s5_hmoe'


def _split(t, sizes):
    return jnp.split(t, [int(s) for s in np.cumsum(sizes)[:-1]], axis=-1)


def _heads(t, n):
    return t.reshape(t.shape[0], t.shape[1], n, -1)


def rmsnorm(t, g):
    tf = t.astype(F32)
    y = tf * lax.rsqrt(jnp.mean(tf * tf, axis=-1, keepdims=True) + EPS)
    return (y * g.astype(F32)).astype(t.dtype)


def modnorm(t, g, shift, scale):
    return rmsnorm(t, g) * (1.0 + scale) + shift


def grid_positions(n_tokens):
    rows_count = n_tokens // GRID_W
    rows = jnp.repeat(jnp.arange(rows_count, dtype=jnp.int32), GRID_W)
    cols = jnp.arange(rows_count * GRID_W, dtype=jnp.int32) % GRID_W
    return rows, cols


def rope_1d(t, pos):
    half = t.shape[-1] // 2
    freqs = ROPE_THETA ** (-jnp.arange(half, dtype=F32) / half)
    ang = pos.astype(F32)[:, None] * freqs
    cos = jnp.cos(ang)[:, None, :]
    sin = jnp.sin(ang)[:, None, :]
    t1 = t[..., :half].astype(F32)
    t2 = t[..., half:].astype(F32)
    return jnp.concatenate([t1 * cos - t2 * sin, t1 * sin + t2 * cos], axis=-1).astype(t.dtype)


def rope_2d(t, rows, cols):
    a = t.shape[-1] // 2
    return jnp.concatenate([rope_1d(t[..., :a], rows), rope_1d(t[..., a:], cols)], axis=-1)


def block_attention(q, k, v):
    bsz, lq, hq, dk = q.shape
    hkv = k.shape[2]
    rep = hq // hkv
    scale = dk ** -0.5
    qb = jnp.moveaxis(q.reshape(bsz, lq // Q_BLOCK, Q_BLOCK, hkv, rep, dk), 1, 0)

    def one_block(qblk):
        s = jnp.einsum('bqgrd,bkgd->bgrqk', qblk, k).astype(F32) * scale
        p = jax.nn.softmax(s, axis=-1).astype(v.dtype)
        return jnp.einsum('bgrqk,bkgd->bqgrd', p, v)

    o = lax.map(one_block, qb)
    return jnp.moveaxis(o, 0, 1).reshape(bsz, lq, hq * v.shape[-1])


def short_conv(u, w, b):
    up = jnp.pad(u, ((0, 0), (1, 1), (0, 0)))
    return up[:, :-2] * w[0] + up[:, 1:-1] * w[1] + up[:, 2:] * w[2] + b


def hyena_filter_rfft(n, w1, b1, w2, b2, w3, freq):
    t = jnp.arange(n, dtype=F32)
    t_norm = t / max(n - 1, 1)
    bands = jnp.linspace(1e-4, HY_BANDS - 1, HY_BANDS, dtype=F32)
    ang = (2.0 * math.pi / n) * t[:, None] * bands
    z = jnp.concatenate([t_norm[:, None], jnp.cos(ang), jnp.sin(ang)], axis=-1)
    fr = freq.astype(F32)
    hdn = jnp.sin(fr * (z @ w1.astype(F32) + b1.astype(F32)))
    hdn = jnp.sin(fr * (hdn @ w2.astype(F32) + b2.astype(F32)))
    taps = (hdn @ w3.astype(F32)).reshape(n, 2, HY_ORDER, A_WIDTH)
    deltas = jnp.linspace(HY_MIN_DECAY, HY_MAX_DECAY, A_WIDTH, dtype=F32)
    taps = taps * jnp.exp(-t_norm[:, None] * deltas)[:, None, None, :]
    fwd, bwd = taps[:, 0], taps[:, 1]
    circ = jnp.concatenate([fwd, jnp.zeros((1, HY_ORDER, A_WIDTH), F32), bwd[:0:-1]], axis=0)
    circ = circ * lax.rsqrt(jnp.sum(circ * circ, axis=0, keepdims=True) + EPS)
    return jnp.fft.rfft(circ, axis=0)


def long_conv(u, filt_f, bias):
    n = u.shape[1]
    uf = jnp.fft.rfft(u, n=2 * n, axis=1)
    y = jnp.fft.irfft(uf * filt_f[None], n=2 * n, axis=1)[:, :n]
    return y + u * bias


def hyena_mixer(u3, conv_w, conv_b, w1, b1, w2, b2, w3, freq, fbias):
    n = u3.shape[1]
    u3 = short_conv(u3, conv_w, conv_b).astype(F32)
    x1, x2, v = _split(u3, [A_WIDTH, A_WIDTH, A_WIDTH])
    filt = hyena_filter_rfft(n, w1, b1, w2, b2, w3, freq)
    fb = fbias.astype(F32)
    z = x1 * long_conv(v, filt[:, 0], fb[0])
    return x2 * long_conv(z, filt[:, 1], fb[1])


def s5_discretize(a_re, a_im, log_dt, b_re, b_im):
    a_re, a_im = a_re.astype(F32), a_im.astype(F32)
    dt = jnp.exp(log_dt.astype(F32))[:, None]
    mag = jnp.exp(a_re * dt)
    ab_re, ab_im = mag * jnp.cos(a_im * dt), mag * jnp.sin(a_im * dt)
    er, ei = ab_re - 1.0, ab_im
    den = a_re * a_re + a_im * a_im
    co_re = (er * a_re + ei * a_im) / den
    co_im = (ei * a_re - er * a_im) / den
    b_re, b_im = b_re.astype(F32), b_im.astype(F32)
    bb_re = co_re[..., None] * b_re - co_im[..., None] * b_im
    bb_im = co_re[..., None] * b_im + co_im[..., None] * b_re
    return ab_re, ab_im, bb_re, bb_im


def _lin_rec_op(e1, e2):
    a1r, a1i, b1r, b1i = e1
    a2r, a2i, b2r, b2i = e2
    return (a2r * a1r - a2i * a1i, a2r * a1i + a2i * a1r,
            a2r * b1r - a2i * b1i + b2r, a2r * b1i + a2i * b1r + b2i)


def s5_scan(u, ab_re, ab_im, bb_re, bb_im, h0, reverse):
    n = u.shape[1]
    bu_re = jnp.einsum('blgh,gph->lbgp', u, bb_re)
    bu_im = jnp.einsum('blgh,gph->lbgp', u, bb_im)
    a_re = jnp.broadcast_to(ab_re[None, None], (n, 1) + ab_re.shape)
    a_im = jnp.broadcast_to(ab_im[None, None], (n, 1) + ab_im.shape)
    pa_re, pa_im, s_re, s_im = lax.associative_scan(_lin_rec_op, (a_re, a_im, bu_re, bu_im), reverse=reverse, axis=0)
    if h0 is not None:
        h_re, h_im = h0
        s_re = s_re + pa_re * h_re - pa_im * h_im
        s_im = s_im + pa_re * h_im + pa_im * h_re
    return s_re, s_im


def s5_readout(s_re, s_im, c_re, c_im):
    return jnp.einsum('lbgp,ghp->blgh', s_re, c_re.astype(F32)) - jnp.einsum('lbgp,ghp->blgh', s_im, c_im.astype(F32))


def s5_mixer(u_lat, u_ctx, a_re, a_im, log_dt, b_re, b_im, c_re, c_im, d_skip, glu_w, glu_b, ctx_needed):
    def grouped(u):
        return u.astype(F32).reshape(u.shape[0], u.shape[1], S5_GROUPS, S5_GROUP)

    ul, uc = grouped(u_lat), grouped(u_ctx)
    dsk = d_skip.astype(F32).reshape(S5_GROUPS, S5_GROUP)
    y_lat = dsk * ul
    y_ctx = dsk * uc if ctx_needed else None
    for direction in range(2):
        reverse = direction == 1
        disc = s5_discretize(a_re[direction], a_im[direction], log_dt[direction], b_re[direction], b_im[direction])
        cr, ci = s5_scan(uc, *disc, None, reverse)
        end = 0 if reverse else -1
        lr, li = s5_scan(ul, *disc, (cr[end], ci[end]), reverse)
        y_lat = y_lat + s5_readout(lr, li, c_re[direction], c_im[direction])
        if ctx_needed:
            y_ctx = y_ctx + s5_readout(cr, ci, c_re[direction], c_im[direction])

    def glu(y):
        g = jax.nn.gelu(y.reshape(y.shape[0], y.shape[1], S5_WIDTH))
        return g * jax.nn.sigmoid(g @ glu_w.astype(F32) + glu_b.astype(F32))

    return glu(y_lat), (glu(y_ctx) if ctx_needed else None)


def even_mixer(p_lat, p_ctx, rows, cols, ctx_needed, conv_w, conv_b, w1, b1, w2, b2, w3, freq, fbias, qk_g):
    sizes = [3 * A_WIDTH, B_Q_HEADS * HEAD_DIM, B_KV_HEADS * HEAD_DIM, B_KV_HEADS * HEAD_DIM]
    hy_l, q_l, k_l, v_l = _split(p_lat, sizes)
    hy_c, q_c, k_c, v_c = _split(p_ctx, sizes)
    filt = (conv_w, conv_b, w1, b1, w2, b2, w3, freq, fbias)
    kc = rmsnorm(_heads(k_c, B_KV_HEADS), qk_g[1])
    vc = _heads(v_c, B_KV_HEADS)
    ql = rope_2d(rmsnorm(_heads(q_l, B_Q_HEADS), qk_g[0]), rows, cols)
    kl = rope_2d(rmsnorm(_heads(k_l, B_KV_HEADS), qk_g[1]), rows, cols)
    vl = _heads(v_l, B_KV_HEADS)
    att_l = block_attention(ql, jnp.concatenate([kc, kl], axis=1), jnp.concatenate([vc, vl], axis=1))
    out_l = jnp.concatenate([hyena_mixer(hy_l, *filt).astype(p_lat.dtype), att_l], axis=-1)
    if not ctx_needed:
        return out_l, None
    qc = rmsnorm(_heads(q_c, B_Q_HEADS), qk_g[0])
    att_c = block_attention(qc, kc, vc)
    out_c = jnp.concatenate([hyena_mixer(hy_c, *filt).astype(p_ctx.dtype), att_c], axis=-1)
    return out_l, out_c


def odd_mixer(p_lat, p_ctx, rows, cols, ctx_needed, q_norm_g, w_uq, kv_norm_g, w_ukv,
              a_re, a_im, log_dt, b_re, b_im, c_re, c_im, d_skip, glu_w, glu_b):
    sizes = [MLA_Q_RANK, MLA_KV_RANK, MLA_ROPE, S5_WIDTH]
    cq_l, ckv_l, kr_l, u_l = _split(p_lat, sizes)
    cq_c, ckv_c, kr_c, u_c = _split(p_ctx, sizes)

    def mla_kv(ckv, kr, use_rope):
        bsz, n = ckv.shape[:2]
        kvu = (rmsnorm(ckv, kv_norm_g) @ w_ukv).reshape(bsz, n, MLA_HEADS, MLA_NOPE + MLA_V)
        kr = kr[:, :, None, :]
        if use_rope:
            kr = rope_2d(kr, rows, cols)
        k = jnp.concatenate([kvu[..., :MLA_NOPE], jnp.broadcast_to(kr, (bsz, n, MLA_HEADS, MLA_ROPE))], axis=-1)
        return k, kvu[..., MLA_NOPE:]

    def mla_q(cq, use_rope):
        bsz, n = cq.shape[:2]
        q = (rmsnorm(cq, q_norm_g) @ w_uq).reshape(bsz, n, MLA_HEADS, MLA_NOPE + MLA_ROPE)
        if use_rope:
            q = jnp.concatenate([q[..., :MLA_NOPE], rope_2d(q[..., MLA_NOPE:], rows, cols)], axis=-1)
        return q

    kc, vc = mla_kv(ckv_c, kr_c, False)
    kl, vl = mla_kv(ckv_l, kr_l, True)
    att_l = block_attention(mla_q(cq_l, True), jnp.concatenate([kc, kl], axis=1), jnp.concatenate([vc, vl], axis=1))
    s5_l, s5_c = s5_mixer(u_l, u_c, a_re, a_im, log_dt, b_re, b_im, c_re, c_im, d_skip, glu_w, glu_b, ctx_needed)
    out_l = jnp.concatenate([att_l, s5_l.astype(p_lat.dtype)], axis=-1)
    if not ctx_needed:
        return out_l, None
    att_c = block_attention(mla_q(cq_c, False), kc, vc)
    return out_l, jnp.concatenate([att_c, s5_c.astype(p_ctx.dtype)], axis=-1)


def hier_moe(h, w_rg, w_re, w_gate, w_up, w_down):
    bsz, n, d = h.shape
    t = h.reshape(-1, d)
    pg = jax.nn.softmax((t @ w_rg).astype(F32), axis=-1)
    g_idx = jnp.argmax(pg, axis=-1)
    g_p = jnp.max(pg, axis=-1)
    le = (t @ w_re).astype(F32).reshape(-1, N_GROUPS, EXPERTS_PER_GROUP)
    le_sel = jnp.take_along_axis(le, g_idx[:, None, None], axis=1)[:, 0]
    top_p, top_i = lax.top_k(jax.nn.softmax(le_sel, axis=-1), TOP_K_INNER)
    top_p = top_p / jnp.sum(top_p, axis=-1, keepdims=True)
    inner = jnp.sum(jax.nn.one_hot(top_i, EXPERTS_PER_GROUP, dtype=F32) * top_p[..., None], axis=1)
    gates = (jax.nn.one_hot(g_idx, N_GROUPS, dtype=F32)[:, :, None]
             * (g_p[:, None] * inner)[:, None, :]).reshape(-1, N_EXPERTS).astype(t.dtype)
    y = jnp.zeros_like(t)
    for e in range(N_EXPERTS):
        hid = jax.nn.silu(t @ w_gate[e]) * (t @ w_up[e])
        y = y + gates[:, e:e + 1] * (hid @ w_down[e])
    return y.reshape(bsz, n, d)


def setup_inputs(seed: int = 0) -> dict:
    key = jax.random.key(seed)
    ks = iter(jax.random.split(key, 64))

    def nrm(shape, scale):
        return jax.random.normal(next(ks), shape, F32) * scale

    d = D_MODEL
    n_idx = jnp.arange(S5_STATE, dtype=F32)
    s5_shape = (N_ODD, 2, S5_GROUPS, S5_STATE)
    return {
        'x': nrm((BATCH, SEQ, d), 1.0),
        'c': nrm((BATCH, d), 1.0),
        'ctx': nrm((BATCH, CTX_LEN, d), 1.0),
        'c_ctx': nrm((d,), 1.0),
        'w_mod': nrm((DEPTH, d, 6 * d), 0.5 * d ** -0.5),
        'b_mod': nrm((DEPTH, 6 * d), 0.02),
        'norm_g': 1.0 + nrm((DEPTH, 2, d), 0.02),
        'e_w_in': nrm((N_EVEN, d, EVEN_IN), d ** -0.5),
        'e_hy_conv_w': nrm((N_EVEN, SHORT_CONV, 3 * A_WIDTH), SHORT_CONV ** -0.5),
        'e_hy_conv_b': nrm((N_EVEN, 3 * A_WIDTH), 0.02),
        'e_hy_w1': nrm((N_EVEN, HY_EMB, HY_HIDDEN), HY_EMB ** -0.5),
        'e_hy_b1': nrm((N_EVEN, HY_HIDDEN), 0.1),
        'e_hy_w2': nrm((N_EVEN, HY_HIDDEN, HY_HIDDEN), HY_HIDDEN ** -0.5),
        'e_hy_b2': nrm((N_EVEN, HY_HIDDEN), 0.1),
        'e_hy_w3': nrm((N_EVEN, HY_HIDDEN, 2 * HY_ORDER * A_WIDTH), HY_HIDDEN ** -0.5),
        'e_hy_freq': 1.0 + nrm((N_EVEN, HY_HIDDEN), 0.1),
        'e_hy_fbias': nrm((N_EVEN, HY_ORDER, A_WIDTH), 0.5),
        'e_qk_g': 1.0 + nrm((N_EVEN, 2, HEAD_DIM), 0.02),
        'e_w_out': nrm((N_EVEN, d, d), d ** -0.5),
        'o_w_in': nrm((N_ODD, d, ODD_IN), d ** -0.5),
        'o_q_norm_g': 1.0 + nrm((N_ODD, MLA_Q_RANK), 0.02),
        'o_w_uq': nrm((N_ODD, MLA_Q_RANK, MLA_HEADS * (MLA_NOPE + MLA_ROPE)), MLA_Q_RANK ** -0.5),
        'o_kv_norm_g': 1.0 + nrm((N_ODD, MLA_KV_RANK), 0.02),
        'o_w_ukv': nrm((N_ODD, MLA_KV_RANK, MLA_HEADS * (MLA_NOPE + MLA_V)), MLA_KV_RANK ** -0.5),
        'o_s5_a_re': -0.5 + nrm(s5_shape, 0.01),
        'o_s5_a_im': math.pi * n_idx + nrm(s5_shape, 0.01),
        'o_s5_log_dt': jax.random.uniform(next(ks), (N_ODD, 2, S5_GROUPS), F32, math.log(S5_DT_MIN), math.log(S5_DT_MAX)),
        'o_s5_b_re': nrm((N_ODD, 2, S5_GROUPS, S5_STATE, S5_GROUP), (2 * S5_GROUP) ** -0.5),
        'o_s5_b_im': nrm((N_ODD, 2, S5_GROUPS, S5_STATE, S5_GROUP), (2 * S5_GROUP) ** -0.5),
        'o_s5_c_re': nrm((N_ODD, 2, S5_GROUPS, S5_GROUP, S5_STATE), 0.5),
        'o_s5_c_im': nrm((N_ODD, 2, S5_GROUPS, S5_GROUP, S5_STATE), 0.5),
        'o_s5_d': nrm((N_ODD, S5_WIDTH), 0.5),
        'o_glu_w': nrm((N_ODD, S5_WIDTH, S5_WIDTH), S5_WIDTH ** -0.5),
        'o_glu_b': nrm((N_ODD, S5_WIDTH), 0.02),
        'o_w_out': nrm((N_ODD, d, d), d ** -0.5),
        'moe_w_rg': nrm((DEPTH, d, N_GROUPS), d ** -0.5),
        'moe_w_re': nrm((DEPTH, d, N_EXPERTS), d ** -0.5),
        'moe_w_gate': nrm((DEPTH, N_EXPERTS, d, D_EXPERT), d ** -0.5),
        'moe_w_up': nrm((DEPTH, N_EXPERTS, d, D_EXPERT), d ** -0.5),
        'moe_w_down': nrm((DEPTH, N_EXPERTS, D_EXPERT, d), D_EXPERT ** -0.5),
        'final_g': 1.0 + nrm((d,), 0.02),
    }


def reference(x, c, ctx, c_ctx, w_mod, b_mod, norm_g,
              e_w_in, e_hy_conv_w, e_hy_conv_b, e_hy_w1, e_hy_b1, e_hy_w2, e_hy_b2, e_hy_w3, e_hy_freq,
              e_hy_fbias, e_qk_g, e_w_out,
              o_w_in, o_q_norm_g, o_w_uq, o_kv_norm_g, o_w_ukv, o_s5_a_re, o_s5_a_im, o_s5_log_dt,
              o_s5_b_re, o_s5_b_im, o_s5_c_re, o_s5_c_im, o_s5_d, o_glu_w, o_glu_b, o_w_out,
              moe_w_rg, moe_w_re, moe_w_gate, moe_w_up, moe_w_down, final_g):
    n_ctx = ctx.shape[1]
    rows, cols = grid_positions(x.shape[1])
    xc = ctx
    for i in range(DEPTH):
        j = i // 2
        ctx_needed = i < DEPTH - 1
        sh1, sc1, g1, sh2, sc2, g2 = [m[:, None, :] for m in jnp.split(jax.nn.silu(c) @ w_mod[i] + b_mod[i], 6, axis=-1)]
        csh1, csc1, cg1, csh2, csc2, cg2 = jnp.split(jax.nn.silu(c_ctx) @ w_mod[i] + b_mod[i], 6, axis=-1)
        h_all = jnp.concatenate([modnorm(xc, norm_g[i, 0], csh1, csc1), modnorm(x, norm_g[i, 0], sh1, sc1)], axis=1)
        if i % 2 == 0:
            proj = h_all @ e_w_in[j]
            mix_l, mix_c = even_mixer(proj[:, n_ctx:], proj[:, :n_ctx], rows, cols, ctx_needed,
                                      e_hy_conv_w[j], e_hy_conv_b[j], e_hy_w1[j], e_hy_b1[j], e_hy_w2[j], e_hy_b2[j],
                                      e_hy_w3[j], e_hy_freq[j], e_hy_fbias[j], e_qk_g[j])
            w_out = e_w_out[j]
        else:
            proj = h_all @ o_w_in[j]
            mix_l, mix_c = odd_mixer(proj[:, n_ctx:], proj[:, :n_ctx], rows, cols, ctx_needed,
                                     o_q_norm_g[j], o_w_uq[j], o_kv_norm_g[j], o_w_ukv[j],
                                     o_s5_a_re[j], o_s5_a_im[j], o_s5_log_dt[j], o_s5_b_re[j], o_s5_b_im[j],
                                     o_s5_c_re[j], o_s5_c_im[j], o_s5_d[j], o_glu_w[j], o_glu_b[j])
            w_out = o_w_out[j]
        moe_p = (moe_w_rg[i], moe_w_re[i], moe_w_gate[i], moe_w_up[i], moe_w_down[i])
        if ctx_needed:
            out = jnp.concatenate([mix_c, mix_l], axis=1) @ w_out
            xc = xc + cg1 * out[:, :n_ctx]
            x = x + g1 * out[:, n_ctx:]
            f = hier_moe(jnp.concatenate([modnorm(xc, norm_g[i, 1], csh2, csc2),
                                          modnorm(x, norm_g[i, 1], sh2, sc2)], axis=1), *moe_p)
            xc = xc + cg2 * f[:, :n_ctx]
            x = x + g2 * f[:, n_ctx:]
        else:
            x = x + g1 * (mix_l @ w_out)
            x = x + g2 * hier_moe(modnorm(x, norm_g[i, 1], sh2, sc2), *moe_p)
    return rmsnorm(x, final_g)
```

```python
import functools
import math

import jax
import jax.numpy as jnp
from jax import lax
from jax.experimental import pallas as pl
from jax.experimental.pallas import tpu as pltpu

F32 = jnp.float32
BF16 = jnp.bfloat16
I32 = jnp.int32

D_MODEL = 1024
GRID_W = 64
HEAD_DIM = 64
A_WIDTH = D_MODEL // 2
B_Q_HEADS = (D_MODEL - A_WIDTH) // HEAD_DIM
B_KV_HEADS = max(1, B_Q_HEADS // 4)
HY_ORDER = 2
HY_BANDS = 16
HY_TARGET = 1e-2
HY_MAX_DECAY = math.log(1.0 / HY_TARGET) / 0.3
HY_MIN_DECAY = math.log(1.0 / HY_TARGET) / 1.5
MLA_HEADS = 8
MLA_NOPE = 64
MLA_ROPE = 32
MLA_V = 64
MLA_Q_RANK = D_MODEL // 4
MLA_KV_RANK = D_MODEL // 8
S5_WIDTH = D_MODEL - MLA_HEADS * MLA_V
S5_GROUP = 16
S5_GROUPS = S5_WIDTH // S5_GROUP
S5_STATE = 64
S5_CHUNK = 16
N_GROUPS = 4
EXPERTS_PER_GROUP = 8
N_EXPERTS = N_GROUPS * EXPERTS_PER_GROUP
D_EXPERT = D_MODEL // 4
ROPE_THETA = 10000.0
EPS = 1e-6

LANES = 128
MOE_TILE = 256
HIGHEST = lax.Precision.HIGHEST
MIB = 1024 * 1024


def _cparams(sem, vmem_mib):
    return pltpu.CompilerParams(dimension_semantics=sem, vmem_limit_bytes=vmem_mib * MIB)


def _dot(a, b):
    return jnp.dot(a, b, preferred_element_type=F32)


def _dot_nt(a, b):
    return lax.dot_general(a, b, (((1,), (1,)), ((), ())), preferred_element_type=F32)


def _split_bf16(a):
    hi = a.astype(BF16)
    lo = (a - hi.astype(F32)).astype(BF16)
    return hi, lo


def _dot3(ah, al, bh, bl):
    return _dot(ah, bh) + (_dot(ah, bl) + _dot(al, bh))


def _sigmoid(x):
    return 1.0 / (1.0 + jnp.exp(-x))


def _rms(x, g, width):
    ms = jnp.sum(x * x, axis=-1, keepdims=True) * (1.0 / width)
    return x * lax.rsqrt(ms + EPS) * g


def _modnorm(x, g, shift, scale):
    return _rms(x, g, x.shape[-1]) * (1.0 + scale) + shift


def _rope(t, cs, s1, s2, half):
    return t * cs + pltpu.roll(t, LANES - half, 1) * s1 + pltpu.roll(t, half, 1) * s2


def _short_conv(u, w, b):
    n = u.shape[0]
    row = lax.broadcasted_iota(I32, u.shape, 0)
    prev = jnp.where(row == 0, 0.0, pltpu.roll(u, 1, 0))
    nxt = jnp.where(row == n - 1, 0.0, pltpu.roll(u, n - 1, 0))
    return prev * w[0:1] + u * w[1:2] + nxt * w[2:3] + b


def _mod_kernel(c_ref, w_ref, b_ref, o_ref):
    c = c_ref[...]
    s = c * _sigmoid(c)
    sh, sl = _split_bf16(s)
    wh, wl = _split_bf16(w_ref[...])
    o_ref[...] = _dot3(sh, sl, wh, wl) + b_ref[...]


def _modulation(cc, w, b):
    r, d = cc.shape
    n = w.shape[1]
    tn = 1536
    return pl.pallas_call(
        _mod_kernel,
        grid=(n // tn,),
        in_specs=[pl.BlockSpec((r, d), lambda j: (0, 0)),
                  pl.BlockSpec((d, tn), lambda j: (0, j)),
                  pl.BlockSpec((1, tn), lambda j: (0, j))],
        out_specs=pl.BlockSpec((r, tn), lambda j: (0, j)),
        out_shape=jax.ShapeDtypeStruct((r, n), F32),
        compiler_params=_cparams(("parallel",), 48),
        name="modulation",
    )(cc, w, b.reshape(1, n))


def _even_in_kernel(x_ref, g_ref, sh_ref, sc_ref, w_ref, qg_ref, kg_ref, cs_ref, s1_ref, s2_ref,
                    hy_ref, q_ref, k_ref, v_ref):
    xn = _modnorm(x_ref[0], g_ref[...], sh_ref[0], sc_ref[0]).astype(BF16)
    p = _dot(xn, w_ref[...])
    hw = 3 * A_WIDTH
    hy_ref[0] = p[:, :hw].astype(BF16)
    cs, s1, s2 = cs_ref[...], s1_ref[...], s2_ref[...]
    for h in range(B_Q_HEADS):
        t = _rms(p[:, hw + LANES * h: hw + LANES * (h + 1)], qg_ref[...], HEAD_DIM)
        q_ref[0, :, LANES * h: LANES * (h + 1)] = _rope(t, cs, s1, s2, HEAD_DIM // 4).astype(BF16)
    kw = hw + LANES * B_Q_HEADS
    for h in range(B_KV_HEADS):
        t = _rms(p[:, kw + LANES * h: kw + LANES * (h + 1)], kg_ref[...], HEAD_DIM)
        k_ref[0, :, LANES * h: LANES * (h + 1)] = _rope(t, cs, s1, s2, HEAD_DIM // 4).astype(BF16)
    vw = kw + LANES * B_KV_HEADS
    v_ref[0] = p[:, vw:].astype(BF16)


def _even_in(x, g, shift, scale, w, qg, kg, tabs):
    bsz, n, d = x.shape
    tm = min(256, n)
    nw = w.shape[1]
    row = lambda b, t: (b, t, 0)
    per_b = lambda b, t: (b, 0, 0)
    fixed = lambda b, t: (0, 0)
    tab = pl.BlockSpec((tm, LANES), lambda b, t: (t, 0))
    return pl.pallas_call(
        _even_in_kernel,
        grid=(bsz, n // tm),
        in_specs=[pl.BlockSpec((1, tm, d), row), pl.BlockSpec((1, d), fixed),
                  pl.BlockSpec((1, 1, d), per_b), pl.BlockSpec((1, 1, d), per_b),
                  pl.BlockSpec((d, nw), fixed), pl.BlockSpec((1, LANES), fixed),
                  pl.BlockSpec((1, LANES), fixed), tab, tab, tab],
        out_specs=[pl.BlockSpec((1, tm, 3 * A_WIDTH), row),
                   pl.BlockSpec((1, tm, LANES * B_Q_HEADS), row),
                   pl.BlockSpec((1, tm, LANES * B_KV_HEADS), row),
                   pl.BlockSpec((1, tm, LANES * B_KV_HEADS), row)],
        out_shape=[jax.ShapeDtypeStruct((bsz, n, 3 * A_WIDTH), BF16),
                   jax.ShapeDtypeStruct((bsz, n, LANES * B_Q_HEADS), BF16),
                   jax.ShapeDtypeStruct((bsz, n, LANES * B_KV_HEADS), BF16),
                   jax.ShapeDtypeStruct((bsz, n, LANES * B_KV_HEADS), BF16)],
        compiler_params=_cparams(("parallel", "parallel"), 48),
        name="even_in",
    )(x, g.reshape(1, d), shift, scale, w, qg, kg, *tabs)


def _odd_in_kernel(x_ref, g_ref, sh_ref, sc_ref, w_ref, qg_ref, kvg_ref, wuq_ref, wuk_ref, wuv_ref,
                   cs_ref, s1_ref, s2_ref, *out_refs, need_q):
    if need_q:
        u_ref, k_ref, v_ref, q_ref = out_refs
    else:
        u_ref, k_ref, v_ref = out_refs
    xn = _modnorm(x_ref[0], g_ref[...], sh_ref[0], sc_ref[0]).astype(BF16)
    p = _dot(xn, w_ref[...])
    c0, c1, c2 = MLA_Q_RANK, MLA_Q_RANK + MLA_KV_RANK, MLA_Q_RANK + MLA_KV_RANK + LANES
    u_ref[0] = p[:, c2:].astype(BF16)
    cs, s1, s2 = cs_ref[...], s1_ref[...], s2_ref[...]
    ckv = _rms(p[:, c0:c1], kvg_ref[...], MLA_KV_RANK).astype(BF16)
    kn = _dot(ckv, wuk_ref[...])
    v_ref[0] = _dot(ckv, wuv_ref[...]).astype(BF16)
    kr = _rope(p[:, c1:c2], cs, s1, s2, MLA_ROPE // 4)
    for h in range(MLA_HEADS):
        k_ref[0, :, LANES * h: LANES * (h + 1)] = (kn[:, LANES * h: LANES * (h + 1)] + kr).astype(BF16)
    if need_q:
        cq = _rms(p[:, :c0], qg_ref[...], MLA_Q_RANK).astype(BF16)
        q = _dot(cq, wuq_ref[...])
        for h in range(MLA_HEADS):
            t = q[:, LANES * h: LANES * (h + 1)]
            q_ref[0, :, LANES * h: LANES * (h + 1)] = _rope(t, cs, s1, s2, MLA_ROPE // 4).astype(BF16)


def _odd_in(x, g, shift, scale, w, qg, kvg, wuq, wuk, wuv, tabs, need_q):
    bsz, n, d = x.shape
    tm = min(256, n)
    hw = LANES * MLA_HEADS
    row = lambda b, t: (b, t, 0)
    per_b = lambda b, t: (b, 0, 0)
    fixed = lambda b, t: (0, 0)
    tab = pl.BlockSpec((tm, LANES), lambda b, t: (t, 0))
    widths = [S5_WIDTH, hw, hw] + ([hw] if need_q else [])
    return pl.pallas_call(
        functools.partial(_odd_in_kernel, need_q=need_q),
        grid=(bsz, n // tm),
        in_specs=[pl.BlockSpec((1, tm, d), row), pl.BlockSpec((1, d), fixed),
                  pl.BlockSpec((1, 1, d), per_b), pl.BlockSpec((1, 1, d), per_b),
                  pl.BlockSpec(w.shape, fixed), pl.BlockSpec((1, MLA_Q_RANK), fixed),
                  pl.BlockSpec((1, MLA_KV_RANK), fixed), pl.BlockSpec(wuq.shape, fixed),
                  pl.BlockSpec(wuk.shape, fixed), pl.BlockSpec(wuv.shape, fixed), tab, tab, tab],
        out_specs=[pl.BlockSpec((1, tm, wd), row) for wd in widths],
        out_shape=[jax.ShapeDtypeStruct((bsz, n, wd), BF16) for wd in widths],
        compiler_params=_cparams(("parallel", "parallel"), 48),
        name="odd_in_q" if need_q else "odd_in",
    )(x, g.reshape(1, d), shift, scale, w, qg, kvg, wuq, wuk, wuv, *tabs)


def _hy_fwd_kernel(wc_ref, ws_ref, u_ref, cw_ref, cb_ref, ka_ref, kb_ref, ya_ref, yb_ref, ubf_ref,
                   *, conv, tf):
    f = pl.program_id(1)

    @pl.when(f == 0)
    def _():
        u = u_ref[0].astype(F32)
        if conv:
            u = _short_conv(u, cw_ref[...], cb_ref[...])
        ubf_ref[...] = u.astype(BF16)

    ub = ubf_ref[...]
    a = _dot(wc_ref[...], ub)
    b = _dot(ws_ref[...], ub)
    ka, kb = ka_ref[...], kb_ref[...]
    first = (lax.broadcasted_iota(I32, a.shape, 0) + f * tf) == 0
    ya_ref[0] = jnp.where(first, a * ka, a * ka - b * kb).astype(BF16)
    yb_ref[0] = jnp.where(first, b * kb, a * kb + b * ka).astype(BF16)


def _hy_fwd(wc, ws, u, col, cw, cb, ka, kb, conv):
    bsz, n, _ = u.shape
    c = A_WIDTH
    tf = min(512, n)
    return pl.pallas_call(
        functools.partial(_hy_fwd_kernel, conv=conv, tf=tf),
        grid=(bsz, n // tf),
        in_specs=[pl.BlockSpec((tf, n), lambda b, f: (f, 0)), pl.BlockSpec((tf, n), lambda b, f: (f, 0)),
                  pl.BlockSpec((1, n, c), lambda b, f: (b, 0, col)),
                  pl.BlockSpec((3, c), lambda b, f: (0, 0)), pl.BlockSpec((1, c), lambda b, f: (0, 0)),
                  pl.BlockSpec((tf, c), lambda b, f: (f, 0)), pl.BlockSpec((tf, c), lambda b, f: (f, 0))],
        out_specs=[pl.BlockSpec((1, tf, c), lambda b, f: (b, f, 0))] * 2,
        out_shape=[jax.ShapeDtypeStruct((bsz, n, c), BF16)] * 2,
        scratch_shapes=[pltpu.VMEM((n, c), BF16)],
        compiler_params=_cparams(("parallel", "arbitrary"), 48),
        name="hyena_fwd",
    )(wc, ws, u, cw, cb, ka, kb)


def _hy_inv_kernel(ci_ref, si_ref, ya_ref, yb_ref, a_ref, g_ref, cwa_ref, cba_ref, cwg_ref, cbg_ref,
                   bias_ref, o_ref, a_sc, g_sc, *, conv_a, tt):
    t = pl.program_id(1)

    @pl.when(t == 0)
    def _():
        a = a_ref[0].astype(F32)
        if conv_a:
            a = _short_conv(a, cwa_ref[...], cba_ref[...])
        a_sc[...] = a
        g_sc[...] = _short_conv(g_ref[0].astype(F32), cwg_ref[...], cbg_ref[...])

    y = _dot(ci_ref[...], ya_ref[0]) + _dot(si_ref[...], yb_ref[0])
    rows = pl.ds(pl.multiple_of(t * tt, tt), tt)
    y = y + a_sc[rows, :] * bias_ref[...]
    o_ref[0] = (g_sc[rows, :] * y).astype(BF16)


def _hy_inv(ci, si, ya, yb, a, a_col, g, g_col, cwa, cba, cwg, cbg, bias, conv_a):
    bsz, n, c = ya.shape
    tt = min(256, n)
    whole = lambda b, t: (b, 0, 0)
    fixed = lambda b, t: (0, 0)
    return pl.pallas_call(
        functools.partial(_hy_inv_kernel, conv_a=conv_a, tt=tt),
        grid=(bsz, n // tt),
        in_specs=[pl.BlockSpec((tt, n), lambda b, t: (t, 0)), pl.BlockSpec((tt, n), lambda b, t: (t, 0)),
                  pl.BlockSpec((1, n, c), whole), pl.BlockSpec((1, n, c), whole),
                  pl.BlockSpec((1, n, c), lambda b, t: (b, 0, a_col)),
                  pl.BlockSpec((1, n, c), lambda b, t: (b, 0, g_col)),
                  pl.BlockSpec((3, c), fixed), pl.BlockSpec((1, c), fixed),
                  pl.BlockSpec((3, c), fixed), pl.BlockSpec((1, c), fixed), pl.BlockSpec((1, c), fixed)],
        out_specs=pl.BlockSpec((1, tt, c), lambda b, t: (b, t, 0)),
        out_shape=jax.ShapeDtypeStruct((bsz, n, c), BF16),
        scratch_shapes=[pltpu.VMEM((n, c), F32), pltpu.VMEM((n, c), F32)],
        compiler_params=_cparams(("parallel", "arbitrary"), 56),
        name="hyena_inv",
    )(ci, si, ya, yb, a, g, cwa, cba, cwg, cbg, bias)


def _dft_mats(n):
    big = 2 * n
    f = jnp.arange(n, dtype=I32)[:, None]
    s = jnp.arange(n, dtype=I32)[None, :]
    ang = ((f * s) % big).astype(F32) * (2.0 * math.pi / big)
    cos, sin = jnp.cos(ang), jnp.sin(ang)
    alt = jnp.where(jnp.arange(n) % 2 == 0, 1.0, -1.0).astype(F32)
    wc = cos
    ws = jnp.where(f == 0, alt[None, :], -sin)
    ci = jnp.where(s == 0, 1.0 / big, (2.0 / big) * cos)
    si = jnp.where(s == 0, alt[:, None] / big, -(2.0 / big) * sin)
    return wc.astype(BF16), ws.astype(BF16), ci.astype(BF16), si.astype(BF16)


def _hyena_filter_spectrum(n, w1, b1, w2, b2, w3, freq):
    t = jnp.arange(n, dtype=F32)
    t_norm = t / max(n - 1, 1)
    bands = jnp.linspace(1e-4, HY_BANDS - 1, HY_BANDS, dtype=F32)
    ang = (2.0 * math.pi / n) * t[:, None] * bands
    z = jnp.concatenate([t_norm[:, None], jnp.cos(ang), jnp.sin(ang)], axis=-1)
    hdn = jnp.sin(freq * (jnp.dot(z, w1, precision=HIGHEST) + b1))
    hdn = jnp.sin(freq * (jnp.dot(hdn, w2, precision=HIGHEST) + b2))
    taps = jnp.dot(hdn, w3, precision=HIGHEST).reshape(n, 2, HY_ORDER, A_WIDTH)
    deltas = jnp.linspace(HY_MIN_DECAY, HY_MAX_DECAY, A_WIDTH, dtype=F32)
    taps = taps * jnp.exp(-t_norm[:, None] * deltas)[:, None, None, :]
    fwd, bwd = taps[:, 0], taps[:, 1]
    circ = jnp.concatenate([fwd, jnp.zeros((1, HY_ORDER, A_WIDTH), F32), bwd[:0:-1]], axis=0)
    circ = circ * lax.rsqrt(jnp.sum(circ * circ, axis=0, keepdims=True) + EPS)
    kf = jnp.fft.rfft(circ, axis=0)
    ka = jnp.real(kf[:n]).astype(F32)
    kb = jnp.concatenate([jnp.real(kf[n:n + 1]), jnp.imag(kf[1:n])], axis=0).astype(F32)
    return ka, kb


def _hyena(hy, cw, cb, filt, fbias):
    n = hy.shape[1]
    c = A_WIDTH
    wc, ws, ci, si = _dft_mats(n)
    ka, kb = _hyena_filter_spectrum(n, *filt)
    cws = [cw[:, i * c:(i + 1) * c] for i in range(3)]
    cbs = [cb[i * c:(i + 1) * c].reshape(1, c) for i in range(3)]
    ya, yb = _hy_fwd(wc, ws, hy, 2, cws[2], cbs[2], ka[:, 0], kb[:, 0], True)
    z = _hy_inv(ci, si, ya, yb, hy, 2, hy, 0, cws[2], cbs[2], cws[0], cbs[0], fbias[0:1], True)
    ya, yb = _hy_fwd(wc, ws, z, 0, cws[2], cbs[2], ka[:, 1], kb[:, 1], False)
    return _hy_inv(ci, si, ya, yb, z, 0, hy, 1, cws[2], cbs[2], cws[1], cbs[1], fbias[1:2], False)


def _attn_kernel(*refs, nseg, hps, kps, scale):
    q_ref = refs[0]
    k_refs = refs[1:1 + nseg]
    v_refs = refs[1 + nseg:1 + 2 * nseg]
    o_ref = refs[1 + 2 * nseg]
    low = lax.broadcasted_iota(I32, (q_ref.shape[1], LANES), 1) < HEAD_DIM
    for pair in range(hps // 2):
        halves = []
        for half in range(2):
            i = 2 * pair + half
            kv = (i * kps) // hps
            q = q_ref[0, :, LANES * i: LANES * (i + 1)]
            ss = [_dot_nt(q, k[0, :, LANES * kv: LANES * (kv + 1)]) * scale for k in k_refs]
            m = ss[0].max(axis=-1, keepdims=True)
            for s in ss[1:]:
                m = jnp.maximum(m, s.max(axis=-1, keepdims=True))
            ps = [jnp.exp(s - m) for s in ss]
            den = ps[0].sum(axis=-1, keepdims=True)
            for p in ps[1:]:
                den = den + p.sum(axis=-1, keepdims=True)
            o = _dot(ps[0].astype(BF16), v_refs[0][0, :, LANES * kv: LANES * (kv + 1)])
            for p, v in zip(ps[1:], v_refs[1:]):
                o = o + _dot(p.astype(BF16), v[0, :, LANES * kv: LANES * (kv + 1)])
            halves.append(o / den)
        o_ref[0, :, LANES * pair: LANES * (pair + 1)] = jnp.where(low, halves[0], halves[1]).astype(BF16)


def _attention(q, ks, vs, hps, kps, scale):
    bsz, lq, qw = q.shape
    nh = qw // LANES
    tq = min(256, lq)
    nseg = len(ks)
    kv_spec = lambda a: pl.BlockSpec((1, a.shape[1], kps * LANES), lambda b, g, t: (b, 0, g))
    return pl.pallas_call(
        functools.partial(_attn_kernel, nseg=nseg, hps=hps, kps=kps, scale=scale),
        grid=(bsz, nh // hps, lq // tq),
        in_specs=[pl.BlockSpec((1, tq, hps * LANES), lambda b, g, t: (b, t, g))]
        + [kv_spec(a) for a in ks] + [kv_spec(a) for a in vs],
        out_specs=pl.BlockSpec((1, tq, hps * HEAD_DIM), lambda b, g, t: (b, t, g)),
        out_shape=jax.ShapeDtypeStruct((bsz, lq, nh * HEAD_DIM), BF16),
        compiler_params=_cparams(("parallel", "parallel", "parallel"), 56),
        name="attention",
    )(q, *ks, *vs)


def _s5_kernel(u_ref, w_ref, a_ref, of_ref, or_ref, y_ref, yi_sc, sf_sc, sr_sc, xf_sc, xr_sc,
               *, nc, ncc, bsz):
    cw = S5_CHUNK * S5_GROUP
    sw = 2 * S5_STATE
    r = _dot(u_ref[0], w_ref[0])
    yi_sc[...] = r[:, :cw]
    sf_sc[...] = r[:, cw:cw + sw]
    sr_sc[...] = r[:, cw + sw:]
    a = a_ref[0]

    def advance(x, s, d):
        return x * a[2 * d:2 * d + 1] + pltpu.roll(x, S5_STATE, 1) * a[2 * d + 1:2 * d + 2] + s

    def rows(c):
        return pl.ds(pl.multiple_of(c * bsz, bsz), bsz)

    def fstep(c, x):
        xf_sc[rows(c), :] = x
        return advance(x, sf_sc[rows(c), :], 0)

    lax.fori_loop(0, nc, fstep, jnp.zeros((bsz, sw), F32))

    def rstep(base):
        def step(i, x):
            c = base - i
            xr_sc[rows(c), :] = x
            return advance(x, sr_sc[rows(c), :], 1)
        return step

    x = lax.fori_loop(0, ncc, rstep(ncc - 1), jnp.zeros((bsz, sw), F32))
    lax.fori_loop(0, nc - ncc, rstep(nc - 1), x)

    lo = ncc * bsz
    y = yi_sc[lo:, :]
    y = y + _dot(xf_sc[lo:, :].astype(BF16), of_ref[0])
    y = y + _dot(xr_sc[lo:, :].astype(BF16), or_ref[0])
    y_ref[0] = y


def _s5_scan(uf, w, a16, of, orr, nc, ncc, bsz):
    g, rws, cw = uf.shape
    sw = 2 * S5_STATE
    out_rows = (nc - ncc) * bsz
    blk = lambda a: pl.BlockSpec((1,) + a.shape[1:], lambda i: (i, 0, 0))
    return pl.pallas_call(
        functools.partial(_s5_kernel, nc=nc, ncc=ncc, bsz=bsz),
        grid=(g,),
        in_specs=[blk(uf), blk(w), blk(a16), blk(of), blk(orr)],
        out_specs=pl.BlockSpec((1, out_rows, cw), lambda i: (i, 0, 0)),
        out_shape=jax.ShapeDtypeStruct((g, out_rows, cw), F32),
        scratch_shapes=[pltpu.VMEM((rws, cw), F32)] + [pltpu.VMEM((rws, sw), F32)] * 4,
        compiler_params=_cparams(("parallel",), 48),
        name="s5_scan",
    )(uf, w, a16, of, orr)


def _s5_matrices(a_re, a_im, log_dt, b_re, b_im, c_re, c_im, d_skip):
    lc, h, p, g = S5_CHUNK, S5_GROUP, S5_STATE, S5_GROUPS
    dt = jnp.exp(log_dt)[..., None]
    lam_re, lam_im = a_re * dt, a_im * dt
    mag = jnp.exp(lam_re)
    er, ei = mag * jnp.cos(lam_im) - 1.0, mag * jnp.sin(lam_im)
    den = a_re * a_re + a_im * a_im
    co_re = (er * a_re + ei * a_im) / den
    co_im = (ei * a_re - er * a_im) / den
    bb_re = co_re[..., None] * b_re - co_im[..., None] * b_im
    bb_im = co_re[..., None] * b_im + co_im[..., None] * b_re
    m = jnp.arange(lc + 1, dtype=F32)[:, None, None, None]
    pw_mag = jnp.exp(m * lam_re[None])
    pw_re, pw_im = pw_mag * jnp.cos(m * lam_im[None]), pw_mag * jnp.sin(m * lam_im[None])
    ab_re = pw_re[..., None] * bb_re[None] - pw_im[..., None] * bb_im[None]
    ab_im = pw_re[..., None] * bb_im[None] + pw_im[..., None] * bb_re[None]
    kk = (jnp.einsum("dgop,mdgpi->mdgoi", c_re, ab_re[:lc], precision=HIGHEST)
          - jnp.einsum("dgop,mdgpi->mdgoi", c_im, ab_im[:lc], precision=HIGHEST))
    j = jnp.arange(lc)[:, None]
    k = jnp.arange(lc)[None, :]
    lag_f = jnp.clip(k - j, 0, lc - 1)
    lag_r = jnp.clip(j - k, 0, lc - 1)
    tf = jnp.where((k >= j)[:, :, None, None, None], kk[lag_f, 0], 0.0)
    tr = jnp.where((j >= k)[:, :, None, None, None], kk[lag_r, 1], 0.0)
    tw = jnp.transpose(tf + tr, (2, 0, 4, 1, 3))
    dsk = d_skip.reshape(g, h)
    eye = (jnp.eye(lc, dtype=F32)[None, :, None, :, None] * jnp.eye(h, dtype=F32)[None, None, :, None, :]
           * dsk[:, None, :, None, None])
    tw = (tw + eye).reshape(g, lc * h, lc * h)
    idx_f = (lc - 1 - jnp.arange(lc))
    idx_r = jnp.arange(lc)

    def to_state(idx, d):
        re = jnp.transpose(ab_re[idx, d], (1, 0, 3, 2))
        im = jnp.transpose(ab_im[idx, d], (1, 0, 3, 2))
        return jnp.concatenate([re, im], axis=-1).reshape(g, lc * h, 2 * p)

    w = jnp.concatenate([tw, to_state(idx_f, 0), to_state(idx_r, 1)], axis=-1)

    def from_state(idx, d):
        pr, pi = pw_re[idx, d], pw_im[idx, d]
        cr, cim = c_re[d], c_im[d]
        mr = cr[None] * pr[:, :, None, :] - cim[None] * pi[:, :, None, :]
        mi = cr[None] * pi[:, :, None, :] + cim[None] * pr[:, :, None, :]
        return jnp.transpose(jnp.concatenate([mr, -mi], axis=-1), (1, 3, 0, 2)).reshape(g, 2 * p, lc * h)

    of = from_state(jnp.arange(lc) + 1, 0)
    orr = from_state(lc - jnp.arange(lc), 1)
    a16 = jnp.stack([jnp.concatenate([pw_re[lc, 0], pw_re[lc, 0]], -1),
                     jnp.concatenate([-pw_im[lc, 0], pw_im[lc, 0]], -1),
                     jnp.concatenate([pw_re[lc, 1], pw_re[lc, 1]], -1),
                     jnp.concatenate([-pw_im[lc, 1], pw_im[lc, 1]], -1)], axis=1)
    return w.astype(BF16), a16.astype(F32), of.astype(BF16), orr.astype(BF16)


def _glu_kernel(y_ref, w_ref, b_ref, o_ref):
    y = y_ref[0]
    inner = 0.7978845608028654 * (y + 0.044715 * (y * y * y))
    gl = 0.5 * y * (1.0 + jnp.tanh(inner))
    o_ref[0] = (gl * _sigmoid(_dot(gl.astype(BF16), w_ref[...]) + b_ref[...])).astype(BF16)


def _glu(y, w, b):
    bsz, n, c = y.shape
    tm = min(512, n)
    return pl.pallas_call(
        _glu_kernel,
        grid=(bsz, n // tm),
        in_specs=[pl.BlockSpec((1, tm, c), lambda i, t: (i, t, 0)),
                  pl.BlockSpec((c, c), lambda i, t: (0, 0)), pl.BlockSpec((1, c), lambda i, t: (0, 0))],
        out_specs=pl.BlockSpec((1, tm, c), lambda i, t: (i, t, 0)),
        out_shape=jax.ShapeDtypeStruct((bsz, n, c), BF16),
        compiler_params=_cparams(("parallel", "parallel"), 32),
        name="s5_glu",
    )(y, w, b.reshape(1, c))


def _s5_mixer(u_c, u_l, mats, glu_w, glu_b):
    bsz, ncx, c = u_c.shape
    nl = u_l.shape[1]
    lc, h, g = S5_CHUNK, S5_GROUP, S5_GROUPS
    ncc, ncl = ncx // lc, nl // lc
    nc = ncc + ncl
    u = jnp.concatenate([u_c, u_l], axis=1).reshape(bsz, nc, lc, g, h)
    uf = jnp.transpose(u, (3, 1, 0, 2, 4)).reshape(g, nc * bsz, lc * h)
    y = _s5_scan(uf, *mats, nc, ncc, bsz)
    y = jnp.transpose(y.reshape(g, ncl, bsz, lc, h), (2, 1, 3, 0, 4)).reshape(bsz, nl, c)
    return _glu(y, glu_w, glu_b)


def _route(lg):
    lane = lax.broadcasted_iota(I32, lg.shape, 1).astype(F32)
    neg = jnp.float32(-1e30)
    far = jnp.float32(LANES)
    is_g = lane < N_GROUPS
    lgm = jnp.where(is_g, lg, neg)
    m = lgm.max(axis=-1, keepdims=True)
    gidx = jnp.where(lgm == m, lane, far).min(axis=-1, keepdims=True)
    gp = 1.0 / jnp.where(is_g, jnp.exp(lgm - m), 0.0).sum(axis=-1, keepdims=True)
    lo = N_GROUPS + EXPERTS_PER_GROUP * gidx
    is_e = (lane >= lo) & (lane < lo + EXPERTS_PER_GROUP)
    lem = jnp.where(is_e, lg, neg)
    l0 = lem.max(axis=-1, keepdims=True)
    i0 = jnp.where((lem == l0) & is_e, lane, far).min(axis=-1, keepdims=True)
    lem1 = jnp.where(lane == i0, neg, lem)
    l1 = lem1.max(axis=-1, keepdims=True)
    i1 = jnp.where((lem1 == l1) & is_e & (lane != i0), lane, far).min(axis=-1, keepdims=True)
    e1 = jnp.exp(l1 - l0)
    w0 = 1.0 / (1.0 + e1)
    w1 = e1 * w0
    out = jnp.where(lane == 0, i0 - N_GROUPS, 0.0)
    out = jnp.where(lane == 1, i1 - N_GROUPS, out)
    out = jnp.where(lane == 2, gp * w0, out)
    return jnp.where(lane == 3, gp * w1, out)


def _proj_res_kernel(a_ref, b_ref, wa_ref, wb_ref, x_ref, g1_ref, ng_ref, sh_ref, sc_ref, rh_ref, rl_ref,
                     xo_ref, h_ref, route_ref):
    o = _dot(a_ref[0], wa_ref[...]) + _dot(b_ref[0], wb_ref[...])
    x = x_ref[0] + g1_ref[0] * o
    xo_ref[0] = x
    h = _modnorm(x, ng_ref[...], sh_ref[0], sc_ref[0])
    h_ref[0] = h
    hh, hl = _split_bf16(h)
    route_ref[0] = _route(_dot3(hh, hl, rh_ref[...], rl_ref[...]))


def _proj_res(a, b, wa, wb, x, g1, ng, shift, scale, rh, rl):
    bsz, n, d = x.shape
    tm = min(256, n)
    row = lambda i, t: (i, t, 0)
    per_b = lambda i, t: (i, 0, 0)
    fixed = lambda i, t: (0, 0)
    return pl.pallas_call(
        _proj_res_kernel,
        grid=(bsz, n // tm),
        in_specs=[pl.BlockSpec((1, tm, a.shape[2]), row), pl.BlockSpec((1, tm, b.shape[2]), row),
                  pl.BlockSpec(wa.shape, fixed), pl.BlockSpec(wb.shape, fixed),
                  pl.BlockSpec((1, tm, d), row), pl.BlockSpec((1, 1, d), per_b), pl.BlockSpec((1, d), fixed),
                  pl.BlockSpec((1, 1, d), per_b), pl.BlockSpec((1, 1, d), per_b),
                  pl.BlockSpec((d, LANES), fixed), pl.BlockSpec((d, LANES), fixed)],
        out_specs=[pl.BlockSpec((1, tm, d), row), pl.BlockSpec((1, tm, d), row),
                   pl.BlockSpec((1, tm, LANES), row)],
        out_shape=[jax.ShapeDtypeStruct((bsz, n, d), F32), jax.ShapeDtypeStruct((bsz, n, d), F32),
                   jax.ShapeDtypeStruct((bsz, n, LANES), F32)],
        compiler_params=_cparams(("parallel", "parallel"), 40),
        name="proj_res",
    )(a, b, wa, wb, x, g1, ng.reshape(1, d), shift, scale, rh, rl)


def _row_copy(src, dst, sem):
    return pltpu.make_async_copy(src, dst, sem)


def _dispatch_kernel(dest_ref, h_ref, xs_in_ref, xs_ref, sem, *, td):
    del xs_in_ref

    def start(r, c):
        for k in range(2):
            d = dest_ref[0, 0, 2 * r + k]
            _row_copy(h_ref.at[pl.ds(r, 1)], xs_ref.at[pl.ds(d, 1)], sem).start()
        return c

    lax.fori_loop(0, td, start, 0)

    def wait(r, c):
        for k in range(2):
            _row_copy(h_ref.at[pl.ds(0, 1)], xs_ref.at[pl.ds(0, 1)], sem).wait()
        return c

    lax.fori_loop(0, td, wait, 0)


def _dispatch(dest, h, xs):
    t, d = h.shape
    td = min(256, t)
    return pl.pallas_call(
        functools.partial(_dispatch_kernel, td=td),
        grid=(t // td,),
        in_specs=[pl.BlockSpec((1, 1, 2 * td), lambda i: (i, 0, 0), memory_space=pltpu.SMEM),
                  pl.BlockSpec((td, d), lambda i: (i, 0)),
                  pl.BlockSpec(memory_space=pl.ANY)],
        out_specs=pl.BlockSpec(memory_space=pl.ANY),
        out_shape=jax.ShapeDtypeStruct(xs.shape, xs.dtype),
        scratch_shapes=[pltpu.SemaphoreType.DMA],
        input_output_aliases={2: 0},
        compiler_params=pltpu.CompilerParams(dimension_semantics=("arbitrary",), has_side_effects=True),
        name="moe_dispatch",
    )(dest.reshape(t // td, 1, 2 * td), h, xs)


def _experts_kernel(te_ref, tv_ref, xs_ref, wg_ref, wu_ref, wd_ref, ys_ref):
    i = pl.program_id(0)

    @pl.when(tv_ref[i] > 0)
    def _():
        x = xs_ref[...].astype(BF16)
        gt = _dot(x, wg_ref[0])
        hid = (gt * _sigmoid(gt)) * _dot(x, wu_ref[0])
        ys_ref[...] = _dot(hid.astype(BF16), wd_ref[0])

    @pl.when(tv_ref[i] == 0)
    def _():
        ys_ref[...] = jnp.zeros(ys_ref.shape, F32)


def _experts(tile_expert, tile_valid, xs, wg, wu, wd):
    rows, d = xs.shape
    nt = rows // MOE_TILE
    wmap = lambda i, te, tv: (te[i], 0, 0)
    return pl.pallas_call(
        _experts_kernel,
        grid_spec=pltpu.PrefetchScalarGridSpec(
            num_scalar_prefetch=2, grid=(nt,),
            in_specs=[pl.BlockSpec((MOE_TILE, d), lambda i, te, tv: (i, 0)),
                      pl.BlockSpec((1, d, D_EXPERT), wmap), pl.BlockSpec((1, d, D_EXPERT), wmap),
                      pl.BlockSpec((1, D_EXPERT, d), wmap)],
            out_specs=pl.BlockSpec((MOE_TILE, d), lambda i, te, tv: (i, 0))),
        out_shape=jax.ShapeDtypeStruct((rows, d), F32),
        compiler_params=_cparams(("arbitrary",), 40),
        name="moe_experts",
    )(tile_expert, tile_valid, xs, wg, wu, wd)


def _combine_kernel(dest_ref, route_ref, x_ref, g2_ref, fg_ref, ys_ref, o_ref, buf, sem, *, tc, final):
    def start(r, c):
        for k in range(2):
            d = dest_ref[0, 0, 2 * r + k]
            _row_copy(ys_ref.at[pl.ds(d, 1)], buf.at[k, pl.ds(r, 1)], sem).start()
        return c

    lax.fori_loop(0, tc, start, 0)

    def wait(r, c):
        for k in range(2):
            _row_copy(ys_ref.at[pl.ds(0, 1)], buf.at[0, pl.ds(0, 1)], sem).wait()
        return c

    lax.fori_loop(0, tc, wait, 0)
    rt = route_ref[0]
    y = rt[:, 2:3] * buf[0] + rt[:, 3:4] * buf[1]
    x = x_ref[0] + g2_ref[0] * y
    if final:
        x = _rms(x, fg_ref[...], x.shape[-1])
    o_ref[0] = x


def _combine(dest, route, x, g2, fg, ys, final):
    bsz, n, d = x.shape
    tc = min(256, n)
    nt = n // tc
    row = lambda i, t: (i, t, 0)
    return pl.pallas_call(
        functools.partial(_combine_kernel, tc=tc, final=final),
        grid=(bsz, nt),
        in_specs=[pl.BlockSpec((1, 1, 2 * tc), lambda i, t: (i * nt + t, 0, 0), memory_space=pltpu.SMEM),
                  pl.BlockSpec((1, tc, LANES), row), pl.BlockSpec((1, tc, d), row),
                  pl.BlockSpec((1, 1, d), lambda i, t: (i, 0, 0)), pl.BlockSpec((1, d), lambda i, t: (0, 0)),
                  pl.BlockSpec(memory_space=pl.ANY)],
        out_specs=pl.BlockSpec((1, tc, d), row),
        out_shape=jax.ShapeDtypeStruct((bsz, n, d), F32),
        scratch_shapes=[pltpu.VMEM((2, tc, d), F32), pltpu.SemaphoreType.DMA],
        compiler_params=_cparams(("arbitrary", "arbitrary"), 32),
        name="moe_combine",
    )(dest.reshape(bsz * nt, 1, 2 * tc), route, x, g2, fg.reshape(1, d), ys)


def _moe_plan(eids):
    a = eids.shape[0]
    onehot = (eids[:, None] == jnp.arange(N_EXPERTS, dtype=I32)[None, :]).astype(I32)
    csum = jnp.cumsum(onehot, axis=0)
    rank = jnp.take_along_axis(csum, eids[:, None], axis=1)[:, 0] - 1
    counts = csum[-1]
    padded = ((counts + MOE_TILE - 1) // MOE_TILE) * MOE_TILE
    ends = jnp.cumsum(padded)
    dest = (ends - padded)[eids] + rank
    nt = a // MOE_TILE + N_EXPERTS
    starts = jnp.arange(nt, dtype=I32) * MOE_TILE
    tile_expert = jnp.minimum(jnp.searchsorted(ends, starts, side="right"), N_EXPERTS - 1).astype(I32)
    tile_valid = (starts < ends[-1]).astype(I32)
    return dest.astype(I32), tile_expert, tile_valid, nt * MOE_TILE


def _moe(streams, wg, wu, wd, final_g, final):
    d = D_MODEL
    eids = jnp.concatenate([s[2][..., :2].reshape(-1) for s in streams]).astype(I32)
    dest, tile_expert, tile_valid, rows = _moe_plan(eids)
    xs = jnp.zeros((rows, d), F32)
    off = 0
    dests = []
    for x, h, route, g2 in streams:
        t = x.shape[0] * x.shape[1]
        dseg = lax.slice(dest, (2 * off,), (2 * (off + t),))
        dests.append(dseg)
        xs = _dispatch(dseg, h.reshape(t, d), xs)
        off += t
    ys = _experts(tile_expert, tile_valid, xs, wg, wu, wd)
    return [_combine(dseg, route, x, g2, final_g, ys, final)
            for (x, h, route, g2), dseg in zip(streams, dests)]


def _rope_tables(n, half, offset, identity):
    cs = jnp.ones((n, LANES), F32)
    z = jnp.zeros((n, LANES), F32)
    if identity:
        return cs, z, z
    rows = (jnp.arange(n, dtype=I32) // GRID_W).astype(F32)
    cols = (jnp.arange(n, dtype=I32) % GRID_W).astype(F32)
    freqs = ROPE_THETA ** (-jnp.arange(half, dtype=F32) / half)
    ar, ac = rows[:, None] * freqs, cols[:, None] * freqs
    zero = jnp.zeros((n, half), F32)
    cos4 = jnp.concatenate([jnp.cos(ar), jnp.cos(ar), jnp.cos(ac), jnp.cos(ac)], axis=1)
    s1 = jnp.concatenate([-jnp.sin(ar), zero, -jnp.sin(ac), zero], axis=1)
    s2 = jnp.concatenate([zero, jnp.sin(ar), zero, jnp.sin(ac)], axis=1)
    place = lambda base, t: lax.dynamic_update_slice(base, t, (0, offset))
    return place(cs, cos4), place(z, s1), place(z, s2)


def _pad_heads(w, nheads, width, left=0):
    k = w.shape[0]
    w = w.reshape(k, nheads, width)
    w = jnp.pad(w, ((0, 0), (0, 0), (left, LANES - width - left)))
    return w.reshape(k, nheads * LANES)


def _dup_heads(w, nheads):
    k = w.shape[0]
    w = w.reshape(k, nheads, HEAD_DIM)
    return jnp.concatenate([w, w], axis=-1).reshape(k, nheads * LANES)


def _lane_pad(v, left=0):
    return jnp.pad(v, (left, LANES - v.shape[0] - left)).reshape(1, LANES)


def kernel(x, c, ctx, c_ctx, w_mod, b_mod, norm_g, e_w_in, e_hy_conv_w, e_hy_conv_b, e_hy_w1, e_hy_b1, e_hy_w2, e_hy_b2, e_hy_w3, e_hy_freq, e_hy_fbias, e_qk_g, e_w_out, o_w_in, o_q_norm_g, o_w_uq, o_kv_norm_g, o_w_ukv, o_s5_a_re, o_s5_a_im, o_s5_log_dt, o_s5_b_re, o_s5_b_im, o_s5_c_re, o_s5_c_im, o_s5_d, o_glu_w, o_glu_b, o_w_out, moe_w_rg, moe_w_re, moe_w_gate, moe_w_up, moe_w_down, final_g):
    bsz, n, d = x.shape
    depth = w_mod.shape[0]
    xc = ctx
    ncx = ctx.shape[1]
    cc = jnp.zeros((bsz + 8, d), F32).at[:bsz].set(c).at[bsz].set(c_ctx)

    for i in range(depth):
        j = i // 2
        last = i == depth - 1
        mods = _modulation(cc, w_mod[i], b_mod[i])
        lat = [m.reshape(bsz, 1, d) for m in jnp.split(mods[:bsz], 6, axis=-1)]
        cxm = [jnp.broadcast_to(m.reshape(1, 1, d), (bsz, 1, d)) for m in jnp.split(mods[bsz], 6, axis=-1)]

        if i % 2 == 0:
            wq = _pad_heads(e_w_in[j][:, 3 * A_WIDTH:3 * A_WIDTH + B_Q_HEADS * HEAD_DIM], B_Q_HEADS, HEAD_DIM)
            k0 = 3 * A_WIDTH + B_Q_HEADS * HEAD_DIM
            wk = _pad_heads(e_w_in[j][:, k0:k0 + B_KV_HEADS * HEAD_DIM], B_KV_HEADS, HEAD_DIM)
            wv = _dup_heads(e_w_in[j][:, k0 + B_KV_HEADS * HEAD_DIM:], B_KV_HEADS)
            w_in = jnp.concatenate([e_w_in[j][:, :3 * A_WIDTH], wq, wk, wv], axis=1).astype(BF16)
            qg, kg = _lane_pad(e_qk_g[j, 0]), _lane_pad(e_qk_g[j, 1])
            filt = (e_hy_w1[j], e_hy_b1[j], e_hy_w2[j], e_hy_b2[j], e_hy_w3[j], e_hy_freq[j])
            hy_l, q_l, k_l, v_l = _even_in(x, norm_g[i, 0], lat[0], lat[1], w_in, qg, kg,
                                           _rope_tables(n, HEAD_DIM // 4, 0, False))
            hy_c, q_c, k_c, v_c = _even_in(xc, norm_g[i, 0], cxm[0], cxm[1], w_in, qg, kg,
                                           _rope_tables(ncx, HEAD_DIM // 4, 0, True))
            scale = HEAD_DIM ** -0.5
            rep = B_Q_HEADS // B_KV_HEADS
            att_l = _attention(q_l, [k_c, k_l], [v_c, v_l], rep, 1, scale)
            mix_l = (_hyena(hy_l, e_hy_conv_w[j], e_hy_conv_b[j], filt, e_hy_fbias[j]), att_l)
            if not last:
                att_c = _attention(q_c, [k_c], [v_c], rep, 1, scale)
                mix_c = (_hyena(hy_c, e_hy_conv_w[j], e_hy_conv_b[j], filt, e_hy_fbias[j]), att_c)
            w_out = e_w_out[j]
            split = A_WIDTH
        else:
            w1 = o_w_in[j]
            c0, c1, c2 = MLA_Q_RANK, MLA_Q_RANK + MLA_KV_RANK, MLA_Q_RANK + MLA_KV_RANK + MLA_ROPE
            w_kr = jnp.pad(w1[:, c1:c2], ((0, 0), (MLA_NOPE, LANES - MLA_NOPE - MLA_ROPE)))
            w_in = jnp.concatenate([w1[:, :c1], w_kr, w1[:, c2:]], axis=1).astype(BF16)
            wuq = _pad_heads(o_w_uq[j], MLA_HEADS, MLA_NOPE + MLA_ROPE).astype(BF16)
            wkv = o_w_ukv[j].reshape(MLA_KV_RANK, MLA_HEADS, MLA_NOPE + MLA_V)
            wuk = _pad_heads(wkv[:, :, :MLA_NOPE].reshape(MLA_KV_RANK, -1), MLA_HEADS, MLA_NOPE).astype(BF16)
            wuv = _dup_heads(wkv[:, :, MLA_NOPE:].reshape(MLA_KV_RANK, -1), MLA_HEADS).astype(BF16)
            qg = o_q_norm_g[j].reshape(1, MLA_Q_RANK)
            kvg = o_kv_norm_g[j].reshape(1, MLA_KV_RANK)
            args = (w_in, qg, kvg, wuq, wuk, wuv)
            u_l, k_l, v_l, q_l = _odd_in(x, norm_g[i, 0], lat[0], lat[1], *args,
                                         _rope_tables(n, MLA_ROPE // 4, MLA_NOPE, False), True)
            outs_c = _odd_in(xc, norm_g[i, 0], cxm[0], cxm[1], *args,
                             _rope_tables(ncx, MLA_ROPE // 4, MLA_NOPE, True), not last)
            u_c, k_c, v_c = outs_c[:3]
            scale = (MLA_NOPE + MLA_ROPE) ** -0.5
            att_l = _attention(q_l, [k_c, k_l], [v_c, v_l], 2, 2, scale)
            mats = _s5_matrices(o_s5_a_re[j], o_s5_a_im[j], o_s5_log_dt[j], o_s5_b_re[j], o_s5_b_im[j],
                                o_s5_c_re[j], o_s5_c_im[j], o_s5_d[j])
            glu_w = o_glu_w[j].astype(BF16)
            mix_l = (att_l, _s5_mixer(u_c, u_l, mats, glu_w, o_glu_b[j]))
            if not last:
                raise NotImplementedError("an odd layer that is not the last needs the context S5 readout")
            w_out = o_w_out[j]
            split = MLA_HEADS * MLA_V

        wa, wb = w_out[:split].astype(BF16), w_out[split:].astype(BF16)
        router = jnp.pad(jnp.concatenate([moe_w_rg[i], moe_w_re[i]], axis=1),
                         ((0, 0), (0, LANES - N_GROUPS - N_EXPERTS)))
        rh = router.astype(BF16)
        rl = (router - rh.astype(F32)).astype(BF16)
        wg, wu, wd = (moe_w_gate[i].astype(BF16), moe_w_up[i].astype(BF16), moe_w_down[i].astype(BF16))

        x, h_l, route_l = _proj_res(*mix_l, wa, wb, x, lat[2], norm_g[i, 1], lat[3], lat[4], rh, rl)
        streams = [(x, h_l, route_l, lat[5])]
        if not last:
            xc, h_c, route_c = _proj_res(*mix_c, wa, wb, xc, cxm[2], norm_g[i, 1], cxm[3], cxm[4], rh, rl)
            streams.append((xc, h_c, route_c, cxm[5]))
        outs = _moe(streams, wg, wu, wd, final_g, last)
        x = outs[0]
        if not last:
            xc = outs[1]
    return x
```

```python
import functools
import math

import jax
import jax.numpy as jnp
from jax import lax
from jax.experimental import pallas as pl
from jax.experimental.pallas import tpu as pltpu

F32 = jnp.float32
BF16 = jnp.bfloat16
I32 = jnp.int32

D_MODEL = 1024
GRID_W = 64
HEAD_DIM = 64
A_WIDTH = D_MODEL // 2
B_Q_HEADS = (D_MODEL - A_WIDTH) // HEAD_DIM
B_KV_HEADS = max(1, B_Q_HEADS // 4)
HY_ORDER = 2
HY_BANDS = 16
HY_TARGET = 1e-2
HY_MAX_DECAY = math.log(1.0 / HY_TARGET) / 0.3
HY_MIN_DECAY = math.log(1.0 / HY_TARGET) / 1.5
MLA_HEADS = 8
MLA_NOPE = 64
MLA_ROPE = 32
MLA_V = 64
MLA_Q_RANK = D_MODEL // 4
MLA_KV_RANK = D_MODEL // 8
S5_WIDTH = D_MODEL - MLA_HEADS * MLA_V
S5_GROUP = 16
S5_GROUPS = S5_WIDTH // S5_GROUP
S5_STATE = 64
S5_CHUNK = 16
N_GROUPS = 4
EXPERTS_PER_GROUP = 8
N_EXPERTS = N_GROUPS * EXPERTS_PER_GROUP
D_EXPERT = D_MODEL // 4
ROPE_THETA = 10000.0
EPS = 1e-6
LOG2E = 1.4426950408889634

LANES = 128
SUBLANES = 8
MOE_TILE = 256
HIGHEST = lax.Precision.HIGHEST
MIB = 1024 * 1024


def _cparams(sem, vmem_mib):
    return pltpu.CompilerParams(dimension_semantics=sem, vmem_limit_bytes=vmem_mib * MIB)


def _dot(a, b):
    return jnp.dot(a, b, preferred_element_type=F32)


def _dot_nt(a, b):
    return lax.dot_general(a, b, (((1,), (1,)), ((), ())), preferred_element_type=F32)


def _split_bf16(a):
    hi = a.astype(BF16)
    lo = (a - hi.astype(F32)).astype(BF16)
    return hi, lo


def _dot3(ah, al, bh, bl):
    return _dot(ah, bh) + (_dot(ah, bl) + _dot(al, bh))


def _sigmoid(x):
    return 1.0 / (1.0 + jnp.exp(-x))


def _rms(x, g, width):
    ms = jnp.sum(x * x, axis=-1, keepdims=True) * (1.0 / width)
    return x * lax.rsqrt(ms + EPS) * g


def _modnorm(x, g, shift, scale):
    return _rms(x, g, x.shape[-1]) * (1.0 + scale) + shift


def _rope(t, cs, s1, s2, half):
    return t * cs + pltpu.roll(t, LANES - half, 1) * s1 + pltpu.roll(t, half, 1) * s2


def _store_token_tiles(ref, base_tok, val):
    n = val.shape[0]
    for s in range(SUBLANES):
        ref[pl.ds(base_tok * SUBLANES + s, n, stride=SUBLANES), :] = val[:, LANES * s: LANES * (s + 1)]


def _load_token_tiles(ref, base_tok, n):
    return jnp.concatenate([ref[pl.ds(base_tok * SUBLANES + s, n, stride=SUBLANES), :]
                            for s in range(SUBLANES)], axis=-1)


def _short_conv(u, w, b):
    n = u.shape[0]
    row = lax.broadcasted_iota(I32, u.shape, 0)
    prev = jnp.where(row == 0, 0.0, pltpu.roll(u, 1, 0))
    nxt = jnp.where(row == n - 1, 0.0, pltpu.roll(u, n - 1, 0))
    return prev * w[0:1] + u * w[1:2] + nxt * w[2:3] + b


def _mod_kernel(c_ref, w_ref, b_ref, o_ref):
    c = c_ref[...]
    s = c * _sigmoid(c)
    sh, sl = _split_bf16(s)
    wh, wl = _split_bf16(w_ref[...])
    o_ref[...] = _dot3(sh, sl, wh, wl) + b_ref[...]


def _modulation(cc, w, b):
    r, d = cc.shape
    n = w.shape[1]
    tn = 1536
    return pl.pallas_call(
        _mod_kernel,
        grid=(n // tn,),
        in_specs=[pl.BlockSpec((r, d), lambda j: (0, 0)),
                  pl.BlockSpec((d, tn), lambda j: (0, j)),
                  pl.BlockSpec((1, tn), lambda j: (0, j))],
        out_specs=pl.BlockSpec((r, tn), lambda j: (0, j)),
        out_shape=jax.ShapeDtypeStruct((r, n), F32),
        compiler_params=_cparams(("parallel",), 48),
        name="modulation",
    )(cc, w, b.reshape(1, n))


def _with_ones(v):
    lane = lax.broadcasted_iota(I32, v.shape, 1) % LANES
    return jnp.where(lane >= HEAD_DIM, 1.0, v)


def _even_in_kernel(x_ref, g_ref, sh_ref, sc_ref, w_ref, qg_ref, kg_ref, cs_ref, s1_ref, s2_ref,
                    hy_ref, q_ref, k_ref, v_ref, *, qscale):
    xn = _modnorm(x_ref[0], g_ref[...], sh_ref[0], sc_ref[0]).astype(BF16)
    p = _dot(xn, w_ref[...])
    hw = 3 * A_WIDTH
    hy_ref[0] = p[:, :hw].astype(BF16)
    cs, s1, s2 = cs_ref[...], s1_ref[...], s2_ref[...]
    for h in range(B_Q_HEADS):
        t = _rms(p[:, hw + LANES * h: hw + LANES * (h + 1)], qg_ref[...], HEAD_DIM)
        q_ref[0, :, LANES * h: LANES * (h + 1)] = (_rope(t, cs, s1, s2, HEAD_DIM // 4) * qscale).astype(BF16)
    kw = hw + LANES * B_Q_HEADS
    for h in range(B_KV_HEADS):
        t = _rms(p[:, kw + LANES * h: kw + LANES * (h + 1)], kg_ref[...], HEAD_DIM)
        k_ref[0, :, LANES * h: LANES * (h + 1)] = _rope(t, cs, s1, s2, HEAD_DIM // 4).astype(BF16)
    vw = kw + LANES * B_KV_HEADS
    v_ref[0] = _with_ones(p[:, vw:]).astype(BF16)


def _even_in(x, g, shift, scale, w, qg, kg, tabs, qscale):
    bsz, n, d = x.shape
    tm = min(256, n)
    nw = w.shape[1]
    row = lambda b, t: (b, t, 0)
    per_b = lambda b, t: (b, 0, 0)
    fixed = lambda b, t: (0, 0)
    tab = pl.BlockSpec((tm, LANES), lambda b, t: (t, 0))
    return pl.pallas_call(
        functools.partial(_even_in_kernel, qscale=qscale),
        grid=(bsz, n // tm),
        in_specs=[pl.BlockSpec((1, tm, d), row), pl.BlockSpec((1, d), fixed),
                  pl.BlockSpec((1, 1, d), per_b), pl.BlockSpec((1, 1, d), per_b),
                  pl.BlockSpec((d, nw), fixed), pl.BlockSpec((1, LANES), fixed),
                  pl.BlockSpec((1, LANES), fixed), tab, tab, tab],
        out_specs=[pl.BlockSpec((1, tm, 3 * A_WIDTH), row),
                   pl.BlockSpec((1, tm, LANES * B_Q_HEADS), row),
                   pl.BlockSpec((1, tm, LANES * B_KV_HEADS), row),
                   pl.BlockSpec((1, tm, LANES * B_KV_HEADS), row)],
        out_shape=[jax.ShapeDtypeStruct((bsz, n, 3 * A_WIDTH), BF16),
                   jax.ShapeDtypeStruct((bsz, n, LANES * B_Q_HEADS), BF16),
                   jax.ShapeDtypeStruct((bsz, n, LANES * B_KV_HEADS), BF16),
                   jax.ShapeDtypeStruct((bsz, n, LANES * B_KV_HEADS), BF16)],
        compiler_params=_cparams(("parallel", "parallel"), 48),
        name="even_in",
    )(x, g.reshape(1, d), shift, scale, w, qg, kg, *tabs)


def _odd_in_kernel(x_ref, g_ref, sh_ref, sc_ref, w_ref, qg_ref, kvg_ref, wuq_ref, wuk_ref, wuv_ref,
                   cs_ref, s1_ref, s2_ref, *out_refs, need_q, qscale):
    if need_q:
        u_ref, k_ref, v_ref, q_ref = out_refs
    else:
        u_ref, k_ref, v_ref = out_refs
    xn = _modnorm(x_ref[0], g_ref[...], sh_ref[0], sc_ref[0]).astype(BF16)
    p = _dot(xn, w_ref[...])
    c0, c1, c2 = MLA_Q_RANK, MLA_Q_RANK + MLA_KV_RANK, MLA_Q_RANK + MLA_KV_RANK + LANES
    u_ref[0] = p[:, c2:].astype(BF16)
    cs, s1, s2 = cs_ref[...], s1_ref[...], s2_ref[...]
    ckv = _rms(p[:, c0:c1], kvg_ref[...], MLA_KV_RANK).astype(BF16)
    kn = _dot(ckv, wuk_ref[...])
    v_ref[0] = _with_ones(_dot(ckv, wuv_ref[...])).astype(BF16)
    kr = _rope(p[:, c1:c2], cs, s1, s2, MLA_ROPE // 4)
    for h in range(MLA_HEADS):
        k_ref[0, :, LANES * h: LANES * (h + 1)] = (kn[:, LANES * h: LANES * (h + 1)] + kr).astype(BF16)
    if need_q:
        cq = _rms(p[:, :c0], qg_ref[...], MLA_Q_RANK).astype(BF16)
        q = _dot(cq, wuq_ref[...])
        for h in range(MLA_HEADS):
            t = q[:, LANES * h: LANES * (h + 1)]
            q_ref[0, :, LANES * h: LANES * (h + 1)] = (_rope(t, cs, s1, s2, MLA_ROPE // 4) * qscale).astype(BF16)


def _odd_in(x, g, shift, scale, w, qg, kvg, wuq, wuk, wuv, tabs, need_q, qscale):
    bsz, n, d = x.shape
    tm = min(256, n)
    hw = LANES * MLA_HEADS
    row = lambda b, t: (b, t, 0)
    per_b = lambda b, t: (b, 0, 0)
    fixed = lambda b, t: (0, 0)
    tab = pl.BlockSpec((tm, LANES), lambda b, t: (t, 0))
    widths = [S5_WIDTH, hw, hw] + ([hw] if need_q else [])
    return pl.pallas_call(
        functools.partial(_odd_in_kernel, need_q=need_q, qscale=qscale),
        grid=(bsz, n // tm),
        in_specs=[pl.BlockSpec((1, tm, d), row), pl.BlockSpec((1, d), fixed),
                  pl.BlockSpec((1, 1, d), per_b), pl.BlockSpec((1, 1, d), per_b),
                  pl.BlockSpec(w.shape, fixed), pl.BlockSpec((1, MLA_Q_RANK), fixed),
                  pl.BlockSpec((1, MLA_KV_RANK), fixed), pl.BlockSpec(wuq.shape, fixed),
                  pl.BlockSpec(wuk.shape, fixed), pl.BlockSpec(wuv.shape, fixed), tab, tab, tab],
        out_specs=[pl.BlockSpec((1, tm, wd), row) for wd in widths],
        out_shape=[jax.ShapeDtypeStruct((bsz, n, wd), BF16) for wd in widths],
        compiler_params=_cparams(("parallel", "parallel"), 48),
        name="odd_in_q" if need_q else "odd_in",
    )(x, g.reshape(1, d), shift, scale, w, qg, kvg, wuq, wuk, wuv, *tabs)


def _hy_fwd_kernel(wc_ref, ws_ref, u_ref, cw_ref, cb_ref, ka_ref, kb_ref, ya_ref, yb_ref, ubf_ref,
                   *, conv, tf):
    f = pl.program_id(1)

    @pl.when(f == 0)
    def _():
        u = u_ref[0].astype(F32)
        if conv:
            u = _short_conv(u, cw_ref[...], cb_ref[...])
        ubf_ref[...] = u.astype(BF16)

    ub = ubf_ref[...]
    a = _dot(wc_ref[...], ub)
    b = _dot(ws_ref[...], ub)
    ka, kb = ka_ref[...], kb_ref[...]
    first = (lax.broadcasted_iota(I32, a.shape, 0) + f * tf) == 0
    ya_ref[0] = jnp.where(first, a * ka, a * ka - b * kb).astype(BF16)
    yb_ref[0] = jnp.where(first, b * kb, a * kb + b * ka).astype(BF16)


def _hy_fwd(wc, ws, u, col, cw, cb, ka, kb, conv):
    bsz, n, _ = u.shape
    c = A_WIDTH
    tf = min(512, n)
    return pl.pallas_call(
        functools.partial(_hy_fwd_kernel, conv=conv, tf=tf),
        grid=(bsz, n // tf),
        in_specs=[pl.BlockSpec((tf, n), lambda b, f: (f, 0)), pl.BlockSpec((tf, n), lambda b, f: (f, 0)),
                  pl.BlockSpec((1, n, c), lambda b, f: (b, 0, col)),
                  pl.BlockSpec((3, c), lambda b, f: (0, 0)), pl.BlockSpec((1, c), lambda b, f: (0, 0)),
                  pl.BlockSpec((tf, c), lambda b, f: (f, 0)), pl.BlockSpec((tf, c), lambda b, f: (f, 0))],
        out_specs=[pl.BlockSpec((1, tf, c), lambda b, f: (b, f, 0))] * 2,
        out_shape=[jax.ShapeDtypeStruct((bsz, n, c), BF16)] * 2,
        scratch_shapes=[pltpu.VMEM((n, c), BF16)],
        compiler_params=_cparams(("parallel", "arbitrary"), 48),
        name="hyena_fwd",
    )(wc, ws, u, cw, cb, ka, kb)


def _hy_inv_kernel(ci_ref, si_ref, ya_ref, yb_ref, a_ref, g_ref, cwa_ref, cba_ref, cwg_ref, cbg_ref,
                   bias_ref, o_ref, a_sc, g_sc, *, conv_a, tt):
    t = pl.program_id(1)

    @pl.when(t == 0)
    def _():
        a = a_ref[0].astype(F32)
        if conv_a:
            a = _short_conv(a, cwa_ref[...], cba_ref[...])
        a_sc[...] = a
        g_sc[...] = _short_conv(g_ref[0].astype(F32), cwg_ref[...], cbg_ref[...])

    y = _dot(ci_ref[...], ya_ref[0]) + _dot(si_ref[...], yb_ref[0])
    rows = pl.ds(pl.multiple_of(t * tt, tt), tt)
    y = y + a_sc[rows, :] * bias_ref[...]
    o_ref[0] = (g_sc[rows, :] * y).astype(BF16)


def _hy_inv(ci, si, ya, yb, a, a_col, g, g_col, cwa, cba, cwg, cbg, bias, conv_a):
    bsz, n, c = ya.shape
    tt = min(256, n)
    whole = lambda b, t: (b, 0, 0)
    fixed = lambda b, t: (0, 0)
    return pl.pallas_call(
        functools.partial(_hy_inv_kernel, conv_a=conv_a, tt=tt),
        grid=(bsz, n // tt),
        in_specs=[pl.BlockSpec((tt, n), lambda b, t: (t, 0)), pl.BlockSpec((tt, n), lambda b, t: (t, 0)),
                  pl.BlockSpec((1, n, c), whole), pl.BlockSpec((1, n, c), whole),
                  pl.BlockSpec((1, n, c), lambda b, t: (b, 0, a_col)),
                  pl.BlockSpec((1, n, c), lambda b, t: (b, 0, g_col)),
                  pl.BlockSpec((3, c), fixed), pl.BlockSpec((1, c), fixed),
                  pl.BlockSpec((3, c), fixed), pl.BlockSpec((1, c), fixed), pl.BlockSpec((1, c), fixed)],
        out_specs=pl.BlockSpec((1, tt, c), lambda b, t: (b, t, 0)),
        out_shape=jax.ShapeDtypeStruct((bsz, n, c), BF16),
        scratch_shapes=[pltpu.VMEM((n, c), F32), pltpu.VMEM((n, c), F32)],
        compiler_params=_cparams(("parallel", "arbitrary"), 56),
        name="hyena_inv",
    )(ci, si, ya, yb, a, g, cwa, cba, cwg, cbg, bias)


def _dft_mats(n):
    big = 2 * n
    f = jnp.arange(n, dtype=I32)[:, None]
    s = jnp.arange(n, dtype=I32)[None, :]
    ang = ((f * s) % big).astype(F32) * (2.0 * math.pi / big)
    cos, sin = jnp.cos(ang), jnp.sin(ang)
    alt = jnp.where(jnp.arange(n) % 2 == 0, 1.0, -1.0).astype(F32)
    wc = cos
    ws = jnp.where(f == 0, alt[None, :], -sin)
    ci = jnp.where(s == 0, 1.0 / big, (2.0 / big) * cos)
    si = jnp.where(s == 0, alt[:, None] / big, -(2.0 / big) * sin)
    return wc.astype(BF16), ws.astype(BF16), ci.astype(BF16), si.astype(BF16)


def _hyena_filter_spectrum(n, w1, b1, w2, b2, w3, freq):
    t = jnp.arange(n, dtype=F32)
    t_norm = t / max(n - 1, 1)
    bands = jnp.linspace(1e-4, HY_BANDS - 1, HY_BANDS, dtype=F32)
    ang = (2.0 * math.pi / n) * t[:, None] * bands
    z = jnp.concatenate([t_norm[:, None], jnp.cos(ang), jnp.sin(ang)], axis=-1)
    hdn = jnp.sin(freq * (jnp.dot(z, w1, precision=HIGHEST) + b1))
    hdn = jnp.sin(freq * (jnp.dot(hdn, w2, precision=HIGHEST) + b2))
    taps = jnp.dot(hdn, w3, precision=HIGHEST).reshape(n, 2, HY_ORDER, A_WIDTH)
    deltas = jnp.linspace(HY_MIN_DECAY, HY_MAX_DECAY, A_WIDTH, dtype=F32)
    taps = taps * jnp.exp(-t_norm[:, None] * deltas)[:, None, None, :]
    fwd, bwd = taps[:, 0], taps[:, 1]
    circ = jnp.concatenate([fwd, jnp.zeros((1, HY_ORDER, A_WIDTH), F32), bwd[:0:-1]], axis=0)
    circ = circ * lax.rsqrt(jnp.sum(circ * circ, axis=0, keepdims=True) + EPS)
    kf = jnp.fft.rfft(circ, axis=0)
    ka = jnp.real(kf[:n]).astype(F32)
    kb = jnp.concatenate([jnp.real(kf[n:n + 1]), jnp.imag(kf[1:n])], axis=0).astype(F32)
    return ka, kb


def _hyena(hy, cw, cb, filt, fbias):
    n = hy.shape[1]
    c = A_WIDTH
    wc, ws, ci, si = _dft_mats(n)
    ka, kb = _hyena_filter_spectrum(n, *filt)
    cws = [cw[:, i * c:(i + 1) * c] for i in range(3)]
    cbs = [cb[i * c:(i + 1) * c].reshape(1, c) for i in range(3)]
    ya, yb = _hy_fwd(wc, ws, hy, 2, cws[2], cbs[2], ka[:, 0], kb[:, 0], True)
    z = _hy_inv(ci, si, ya, yb, hy, 2, hy, 0, cws[2], cbs[2], cws[0], cbs[0], fbias[0:1], True)
    ya, yb = _hy_fwd(wc, ws, z, 0, cws[2], cbs[2], ka[:, 1], kb[:, 1], False)
    return _hy_inv(ci, si, ya, yb, z, 0, hy, 1, cws[2], cbs[2], cws[1], cbs[1], fbias[1:2], False)


def _attn_kernel(*refs, nseg, hps, kps):
    q_ref = refs[0]
    k_refs = refs[1:1 + nseg]
    v_refs = refs[1 + nseg:1 + 2 * nseg]
    o_ref = refs[1 + 2 * nseg]
    low = lax.broadcasted_iota(I32, (q_ref.shape[1], LANES), 1) < HEAD_DIM
    for pair in range(hps // 2):
        accs = []
        for half in range(2):
            i = 2 * pair + half
            kv = (i * kps) // hps
            q = q_ref[0, :, LANES * i: LANES * (i + 1)]
            ss = [_dot_nt(q, k[0, :, LANES * kv: LANES * (kv + 1)]) for k in k_refs]
            m = ss[0].max(axis=-1, keepdims=True)
            for s in ss[1:]:
                m = jnp.maximum(m, s.max(axis=-1, keepdims=True))
            acc = None
            for s, v in zip(ss, v_refs):
                p = jnp.exp2((s - m).astype(BF16))
                o = _dot(p, v[0, :, LANES * kv: LANES * (kv + 1)])
                acc = o if acc is None else acc + o
            accs.append(acc)
        num = jnp.where(low, accs[0], pltpu.roll(accs[1], HEAD_DIM, 1))
        den = jnp.where(low, pltpu.roll(accs[0], HEAD_DIM, 1), accs[1])
        o_ref[0, :, LANES * pair: LANES * (pair + 1)] = (num / den).astype(BF16)


def _attention(q, ks, vs, hps, kps):
    bsz, lq, qw = q.shape
    nh = qw // LANES
    tq = min(256, lq)
    nseg = len(ks)
    kv_spec = lambda a: pl.BlockSpec((1, a.shape[1], kps * LANES), lambda b, g, t: (b, 0, g))
    return pl.pallas_call(
        functools.partial(_attn_kernel, nseg=nseg, hps=hps, kps=kps),
        grid=(bsz, nh // hps, lq // tq),
        in_specs=[pl.BlockSpec((1, tq, hps * LANES), lambda b, g, t: (b, t, g))]
        + [kv_spec(a) for a in ks] + [kv_spec(a) for a in vs],
        out_specs=pl.BlockSpec((1, tq, hps * HEAD_DIM), lambda b, g, t: (b, t, g)),
        out_shape=jax.ShapeDtypeStruct((bsz, lq, nh * HEAD_DIM), BF16),
        compiler_params=_cparams(("parallel", "parallel", "parallel"), 56),
        name="attention",
    )(q, *ks, *vs)


def _s5_kernel(u_ref, w_ref, a_ref, of_ref, or_ref, y_ref, yi_sc, sf_sc, sr_sc, xf_sc, xr_sc,
               *, nc, ncc, bsz):
    cw = S5_CHUNK * S5_GROUP
    sw = 2 * S5_STATE
    r = _dot(u_ref[0], w_ref[0])
    yi_sc[...] = r[:, :cw]
    sf = r[:, cw:cw + sw]
    sr = r[:, cw + sw:]
    sf_sc[0] = sf
    sr_sc[0] = sr
    sf_sc[1] = pltpu.roll(sf, S5_STATE, 1)
    sr_sc[1] = pltpu.roll(sr, S5_STATE, 1)
    a = a_ref[0]

    def advance(st, s_sc, rws, d):
        x, xs = st
        a0, a1 = a[2 * d:2 * d + 1], a[2 * d + 1:2 * d + 2]
        return (x * a0 + xs * a1 + s_sc[0, rws, :], xs * a0 - x * a1 + s_sc[1, rws, :])

    def rows(c):
        return pl.ds(pl.multiple_of(c * bsz, bsz), bsz)

    zero = (jnp.zeros((bsz, sw), F32), jnp.zeros((bsz, sw), F32))

    def fstep(c, st):
        xf_sc[rows(c), :] = st[0]
        return advance(st, sf_sc, rows(c), 0)

    lax.fori_loop(0, nc, fstep, zero)

    def rstep(base):
        def step(i, st):
            c = base - i
            xr_sc[rows(c), :] = st[0]
            return advance(st, sr_sc, rows(c), 1)
        return step

    st = lax.fori_loop(0, ncc, rstep(ncc - 1), zero)
    lax.fori_loop(0, nc - ncc, rstep(nc - 1), st)

    lo = ncc * bsz
    y = yi_sc[lo:, :]
    y = y + _dot(xf_sc[lo:, :].astype(BF16), of_ref[0])
    y = y + _dot(xr_sc[lo:, :].astype(BF16), or_ref[0])
    y_ref[0] = y


def _s5_scan(uf, w, a16, of, orr, nc, ncc, bsz):
    g, rws, cw = uf.shape
    sw = 2 * S5_STATE
    out_rows = (nc - ncc) * bsz
    blk = lambda a: pl.BlockSpec((1,) + a.shape[1:], lambda i: (i, 0, 0))
    return pl.pallas_call(
        functools.partial(_s5_kernel, nc=nc, ncc=ncc, bsz=bsz),
        grid=(g,),
        in_specs=[blk(uf), blk(w), blk(a16), blk(of), blk(orr)],
        out_specs=pl.BlockSpec((1, out_rows, cw), lambda i: (i, 0, 0)),
        out_shape=jax.ShapeDtypeStruct((g, out_rows, cw), F32),
        scratch_shapes=[pltpu.VMEM((rws, cw), F32)] + [pltpu.VMEM((2, rws, sw), F32)] * 2
        + [pltpu.VMEM((rws, sw), F32)] * 2,
        compiler_params=_cparams(("parallel",), 48),
        name="s5_scan",
    )(uf, w, a16, of, orr)


def _s5_matrices(a_re, a_im, log_dt, b_re, b_im, c_re, c_im, d_skip):
    lc, h, p, g = S5_CHUNK, S5_GROUP, S5_STATE, S5_GROUPS
    dt = jnp.exp(log_dt)[..., None]
    lam_re, lam_im = a_re * dt, a_im * dt
    mag = jnp.exp(lam_re)
    er, ei = mag * jnp.cos(lam_im) - 1.0, mag * jnp.sin(lam_im)
    den = a_re * a_re + a_im * a_im
    co_re = (er * a_re + ei * a_im) / den
    co_im = (ei * a_re - er * a_im) / den
    bb_re = co_re[..., None] * b_re - co_im[..., None] * b_im
    bb_im = co_re[..., None] * b_im + co_im[..., None] * b_re
    m = jnp.arange(lc + 1, dtype=F32)[:, None, None, None]
    pw_mag = jnp.exp(m * lam_re[None])
    pw_re, pw_im = pw_mag * jnp.cos(m * lam_im[None]), pw_mag * jnp.sin(m * lam_im[None])
    ab_re = pw_re[..., None] * bb_re[None] - pw_im[..., None] * bb_im[None]
    ab_im = pw_re[..., None] * bb_im[None] + pw_im[..., None] * bb_re[None]
    kk = (jnp.einsum("dgop,mdgpi->mdgoi", c_re, ab_re[:lc], precision=HIGHEST)
          - jnp.einsum("dgop,mdgpi->mdgoi", c_im, ab_im[:lc], precision=HIGHEST))
    j = jnp.arange(lc)[:, None]
    k = jnp.arange(lc)[None, :]
    lag_f = jnp.clip(k - j, 0, lc - 1)
    lag_r = jnp.clip(j - k, 0, lc - 1)
    tf = jnp.where((k >= j)[:, :, None, None, None], kk[lag_f, 0], 0.0)
    tr = jnp.where((j >= k)[:, :, None, None, None], kk[lag_r, 1], 0.0)
    tw = jnp.transpose(tf + tr, (2, 0, 4, 1, 3))
    dsk = d_skip.reshape(g, h)
    eye = (jnp.eye(lc, dtype=F32)[None, :, None, :, None] * jnp.eye(h, dtype=F32)[None, None, :, None, :]
           * dsk[:, None, :, None, None])
    tw = (tw + eye).reshape(g, lc * h, lc * h)
    idx_f = (lc - 1 - jnp.arange(lc))
    idx_r = jnp.arange(lc)

    def to_state(idx, d):
        re = jnp.transpose(ab_re[idx, d], (1, 0, 3, 2))
        im = jnp.transpose(ab_im[idx, d], (1, 0, 3, 2))
        return jnp.concatenate([re, im], axis=-1).reshape(g, lc * h, 2 * p)

    w = jnp.concatenate([tw, to_state(idx_f, 0), to_state(idx_r, 1)], axis=-1)

    def from_state(idx, d):
        pr, pi = pw_re[idx, d], pw_im[idx, d]
        cr, cim = c_re[d], c_im[d]
        mr = cr[None] * pr[:, :, None, :] - cim[None] * pi[:, :, None, :]
        mi = cr[None] * pi[:, :, None, :] + cim[None] * pr[:, :, None, :]
        return jnp.transpose(jnp.concatenate([mr, -mi], axis=-1), (1, 3, 0, 2)).reshape(g, 2 * p, lc * h)

    of = from_state(jnp.arange(lc) + 1, 0)
    orr = from_state(lc - jnp.arange(lc), 1)
    a16 = jnp.stack([jnp.concatenate([pw_re[lc, 0], pw_re[lc, 0]], -1),
                     jnp.concatenate([-pw_im[lc, 0], pw_im[lc, 0]], -1),
                     jnp.concatenate([pw_re[lc, 1], pw_re[lc, 1]], -1),
                     jnp.concatenate([-pw_im[lc, 1], pw_im[lc, 1]], -1)], axis=1)
    return w.astype(BF16), a16.astype(F32), of.astype(BF16), orr.astype(BF16)


def _glu_kernel(y_ref, w_ref, b_ref, o_ref):
    y = y_ref[0]
    inner = 0.7978845608028654 * (y + 0.044715 * (y * y * y))
    gl = 0.5 * y * (1.0 + jnp.tanh(inner))
    o_ref[0] = (gl * _sigmoid(_dot(gl.astype(BF16), w_ref[...]) + b_ref[...])).astype(BF16)


def _glu(y, w, b):
    bsz, n, c = y.shape
    tm = min(512, n)
    return pl.pallas_call(
        _glu_kernel,
        grid=(bsz, n // tm),
        in_specs=[pl.BlockSpec((1, tm, c), lambda i, t: (i, t, 0)),
                  pl.BlockSpec((c, c), lambda i, t: (0, 0)), pl.BlockSpec((1, c), lambda i, t: (0, 0))],
        out_specs=pl.BlockSpec((1, tm, c), lambda i, t: (i, t, 0)),
        out_shape=jax.ShapeDtypeStruct((bsz, n, c), BF16),
        compiler_params=_cparams(("parallel", "parallel"), 32),
        name="s5_glu",
    )(y, w, b.reshape(1, c))


def _s5_mixer(u_c, u_l, mats, glu_w, glu_b):
    bsz, ncx, c = u_c.shape
    nl = u_l.shape[1]
    lc, h, g = S5_CHUNK, S5_GROUP, S5_GROUPS
    ncc, ncl = ncx // lc, nl // lc
    nc = ncc + ncl
    u = jnp.concatenate([u_c, u_l], axis=1).reshape(bsz, nc, lc, g, h)
    uf = jnp.transpose(u, (3, 1, 0, 2, 4)).reshape(g, nc * bsz, lc * h)
    y = _s5_scan(uf, *mats, nc, ncc, bsz)
    y = jnp.transpose(y.reshape(g, ncl, bsz, lc, h), (2, 1, 3, 0, 4)).reshape(bsz, nl, c)
    return _glu(y, glu_w, glu_b)


def _route(lg):
    lane = lax.broadcasted_iota(I32, lg.shape, 1).astype(F32)
    neg = jnp.float32(-1e30)
    far = jnp.float32(LANES)
    is_g = lane < N_GROUPS
    lgm = jnp.where(is_g, lg, neg)
    m = lgm.max(axis=-1, keepdims=True)
    gidx = jnp.where(lgm == m, lane, far).min(axis=-1, keepdims=True)
    gp = 1.0 / jnp.where(is_g, jnp.exp(lgm - m), 0.0).sum(axis=-1, keepdims=True)
    lo = N_GROUPS + EXPERTS_PER_GROUP * gidx
    is_e = (lane >= lo) & (lane < lo + EXPERTS_PER_GROUP)
    lem = jnp.where(is_e, lg, neg)
    l0 = lem.max(axis=-1, keepdims=True)
    i0 = jnp.where((lem == l0) & is_e, lane, far).min(axis=-1, keepdims=True)
    lem1 = jnp.where(lane == i0, neg, lem)
    l1 = lem1.max(axis=-1, keepdims=True)
    i1 = jnp.where((lem1 == l1) & is_e & (lane != i0), lane, far).min(axis=-1, keepdims=True)
    e1 = jnp.exp(l1 - l0)
    w0 = 1.0 / (1.0 + e1)
    w1 = e1 * w0
    out = jnp.where(lane == 0, i0 - N_GROUPS, 0.0)
    out = jnp.where(lane == 1, i1 - N_GROUPS, out)
    out = jnp.where(lane == 2, gp * w0, out)
    return jnp.where(lane == 3, gp * w1, out)


def _proj_res_kernel(a_ref, b_ref, wa_ref, wb_ref, x_ref, g1_ref, ng_ref, sh_ref, sc_ref, rh_ref, rl_ref,
                     xo_ref, h_ref, route_ref):
    o = _dot(a_ref[0], wa_ref[...]) + _dot(b_ref[0], wb_ref[...])
    x = x_ref[0] + g1_ref[0] * o
    xo_ref[0] = x
    h = _modnorm(x, ng_ref[...], sh_ref[0], sc_ref[0])
    _store_token_tiles(h_ref, 0, h)
    hh, hl = _split_bf16(h)
    route_ref[0] = _route(_dot3(hh, hl, rh_ref[...], rl_ref[...]))


def _proj_res(a, b, wa, wb, x, g1, ng, shift, scale, rh, rl):
    bsz, n, d = x.shape
    tm = min(256, n)
    row = lambda i, t: (i, t, 0)
    per_b = lambda i, t: (i, 0, 0)
    fixed = lambda i, t: (0, 0)
    return pl.pallas_call(
        _proj_res_kernel,
        grid=(bsz, n // tm),
        in_specs=[pl.BlockSpec((1, tm, a.shape[2]), row), pl.BlockSpec((1, tm, b.shape[2]), row),
                  pl.BlockSpec(wa.shape, fixed), pl.BlockSpec(wb.shape, fixed),
                  pl.BlockSpec((1, tm, d), row), pl.BlockSpec((1, 1, d), per_b), pl.BlockSpec((1, d), fixed),
                  pl.BlockSpec((1, 1, d), per_b), pl.BlockSpec((1, 1, d), per_b),
                  pl.BlockSpec((d, LANES), fixed), pl.BlockSpec((d, LANES), fixed)],
        out_specs=[pl.BlockSpec((1, tm, d), row),
                   pl.BlockSpec((tm * SUBLANES, LANES), lambda i, t: (i * (n // tm) + t, 0)),
                   pl.BlockSpec((1, tm, LANES), row)],
        out_shape=[jax.ShapeDtypeStruct((bsz, n, d), F32),
                   jax.ShapeDtypeStruct((bsz * n * SUBLANES, LANES), F32),
                   jax.ShapeDtypeStruct((bsz, n, LANES), F32)],
        compiler_params=_cparams(("parallel", "parallel"), 40),
        name="proj_res",
    )(a, b, wa, wb, x, g1, ng.reshape(1, d), shift, scale, rh, rl)


def _token_copy(src, src_tok, dst, dst_tok, sem):
    rows = lambda t: pl.ds(pl.multiple_of(t * SUBLANES, SUBLANES), SUBLANES)
    return pltpu.make_async_copy(src.at[rows(src_tok)], dst.at[rows(dst_tok)], sem)


def _experts_kernel(te_ref, tr_ref, src_ref, srcn_ref, dst_ref, h_ref, wg_ref, wu_ref, wd_ref, ys_ref,
                    xbuf, obuf, gsem, ssem):
    del te_ref
    i = pl.program_id(0)
    nt = pl.num_programs(0)
    slot = i % 2
    other = 1 - slot
    nxt = jnp.minimum(i + 1, nt - 1)
    next_used = jnp.logical_and(i + 1 < nt, tr_ref[nxt] > 0)

    def start_gather(idx_ref, s):
        def body(r, c):
            _token_copy(h_ref, idx_ref[0, 0, r], xbuf, s * MOE_TILE + r, gsem.at[s]).start()
            return c
        lax.fori_loop(0, MOE_TILE, body, 0, unroll=8)

    def wait_tokens(src, dst, sem, count):
        def body(r, c):
            _token_copy(src, 0, dst, 0, sem).wait()
            return c
        lax.fori_loop(0, count, body, 0)

    @pl.when(tr_ref[i] > 0)
    def _():
        @pl.when(i == 0)
        def _():
            start_gather(src_ref, 0)

        @pl.when(next_used)
        def _():
            start_gather(srcn_ref, other)

        wait_tokens(h_ref, xbuf, gsem.at[slot], MOE_TILE)

        @pl.when(i >= 2)
        def _():
            wait_tokens(obuf, ys_ref, ssem.at[slot], tr_ref[jnp.maximum(i - 2, 0)])

        base = slot * MOE_TILE
        x = _load_token_tiles(xbuf, base, MOE_TILE).astype(BF16)
        gt = _dot(x, wg_ref[0])
        hid = (gt * _sigmoid(gt)) * _dot(x, wu_ref[0])
        _store_token_tiles(obuf, base, _dot(hid.astype(BF16), wd_ref[0]))

        def scatter(r, c):
            _token_copy(obuf, base + r, ys_ref, dst_ref[0, 0, r], ssem.at[slot]).start()
            return c
        lax.fori_loop(0, tr_ref[i], scatter, 0)

        @pl.when(jnp.logical_not(next_used))
        def _():
            wait_tokens(obuf, ys_ref, ssem.at[slot], tr_ref[i])

            @pl.when(i >= 1)
            def _():
                wait_tokens(obuf, ys_ref, ssem.at[other], tr_ref[jnp.maximum(i - 1, 0)])


def _experts(tile_expert, tile_real, src_tok, dst_row, h, n_assign, wg, wu, wd):
    nt = tile_expert.shape[0]
    d = D_MODEL
    wmap = lambda i, te, tv: (te[i], 0, 0)
    idx = lambda a: a.reshape(nt, 1, MOE_TILE)
    smem = lambda f: pl.BlockSpec((1, 1, MOE_TILE), f, memory_space=pltpu.SMEM)
    tile_rows = MOE_TILE * SUBLANES
    return pl.pallas_call(
        _experts_kernel,
        grid_spec=pltpu.PrefetchScalarGridSpec(
            num_scalar_prefetch=2, grid=(nt,),
            in_specs=[smem(lambda i, te, tv: (i, 0, 0)),
                      smem(lambda i, te, tv: (jnp.minimum(i + 1, nt - 1), 0, 0)),
                      smem(lambda i, te, tv: (i, 0, 0)),
                      pl.BlockSpec(memory_space=pl.ANY),
                      pl.BlockSpec((1, d, D_EXPERT), wmap), pl.BlockSpec((1, d, D_EXPERT), wmap),
                      pl.BlockSpec((1, D_EXPERT, d), wmap)],
            out_specs=pl.BlockSpec(memory_space=pl.ANY),
            scratch_shapes=[pltpu.VMEM((2 * tile_rows, LANES), F32), pltpu.VMEM((2 * tile_rows, LANES), F32),
                            pltpu.SemaphoreType.DMA((2,)), pltpu.SemaphoreType.DMA((2,))]),
        out_shape=jax.ShapeDtypeStruct((n_assign * SUBLANES, LANES), F32),
        compiler_params=pltpu.CompilerParams(dimension_semantics=("arbitrary",), vmem_limit_bytes=40 * MIB,
                                             has_side_effects=True),
        name="moe_experts",
    )(tile_expert, tile_real, idx(src_tok), idx(src_tok), idx(dst_row), h, wg, wu, wd)


def _combine_kernel(route_ref, x_ref, g2_ref, fg_ref, ys_ref, o_ref, *, tc, final):
    rt = route_ref[0]
    g0, g1 = rt[:, 2:3], rt[:, 3:4]
    cols = [g0 * ys_ref[pl.ds(s, tc, stride=2 * SUBLANES), :]
            + g1 * ys_ref[pl.ds(SUBLANES + s, tc, stride=2 * SUBLANES), :] for s in range(SUBLANES)]
    x = x_ref[0] + g2_ref[0] * jnp.concatenate(cols, axis=-1)
    if final:
        x = _rms(x, fg_ref[...], x.shape[-1])
    o_ref[0] = x


def _combine(route, x, g2, fg, ys, tok_off, final):
    bsz, n, d = x.shape
    tc = min(256, n)
    nt = n // tc
    row = lambda i, t: (i, t, 0)
    blk = tok_off // tc
    return pl.pallas_call(
        functools.partial(_combine_kernel, tc=tc, final=final),
        grid=(bsz, nt),
        in_specs=[pl.BlockSpec((1, tc, LANES), row), pl.BlockSpec((1, tc, d), row),
                  pl.BlockSpec((1, 1, d), lambda i, t: (i, 0, 0)), pl.BlockSpec((1, d), lambda i, t: (0, 0)),
                  pl.BlockSpec((2 * tc * SUBLANES, LANES), lambda i, t: (blk + i * nt + t, 0))],
        out_specs=pl.BlockSpec((1, tc, d), row),
        out_shape=jax.ShapeDtypeStruct((bsz, n, d), F32),
        compiler_params=_cparams(("parallel", "parallel"), 32),
        name="moe_combine",
    )(route, x, g2, fg.reshape(1, d), ys)


def _moe_plan(eids):
    a = eids.shape[0]
    order = jnp.argsort(eids, stable=True).astype(I32)
    counts = (eids[:, None] == jnp.arange(N_EXPERTS, dtype=I32)[None, :]).astype(I32).sum(axis=0)
    padded = ((counts + MOE_TILE - 1) // MOE_TILE) * MOE_TILE
    ends = jnp.cumsum(padded)
    first = jnp.cumsum(counts) - counts
    nt = a // MOE_TILE + N_EXPERTS
    starts = jnp.arange(nt, dtype=I32) * MOE_TILE
    tile_expert = jnp.minimum((starts[:, None] >= ends[None, :]).astype(I32).sum(axis=1), N_EXPERTS - 1)
    tile_pos = starts - (ends - padded)[tile_expert]
    tile_real = jnp.where(starts < ends[-1], jnp.clip(counts[tile_expert] - tile_pos, 0, MOE_TILE), 0)
    slot = jnp.arange(nt * MOE_TILE, dtype=I32)
    e = jnp.repeat(tile_expert, MOE_TILE)
    pos = slot - (ends - padded)[e]
    real = (slot % MOE_TILE) < jnp.repeat(tile_real, MOE_TILE)
    assign = order[jnp.clip(first[e] + pos, 0, a - 1)]
    src_tok = jnp.where(real, assign // 2, 0)
    dst_row = jnp.where(real, assign, 0)
    return src_tok, dst_row, tile_expert.astype(I32), tile_real.astype(I32)


def _moe(streams, wg, wu, wd, final_g, final):
    eids = jnp.concatenate([s[2][..., :2].reshape(-1) for s in streams]).astype(I32)
    src_tok, dst_row, tile_expert, tile_real = _moe_plan(eids)
    h = streams[0][1] if len(streams) == 1 else jnp.concatenate([s[1] for s in streams], axis=0)
    ys = _experts(tile_expert, tile_real, src_tok, dst_row, h, eids.shape[0], wg, wu, wd)
    outs, off = [], 0
    for x, _, route, g2 in streams:
        outs.append(_combine(route, x, g2, final_g, ys, off, final))
        off += x.shape[0] * x.shape[1]
    return outs


def _rope_tables(n, half, offset, identity):
    cs = jnp.ones((n, LANES), F32)
    z = jnp.zeros((n, LANES), F32)
    if identity:
        return cs, z, z
    rows = (jnp.arange(n, dtype=I32) // GRID_W).astype(F32)
    cols = (jnp.arange(n, dtype=I32) % GRID_W).astype(F32)
    freqs = ROPE_THETA ** (-jnp.arange(half, dtype=F32) / half)
    ar, ac = rows[:, None] * freqs, cols[:, None] * freqs
    zero = jnp.zeros((n, half), F32)
    cos4 = jnp.concatenate([jnp.cos(ar), jnp.cos(ar), jnp.cos(ac), jnp.cos(ac)], axis=1)
    s1 = jnp.concatenate([-jnp.sin(ar), zero, -jnp.sin(ac), zero], axis=1)
    s2 = jnp.concatenate([zero, jnp.sin(ar), zero, jnp.sin(ac)], axis=1)
    place = lambda base, t: lax.dynamic_update_slice(base, t, (0, offset))
    return place(cs, cos4), place(z, s1), place(z, s2)


def _pad_heads(w, nheads, width, left=0):
    k = w.shape[0]
    w = w.reshape(k, nheads, width)
    w = jnp.pad(w, ((0, 0), (0, 0), (left, LANES - width - left)))
    return w.reshape(k, nheads * LANES)


def _lane_pad(v, left=0):
    return jnp.pad(v, (left, LANES - v.shape[0] - left)).reshape(1, LANES)


def kernel(x, c, ctx, c_ctx, w_mod, b_mod, norm_g, e_w_in, e_hy_conv_w, e_hy_conv_b, e_hy_w1, e_hy_b1, e_hy_w2, e_hy_b2, e_hy_w3, e_hy_freq, e_hy_fbias, e_qk_g, e_w_out, o_w_in, o_q_norm_g, o_w_uq, o_kv_norm_g, o_w_ukv, o_s5_a_re, o_s5_a_im, o_s5_log_dt, o_s5_b_re, o_s5_b_im, o_s5_c_re, o_s5_c_im, o_s5_d, o_glu_w, o_glu_b, o_w_out, moe_w_rg, moe_w_re, moe_w_gate, moe_w_up, moe_w_down, final_g):
    bsz, n, d = x.shape
    depth = w_mod.shape[0]
    xc = ctx
    ncx = ctx.shape[1]
    cc = jnp.zeros((bsz + 8, d), F32).at[:bsz].set(c).at[bsz].set(c_ctx)

    for i in range(depth):
        j = i // 2
        last = i == depth - 1
        mods = _modulation(cc, w_mod[i], b_mod[i])
        lat = [m.reshape(bsz, 1, d) for m in jnp.split(mods[:bsz], 6, axis=-1)]
        cxm = [jnp.broadcast_to(m.reshape(1, 1, d), (bsz, 1, d)) for m in jnp.split(mods[bsz], 6, axis=-1)]

        if i % 2 == 0:
            wq = _pad_heads(e_w_in[j][:, 3 * A_WIDTH:3 * A_WIDTH + B_Q_HEADS * HEAD_DIM], B_Q_HEADS, HEAD_DIM)
            k0 = 3 * A_WIDTH + B_Q_HEADS * HEAD_DIM
            wk = _pad_heads(e_w_in[j][:, k0:k0 + B_KV_HEADS * HEAD_DIM], B_KV_HEADS, HEAD_DIM)
            wv = _pad_heads(e_w_in[j][:, k0 + B_KV_HEADS * HEAD_DIM:], B_KV_HEADS, HEAD_DIM)
            w_in = jnp.concatenate([e_w_in[j][:, :3 * A_WIDTH], wq, wk, wv], axis=1).astype(BF16)
            qg, kg = _lane_pad(e_qk_g[j, 0]), _lane_pad(e_qk_g[j, 1])
            filt = (e_hy_w1[j], e_hy_b1[j], e_hy_w2[j], e_hy_b2[j], e_hy_w3[j], e_hy_freq[j])
            qscale = HEAD_DIM ** -0.5 * LOG2E
            hy_l, q_l, k_l, v_l = _even_in(x, norm_g[i, 0], lat[0], lat[1], w_in, qg, kg,
                                           _rope_tables(n, HEAD_DIM // 4, 0, False), qscale)
            hy_c, q_c, k_c, v_c = _even_in(xc, norm_g[i, 0], cxm[0], cxm[1], w_in, qg, kg,
                                           _rope_tables(ncx, HEAD_DIM // 4, 0, True), qscale)
            rep = B_Q_HEADS // B_KV_HEADS
            att_l = _attention(q_l, [k_c, k_l], [v_c, v_l], rep, 1)
            mix_l = (_hyena(hy_l, e_hy_conv_w[j], e_hy_conv_b[j], filt, e_hy_fbias[j]), att_l)
            if not last:
                att_c = _attention(q_c, [k_c], [v_c], rep, 1)
                mix_c = (_hyena(hy_c, e_hy_conv_w[j], e_hy_conv_b[j], filt, e_hy_fbias[j]), att_c)
            w_out = e_w_out[j]
            split = A_WIDTH
        else:
            w1 = o_w_in[j]
            c0, c1, c2 = MLA_Q_RANK, MLA_Q_RANK + MLA_KV_RANK, MLA_Q_RANK + MLA_KV_RANK + MLA_ROPE
            w_kr = jnp.pad(w1[:, c1:c2], ((0, 0), (MLA_NOPE, LANES - MLA_NOPE - MLA_ROPE)))
            w_in = jnp.concatenate([w1[:, :c1], w_kr, w1[:, c2:]], axis=1).astype(BF16)
            wuq = _pad_heads(o_w_uq[j], MLA_HEADS, MLA_NOPE + MLA_ROPE).astype(BF16)
            wkv = o_w_ukv[j].reshape(MLA_KV_RANK, MLA_HEADS, MLA_NOPE + MLA_V)
            wuk = _pad_heads(wkv[:, :, :MLA_NOPE].reshape(MLA_KV_RANK, -1), MLA_HEADS, MLA_NOPE).astype(BF16)
            wuv = _pad_heads(wkv[:, :, MLA_NOPE:].reshape(MLA_KV_RANK, -1), MLA_HEADS, MLA_V).astype(BF16)
            qg = o_q_norm_g[j].reshape(1, MLA_Q_RANK)
            kvg = o_kv_norm_g[j].reshape(1, MLA_KV_RANK)
            args = (w_in, qg, kvg, wuq, wuk, wuv)
            qscale = (MLA_NOPE + MLA_ROPE) ** -0.5 * LOG2E
            u_l, k_l, v_l, q_l = _odd_in(x, norm_g[i, 0], lat[0], lat[1], *args,
                                         _rope_tables(n, MLA_ROPE // 4, MLA_NOPE, False), True, qscale)
            outs_c = _odd_in(xc, norm_g[i, 0], cxm[0], cxm[1], *args,
                             _rope_tables(ncx, MLA_ROPE // 4, MLA_NOPE, True), not last, qscale)
            u_c, k_c, v_c = outs_c[:3]
            att_l = _attention(q_l, [k_c, k_l], [v_c, v_l], 2, 2)
            mats = _s5_matrices(o_s5_a_re[j], o_s5_a_im[j], o_s5_log_dt[j], o_s5_b_re[j], o_s5_b_im[j],
                                o_s5_c_re[j], o_s5_c_im[j], o_s5_d[j])
            glu_w = o_glu_w[j].astype(BF16)
            mix_l = (att_l, _s5_mixer(u_c, u_l, mats, glu_w, o_glu_b[j]))
            if not last:
                raise NotImplementedError("an odd layer that is not the last needs the context S5 readout")
            w_out = o_w_out[j]
            split = MLA_HEADS * MLA_V

        wa, wb = w_out[:split].astype(BF16), w_out[split:].astype(BF16)
        router = jnp.pad(jnp.concatenate([moe_w_rg[i], moe_w_re[i]], axis=1),
                         ((0, 0), (0, LANES - N_GROUPS - N_EXPERTS)))
        rh = router.astype(BF16)
        rl = (router - rh.astype(F32)).astype(BF16)
        wg, wu, wd = (moe_w_gate[i].astype(BF16), moe_w_up[i].astype(BF16), moe_w_down[i].astype(BF16))

        x, h_l, route_l = _proj_res(*mix_l, wa, wb, x, lat[2], norm_g[i, 1], lat[3], lat[4], rh, rl)
        streams = [(x, h_l, route_l, lat[5])]
        if not last:
            xc, h_c, route_c = _proj_res(*mix_c, wa, wb, xc, cxm[2], norm_g[i, 1], cxm[3], cxm[4], rh, rl)
            streams.append((xc, h_c, route_c, cxm[5]))
        outs = _moe(streams, wg, wu, wd, final_g, last)
        x = outs[0]
        if not last:
            xc = outs[1]
    return x
```

```python
import functools
import math

import jax
import jax.numpy as jnp
from jax import lax
from jax.experimental import pallas as pl
from jax.experimental.pallas import tpu as pltpu

F32 = jnp.float32
BF16 = jnp.bfloat16
I32 = jnp.int32

D_MODEL = 1024
GRID_W = 64
HEAD_DIM = 64
A_WIDTH = D_MODEL // 2
B_Q_HEADS = (D_MODEL - A_WIDTH) // HEAD_DIM
B_KV_HEADS = max(1, B_Q_HEADS // 4)
HY_ORDER = 2
HY_BANDS = 16
HY_TARGET = 1e-2
HY_MAX_DECAY = math.log(1.0 / HY_TARGET) / 0.3
HY_MIN_DECAY = math.log(1.0 / HY_TARGET) / 1.5
MLA_HEADS = 8
MLA_NOPE = 64
MLA_ROPE = 32
MLA_V = 64
MLA_Q_RANK = D_MODEL // 4
MLA_KV_RANK = D_MODEL // 8
S5_WIDTH = D_MODEL - MLA_HEADS * MLA_V
S5_GROUP = 16
S5_GROUPS = S5_WIDTH // S5_GROUP
S5_STATE = 64
S5_CHUNK = 16
N_GROUPS = 4
EXPERTS_PER_GROUP = 8
N_EXPERTS = N_GROUPS * EXPERTS_PER_GROUP
D_EXPERT = D_MODEL // 4
ROPE_THETA = 10000.0
EPS = 1e-6
LOG2E = 1.4426950408889634

LANES = 128
SUBLANES = 8
MOE_TILE = 256
HIGHEST = lax.Precision.HIGHEST
MIB = 1024 * 1024


def _cparams(sem, vmem_mib):
    return pltpu.CompilerParams(dimension_semantics=sem, vmem_limit_bytes=vmem_mib * MIB)


def _dot(a, b):
    return jnp.dot(a, b, preferred_element_type=F32)


def _dot_nt(a, b):
    return lax.dot_general(a, b, (((1,), (1,)), ((), ())), preferred_element_type=F32)


def _split_bf16(a):
    hi = a.astype(BF16)
    lo = (a - hi.astype(F32)).astype(BF16)
    return hi, lo


def _dot3(ah, al, bh, bl):
    return _dot(ah, bh) + (_dot(ah, bl) + _dot(al, bh))


def _sigmoid(x):
    return 1.0 / (1.0 + jnp.exp(-x))


def _rms(x, g, width):
    ms = jnp.sum(x * x, axis=-1, keepdims=True) * (1.0 / width)
    return x * lax.rsqrt(ms + EPS) * g


def _modnorm(x, g, shift, scale):
    return _rms(x, g, x.shape[-1]) * (1.0 + scale) + shift


def _rope(t, cs, s1, s2, half):
    return t * cs + pltpu.roll(t, LANES - half, 1) * s1 + pltpu.roll(t, half, 1) * s2


def _store_token_tiles(ref, base_tok, val):
    n = val.shape[0]
    for s in range(SUBLANES):
        ref[pl.ds(base_tok * SUBLANES + s, n, stride=SUBLANES), :] = val[:, LANES * s: LANES * (s + 1)]


def _load_token_tiles(ref, base_tok, n):
    return jnp.concatenate([ref[pl.ds(base_tok * SUBLANES + s, n, stride=SUBLANES), :]
                            for s in range(SUBLANES)], axis=-1)


def _short_conv(u, w, b):
    n = u.shape[0]
    row = lax.broadcasted_iota(I32, u.shape, 0)
    prev = jnp.where(row == 0, 0.0, pltpu.roll(u, 1, 0))
    nxt = jnp.where(row == n - 1, 0.0, pltpu.roll(u, n - 1, 0))
    return prev * w[0:1] + u * w[1:2] + nxt * w[2:3] + b


def _mod_kernel(c_ref, w_ref, b_ref, o_ref):
    c = c_ref[...]
    s = c * _sigmoid(c)
    sh, sl = _split_bf16(s)
    wh, wl = _split_bf16(w_ref[...])
    o_ref[...] = _dot3(sh, sl, wh, wl) + b_ref[...]


def _modulation(cc, w, b):
    r, d = cc.shape
    n = w.shape[1]
    tn = 1536
    return pl.pallas_call(
        _mod_kernel,
        grid=(n // tn,),
        in_specs=[pl.BlockSpec((r, d), lambda j: (0, 0)),
                  pl.BlockSpec((d, tn), lambda j: (0, j)),
                  pl.BlockSpec((1, tn), lambda j: (0, j))],
        out_specs=pl.BlockSpec((r, tn), lambda j: (0, j)),
        out_shape=jax.ShapeDtypeStruct((r, n), F32),
        compiler_params=_cparams(("parallel",), 48),
        name="modulation",
    )(cc, w, b.reshape(1, n))


def _with_ones(v):
    lane = lax.broadcasted_iota(I32, v.shape, 1) % LANES
    return jnp.where(lane >= HEAD_DIM, 1.0, v)


def _even_in_kernel(x_ref, g_ref, sh_ref, sc_ref, w_ref, qg_ref, kg_ref, cs_ref, s1_ref, s2_ref,
                    hy_ref, q_ref, k_ref, v_ref, *, qscale):
    xn = _modnorm(x_ref[0], g_ref[...], sh_ref[0], sc_ref[0]).astype(BF16)
    p = _dot(xn, w_ref[...])
    hw = 3 * A_WIDTH
    hy_ref[0] = p[:, :hw].astype(BF16)
    cs, s1, s2 = cs_ref[...], s1_ref[...], s2_ref[...]
    for h in range(B_Q_HEADS):
        t = _rms(p[:, hw + LANES * h: hw + LANES * (h + 1)], qg_ref[...], HEAD_DIM)
        q_ref[0, :, LANES * h: LANES * (h + 1)] = (_rope(t, cs, s1, s2, HEAD_DIM // 4) * qscale).astype(BF16)
    kw = hw + LANES * B_Q_HEADS
    for h in range(B_KV_HEADS):
        t = _rms(p[:, kw + LANES * h: kw + LANES * (h + 1)], kg_ref[...], HEAD_DIM)
        k_ref[0, :, LANES * h: LANES * (h + 1)] = _rope(t, cs, s1, s2, HEAD_DIM // 4).astype(BF16)
    vw = kw + LANES * B_KV_HEADS
    v_ref[0] = _with_ones(p[:, vw:]).astype(BF16)


def _even_in(x, g, shift, scale, w, qg, kg, tabs, qscale):
    bsz, n, d = x.shape
    tm = min(256, n)
    nw = w.shape[1]
    row = lambda b, t: (b, t, 0)
    per_b = lambda b, t: (b, 0, 0)
    fixed = lambda b, t: (0, 0)
    tab = pl.BlockSpec((tm, LANES), lambda b, t: (t, 0))
    return pl.pallas_call(
        functools.partial(_even_in_kernel, qscale=qscale),
        grid=(bsz, n // tm),
        in_specs=[pl.BlockSpec((1, tm, d), row), pl.BlockSpec((1, d), fixed),
                  pl.BlockSpec((1, 1, d), per_b), pl.BlockSpec((1, 1, d), per_b),
                  pl.BlockSpec((d, nw), fixed), pl.BlockSpec((1, LANES), fixed),
                  pl.BlockSpec((1, LANES), fixed), tab, tab, tab],
        out_specs=[pl.BlockSpec((1, tm, 3 * A_WIDTH), row),
                   pl.BlockSpec((1, tm, LANES * B_Q_HEADS), row),
                   pl.BlockSpec((1, tm, LANES * B_KV_HEADS), row),
                   pl.BlockSpec((1, tm, LANES * B_KV_HEADS), row)],
        out_shape=[jax.ShapeDtypeStruct((bsz, n, 3 * A_WIDTH), BF16),
                   jax.ShapeDtypeStruct((bsz, n, LANES * B_Q_HEADS), BF16),
                   jax.ShapeDtypeStruct((bsz, n, LANES * B_KV_HEADS), BF16),
                   jax.ShapeDtypeStruct((bsz, n, LANES * B_KV_HEADS), BF16)],
        compiler_params=_cparams(("parallel", "parallel"), 48),
        name="even_in",
    )(x, g.reshape(1, d), shift, scale, w, qg, kg, *tabs)


def _odd_in_kernel(x_ref, g_ref, sh_ref, sc_ref, w_ref, qg_ref, kvg_ref, wuq_ref, wuk_ref, wuv_ref,
                   cs_ref, s1_ref, s2_ref, *out_refs, need_q, qscale):
    if need_q:
        u_ref, k_ref, v_ref, q_ref = out_refs
    else:
        u_ref, k_ref, v_ref = out_refs
    xn = _modnorm(x_ref[0], g_ref[...], sh_ref[0], sc_ref[0]).astype(BF16)
    p = _dot(xn, w_ref[...])
    c0, c1, c2 = MLA_Q_RANK, MLA_Q_RANK + MLA_KV_RANK, MLA_Q_RANK + MLA_KV_RANK + LANES
    u_ref[0] = p[:, c2:].astype(BF16)
    cs, s1, s2 = cs_ref[...], s1_ref[...], s2_ref[...]
    ckv = _rms(p[:, c0:c1], kvg_ref[...], MLA_KV_RANK).astype(BF16)
    kn = _dot(ckv, wuk_ref[...])
    v_ref[0] = _with_ones(_dot(ckv, wuv_ref[...])).astype(BF16)
    kr = _rope(p[:, c1:c2], cs, s1, s2, MLA_ROPE // 4)
    for h in range(MLA_HEADS):
        k_ref[0, :, LANES * h: LANES * (h + 1)] = (kn[:, LANES * h: LANES * (h + 1)] + kr).astype(BF16)
    if need_q:
        cq = _rms(p[:, :c0], qg_ref[...], MLA_Q_RANK).astype(BF16)
        q = _dot(cq, wuq_ref[...])
        for h in range(MLA_HEADS):
            t = q[:, LANES * h: LANES * (h + 1)]
            q_ref[0, :, LANES * h: LANES * (h + 1)] = (_rope(t, cs, s1, s2, MLA_ROPE // 4) * qscale).astype(BF16)


def _odd_in(x, g, shift, scale, w, qg, kvg, wuq, wuk, wuv, tabs, need_q, qscale):
    bsz, n, d = x.shape
    tm = min(256, n)
    hw = LANES * MLA_HEADS
    row = lambda b, t: (b, t, 0)
    per_b = lambda b, t: (b, 0, 0)
    fixed = lambda b, t: (0, 0)
    tab = pl.BlockSpec((tm, LANES), lambda b, t: (t, 0))
    widths = [S5_WIDTH, hw, hw] + ([hw] if need_q else [])
    return pl.pallas_call(
        functools.partial(_odd_in_kernel, need_q=need_q, qscale=qscale),
        grid=(bsz, n // tm),
        in_specs=[pl.BlockSpec((1, tm, d), row), pl.BlockSpec((1, d), fixed),
                  pl.BlockSpec((1, 1, d), per_b), pl.BlockSpec((1, 1, d), per_b),
                  pl.BlockSpec(w.shape, fixed), pl.BlockSpec((1, MLA_Q_RANK), fixed),
                  pl.BlockSpec((1, MLA_KV_RANK), fixed), pl.BlockSpec(wuq.shape, fixed),
                  pl.BlockSpec(wuk.shape, fixed), pl.BlockSpec(wuv.shape, fixed), tab, tab, tab],
        out_specs=[pl.BlockSpec((1, tm, wd), row) for wd in widths],
        out_shape=[jax.ShapeDtypeStruct((bsz, n, wd), BF16) for wd in widths],
        compiler_params=_cparams(("parallel", "parallel"), 48),
        name="odd_in_q" if need_q else "odd_in",
    )(x, g.reshape(1, d), shift, scale, w, qg, kvg, wuq, wuk, wuv, *tabs)


def _hy_fwd_kernel(wc_ref, ws_ref, u_ref, cw_ref, cb_ref, ka_ref, kb_ref, ya_ref, yb_ref, ubf_ref,
                   *, conv, tf):
    f = pl.program_id(1)

    @pl.when(f == 0)
    def _():
        u = u_ref[0].astype(F32)
        if conv:
            u = _short_conv(u, cw_ref[...], cb_ref[...])
        ubf_ref[...] = u.astype(BF16)

    ub = ubf_ref[...]
    a = _dot(wc_ref[...], ub)
    b = _dot(ws_ref[...], ub)
    ka, kb = ka_ref[...], kb_ref[...]
    first = (lax.broadcasted_iota(I32, a.shape, 0) + f * tf) == 0
    ya_ref[0] = jnp.where(first, a * ka, a * ka - b * kb).astype(BF16)
    yb_ref[0] = jnp.where(first, b * kb, a * kb + b * ka).astype(BF16)


def _hy_fwd(wc, ws, u, col, cw, cb, ka, kb, conv):
    bsz, n, _ = u.shape
    c = A_WIDTH
    tf = min(512, n)
    return pl.pallas_call(
        functools.partial(_hy_fwd_kernel, conv=conv, tf=tf),
        grid=(bsz, n // tf),
        in_specs=[pl.BlockSpec((tf, n), lambda b, f: (f, 0)), pl.BlockSpec((tf, n), lambda b, f: (f, 0)),
                  pl.BlockSpec((1, n, c), lambda b, f: (b, 0, col)),
                  pl.BlockSpec((3, c), lambda b, f: (0, 0)), pl.BlockSpec((1, c), lambda b, f: (0, 0)),
                  pl.BlockSpec((tf, c), lambda b, f: (f, 0)), pl.BlockSpec((tf, c), lambda b, f: (f, 0))],
        out_specs=[pl.BlockSpec((1, tf, c), lambda b, f: (b, f, 0))] * 2,
        out_shape=[jax.ShapeDtypeStruct((bsz, n, c), BF16)] * 2,
        scratch_shapes=[pltpu.VMEM((n, c), BF16)],
        compiler_params=_cparams(("parallel", "arbitrary"), 48),
        name="hyena_fwd",
    )(wc, ws, u, cw, cb, ka, kb)


def _hy_inv_kernel(ci_ref, si_ref, ya_ref, yb_ref, a_ref, g_ref, cwa_ref, cba_ref, cwg_ref, cbg_ref,
                   bias_ref, o_ref, a_sc, g_sc, *, conv_a, tt):
    t = pl.program_id(1)

    @pl.when(t == 0)
    def _():
        a = a_ref[0].astype(F32)
        if conv_a:
            a = _short_conv(a, cwa_ref[...], cba_ref[...])
        a_sc[...] = a
        g_sc[...] = _short_conv(g_ref[0].astype(F32), cwg_ref[...], cbg_ref[...])

    y = _dot(ci_ref[...], ya_ref[0]) + _dot(si_ref[...], yb_ref[0])
    rows = pl.ds(pl.multiple_of(t * tt, tt), tt)
    y = y + a_sc[rows, :] * bias_ref[...]
    o_ref[0] = (g_sc[rows, :] * y).astype(BF16)


def _hy_inv(ci, si, ya, yb, a, a_col, g, g_col, cwa, cba, cwg, cbg, bias, conv_a):
    bsz, n, c = ya.shape
    tt = min(256, n)
    whole = lambda b, t: (b, 0, 0)
    fixed = lambda b, t: (0, 0)
    return pl.pallas_call(
        functools.partial(_hy_inv_kernel, conv_a=conv_a, tt=tt),
        grid=(bsz, n // tt),
        in_specs=[pl.BlockSpec((tt, n), lambda b, t: (t, 0)), pl.BlockSpec((tt, n), lambda b, t: (t, 0)),
                  pl.BlockSpec((1, n, c), whole), pl.BlockSpec((1, n, c), whole),
                  pl.BlockSpec((1, n, c), lambda b, t: (b, 0, a_col)),
                  pl.BlockSpec((1, n, c), lambda b, t: (b, 0, g_col)),
                  pl.BlockSpec((3, c), fixed), pl.BlockSpec((1, c), fixed),
                  pl.BlockSpec((3, c), fixed), pl.BlockSpec((1, c), fixed), pl.BlockSpec((1, c), fixed)],
        out_specs=pl.BlockSpec((1, tt, c), lambda b, t: (b, t, 0)),
        out_shape=jax.ShapeDtypeStruct((bsz, n, c), BF16),
        scratch_shapes=[pltpu.VMEM((n, c), F32), pltpu.VMEM((n, c), F32)],
        compiler_params=_cparams(("parallel", "arbitrary"), 56),
        name="hyena_inv",
    )(ci, si, ya, yb, a, g, cwa, cba, cwg, cbg, bias)


def _dft_mats(n):
    big = 2 * n
    f = jnp.arange(n, dtype=I32)[:, None]
    s = jnp.arange(n, dtype=I32)[None, :]
    ang = ((f * s) % big).astype(F32) * (2.0 * math.pi / big)
    cos, sin = jnp.cos(ang), jnp.sin(ang)
    alt = jnp.where(jnp.arange(n) % 2 == 0, 1.0, -1.0).astype(F32)
    wc = cos
    ws = jnp.where(f == 0, alt[None, :], -sin)
    ci = jnp.where(s == 0, 1.0 / big, (2.0 / big) * cos)
    si = jnp.where(s == 0, alt[:, None] / big, -(2.0 / big) * sin)
    return wc.astype(BF16), ws.astype(BF16), ci.astype(BF16), si.astype(BF16)


def _hyena_filter_spectrum(n, w1, b1, w2, b2, w3, freq):
    t = jnp.arange(n, dtype=F32)
    t_norm = t / max(n - 1, 1)
    bands = jnp.linspace(1e-4, HY_BANDS - 1, HY_BANDS, dtype=F32)
    ang = (2.0 * math.pi / n) * t[:, None] * bands
    z = jnp.concatenate([t_norm[:, None], jnp.cos(ang), jnp.sin(ang)], axis=-1)
    hdn = jnp.sin(freq * (jnp.dot(z, w1, precision=HIGHEST) + b1))
    hdn = jnp.sin(freq * (jnp.dot(hdn, w2, precision=HIGHEST) + b2))
    taps = jnp.dot(hdn, w3, precision=HIGHEST).reshape(n, 2, HY_ORDER, A_WIDTH)
    deltas = jnp.linspace(HY_MIN_DECAY, HY_MAX_DECAY, A_WIDTH, dtype=F32)
    taps = taps * jnp.exp(-t_norm[:, None] * deltas)[:, None, None, :]
    fwd, bwd = taps[:, 0], taps[:, 1]
    circ = jnp.concatenate([fwd, jnp.zeros((1, HY_ORDER, A_WIDTH), F32), bwd[:0:-1]], axis=0)
    circ = circ * lax.rsqrt(jnp.sum(circ * circ, axis=0, keepdims=True) + EPS)
    kf = jnp.fft.rfft(circ, axis=0)
    ka = jnp.real(kf[:n]).astype(F32)
    kb = jnp.concatenate([jnp.real(kf[n:n + 1]), jnp.imag(kf[1:n])], axis=0).astype(F32)
    return ka, kb


def _hyena(hy, cw, cb, filt, fbias):
    n = hy.shape[1]
    c = A_WIDTH
    wc, ws, ci, si = _dft_mats(n)
    ka, kb = _hyena_filter_spectrum(n, *filt)
    cws = [cw[:, i * c:(i + 1) * c] for i in range(3)]
    cbs = [cb[i * c:(i + 1) * c].reshape(1, c) for i in range(3)]
    ya, yb = _hy_fwd(wc, ws, hy, 2, cws[2], cbs[2], ka[:, 0], kb[:, 0], True)
    z = _hy_inv(ci, si, ya, yb, hy, 2, hy, 0, cws[2], cbs[2], cws[0], cbs[0], fbias[0:1], True)
    ya, yb = _hy_fwd(wc, ws, z, 0, cws[2], cbs[2], ka[:, 1], kb[:, 1], False)
    return _hy_inv(ci, si, ya, yb, z, 0, hy, 1, cws[2], cbs[2], cws[1], cbs[1], fbias[1:2], False)


def _attn_kernel(*refs, nseg, hps, kps):
    q_ref = refs[0]
    k_refs = refs[1:1 + nseg]
    v_refs = refs[1 + nseg:1 + 2 * nseg]
    o_ref = refs[1 + 2 * nseg]
    low = lax.broadcasted_iota(I32, (q_ref.shape[1], LANES), 1) < HEAD_DIM
    for pair in range(hps // 2):
        accs = []
        for half in range(2):
            i = 2 * pair + half
            kv = (i * kps) // hps
            q = q_ref[0, :, LANES * i: LANES * (i + 1)]
            ss = [_dot_nt(q, k[0, :, LANES * kv: LANES * (kv + 1)]) for k in k_refs]
            m = ss[0].max(axis=-1, keepdims=True)
            for s in ss[1:]:
                m = jnp.maximum(m, s.max(axis=-1, keepdims=True))
            acc = None
            for s, v in zip(ss, v_refs):
                p = jnp.exp2((s - m).astype(BF16))
                o = _dot(p, v[0, :, LANES * kv: LANES * (kv + 1)])
                acc = o if acc is None else acc + o
            accs.append(acc)
        num = jnp.where(low, accs[0], pltpu.roll(accs[1], HEAD_DIM, 1))
        den = jnp.where(low, pltpu.roll(accs[0], HEAD_DIM, 1), accs[1])
        o_ref[0, :, LANES * pair: LANES * (pair + 1)] = (num / den).astype(BF16)


def _attention(q, ks, vs, hps, kps):
    bsz, lq, qw = q.shape
    nh = qw // LANES
    tq = min(256, lq)
    nseg = len(ks)
    kv_spec = lambda a: pl.BlockSpec((1, a.shape[1], kps * LANES), lambda b, g, t: (b, 0, g))
    return pl.pallas_call(
        functools.partial(_attn_kernel, nseg=nseg, hps=hps, kps=kps),
        grid=(bsz, nh // hps, lq // tq),
        in_specs=[pl.BlockSpec((1, tq, hps * LANES), lambda b, g, t: (b, t, g))]
        + [kv_spec(a) for a in ks] + [kv_spec(a) for a in vs],
        out_specs=pl.BlockSpec((1, tq, hps * HEAD_DIM), lambda b, g, t: (b, t, g)),
        out_shape=jax.ShapeDtypeStruct((bsz, lq, nh * HEAD_DIM), BF16),
        compiler_params=_cparams(("parallel", "parallel", "parallel"), 56),
        name="attention",
    )(q, *ks, *vs)


def _s5_kernel(u_ref, w_ref, a_ref, of_ref, or_ref, y_ref, yi_sc, sf_sc, sr_sc, xf_sc, xr_sc,
               *, nc, ncc, bsz):
    cw = S5_CHUNK * S5_GROUP
    sw = 2 * S5_STATE
    r = _dot(u_ref[0], w_ref[0])
    yi_sc[...] = r[:, :cw]
    sf = r[:, cw:cw + sw]
    sr = r[:, cw + sw:]
    sf_sc[0] = sf
    sr_sc[0] = sr
    sf_sc[1] = pltpu.roll(sf, S5_STATE, 1)
    sr_sc[1] = pltpu.roll(sr, S5_STATE, 1)
    a = a_ref[0]

    def advance(st, s_sc, rws, d):
        x, xs = st
        a0, a1 = a[2 * d:2 * d + 1], a[2 * d + 1:2 * d + 2]
        return (x * a0 + xs * a1 + s_sc[0, rws, :], xs * a0 - x * a1 + s_sc[1, rws, :])

    def rows(c):
        return pl.ds(pl.multiple_of(c * bsz, bsz), bsz)

    zero = (jnp.zeros((bsz, sw), F32), jnp.zeros((bsz, sw), F32))

    def fstep(c, st):
        xf_sc[rows(c), :] = st[0]
        return advance(st, sf_sc, rows(c), 0)

    lax.fori_loop(0, nc, fstep, zero)

    def rstep(base):
        def step(i, st):
            c = base - i
            xr_sc[rows(c), :] = st[0]
            return advance(st, sr_sc, rows(c), 1)
        return step

    st = lax.fori_loop(0, ncc, rstep(ncc - 1), zero)
    lax.fori_loop(0, nc - ncc, rstep(nc - 1), st)

    lo = ncc * bsz
    y = yi_sc[lo:, :]
    y = y + _dot(xf_sc[lo:, :].astype(BF16), of_ref[0])
    y = y + _dot(xr_sc[lo:, :].astype(BF16), or_ref[0])
    y_ref[0] = y


def _s5_scan(uf, w, a16, of, orr, nc, ncc, bsz):
    g, rws, cw = uf.shape
    sw = 2 * S5_STATE
    out_rows = (nc - ncc) * bsz
    blk = lambda a: pl.BlockSpec((1,) + a.shape[1:], lambda i: (i, 0, 0))
    return pl.pallas_call(
        functools.partial(_s5_kernel, nc=nc, ncc=ncc, bsz=bsz),
        grid=(g,),
        in_specs=[blk(uf), blk(w), blk(a16), blk(of), blk(orr)],
        out_specs=pl.BlockSpec((1, out_rows, cw), lambda i: (i, 0, 0)),
        out_shape=jax.ShapeDtypeStruct((g, out_rows, cw), F32),
        scratch_shapes=[pltpu.VMEM((rws, cw), F32)] + [pltpu.VMEM((2, rws, sw), F32)] * 2
        + [pltpu.VMEM((rws, sw), F32)] * 2,
        compiler_params=_cparams(("parallel",), 48),
        name="s5_scan",
    )(uf, w, a16, of, orr)


def _s5_matrices(a_re, a_im, log_dt, b_re, b_im, c_re, c_im, d_skip):
    lc, h, p, g = S5_CHUNK, S5_GROUP, S5_STATE, S5_GROUPS
    dt = jnp.exp(log_dt)[..., None]
    lam_re, lam_im = a_re * dt, a_im * dt
    mag = jnp.exp(lam_re)
    er, ei = mag * jnp.cos(lam_im) - 1.0, mag * jnp.sin(lam_im)
    den = a_re * a_re + a_im * a_im
    co_re = (er * a_re + ei * a_im) / den
    co_im = (ei * a_re - er * a_im) / den
    bb_re = co_re[..., None] * b_re - co_im[..., None] * b_im
    bb_im = co_re[..., None] * b_im + co_im[..., None] * b_re
    m = jnp.arange(lc + 1, dtype=F32)[:, None, None, None]
    pw_mag = jnp.exp(m * lam_re[None])
    pw_re, pw_im = pw_mag * jnp.cos(m * lam_im[None]), pw_mag * jnp.sin(m * lam_im[None])
    ab_re = pw_re[..., None] * bb_re[None] - pw_im[..., None] * bb_im[None]
    ab_im = pw_re[..., None] * bb_im[None] + pw_im[..., None] * bb_re[None]
    kk = (jnp.einsum("dgop,mdgpi->mdgoi", c_re, ab_re[:lc], precision=HIGHEST)
          - jnp.einsum("dgop,mdgpi->mdgoi", c_im, ab_im[:lc], precision=HIGHEST))
    j = jnp.arange(lc)[:, None]
    k = jnp.arange(lc)[None, :]
    lag_f = jnp.clip(k - j, 0, lc - 1)
    lag_r = jnp.clip(j - k, 0, lc - 1)
    tf = jnp.where((k >= j)[:, :, None, None, None], kk[lag_f, 0], 0.0)
    tr = jnp.where((j >= k)[:, :, None, None, None], kk[lag_r, 1], 0.0)
    tw = jnp.transpose(tf + tr, (2, 0, 4, 1, 3))
    dsk = d_skip.reshape(g, h)
    eye = (jnp.eye(lc, dtype=F32)[None, :, None, :, None] * jnp.eye(h, dtype=F32)[None, None, :, None, :]
           * dsk[:, None, :, None, None])
    tw = (tw + eye).reshape(g, lc * h, lc * h)
    idx_f = (lc - 1 - jnp.arange(lc))
    idx_r = jnp.arange(lc)

    def to_state(idx, d):
        re = jnp.transpose(ab_re[idx, d], (1, 0, 3, 2))
        im = jnp.transpose(ab_im[idx, d], (1, 0, 3, 2))
        return jnp.concatenate([re, im], axis=-1).reshape(g, lc * h, 2 * p)

    w = jnp.concatenate([tw, to_state(idx_f, 0), to_state(idx_r, 1)], axis=-1)

    def from_state(idx, d):
        pr, pi = pw_re[idx, d], pw_im[idx, d]
        cr, cim = c_re[d], c_im[d]
        mr = cr[None] * pr[:, :, None, :] - cim[None] * pi[:, :, None, :]
        mi = cr[None] * pi[:, :, None, :] + cim[None] * pr[:, :, None, :]
        return jnp.transpose(jnp.concatenate([mr, -mi], axis=-1), (1, 3, 0, 2)).reshape(g, 2 * p, lc * h)

    of = from_state(jnp.arange(lc) + 1, 0)
    orr = from_state(lc - jnp.arange(lc), 1)
    a16 = jnp.stack([jnp.concatenate([pw_re[lc, 0], pw_re[lc, 0]], -1),
                     jnp.concatenate([-pw_im[lc, 0], pw_im[lc, 0]], -1),
                     jnp.concatenate([pw_re[lc, 1], pw_re[lc, 1]], -1),
                     jnp.concatenate([-pw_im[lc, 1], pw_im[lc, 1]], -1)], axis=1)
    return w.astype(BF16), a16.astype(F32), of.astype(BF16), orr.astype(BF16)


def _glu_kernel(y_ref, w_ref, b_ref, o_ref):
    y = y_ref[0]
    inner = 0.7978845608028654 * (y + 0.044715 * (y * y * y))
    gl = 0.5 * y * (1.0 + jnp.tanh(inner))
    o_ref[0] = (gl * _sigmoid(_dot(gl.astype(BF16), w_ref[...]) + b_ref[...])).astype(BF16)


def _glu(y, w, b):
    bsz, n, c = y.shape
    tm = min(512, n)
    return pl.pallas_call(
        _glu_kernel,
        grid=(bsz, n // tm),
        in_specs=[pl.BlockSpec((1, tm, c), lambda i, t: (i, t, 0)),
                  pl.BlockSpec((c, c), lambda i, t: (0, 0)), pl.BlockSpec((1, c), lambda i, t: (0, 0))],
        out_specs=pl.BlockSpec((1, tm, c), lambda i, t: (i, t, 0)),
        out_shape=jax.ShapeDtypeStruct((bsz, n, c), BF16),
        compiler_params=_cparams(("parallel", "parallel"), 32),
        name="s5_glu",
    )(y, w, b.reshape(1, c))


def _s5_mixer(u_c, u_l, mats, glu_w, glu_b):
    bsz, ncx, c = u_c.shape
    nl = u_l.shape[1]
    lc, h, g = S5_CHUNK, S5_GROUP, S5_GROUPS
    ncc, ncl = ncx // lc, nl // lc
    nc = ncc + ncl
    u = jnp.concatenate([u_c, u_l], axis=1).reshape(bsz, nc, lc, g, h)
    uf = jnp.transpose(u, (3, 1, 0, 2, 4)).reshape(g, nc * bsz, lc * h)
    y = _s5_scan(uf, *mats, nc, ncc, bsz)
    y = jnp.transpose(y.reshape(g, ncl, bsz, lc, h), (2, 1, 3, 0, 4)).reshape(bsz, nl, c)
    return _glu(y, glu_w, glu_b)


def _route(lg):
    lane = lax.broadcasted_iota(I32, lg.shape, 1).astype(F32)
    neg = jnp.float32(-1e30)
    far = jnp.float32(LANES)
    is_g = lane < N_GROUPS
    lgm = jnp.where(is_g, lg, neg)
    m = lgm.max(axis=-1, keepdims=True)
    gidx = jnp.where(lgm == m, lane, far).min(axis=-1, keepdims=True)
    gp = 1.0 / jnp.where(is_g, jnp.exp(lgm - m), 0.0).sum(axis=-1, keepdims=True)
    lo = N_GROUPS + EXPERTS_PER_GROUP * gidx
    is_e = (lane >= lo) & (lane < lo + EXPERTS_PER_GROUP)
    lem = jnp.where(is_e, lg, neg)
    l0 = lem.max(axis=-1, keepdims=True)
    i0 = jnp.where((lem == l0) & is_e, lane, far).min(axis=-1, keepdims=True)
    lem1 = jnp.where(lane == i0, neg, lem)
    l1 = lem1.max(axis=-1, keepdims=True)
    i1 = jnp.where((lem1 == l1) & is_e & (lane != i0), lane, far).min(axis=-1, keepdims=True)
    e1 = jnp.exp(l1 - l0)
    w0 = 1.0 / (1.0 + e1)
    w1 = e1 * w0
    out = jnp.where(lane == 0, i0 - N_GROUPS, 0.0)
    out = jnp.where(lane == 1, i1 - N_GROUPS, out)
    out = jnp.where(lane == 2, gp * w0, out)
    return jnp.where(lane == 3, gp * w1, out)


def _proj_res_kernel(a_ref, b_ref, wa_ref, wb_ref, x_ref, g1_ref, ng_ref, sh_ref, sc_ref, rh_ref, rl_ref,
                     xo_ref, h_ref, route_ref):
    o = _dot(a_ref[0], wa_ref[...]) + _dot(b_ref[0], wb_ref[...])
    x = x_ref[0] + g1_ref[0] * o
    xo_ref[0] = x
    h = _modnorm(x, ng_ref[...], sh_ref[0], sc_ref[0])
    _store_token_tiles(h_ref, 0, h)
    hh, hl = _split_bf16(h)
    route_ref[0] = _route(_dot3(hh, hl, rh_ref[...], rl_ref[...]))


def _proj_res(a, b, wa, wb, x, g1, ng, shift, scale, rh, rl):
    bsz, n, d = x.shape
    tm = min(256, n)
    row = lambda i, t: (i, t, 0)
    per_b = lambda i, t: (i, 0, 0)
    fixed = lambda i, t: (0, 0)
    return pl.pallas_call(
        _proj_res_kernel,
        grid=(bsz, n // tm),
        in_specs=[pl.BlockSpec((1, tm, a.shape[2]), row), pl.BlockSpec((1, tm, b.shape[2]), row),
                  pl.BlockSpec(wa.shape, fixed), pl.BlockSpec(wb.shape, fixed),
                  pl.BlockSpec((1, tm, d), row), pl.BlockSpec((1, 1, d), per_b), pl.BlockSpec((1, d), fixed),
                  pl.BlockSpec((1, 1, d), per_b), pl.BlockSpec((1, 1, d), per_b),
                  pl.BlockSpec((d, LANES), fixed), pl.BlockSpec((d, LANES), fixed)],
        out_specs=[pl.BlockSpec((1, tm, d), row),
                   pl.BlockSpec((tm * SUBLANES, LANES), lambda i, t: (i * (n // tm) + t, 0)),
                   pl.BlockSpec((1, tm, LANES), row)],
        out_shape=[jax.ShapeDtypeStruct((bsz, n, d), F32),
                   jax.ShapeDtypeStruct((bsz * n * SUBLANES, LANES), F32),
                   jax.ShapeDtypeStruct((bsz, n, LANES), F32)],
        compiler_params=_cparams(("parallel", "parallel"), 40),
        name="proj_res",
    )(a, b, wa, wb, x, g1, ng.reshape(1, d), shift, scale, rh, rl)


def _token_copy(src, src_tok, dst, dst_tok, sem):
    rows = lambda t: pl.ds(pl.multiple_of(t * SUBLANES, SUBLANES), SUBLANES)
    return pltpu.make_async_copy(src.at[rows(src_tok)], dst.at[rows(dst_tok)], sem)


def _tile_copy(src, src_tok, dst, dst_tok, sem):
    n = MOE_TILE * SUBLANES
    rows = lambda t: pl.ds(pl.multiple_of(t * SUBLANES, SUBLANES), n)
    return pltpu.make_async_copy(src.at[rows(src_tok)], dst.at[rows(dst_tok)], sem)


def _experts_kernel(te_ref, src_ref, srcn_ref, dst_ref, h_ref, wg_ref, wu_ref, wd_ref, ys_ref,
                    xbuf, obuf, gsem, ssem):
    del te_ref
    i = pl.program_id(0)
    nt = pl.num_programs(0)
    slot = i % 2
    other = 1 - slot

    def start_gather(idx_ref, s):
        def body(j, c):
            for k in range(2):
                r = 2 * j + k
                _token_copy(h_ref, idx_ref[0, 0, r], xbuf, s * MOE_TILE + r, gsem.at[s]).start(priority=k)
            return c
        lax.fori_loop(0, MOE_TILE // 2, body, 0, unroll=4)

    @pl.when(i == 0)
    def _():
        start_gather(src_ref, 0)

    @pl.when(i + 1 < nt)
    def _():
        start_gather(srcn_ref, other)

    base = slot * MOE_TILE
    _tile_copy(h_ref, 0, xbuf, base, gsem.at[slot]).wait()

    @pl.when(i >= 2)
    def _():
        _tile_copy(obuf, base, ys_ref, 0, ssem.at[slot]).wait()

    x = _load_token_tiles(xbuf, base, MOE_TILE).astype(BF16)
    gt = _dot(x, wg_ref[0])
    hid = (gt * _sigmoid(gt)) * _dot(x, wu_ref[0])
    _store_token_tiles(obuf, base, _dot(hid.astype(BF16), wd_ref[0]))

    def scatter(j, c):
        for k in range(2):
            r = 2 * j + k
            _token_copy(obuf, base + r, ys_ref, dst_ref[0, 0, r], ssem.at[slot]).start(priority=k)
        return c
    lax.fori_loop(0, MOE_TILE // 2, scatter, 0, unroll=4)

    @pl.when(i == nt - 1)
    def _():
        _tile_copy(obuf, base, ys_ref, 0, ssem.at[slot]).wait()
        _tile_copy(obuf, other * MOE_TILE, ys_ref, 0, ssem.at[other]).wait()


def _experts(tile_expert, src_tok, dst_row, h, wg, wu, wd):
    nt = tile_expert.shape[0]
    d = D_MODEL
    wmap = lambda i, te: (te[i], 0, 0)
    idx = lambda a: a.reshape(nt, 1, MOE_TILE)
    smem = lambda f: pl.BlockSpec((1, 1, MOE_TILE), f, memory_space=pltpu.SMEM)
    tile_rows = MOE_TILE * SUBLANES
    return pl.pallas_call(
        _experts_kernel,
        grid_spec=pltpu.PrefetchScalarGridSpec(
            num_scalar_prefetch=1, grid=(nt,),
            in_specs=[smem(lambda i, te: (i, 0, 0)),
                      smem(lambda i, te: (jnp.minimum(i + 1, nt - 1), 0, 0)),
                      smem(lambda i, te: (i, 0, 0)),
                      pl.BlockSpec(memory_space=pl.ANY),
                      pl.BlockSpec((1, d, D_EXPERT), wmap), pl.BlockSpec((1, d, D_EXPERT), wmap),
                      pl.BlockSpec((1, D_EXPERT, d), wmap)],
            out_specs=pl.BlockSpec(memory_space=pl.ANY),
            scratch_shapes=[pltpu.VMEM((2 * tile_rows, LANES), F32), pltpu.VMEM((2 * tile_rows, LANES), F32),
                            pltpu.SemaphoreType.DMA((2,)), pltpu.SemaphoreType.DMA((2,))]),
        out_shape=jax.ShapeDtypeStruct((nt * tile_rows, LANES), F32),
        compiler_params=pltpu.CompilerParams(dimension_semantics=("arbitrary",), vmem_limit_bytes=40 * MIB,
                                             has_side_effects=True),
        name="moe_experts",
    )(tile_expert, idx(src_tok), idx(src_tok), idx(dst_row), h, wg, wu, wd)


def _combine_kernel(route_ref, x_ref, g2_ref, fg_ref, ys_ref, o_ref, *, tc, final):
    rt = route_ref[0]
    g0, g1 = rt[:, 2:3], rt[:, 3:4]
    cols = [g0 * ys_ref[pl.ds(s, tc, stride=2 * SUBLANES), :]
            + g1 * ys_ref[pl.ds(SUBLANES + s, tc, stride=2 * SUBLANES), :] for s in range(SUBLANES)]
    x = x_ref[0] + g2_ref[0] * jnp.concatenate(cols, axis=-1)
    if final:
        x = _rms(x, fg_ref[...], x.shape[-1])
    o_ref[0] = x


def _combine(route, x, g2, fg, ys, tok_off, final):
    bsz, n, d = x.shape
    tc = min(256, n)
    nt = n // tc
    row = lambda i, t: (i, t, 0)
    blk = tok_off // tc
    return pl.pallas_call(
        functools.partial(_combine_kernel, tc=tc, final=final),
        grid=(bsz, nt),
        in_specs=[pl.BlockSpec((1, tc, LANES), row), pl.BlockSpec((1, tc, d), row),
                  pl.BlockSpec((1, 1, d), lambda i, t: (i, 0, 0)), pl.BlockSpec((1, d), lambda i, t: (0, 0)),
                  pl.BlockSpec((2 * tc * SUBLANES, LANES), lambda i, t: (blk + i * nt + t, 0))],
        out_specs=pl.BlockSpec((1, tc, d), row),
        out_shape=jax.ShapeDtypeStruct((bsz, n, d), F32),
        compiler_params=_cparams(("parallel", "parallel"), 32),
        name="moe_combine",
    )(route, x, g2, fg.reshape(1, d), ys)


def _moe_plan(eids):
    a = eids.shape[0]
    order = jnp.argsort(eids, stable=True).astype(I32)
    counts = (eids[:, None] == jnp.arange(N_EXPERTS, dtype=I32)[None, :]).astype(I32).sum(axis=0)
    padded = ((counts + MOE_TILE - 1) // MOE_TILE) * MOE_TILE
    ends = jnp.cumsum(padded)
    first = jnp.cumsum(counts) - counts
    nt = a // MOE_TILE + N_EXPERTS
    starts = jnp.arange(nt, dtype=I32) * MOE_TILE
    tile_expert = jnp.minimum((starts[:, None] >= ends[None, :]).astype(I32).sum(axis=1), N_EXPERTS - 1)
    tile_pos = starts - (ends - padded)[tile_expert]
    tile_real = jnp.where(starts < ends[-1], jnp.clip(counts[tile_expert] - tile_pos, 0, MOE_TILE), 0)
    slot = jnp.arange(nt * MOE_TILE, dtype=I32)
    e = jnp.repeat(tile_expert, MOE_TILE)
    pos = slot - (ends - padded)[e]
    real = (slot % MOE_TILE) < jnp.repeat(tile_real, MOE_TILE)
    assign = order[jnp.clip(first[e] + pos, 0, a - 1)]
    src_tok = jnp.where(real, assign // 2, 0)
    pad_rank = jnp.cumsum(jnp.logical_not(real).astype(I32)) - 1
    dst_row = jnp.where(real, assign, a + pad_rank)
    return src_tok, dst_row, tile_expert.astype(I32)


def _moe(streams, wg, wu, wd, final_g, final):
    eids = jnp.concatenate([s[2][..., :2].reshape(-1) for s in streams]).astype(I32)
    src_tok, dst_row, tile_expert = _moe_plan(eids)
    h = streams[0][1] if len(streams) == 1 else jnp.concatenate([s[1] for s in streams], axis=0)
    ys = _experts(tile_expert, src_tok, dst_row, h, wg, wu, wd)
    outs, off = [], 0
    for x, _, route, g2 in streams:
        outs.append(_combine(route, x, g2, final_g, ys, off, final))
        off += x.shape[0] * x.shape[1]
    return outs


def _rope_tables(n, half, offset, identity):
    cs = jnp.ones((n, LANES), F32)
    z = jnp.zeros((n, LANES), F32)
    if identity:
        return cs, z, z
    rows = (jnp.arange(n, dtype=I32) // GRID_W).astype(F32)
    cols = (jnp.arange(n, dtype=I32) % GRID_W).astype(F32)
    freqs = ROPE_THETA ** (-jnp.arange(half, dtype=F32) / half)
    ar, ac = rows[:, None] * freqs, cols[:, None] * freqs
    zero = jnp.zeros((n, half), F32)
    cos4 = jnp.concatenate([jnp.cos(ar), jnp.cos(ar), jnp.cos(ac), jnp.cos(ac)], axis=1)
    s1 = jnp.concatenate([-jnp.sin(ar), zero, -jnp.sin(ac), zero], axis=1)
    s2 = jnp.concatenate([zero, jnp.sin(ar), zero, jnp.sin(ac)], axis=1)
    place = lambda base, t: lax.dynamic_update_slice(base, t, (0, offset))
    return place(cs, cos4), place(z, s1), place(z, s2)


def _pad_heads(w, nheads, width, left=0):
    k = w.shape[0]
    w = w.reshape(k, nheads, width)
    w = jnp.pad(w, ((0, 0), (0, 0), (left, LANES - width - left)))
    return w.reshape(k, nheads * LANES)


def _lane_pad(v, left=0):
    return jnp.pad(v, (left, LANES - v.shape[0] - left)).reshape(1, LANES)


def kernel(x, c, ctx, c_ctx, w_mod, b_mod, norm_g, e_w_in, e_hy_conv_w, e_hy_conv_b, e_hy_w1, e_hy_b1, e_hy_w2, e_hy_b2, e_hy_w3, e_hy_freq, e_hy_fbias, e_qk_g, e_w_out, o_w_in, o_q_norm_g, o_w_uq, o_kv_norm_g, o_w_ukv, o_s5_a_re, o_s5_a_im, o_s5_log_dt, o_s5_b_re, o_s5_b_im, o_s5_c_re, o_s5_c_im, o_s5_d, o_glu_w, o_glu_b, o_w_out, moe_w_rg, moe_w_re, moe_w_gate, moe_w_up, moe_w_down, final_g):
    bsz, n, d = x.shape
    depth = w_mod.shape[0]
    xc = ctx
    ncx = ctx.shape[1]
    cc = jnp.zeros((bsz + 8, d), F32).at[:bsz].set(c).at[bsz].set(c_ctx)

    for i in range(depth):
        j = i // 2
        last = i == depth - 1
        mods = _modulation(cc, w_mod[i], b_mod[i])
        lat = [m.reshape(bsz, 1, d) for m in jnp.split(mods[:bsz], 6, axis=-1)]
        cxm = [jnp.broadcast_to(m.reshape(1, 1, d), (bsz, 1, d)) for m in jnp.split(mods[bsz], 6, axis=-1)]

        if i % 2 == 0:
            wq = _pad_heads(e_w_in[j][:, 3 * A_WIDTH:3 * A_WIDTH + B_Q_HEADS * HEAD_DIM], B_Q_HEADS, HEAD_DIM)
            k0 = 3 * A_WIDTH + B_Q_HEADS * HEAD_DIM
            wk = _pad_heads(e_w_in[j][:, k0:k0 + B_KV_HEADS * HEAD_DIM], B_KV_HEADS, HEAD_DIM)
            wv = _pad_heads(e_w_in[j][:, k0 + B_KV_HEADS * HEAD_DIM:], B_KV_HEADS, HEAD_DIM)
            w_in = jnp.concatenate([e_w_in[j][:, :3 * A_WIDTH], wq, wk, wv], axis=1).astype(BF16)
            qg, kg = _lane_pad(e_qk_g[j, 0]), _lane_pad(e_qk_g[j, 1])
            filt = (e_hy_w1[j], e_hy_b1[j], e_hy_w2[j], e_hy_b2[j], e_hy_w3[j], e_hy_freq[j])
            qscale = HEAD_DIM ** -0.5 * LOG2E
            hy_l, q_l, k_l, v_l = _even_in(x, norm_g[i, 0], lat[0], lat[1], w_in, qg, kg,
                                           _rope_tables(n, HEAD_DIM // 4, 0, False), qscale)
            hy_c, q_c, k_c, v_c = _even_in(xc, norm_g[i, 0], cxm[0], cxm[1], w_in, qg, kg,
                                           _rope_tables(ncx, HEAD_DIM // 4, 0, True), qscale)
            rep = B_Q_HEADS // B_KV_HEADS
            att_l = _attention(q_l, [k_c, k_l], [v_c, v_l], rep, 1)
            mix_l = (_hyena(hy_l, e_hy_conv_w[j], e_hy_conv_b[j], filt, e_hy_fbias[j]), att_l)
            if not last:
                att_c = _attention(q_c, [k_c], [v_c], rep, 1)
                mix_c = (_hyena(hy_c, e_hy_conv_w[j], e_hy_conv_b[j], filt, e_hy_fbias[j]), att_c)
            w_out = e_w_out[j]
            split = A_WIDTH
        else:
            w1 = o_w_in[j]
            c0, c1, c2 = MLA_Q_RANK, MLA_Q_RANK + MLA_KV_RANK, MLA_Q_RANK + MLA_KV_RANK + MLA_ROPE
            w_kr = jnp.pad(w1[:, c1:c2], ((0, 0), (MLA_NOPE, LANES - MLA_NOPE - MLA_ROPE)))
            w_in = jnp.concatenate([w1[:, :c1], w_kr, w1[:, c2:]], axis=1).astype(BF16)
            wuq = _pad_heads(o_w_uq[j], MLA_HEADS, MLA_NOPE + MLA_ROPE).astype(BF16)
            wkv = o_w_ukv[j].reshape(MLA_KV_RANK, MLA_HEADS, MLA_NOPE + MLA_V)
            wuk = _pad_heads(wkv[:, :, :MLA_NOPE].reshape(MLA_KV_RANK, -1), MLA_HEADS, MLA_NOPE).astype(BF16)
            wuv = _pad_heads(wkv[:, :, MLA_NOPE:].reshape(MLA_KV_RANK, -1), MLA_HEADS, MLA_V).astype(BF16)
            qg = o_q_norm_g[j].reshape(1, MLA_Q_RANK)
            kvg = o_kv_norm_g[j].reshape(1, MLA_KV_RANK)
            args = (w_in, qg, kvg, wuq, wuk, wuv)
            qscale = (MLA_NOPE + MLA_ROPE) ** -0.5 * LOG2E
            u_l, k_l, v_l, q_l = _odd_in(x, norm_g[i, 0], lat[0], lat[1], *args,
                                         _rope_tables(n, MLA_ROPE // 4, MLA_NOPE, False), True, qscale)
            outs_c = _odd_in(xc, norm_g[i, 0], cxm[0], cxm[1], *args,
                             _rope_tables(ncx, MLA_ROPE // 4, MLA_NOPE, True), not last, qscale)
            u_c, k_c, v_c = outs_c[:3]
            att_l = _attention(q_l, [k_c, k_l], [v_c, v_l], 2, 2)
            mats = _s5_matrices(o_s5_a_re[j], o_s5_a_im[j], o_s5_log_dt[j], o_s5_b_re[j], o_s5_b_im[j],
                                o_s5_c_re[j], o_s5_c_im[j], o_s5_d[j])
            glu_w = o_glu_w[j].astype(BF16)
            mix_l = (att_l, _s5_mixer(u_c, u_l, mats, glu_w, o_glu_b[j]))
            if not last:
                raise NotImplementedError("an odd layer that is not the last needs the context S5 readout")
            w_out = o_w_out[j]
            split = MLA_HEADS * MLA_V

        wa, wb = w_out[:split].astype(BF16), w_out[split:].astype(BF16)
        router = jnp.pad(jnp.concatenate([moe_w_rg[i], moe_w_re[i]], axis=1),
                         ((0, 0), (0, LANES - N_GROUPS - N_EXPERTS)))
        rh = router.astype(BF16)
        rl = (router - rh.astype(F32)).astype(BF16)
        wg, wu, wd = (moe_w_gate[i].astype(BF16), moe_w_up[i].astype(BF16), moe_w_down[i].astype(BF16))

        x, h_l, route_l = _proj_res(*mix_l, wa, wb, x, lat[2], norm_g[i, 1], lat[3], lat[4], rh, rl)
        streams = [(x, h_l, route_l, lat[5])]
        if not last:
            xc, h_c, route_c = _proj_res(*mix_c, wa, wb, xc, cxm[2], norm_g[i, 1], cxm[3], cxm[4], rh, rl)
            streams.append((xc, h_c, route_c, cxm[5]))
        outs = _moe(streams, wg, wu, wd, final_g, last)
        x = outs[0]
        if not last:
            xc = outs[1]
    return x
```

```python
import functools
import math

import jax
import jax.numpy as jnp
from jax import lax
from jax.experimental import pallas as pl
from jax.experimental.pallas import tpu as pltpu

F32 = jnp.float32
BF16 = jnp.bfloat16
I32 = jnp.int32

D_MODEL = 1024
GRID_W = 64
HEAD_DIM = 64
A_WIDTH = D_MODEL // 2
B_Q_HEADS = (D_MODEL - A_WIDTH) // HEAD_DIM
B_KV_HEADS = max(1, B_Q_HEADS // 4)
HY_ORDER = 2
HY_BANDS = 16
HY_TARGET = 1e-2
HY_MAX_DECAY = math.log(1.0 / HY_TARGET) / 0.3
HY_MIN_DECAY = math.log(1.0 / HY_TARGET) / 1.5
MLA_HEADS = 8
MLA_NOPE = 64
MLA_ROPE = 32
MLA_V = 64
MLA_Q_RANK = D_MODEL // 4
MLA_KV_RANK = D_MODEL // 8
S5_WIDTH = D_MODEL - MLA_HEADS * MLA_V
S5_GROUP = 16
S5_GROUPS = S5_WIDTH // S5_GROUP
S5_STATE = 64
S5_CHUNK = 16
S5_TILE_GROUPS = 128 // S5_GROUP
N_GROUPS = 4
EXPERTS_PER_GROUP = 8
N_EXPERTS = N_GROUPS * EXPERTS_PER_GROUP
D_EXPERT = D_MODEL // 4
ROPE_THETA = 10000.0
EPS = 1e-6
LOG2E = 1.4426950408889634

LANES = 128
SUBLANES = 8
MOE_TILE = 256
HIGHEST = lax.Precision.HIGHEST
MIB = 1024 * 1024


def _cparams(sem, vmem_mib):
    return pltpu.CompilerParams(dimension_semantics=sem, vmem_limit_bytes=vmem_mib * MIB)


def _dot(a, b):
    return jnp.dot(a, b, preferred_element_type=F32)


def _dot_nt(a, b):
    return lax.dot_general(a, b, (((1,), (1,)), ((), ())), preferred_element_type=F32)


def _split_bf16(a):
    hi = a.astype(BF16)
    lo = (a - hi.astype(F32)).astype(BF16)
    return hi, lo


def _dot3(ah, al, bh, bl):
    return _dot(ah, bh) + (_dot(ah, bl) + _dot(al, bh))


def _sigmoid(x):
    return 1.0 / (1.0 + jnp.exp(-x))


def _rms(x, g, width):
    ms = jnp.sum(x * x, axis=-1, keepdims=True) * (1.0 / width)
    return x * lax.rsqrt(ms + EPS) * g


def _modnorm(x, g, shift, scale):
    return _rms(x, g, x.shape[-1]) * (1.0 + scale) + shift


def _rope(t, cs, s1, s2, half):
    return t * cs + pltpu.roll(t, LANES - half, 1) * s1 + pltpu.roll(t, half, 1) * s2


def _store_token_tiles(ref, base_tok, val):
    n = val.shape[0]
    for s in range(SUBLANES):
        ref[pl.ds(base_tok * SUBLANES + s, n, stride=SUBLANES), :] = val[:, LANES * s: LANES * (s + 1)]


def _load_token_tiles(ref, base_tok, n):
    return jnp.concatenate([ref[pl.ds(base_tok * SUBLANES + s, n, stride=SUBLANES), :]
                            for s in range(SUBLANES)], axis=-1)


def _short_conv(u, w, b):
    n = u.shape[0]
    row = lax.broadcasted_iota(I32, u.shape, 0)
    prev = jnp.where(row == 0, 0.0, pltpu.roll(u, 1, 0))
    nxt = jnp.where(row == n - 1, 0.0, pltpu.roll(u, n - 1, 0))
    return prev * w[0:1] + u * w[1:2] + nxt * w[2:3] + b


def _mod_kernel(c_ref, w_ref, b_ref, o_ref):
    c = c_ref[...]
    s = c * _sigmoid(c)
    sh, sl = _split_bf16(s)
    wh, wl = _split_bf16(w_ref[...])
    o_ref[...] = _dot3(sh, sl, wh, wl) + b_ref[...]


def _modulation(cc, w, b):
    r, d = cc.shape
    n = w.shape[1]
    tn = 1536
    return pl.pallas_call(
        _mod_kernel,
        grid=(n // tn,),
        in_specs=[pl.BlockSpec((r, d), lambda j: (0, 0)),
                  pl.BlockSpec((d, tn), lambda j: (0, j)),
                  pl.BlockSpec((1, tn), lambda j: (0, j))],
        out_specs=pl.BlockSpec((r, tn), lambda j: (0, j)),
        out_shape=jax.ShapeDtypeStruct((r, n), F32),
        compiler_params=_cparams(("parallel",), 48),
        name="modulation",
    )(cc, w, b.reshape(1, n))


def _with_ones(v):
    lane = lax.broadcasted_iota(I32, v.shape, 1) % LANES
    return jnp.where(lane >= HEAD_DIM, 1.0, v)


def _even_in_kernel(x_ref, g_ref, sh_ref, sc_ref, w_ref, qg_ref, kg_ref, cs_ref, s1_ref, s2_ref,
                    hy_ref, q_ref, k_ref, v_ref, *, qscale):
    xn = _modnorm(x_ref[0], g_ref[...], sh_ref[0], sc_ref[0]).astype(BF16)
    p = _dot(xn, w_ref[...])
    hw = 3 * A_WIDTH
    hy_ref[0] = p[:, :hw].astype(BF16)
    cs, s1, s2 = cs_ref[...], s1_ref[...], s2_ref[...]
    for h in range(B_Q_HEADS):
        t = _rms(p[:, hw + LANES * h: hw + LANES * (h + 1)], qg_ref[...], HEAD_DIM)
        q_ref[0, :, LANES * h: LANES * (h + 1)] = (_rope(t, cs, s1, s2, HEAD_DIM // 4) * qscale).astype(BF16)
    kw = hw + LANES * B_Q_HEADS
    for h in range(B_KV_HEADS):
        t = _rms(p[:, kw + LANES * h: kw + LANES * (h + 1)], kg_ref[...], HEAD_DIM)
        k_ref[0, :, LANES * h: LANES * (h + 1)] = _rope(t, cs, s1, s2, HEAD_DIM // 4).astype(BF16)
    vw = kw + LANES * B_KV_HEADS
    v_ref[0] = _with_ones(p[:, vw:]).astype(BF16)


def _even_in(x, g, shift, scale, w, qg, kg, tabs, qscale):
    bsz, n, d = x.shape
    tm = min(256, n)
    nw = w.shape[1]
    row = lambda b, t: (b, t, 0)
    per_b = lambda b, t: (b, 0, 0)
    fixed = lambda b, t: (0, 0)
    tab = pl.BlockSpec((tm, LANES), lambda b, t: (t, 0))
    return pl.pallas_call(
        functools.partial(_even_in_kernel, qscale=qscale),
        grid=(bsz, n // tm),
        in_specs=[pl.BlockSpec((1, tm, d), row), pl.BlockSpec((1, d), fixed),
                  pl.BlockSpec((1, 1, d), per_b), pl.BlockSpec((1, 1, d), per_b),
                  pl.BlockSpec((d, nw), fixed), pl.BlockSpec((1, LANES), fixed),
                  pl.BlockSpec((1, LANES), fixed), tab, tab, tab],
        out_specs=[pl.BlockSpec((1, tm, 3 * A_WIDTH), row),
                   pl.BlockSpec((1, tm, LANES * B_Q_HEADS), row),
                   pl.BlockSpec((1, tm, LANES * B_KV_HEADS), row),
                   pl.BlockSpec((1, tm, LANES * B_KV_HEADS), row)],
        out_shape=[jax.ShapeDtypeStruct((bsz, n, 3 * A_WIDTH), BF16),
                   jax.ShapeDtypeStruct((bsz, n, LANES * B_Q_HEADS), BF16),
                   jax.ShapeDtypeStruct((bsz, n, LANES * B_KV_HEADS), BF16),
                   jax.ShapeDtypeStruct((bsz, n, LANES * B_KV_HEADS), BF16)],
        compiler_params=_cparams(("parallel", "parallel"), 48),
        name="even_in",
    )(x, g.reshape(1, d), shift, scale, w, qg, kg, *tabs)


def _odd_in_kernel(x_ref, g_ref, sh_ref, sc_ref, w_ref, qg_ref, kvg_ref, wuq_ref, wuk_ref, wuv_ref,
                   cs_ref, s1_ref, s2_ref, *out_refs, need_q, qscale):
    if need_q:
        u_ref, k_ref, v_ref, q_ref = out_refs
    else:
        u_ref, k_ref, v_ref = out_refs
    xn = _modnorm(x_ref[0], g_ref[...], sh_ref[0], sc_ref[0]).astype(BF16)
    p = _dot(xn, w_ref[...])
    c0, c1, c2 = MLA_Q_RANK, MLA_Q_RANK + MLA_KV_RANK, MLA_Q_RANK + MLA_KV_RANK + LANES
    u_ref[0] = p[:, c2:]
    cs, s1, s2 = cs_ref[...], s1_ref[...], s2_ref[...]
    ckv = _rms(p[:, c0:c1], kvg_ref[...], MLA_KV_RANK).astype(BF16)
    kn = _dot(ckv, wuk_ref[...])
    v_ref[0] = _with_ones(_dot(ckv, wuv_ref[...])).astype(BF16)
    kr = _rope(p[:, c1:c2], cs, s1, s2, MLA_ROPE // 4)
    for h in range(MLA_HEADS):
        k_ref[0, :, LANES * h: LANES * (h + 1)] = (kn[:, LANES * h: LANES * (h + 1)] + kr).astype(BF16)
    if need_q:
        cq = _rms(p[:, :c0], qg_ref[...], MLA_Q_RANK).astype(BF16)
        q = _dot(cq, wuq_ref[...])
        for h in range(MLA_HEADS):
            t = q[:, LANES * h: LANES * (h + 1)]
            q_ref[0, :, LANES * h: LANES * (h + 1)] = (_rope(t, cs, s1, s2, MLA_ROPE // 4) * qscale).astype(BF16)


def _odd_in(x, g, shift, scale, w, qg, kvg, wuq, wuk, wuv, tabs, need_q, qscale):
    bsz, n, d = x.shape
    tm = min(256, n)
    hw = LANES * MLA_HEADS
    row = lambda b, t: (b, t, 0)
    per_b = lambda b, t: (b, 0, 0)
    fixed = lambda b, t: (0, 0)
    tab = pl.BlockSpec((tm, LANES), lambda b, t: (t, 0))
    widths = [S5_WIDTH, hw, hw] + ([hw] if need_q else [])
    return pl.pallas_call(
        functools.partial(_odd_in_kernel, need_q=need_q, qscale=qscale),
        grid=(bsz, n // tm),
        in_specs=[pl.BlockSpec((1, tm, d), row), pl.BlockSpec((1, d), fixed),
                  pl.BlockSpec((1, 1, d), per_b), pl.BlockSpec((1, 1, d), per_b),
                  pl.BlockSpec(w.shape, fixed), pl.BlockSpec((1, MLA_Q_RANK), fixed),
                  pl.BlockSpec((1, MLA_KV_RANK), fixed), pl.BlockSpec(wuq.shape, fixed),
                  pl.BlockSpec(wuk.shape, fixed), pl.BlockSpec(wuv.shape, fixed), tab, tab, tab],
        out_specs=[pl.BlockSpec((1, tm, wd), row) for wd in widths],
        out_shape=[jax.ShapeDtypeStruct((bsz, n, wd), F32 if k == 0 else BF16) for k, wd in enumerate(widths)],
        compiler_params=_cparams(("parallel", "parallel"), 48),
        name="odd_in_q" if need_q else "odd_in",
    )(x, g.reshape(1, d), shift, scale, w, qg, kvg, wuq, wuk, wuv, *tabs)


def _hy_fwd_kernel(wc_ref, ws_ref, u_ref, cw_ref, cb_ref, ka_ref, kb_ref, ya_ref, yb_ref, ubf_ref,
                   *, conv, tf):
    f = pl.program_id(1)

    @pl.when(f == 0)
    def _():
        u = u_ref[0].astype(F32)
        if conv:
            u = _short_conv(u, cw_ref[...], cb_ref[...])
        ubf_ref[...] = u.astype(BF16)

    ub = ubf_ref[...]
    a = _dot(wc_ref[...], ub)
    b = _dot(ws_ref[...], ub)
    ka, kb = ka_ref[...], kb_ref[...]
    first = (lax.broadcasted_iota(I32, a.shape, 0) + f * tf) == 0
    ya_ref[0] = jnp.where(first, a * ka, a * ka - b * kb).astype(BF16)
    yb_ref[0] = jnp.where(first, b * kb, a * kb + b * ka).astype(BF16)


def _hy_fwd(wc, ws, u, col, cw, cb, ka, kb, conv):
    bsz, n, _ = u.shape
    c = A_WIDTH
    tf = min(512, n)
    return pl.pallas_call(
        functools.partial(_hy_fwd_kernel, conv=conv, tf=tf),
        grid=(bsz, n // tf),
        in_specs=[pl.BlockSpec((tf, n), lambda b, f: (f, 0)), pl.BlockSpec((tf, n), lambda b, f: (f, 0)),
                  pl.BlockSpec((1, n, c), lambda b, f: (b, 0, col)),
                  pl.BlockSpec((3, c), lambda b, f: (0, 0)), pl.BlockSpec((1, c), lambda b, f: (0, 0)),
                  pl.BlockSpec((tf, c), lambda b, f: (f, 0)), pl.BlockSpec((tf, c), lambda b, f: (f, 0))],
        out_specs=[pl.BlockSpec((1, tf, c), lambda b, f: (b, f, 0))] * 2,
        out_shape=[jax.ShapeDtypeStruct((bsz, n, c), BF16)] * 2,
        scratch_shapes=[pltpu.VMEM((n, c), BF16)],
        compiler_params=_cparams(("parallel", "arbitrary"), 48),
        name="hyena_fwd",
    )(wc, ws, u, cw, cb, ka, kb)


def _hy_inv_kernel(ci_ref, si_ref, ya_ref, yb_ref, a_ref, g_ref, cwa_ref, cba_ref, cwg_ref, cbg_ref,
                   bias_ref, o_ref, a_sc, g_sc, *, conv_a, tt):
    t = pl.program_id(1)

    @pl.when(t == 0)
    def _():
        a = a_ref[0].astype(F32)
        if conv_a:
            a = _short_conv(a, cwa_ref[...], cba_ref[...])
        a_sc[...] = a
        g_sc[...] = _short_conv(g_ref[0].astype(F32), cwg_ref[...], cbg_ref[...])

    y = _dot(ci_ref[...], ya_ref[0]) + _dot(si_ref[...], yb_ref[0])
    rows = pl.ds(pl.multiple_of(t * tt, tt), tt)
    y = y + a_sc[rows, :] * bias_ref[...]
    o_ref[0] = (g_sc[rows, :] * y).astype(BF16)


def _hy_inv(ci, si, ya, yb, a, a_col, g, g_col, cwa, cba, cwg, cbg, bias, conv_a):
    bsz, n, c = ya.shape
    tt = min(256, n)
    whole = lambda b, t: (b, 0, 0)
    fixed = lambda b, t: (0, 0)
    return pl.pallas_call(
        functools.partial(_hy_inv_kernel, conv_a=conv_a, tt=tt),
        grid=(bsz, n // tt),
        in_specs=[pl.BlockSpec((tt, n), lambda b, t: (t, 0)), pl.BlockSpec((tt, n), lambda b, t: (t, 0)),
                  pl.BlockSpec((1, n, c), whole), pl.BlockSpec((1, n, c), whole),
                  pl.BlockSpec((1, n, c), lambda b, t: (b, 0, a_col)),
                  pl.BlockSpec((1, n, c), lambda b, t: (b, 0, g_col)),
                  pl.BlockSpec((3, c), fixed), pl.BlockSpec((1, c), fixed),
                  pl.BlockSpec((3, c), fixed), pl.BlockSpec((1, c), fixed), pl.BlockSpec((1, c), fixed)],
        out_specs=pl.BlockSpec((1, tt, c), lambda b, t: (b, t, 0)),
        out_shape=jax.ShapeDtypeStruct((bsz, n, c), BF16),
        scratch_shapes=[pltpu.VMEM((n, c), F32), pltpu.VMEM((n, c), F32)],
        compiler_params=_cparams(("parallel", "arbitrary"), 56),
        name="hyena_inv",
    )(ci, si, ya, yb, a, g, cwa, cba, cwg, cbg, bias)


def _dft_mats(n):
    big = 2 * n
    f = jnp.arange(n, dtype=I32)[:, None]
    s = jnp.arange(n, dtype=I32)[None, :]
    ang = ((f * s) % big).astype(F32) * (2.0 * math.pi / big)
    cos, sin = jnp.cos(ang), jnp.sin(ang)
    alt = jnp.where(jnp.arange(n) % 2 == 0, 1.0, -1.0).astype(F32)
    wc = cos
    ws = jnp.where(f == 0, alt[None, :], -sin)
    ci = jnp.where(s == 0, 1.0 / big, (2.0 / big) * cos)
    si = jnp.where(s == 0, alt[:, None] / big, -(2.0 / big) * sin)
    return wc.astype(BF16), ws.astype(BF16), ci.astype(BF16), si.astype(BF16)


def _hyena_filter_spectrum(n, w1, b1, w2, b2, w3, freq):
    t = jnp.arange(n, dtype=F32)
    t_norm = t / max(n - 1, 1)
    bands = jnp.linspace(1e-4, HY_BANDS - 1, HY_BANDS, dtype=F32)
    ang = (2.0 * math.pi / n) * t[:, None] * bands
    z = jnp.concatenate([t_norm[:, None], jnp.cos(ang), jnp.sin(ang)], axis=-1)
    hdn = jnp.sin(freq * (jnp.dot(z, w1, precision=HIGHEST) + b1))
    hdn = jnp.sin(freq * (jnp.dot(hdn, w2, precision=HIGHEST) + b2))
    taps = jnp.dot(hdn, w3, precision=HIGHEST).reshape(n, 2, HY_ORDER, A_WIDTH)
    deltas = jnp.linspace(HY_MIN_DECAY, HY_MAX_DECAY, A_WIDTH, dtype=F32)
    taps = taps * jnp.exp(-t_norm[:, None] * deltas)[:, None, None, :]
    fwd, bwd = taps[:, 0], taps[:, 1]
    circ = jnp.concatenate([fwd, jnp.zeros((1, HY_ORDER, A_WIDTH), F32), bwd[:0:-1]], axis=0)
    circ = circ * lax.rsqrt(jnp.sum(circ * circ, axis=0, keepdims=True) + EPS)
    kf = jnp.fft.rfft(circ, axis=0)
    ka = jnp.real(kf[:n]).astype(F32)
    kb = jnp.concatenate([jnp.real(kf[n:n + 1]), jnp.imag(kf[1:n])], axis=0).astype(F32)
    return ka, kb


def _hyena(hy, cw, cb, filt, fbias):
    n = hy.shape[1]
    c = A_WIDTH
    wc, ws, ci, si = _dft_mats(n)
    ka, kb = _hyena_filter_spectrum(n, *filt)
    cws = [cw[:, i * c:(i + 1) * c] for i in range(3)]
    cbs = [cb[i * c:(i + 1) * c].reshape(1, c) for i in range(3)]
    ya, yb = _hy_fwd(wc, ws, hy, 2, cws[2], cbs[2], ka[:, 0], kb[:, 0], True)
    z = _hy_inv(ci, si, ya, yb, hy, 2, hy, 0, cws[2], cbs[2], cws[0], cbs[0], fbias[0:1], True)
    ya, yb = _hy_fwd(wc, ws, z, 0, cws[2], cbs[2], ka[:, 1], kb[:, 1], False)
    return _hy_inv(ci, si, ya, yb, z, 0, hy, 1, cws[2], cbs[2], cws[1], cbs[1], fbias[1:2], False)


def _attn_kernel(*refs, nseg, hps, kps):
    q_ref = refs[0]
    k_refs = refs[1:1 + nseg]
    v_refs = refs[1 + nseg:1 + 2 * nseg]
    o_ref = refs[1 + 2 * nseg]
    low = lax.broadcasted_iota(I32, (q_ref.shape[1], LANES), 1) < HEAD_DIM
    for pair in range(hps // 2):
        accs = []
        for half in range(2):
            i = 2 * pair + half
            kv = (i * kps) // hps
            q = q_ref[0, :, LANES * i: LANES * (i + 1)]
            ss = [_dot_nt(q, k[0, :, LANES * kv: LANES * (kv + 1)]) for k in k_refs]
            m = ss[0].max(axis=-1, keepdims=True)
            for s in ss[1:]:
                m = jnp.maximum(m, s.max(axis=-1, keepdims=True))
            acc = None
            for s, v in zip(ss, v_refs):
                p = jnp.exp2((s - m).astype(BF16))
                o = _dot(p, v[0, :, LANES * kv: LANES * (kv + 1)])
                acc = o if acc is None else acc + o
            accs.append(acc)
        num = jnp.where(low, accs[0], pltpu.roll(accs[1], HEAD_DIM, 1))
        den = jnp.where(low, pltpu.roll(accs[0], HEAD_DIM, 1), accs[1])
        o_ref[0, :, LANES * pair: LANES * (pair + 1)] = (num / den).astype(BF16)


def _attention(q, ks, vs, hps, kps):
    bsz, lq, qw = q.shape
    nh = qw // LANES
    tq = min(256, lq)
    nseg = len(ks)
    kv_spec = lambda a: pl.BlockSpec((1, a.shape[1], kps * LANES), lambda b, g, t: (b, 0, g))
    return pl.pallas_call(
        functools.partial(_attn_kernel, nseg=nseg, hps=hps, kps=kps),
        grid=(bsz, nh // hps, lq // tq),
        in_specs=[pl.BlockSpec((1, tq, hps * LANES), lambda b, g, t: (b, t, g))]
        + [kv_spec(a) for a in ks] + [kv_spec(a) for a in vs],
        out_specs=pl.BlockSpec((1, tq, hps * HEAD_DIM), lambda b, g, t: (b, t, g)),
        out_shape=jax.ShapeDtypeStruct((bsz, lq, nh * HEAD_DIM), BF16),
        compiler_params=_cparams(("parallel", "parallel", "parallel"), 56),
        name="attention",
    )(q, *ks, *vs)


def _onehot_bf16(cond):
    return jnp.where(cond, 1.0, 0.0).astype(BF16)


def _s5_kernel(uc_ref, ul_ref, w_ref, a_ref, of_ref, or_ref, y_ref, xs_sc, yi_sc, s_sc, x_sc, ys_sc,
               *, bb, ncc, ncl):
    lc, gh, gt = S5_CHUNK, S5_GROUP, S5_TILE_GROUPS
    nc = ncc + ncl
    cw = lc * gh
    hb = gh.bit_length() - 1
    lb = LANES.bit_length() - 1

    for b in range(bb):
        for j in range(lc):
            p, half = divmod(j, 2)
            lanes = slice(half * LANES, (half + 1) * LANES)
            xs_sc[p, b * nc: b * nc + ncc, lanes] = uc_ref[b, pl.ds(j, ncc, stride=lc), :].astype(BF16)
            xs_sc[p, b * nc + ncc: (b + 1) * nc, lanes] = ul_ref[b, pl.ds(j, ncl, stride=lc), :].astype(BF16)

    r_i = lax.broadcasted_iota(I32, (2 * LANES, cw), 0)
    c_i = lax.broadcasted_iota(I32, (2 * LANES, cw), 1)
    same_h = (r_i & (gh - 1)) == (c_i & (gh - 1))
    r_grp = (r_i & (LANES - 1)) >> hb
    c_off = (c_i >> hb) - (r_i >> lb)
    for pr in range(gt // 2):
        rs = []
        for side in range(2):
            q = 2 * pr + side
            u = None
            for p in range(lc // 2):
                sel = _onehot_bf16(same_h & (r_grp == q) & (c_off == 2 * p))
                d = _dot(xs_sc[p], sel)
                u = d if u is None else u + d
            r = _dot(u.astype(BF16), w_ref[q])
            yi_sc[q] = r[:, :cw]
            rs.append(r)
        for kind in range(4):
            lo = cw + kind * LANES
            s_sc[kind, pr] = rs[0][:, lo:lo + LANES] + rs[1][:, lo:lo + LANES]

    a = a_ref[0]
    npair = gt // 2

    def advance(st, c, d):
        rws = pl.ds(c, bb, stride=nc)
        new = []
        for t in range(npair):
            xr, xi = st[2 * t], st[2 * t + 1]
            ar = a[2 * d:2 * d + 1, t * LANES:(t + 1) * LANES]
            ai = a[2 * d + 1:2 * d + 2, t * LANES:(t + 1) * LANES]
            x_sc[2 * d, t, rws, :] = xr
            x_sc[2 * d + 1, t, rws, :] = xi
            new.append(ar * xr - ai * xi + s_sc[2 * d, t, rws, :])
            new.append(ar * xi + ai * xr + s_sc[2 * d + 1, t, rws, :])
        return tuple(new)

    zero = tuple(jnp.zeros((bb, LANES), F32) for _ in range(2 * npair))
    lax.fori_loop(0, nc, lambda c, st: advance(st, c, 0), zero)
    st = lax.fori_loop(0, ncc, lambda i, st: advance(st, ncc - 1 - i, 1), zero)
    lax.fori_loop(0, ncl, lambda i, st: advance(st, nc - 1 - i, 1), st)

    def latent(ref, *idx):
        return jnp.concatenate([ref[idx + (slice(b * nc + ncc, (b + 1) * nc), slice(None))]
                                for b in range(bb)], axis=0)

    for q in range(gt):
        y = latent(yi_sc, q)
        for d, o_ref in ((0, of_ref), (1, or_ref)):
            for part in range(2):
                y = y + _dot(latent(x_sc, 2 * d + part, q // 2).astype(BF16), o_ref[q, part])
        for kt in range(cw // LANES):
            ys_sc[kt, :, q * LANES:(q + 1) * LANES] = y[:, kt * LANES:(kt + 1) * LANES].astype(BF16)

    r2 = lax.broadcasted_iota(I32, (gt * LANES, LANES), 0)
    c2 = lax.broadcasted_iota(I32, (gt * LANES, LANES), 1)
    keep = ((r2 & (gh - 1)) == (c2 & (gh - 1))) & ((r2 >> lb) == (c2 >> hb))
    r_k = (r2 & (LANES - 1)) >> hb
    for k in range(lc):
        kt, kk = divmod(k, gt)
        z = _dot(ys_sc[kt], _onehot_bf16(keep & (r_k == kk)))
        for b in range(bb):
            y_ref[b, pl.ds(k, ncl, stride=lc), :] = z[b * ncl:(b + 1) * ncl, :]


def _s5_scan(u_c, u_l, w, a_tab, of, orr):
    bsz, ncx, c = u_c.shape
    n = u_l.shape[1]
    lc, gt = S5_CHUNK, S5_TILE_GROUPS
    ncc, ncl = ncx // lc, n // lc
    bb = min(4, bsz)
    rows = bb * (ncc + ncl)
    cw = lc * S5_GROUP
    sw = (gt // 2) * LANES
    return pl.pallas_call(
        functools.partial(_s5_kernel, bb=bb, ncc=ncc, ncl=ncl),
        grid=(c // LANES, bsz // bb),
        in_specs=[pl.BlockSpec((bb, ncx, LANES), lambda t, i: (i, 0, t)),
                  pl.BlockSpec((bb, n, LANES), lambda t, i: (i, 0, t)),
                  pl.BlockSpec((gt,) + w.shape[1:], lambda t, i: (t, 0, 0)),
                  pl.BlockSpec((1,) + a_tab.shape[1:], lambda t, i: (t, 0, 0)),
                  pl.BlockSpec((gt,) + of.shape[1:], lambda t, i: (t, 0, 0, 0)),
                  pl.BlockSpec((gt,) + orr.shape[1:], lambda t, i: (t, 0, 0, 0))],
        out_specs=pl.BlockSpec((bb, n, LANES), lambda t, i: (i, 0, t)),
        out_shape=jax.ShapeDtypeStruct((bsz, n, c), F32),
        scratch_shapes=[pltpu.VMEM((lc // 2, rows, 2 * LANES), BF16), pltpu.VMEM((gt, rows, cw), F32),
                        pltpu.VMEM((4, gt // 2, rows, LANES), F32), pltpu.VMEM((4, gt // 2, rows, LANES), F32),
                        pltpu.VMEM((cw // LANES, bb * ncl, gt * LANES), BF16)],
        compiler_params=_cparams(("parallel", "parallel"), 56),
        name="s5_scan",
    )(u_c, u_l, w, a_tab, of, orr)


def _s5_matrices(a_re, a_im, log_dt, b_re, b_im, c_re, c_im, d_skip):
    lc, h, p, g = S5_CHUNK, S5_GROUP, S5_STATE, S5_GROUPS
    dt = jnp.exp(log_dt)[..., None]
    lam_re, lam_im = a_re * dt, a_im * dt
    mag = jnp.exp(lam_re)
    er, ei = mag * jnp.cos(lam_im) - 1.0, mag * jnp.sin(lam_im)
    den = a_re * a_re + a_im * a_im
    co_re = (er * a_re + ei * a_im) / den
    co_im = (ei * a_re - er * a_im) / den
    bb_re = co_re[..., None] * b_re - co_im[..., None] * b_im
    bb_im = co_re[..., None] * b_im + co_im[..., None] * b_re
    m = jnp.arange(lc + 1, dtype=F32)[:, None, None, None]
    pw_mag = jnp.exp(m * lam_re[None])
    pw_re, pw_im = pw_mag * jnp.cos(m * lam_im[None]), pw_mag * jnp.sin(m * lam_im[None])
    ab_re = pw_re[..., None] * bb_re[None] - pw_im[..., None] * bb_im[None]
    ab_im = pw_re[..., None] * bb_im[None] + pw_im[..., None] * bb_re[None]
    kk = (jnp.einsum("dgop,mdgpi->mdgoi", c_re, ab_re[:lc], precision=HIGHEST)
          - jnp.einsum("dgop,mdgpi->mdgoi", c_im, ab_im[:lc], precision=HIGHEST))
    j = jnp.arange(lc)[:, None]
    k = jnp.arange(lc)[None, :]
    lag_f = jnp.clip(k - j, 0, lc - 1)
    lag_r = jnp.clip(j - k, 0, lc - 1)
    tf = jnp.where((k >= j)[:, :, None, None, None], kk[lag_f, 0], 0.0)
    tr = jnp.where((j >= k)[:, :, None, None, None], kk[lag_r, 1], 0.0)
    tw = jnp.transpose(tf + tr, (2, 0, 4, 1, 3))
    dsk = d_skip.reshape(g, h)
    eye = (jnp.eye(lc, dtype=F32)[None, :, None, :, None] * jnp.eye(h, dtype=F32)[None, None, :, None, :]
           * dsk[:, None, :, None, None])
    tw = (tw + eye).reshape(g, lc * h, lc * h)
    idx_f = (lc - 1 - jnp.arange(lc))
    idx_r = jnp.arange(lc)

    odd = (jnp.arange(g) % 2 == 1)

    def place(t, axis):
        z = jnp.zeros_like(t)
        shape = [g] + [1] * (t.ndim - 1)
        return jnp.where(odd.reshape(shape), jnp.concatenate([z, t], axis=axis),
                         jnp.concatenate([t, z], axis=axis))

    def to_state(idx, d):
        re = jnp.transpose(ab_re[idx, d], (1, 0, 3, 2)).reshape(g, lc * h, p)
        im = jnp.transpose(ab_im[idx, d], (1, 0, 3, 2)).reshape(g, lc * h, p)
        return [place(re, 2), place(im, 2)]

    w = jnp.concatenate([tw] + to_state(idx_f, 0) + to_state(idx_r, 1), axis=-1)

    def from_state(idx, d):
        pr, pi = pw_re[idx, d], pw_im[idx, d]
        cr, cim = c_re[d], c_im[d]
        mr = cr[None] * pr[:, :, None, :] - cim[None] * pi[:, :, None, :]
        mi = cr[None] * pi[:, :, None, :] + cim[None] * pr[:, :, None, :]
        rows = lambda m: place(jnp.transpose(m, (1, 3, 0, 2)).reshape(g, p, lc * h), 1)
        return jnp.stack([rows(mr), rows(-mi)], axis=1)

    of = from_state(jnp.arange(lc) + 1, 0)
    orr = from_state(lc - jnp.arange(lc), 1)
    tiles = g // S5_TILE_GROUPS
    a_tab = jnp.stack([pw_re[lc, 0].reshape(tiles, -1), pw_im[lc, 0].reshape(tiles, -1),
                       pw_re[lc, 1].reshape(tiles, -1), pw_im[lc, 1].reshape(tiles, -1)], axis=1)
    return w.astype(BF16), a_tab.astype(F32), of.astype(BF16), orr.astype(BF16)


def _glu_kernel(y_ref, w_ref, b_ref, o_ref):
    y = y_ref[0]
    inner = 0.7978845608028654 * (y + 0.044715 * (y * y * y))
    gl = 0.5 * y * (1.0 + jnp.tanh(inner))
    o_ref[0] = (gl * _sigmoid(_dot(gl.astype(BF16), w_ref[...]) + b_ref[...])).astype(BF16)


def _glu(y, w, b):
    bsz, n, c = y.shape
    tm = min(512, n)
    return pl.pallas_call(
        _glu_kernel,
        grid=(bsz, n // tm),
        in_specs=[pl.BlockSpec((1, tm, c), lambda i, t: (i, t, 0)),
                  pl.BlockSpec((c, c), lambda i, t: (0, 0)), pl.BlockSpec((1, c), lambda i, t: (0, 0))],
        out_specs=pl.BlockSpec((1, tm, c), lambda i, t: (i, t, 0)),
        out_shape=jax.ShapeDtypeStruct((bsz, n, c), BF16),
        compiler_params=_cparams(("parallel", "parallel"), 32),
        name="s5_glu",
    )(y, w, b.reshape(1, c))


def _s5_mixer(u_c, u_l, mats, glu_w, glu_b):
    return _glu(_s5_scan(u_c, u_l, *mats), glu_w, glu_b)


def _route(lg):
    lane = lax.broadcasted_iota(I32, lg.shape, 1).astype(F32)
    neg = jnp.float32(-1e30)
    far = jnp.float32(LANES)
    is_g = lane < N_GROUPS
    lgm = jnp.where(is_g, lg, neg)
    m = lgm.max(axis=-1, keepdims=True)
    gidx = jnp.where(lgm == m, lane, far).min(axis=-1, keepdims=True)
    gp = 1.0 / jnp.where(is_g, jnp.exp(lgm - m), 0.0).sum(axis=-1, keepdims=True)
    lo = N_GROUPS + EXPERTS_PER_GROUP * gidx
    is_e = (lane >= lo) & (lane < lo + EXPERTS_PER_GROUP)
    lem = jnp.where(is_e, lg, neg)
    l0 = lem.max(axis=-1, keepdims=True)
    i0 = jnp.where((lem == l0) & is_e, lane, far).min(axis=-1, keepdims=True)
    lem1 = jnp.where(lane == i0, neg, lem)
    l1 = lem1.max(axis=-1, keepdims=True)
    i1 = jnp.where((lem1 == l1) & is_e & (lane != i0), lane, far).min(axis=-1, keepdims=True)
    e1 = jnp.exp(l1 - l0)
    w0 = 1.0 / (1.0 + e1)
    w1 = e1 * w0
    out = jnp.where(lane == 0, i0 - N_GROUPS, 0.0)
    out = jnp.where(lane == 1, i1 - N_GROUPS, out)
    out = jnp.where(lane == 2, gp * w0, out)
    return jnp.where(lane == 3, gp * w1, out)


def _proj_res_kernel(a_ref, b_ref, wa_ref, wb_ref, x_ref, g1_ref, ng_ref, sh_ref, sc_ref, rh_ref, rl_ref,
                     xo_ref, h_ref, route_ref):
    o = _dot(a_ref[0], wa_ref[...]) + _dot(b_ref[0], wb_ref[...])
    x = x_ref[0] + g1_ref[0] * o
    xo_ref[0] = x
    h = _modnorm(x, ng_ref[...], sh_ref[0], sc_ref[0])
    _store_token_tiles(h_ref, 0, h)
    hh, hl = _split_bf16(h)
    route_ref[0] = _route(_dot3(hh, hl, rh_ref[...], rl_ref[...]))


def _proj_res(a, b, wa, wb, x, g1, ng, shift, scale, rh, rl):
    bsz, n, d = x.shape
    tm = min(256, n)
    row = lambda i, t: (i, t, 0)
    per_b = lambda i, t: (i, 0, 0)
    fixed = lambda i, t: (0, 0)
    return pl.pallas_call(
        _proj_res_kernel,
        grid=(bsz, n // tm),
        in_specs=[pl.BlockSpec((1, tm, a.shape[2]), row), pl.BlockSpec((1, tm, b.shape[2]), row),
                  pl.BlockSpec(wa.shape, fixed), pl.BlockSpec(wb.shape, fixed),
                  pl.BlockSpec((1, tm, d), row), pl.BlockSpec((1, 1, d), per_b), pl.BlockSpec((1, d), fixed),
                  pl.BlockSpec((1, 1, d), per_b), pl.BlockSpec((1, 1, d), per_b),
                  pl.BlockSpec((d, LANES), fixed), pl.BlockSpec((d, LANES), fixed)],
        out_specs=[pl.BlockSpec((1, tm, d), row),
                   pl.BlockSpec((tm * SUBLANES, LANES), lambda i, t: (i * (n // tm) + t, 0)),
                   pl.BlockSpec((1, tm, LANES), row)],
        out_shape=[jax.ShapeDtypeStruct((bsz, n, d), F32),
                   jax.ShapeDtypeStruct((bsz * n * SUBLANES, LANES), F32),
                   jax.ShapeDtypeStruct((bsz, n, LANES), F32)],
        compiler_params=_cparams(("parallel", "parallel"), 40),
        name="proj_res",
    )(a, b, wa, wb, x, g1, ng.reshape(1, d), shift, scale, rh, rl)


def _token_copy(src, src_tok, dst, dst_tok, sem):
    rows = lambda t: pl.ds(pl.multiple_of(t * SUBLANES, SUBLANES), SUBLANES)
    return pltpu.make_async_copy(src.at[rows(src_tok)], dst.at[rows(dst_tok)], sem)


def _tile_copy(src, src_tok, dst, dst_tok, sem):
    n = MOE_TILE * SUBLANES
    rows = lambda t: pl.ds(pl.multiple_of(t * SUBLANES, SUBLANES), n)
    return pltpu.make_async_copy(src.at[rows(src_tok)], dst.at[rows(dst_tok)], sem)


def _experts_kernel(te_ref, src_ref, srcn_ref, dst_ref, h_ref, wg_ref, wu_ref, wd_ref, ys_ref,
                    xbuf, obuf, gsem, ssem):
    del te_ref
    i = pl.program_id(0)
    nt = pl.num_programs(0)
    slot = i % 2
    other = 1 - slot

    def start_gather(idx_ref, s):
        def body(j, c):
            for k in range(2):
                r = 2 * j + k
                _token_copy(h_ref, idx_ref[0, 0, r], xbuf, s * MOE_TILE + r, gsem.at[s]).start(priority=k)
            return c
        lax.fori_loop(0, MOE_TILE // 2, body, 0, unroll=4)

    @pl.when(i == 0)
    def _():
        start_gather(src_ref, 0)

    @pl.when(i + 1 < nt)
    def _():
        start_gather(srcn_ref, other)

    base = slot * MOE_TILE
    _tile_copy(h_ref, 0, xbuf, base, gsem.at[slot]).wait()

    @pl.when(i >= 2)
    def _():
        _tile_copy(obuf, base, ys_ref, 0, ssem.at[slot]).wait()

    x = _load_token_tiles(xbuf, base, MOE_TILE).astype(BF16)
    gt = _dot(x, wg_ref[0, 0].astype(BF16))
    hid = (gt * _sigmoid(gt)) * _dot(x, wu_ref[0, 0].astype(BF16))
    _store_token_tiles(obuf, base, _dot(hid.astype(BF16), wd_ref[0, 0].astype(BF16)))

    def scatter(j, c):
        for k in range(2):
            r = 2 * j + k
            _token_copy(obuf, base + r, ys_ref, dst_ref[0, 0, r], ssem.at[slot]).start(priority=k)
        return c
    lax.fori_loop(0, MOE_TILE // 2, scatter, 0, unroll=4)

    @pl.when(i == nt - 1)
    def _():
        _tile_copy(obuf, base, ys_ref, 0, ssem.at[slot]).wait()
        _tile_copy(obuf, other * MOE_TILE, ys_ref, 0, ssem.at[other]).wait()


def _experts(tile_expert, src_tok, dst_row, h, layer, wg, wu, wd):
    nt = tile_expert.shape[0]
    d = D_MODEL
    wmap = lambda i, te: (layer, te[i], 0, 0)
    idx = lambda a: a.reshape(nt, 1, MOE_TILE)
    smem = lambda f: pl.BlockSpec((1, 1, MOE_TILE), f, memory_space=pltpu.SMEM)
    tile_rows = MOE_TILE * SUBLANES
    return pl.pallas_call(
        _experts_kernel,
        grid_spec=pltpu.PrefetchScalarGridSpec(
            num_scalar_prefetch=1, grid=(nt,),
            in_specs=[smem(lambda i, te: (i, 0, 0)),
                      smem(lambda i, te: (jnp.minimum(i + 1, nt - 1), 0, 0)),
                      smem(lambda i, te: (i, 0, 0)),
                      pl.BlockSpec(memory_space=pl.ANY),
                      pl.BlockSpec((1, 1, d, D_EXPERT), wmap), pl.BlockSpec((1, 1, d, D_EXPERT), wmap),
                      pl.BlockSpec((1, 1, D_EXPERT, d), wmap)],
            out_specs=pl.BlockSpec(memory_space=pl.ANY),
            scratch_shapes=[pltpu.VMEM((2 * tile_rows, LANES), F32), pltpu.VMEM((2 * tile_rows, LANES), F32),
                            pltpu.SemaphoreType.DMA((2,)), pltpu.SemaphoreType.DMA((2,))]),
        out_shape=jax.ShapeDtypeStruct((nt * tile_rows, LANES), F32),
        compiler_params=pltpu.CompilerParams(dimension_semantics=("arbitrary",), vmem_limit_bytes=40 * MIB,
                                             has_side_effects=True),
        name="moe_experts",
    )(tile_expert, idx(src_tok), idx(src_tok), idx(dst_row), h, wg, wu, wd)


def _combine_kernel(route_ref, x_ref, g2_ref, fg_ref, ys_ref, o_ref, *, tc, final):
    rt = route_ref[0]
    g0, g1 = rt[:, 2:3], rt[:, 3:4]
    cols = [g0 * ys_ref[pl.ds(s, tc, stride=2 * SUBLANES), :]
            + g1 * ys_ref[pl.ds(SUBLANES + s, tc, stride=2 * SUBLANES), :] for s in range(SUBLANES)]
    x = x_ref[0] + g2_ref[0] * jnp.concatenate(cols, axis=-1)
    if final:
        x = _rms(x, fg_ref[...], x.shape[-1])
    o_ref[0] = x


def _combine(route, x, g2, fg, ys, tok_off, final):
    bsz, n, d = x.shape
    tc = min(256, n)
    nt = n // tc
    row = lambda i, t: (i, t, 0)
    blk = tok_off // tc
    return pl.pallas_call(
        functools.partial(_combine_kernel, tc=tc, final=final),
        grid=(bsz, nt),
        in_specs=[pl.BlockSpec((1, tc, LANES), row), pl.BlockSpec((1, tc, d), row),
                  pl.BlockSpec((1, 1, d), lambda i, t: (i, 0, 0)), pl.BlockSpec((1, d), lambda i, t: (0, 0)),
                  pl.BlockSpec((2 * tc * SUBLANES, LANES), lambda i, t: (blk + i * nt + t, 0))],
        out_specs=pl.BlockSpec((1, tc, d), row),
        out_shape=jax.ShapeDtypeStruct((bsz, n, d), F32),
        compiler_params=_cparams(("parallel", "parallel"), 32),
        name="moe_combine",
    )(route, x, g2, fg.reshape(1, d), ys)


def _moe_plan(eids):
    a = eids.shape[0]
    order = jnp.argsort(eids, stable=True).astype(I32)
    counts = (eids[:, None] == jnp.arange(N_EXPERTS, dtype=I32)[None, :]).astype(I32).sum(axis=0)
    padded = ((counts + MOE_TILE - 1) // MOE_TILE) * MOE_TILE
    ends = jnp.cumsum(padded)
    first = jnp.cumsum(counts) - counts
    nt = a // MOE_TILE + N_EXPERTS
    starts = jnp.arange(nt, dtype=I32) * MOE_TILE
    tile_expert = jnp.minimum((starts[:, None] >= ends[None, :]).astype(I32).sum(axis=1), N_EXPERTS - 1)
    tile_pos = starts - (ends - padded)[tile_expert]
    tile_real = jnp.where(starts < ends[-1], jnp.clip(counts[tile_expert] - tile_pos, 0, MOE_TILE), 0)
    slot = jnp.arange(nt * MOE_TILE, dtype=I32)
    e = jnp.repeat(tile_expert, MOE_TILE)
    pos = slot - (ends - padded)[e]
    real = (slot % MOE_TILE) < jnp.repeat(tile_real, MOE_TILE)
    assign = order[jnp.clip(first[e] + pos, 0, a - 1)]
    src_tok = jnp.where(real, assign // 2, 0)
    pad_rank = jnp.cumsum(jnp.logical_not(real).astype(I32)) - 1
    dst_row = jnp.where(real, assign, a + pad_rank)
    return src_tok, dst_row, tile_expert.astype(I32)


def _moe(streams, layer, wg, wu, wd, final_g, final):
    eids = jnp.concatenate([s[2][..., :2].reshape(-1) for s in streams]).astype(I32)
    src_tok, dst_row, tile_expert = _moe_plan(eids)
    h = streams[0][1] if len(streams) == 1 else jnp.concatenate([s[1] for s in streams], axis=0)
    ys = _experts(tile_expert, src_tok, dst_row, h, layer, wg, wu, wd)
    outs, off = [], 0
    for x, _, route, g2 in streams:
        outs.append(_combine(route, x, g2, final_g, ys, off, final))
        off += x.shape[0] * x.shape[1]
    return outs


def _rope_tables(n, half, offset, identity):
    cs = jnp.ones((n, LANES), F32)
    z = jnp.zeros((n, LANES), F32)
    if identity:
        return cs, z, z
    rows = (jnp.arange(n, dtype=I32) // GRID_W).astype(F32)
    cols = (jnp.arange(n, dtype=I32) % GRID_W).astype(F32)
    freqs = ROPE_THETA ** (-jnp.arange(half, dtype=F32) / half)
    ar, ac = rows[:, None] * freqs, cols[:, None] * freqs
    zero = jnp.zeros((n, half), F32)
    cos4 = jnp.concatenate([jnp.cos(ar), jnp.cos(ar), jnp.cos(ac), jnp.cos(ac)], axis=1)
    s1 = jnp.concatenate([-jnp.sin(ar), zero, -jnp.sin(ac), zero], axis=1)
    s2 = jnp.concatenate([zero, jnp.sin(ar), zero, jnp.sin(ac)], axis=1)
    place = lambda base, t: lax.dynamic_update_slice(base, t, (0, offset))
    return place(cs, cos4), place(z, s1), place(z, s2)


def _pad_heads(w, nheads, width, left=0):
    k = w.shape[0]
    w = w.reshape(k, nheads, width)
    w = jnp.pad(w, ((0, 0), (0, 0), (left, LANES - width - left)))
    return w.reshape(k, nheads * LANES)


def _lane_pad(v, left=0):
    return jnp.pad(v, (left, LANES - v.shape[0] - left)).reshape(1, LANES)


def kernel(x, c, ctx, c_ctx, w_mod, b_mod, norm_g, e_w_in, e_hy_conv_w, e_hy_conv_b, e_hy_w1, e_hy_b1, e_hy_w2, e_hy_b2, e_hy_w3, e_hy_freq, e_hy_fbias, e_qk_g, e_w_out, o_w_in, o_q_norm_g, o_w_uq, o_kv_norm_g, o_w_ukv, o_s5_a_re, o_s5_a_im, o_s5_log_dt, o_s5_b_re, o_s5_b_im, o_s5_c_re, o_s5_c_im, o_s5_d, o_glu_w, o_glu_b, o_w_out, moe_w_rg, moe_w_re, moe_w_gate, moe_w_up, moe_w_down, final_g):
    bsz, n, d = x.shape
    depth = w_mod.shape[0]
    xc = ctx
    ncx = ctx.shape[1]
    cc = jnp.zeros((bsz + 8, d), F32).at[:bsz].set(c).at[bsz].set(c_ctx)

    for i in range(depth):
        j = i // 2
        last = i == depth - 1
        mods = _modulation(cc, w_mod[i], b_mod[i])
        lat = [m.reshape(bsz, 1, d) for m in jnp.split(mods[:bsz], 6, axis=-1)]
        cxm = [jnp.broadcast_to(m.reshape(1, 1, d), (bsz, 1, d)) for m in jnp.split(mods[bsz], 6, axis=-1)]

        if i % 2 == 0:
            wq = _pad_heads(e_w_in[j][:, 3 * A_WIDTH:3 * A_WIDTH + B_Q_HEADS * HEAD_DIM], B_Q_HEADS, HEAD_DIM)
            k0 = 3 * A_WIDTH + B_Q_HEADS * HEAD_DIM
            wk = _pad_heads(e_w_in[j][:, k0:k0 + B_KV_HEADS * HEAD_DIM], B_KV_HEADS, HEAD_DIM)
            wv = _pad_heads(e_w_in[j][:, k0 + B_KV_HEADS * HEAD_DIM:], B_KV_HEADS, HEAD_DIM)
            w_in = jnp.concatenate([e_w_in[j][:, :3 * A_WIDTH], wq, wk, wv], axis=1).astype(BF16)
            qg, kg = _lane_pad(e_qk_g[j, 0]), _lane_pad(e_qk_g[j, 1])
            filt = (e_hy_w1[j], e_hy_b1[j], e_hy_w2[j], e_hy_b2[j], e_hy_w3[j], e_hy_freq[j])
            qscale = HEAD_DIM ** -0.5 * LOG2E
            hy_l, q_l, k_l, v_l = _even_in(x, norm_g[i, 0], lat[0], lat[1], w_in, qg, kg,
                                           _rope_tables(n, HEAD_DIM // 4, 0, False), qscale)
            hy_c, q_c, k_c, v_c = _even_in(xc, norm_g[i, 0], cxm[0], cxm[1], w_in, qg, kg,
                                           _rope_tables(ncx, HEAD_DIM // 4, 0, True), qscale)
            rep = B_Q_HEADS // B_KV_HEADS
            att_l = _attention(q_l, [k_c, k_l], [v_c, v_l], rep, 1)
            mix_l = (_hyena(hy_l, e_hy_conv_w[j], e_hy_conv_b[j], filt, e_hy_fbias[j]), att_l)
            if not last:
                att_c = _attention(q_c, [k_c], [v_c], rep, 1)
                mix_c = (_hyena(hy_c, e_hy_conv_w[j], e_hy_conv_b[j], filt, e_hy_fbias[j]), att_c)
            w_out = e_w_out[j]
            split = A_WIDTH
        else:
            w1 = o_w_in[j]
            c0, c1, c2 = MLA_Q_RANK, MLA_Q_RANK + MLA_KV_RANK, MLA_Q_RANK + MLA_KV_RANK + MLA_ROPE
            w_kr = jnp.pad(w1[:, c1:c2], ((0, 0), (MLA_NOPE, LANES - MLA_NOPE - MLA_ROPE)))
            w_in = jnp.concatenate([w1[:, :c1], w_kr, w1[:, c2:]], axis=1).astype(BF16)
            wuq = _pad_heads(o_w_uq[j], MLA_HEADS, MLA_NOPE + MLA_ROPE).astype(BF16)
            wkv = o_w_ukv[j].reshape(MLA_KV_RANK, MLA_HEADS, MLA_NOPE + MLA_V)
            wuk = _pad_heads(wkv[:, :, :MLA_NOPE].reshape(MLA_KV_RANK, -1), MLA_HEADS, MLA_NOPE).astype(BF16)
            wuv = _pad_heads(wkv[:, :, MLA_NOPE:].reshape(MLA_KV_RANK, -1), MLA_HEADS, MLA_V).astype(BF16)
            qg = o_q_norm_g[j].reshape(1, MLA_Q_RANK)
            kvg = o_kv_norm_g[j].reshape(1, MLA_KV_RANK)
            args = (w_in, qg, kvg, wuq, wuk, wuv)
            qscale = (MLA_NOPE + MLA_ROPE) ** -0.5 * LOG2E
            u_l, k_l, v_l, q_l = _odd_in(x, norm_g[i, 0], lat[0], lat[1], *args,
                                         _rope_tables(n, MLA_ROPE // 4, MLA_NOPE, False), True, qscale)
            outs_c = _odd_in(xc, norm_g[i, 0], cxm[0], cxm[1], *args,
                             _rope_tables(ncx, MLA_ROPE // 4, MLA_NOPE, True), not last, qscale)
            u_c, k_c, v_c = outs_c[:3]
            att_l = _attention(q_l, [k_c, k_l], [v_c, v_l], 2, 2)
            mats = _s5_matrices(o_s5_a_re[j], o_s5_a_im[j], o_s5_log_dt[j], o_s5_b_re[j], o_s5_b_im[j],
                                o_s5_c_re[j], o_s5_c_im[j], o_s5_d[j])
            glu_w = o_glu_w[j].astype(BF16)
            mix_l = (att_l, _s5_mixer(u_c, u_l, mats, glu_w, o_glu_b[j]))
            if not last:
                raise NotImplementedError("an odd layer that is not the last needs the context S5 readout")
            w_out = o_w_out[j]
            split = MLA_HEADS * MLA_V

        wa, wb = w_out[:split].astype(BF16), w_out[split:].astype(BF16)
        router = jnp.pad(jnp.concatenate([moe_w_rg[i], moe_w_re[i]], axis=1),
                         ((0, 0), (0, LANES - N_GROUPS - N_EXPERTS)))
        rh = router.astype(BF16)
        rl = (router - rh.astype(F32)).astype(BF16)

        x, h_l, route_l = _proj_res(*mix_l, wa, wb, x, lat[2], norm_g[i, 1], lat[3], lat[4], rh, rl)
        streams = [(x, h_l, route_l, lat[5])]
        if not last:
            xc, h_c, route_c = _proj_res(*mix_c, wa, wb, xc, cxm[2], norm_g[i, 1], cxm[3], cxm[4], rh, rl)
            streams.append((xc, h_c, route_c, cxm[5]))
        outs = _moe(streams, i, moe_w_gate, moe_w_up, moe_w_down, final_g, last)
        x = outs[0]
        if not last:
            xc = outs[1]
    return x
```

```python
import functools
import math

import jax
import jax.numpy as jnp
from jax import lax
from jax.experimental import pallas as pl
from jax.experimental.pallas import tpu as pltpu

F32 = jnp.float32
BF16 = jnp.bfloat16
I32 = jnp.int32

D_MODEL = 1024
GRID_W = 64
HEAD_DIM = 64
A_WIDTH = D_MODEL // 2
B_Q_HEADS = (D_MODEL - A_WIDTH) // HEAD_DIM
B_KV_HEADS = max(1, B_Q_HEADS // 4)
HY_ORDER = 2
HY_BANDS = 16
HY_TARGET = 1e-2
HY_MAX_DECAY = math.log(1.0 / HY_TARGET) / 0.3
HY_MIN_DECAY = math.log(1.0 / HY_TARGET) / 1.5
MLA_HEADS = 8
MLA_NOPE = 64
MLA_ROPE = 32
MLA_V = 64
MLA_Q_RANK = D_MODEL // 4
MLA_KV_RANK = D_MODEL // 8
S5_WIDTH = D_MODEL - MLA_HEADS * MLA_V
S5_GROUP = 16
S5_GROUPS = S5_WIDTH // S5_GROUP
S5_STATE = 64
S5_CHUNK = 16
S5_TILE_GROUPS = 128 // S5_GROUP
N_GROUPS = 4
EXPERTS_PER_GROUP = 8
N_EXPERTS = N_GROUPS * EXPERTS_PER_GROUP
D_EXPERT = D_MODEL // 4
ROPE_THETA = 10000.0
EPS = 1e-6
LOG2E = 1.4426950408889634

LANES = 128
SUBLANES = 8
MOE_TILE = 256
HIGHEST = lax.Precision.HIGHEST
MIB = 1024 * 1024


def _cparams(sem, vmem_mib):
    return pltpu.CompilerParams(dimension_semantics=sem, vmem_limit_bytes=vmem_mib * MIB)


def _dot(a, b):
    return jnp.dot(a, b, preferred_element_type=F32)


def _dot_nt(a, b):
    return lax.dot_general(a, b, (((1,), (1,)), ((), ())), preferred_element_type=F32)


def _split_bf16(a):
    hi = a.astype(BF16)
    lo = (a - hi.astype(F32)).astype(BF16)
    return hi, lo


def _dot3(ah, al, bh, bl):
    return _dot(ah, bh) + (_dot(ah, bl) + _dot(al, bh))


def _sigmoid(x):
    return 1.0 / (1.0 + jnp.exp(-x))


def _rms(x, g, width):
    ms = jnp.sum(x * x, axis=-1, keepdims=True) * (1.0 / width)
    return x * lax.rsqrt(ms + EPS) * g


def _modnorm(x, g, shift, scale):
    return _rms(x, g, x.shape[-1]) * (1.0 + scale) + shift


def _rope(t, cs, s1, s2, half):
    return t * cs + pltpu.roll(t, LANES - half, 1) * s1 + pltpu.roll(t, half, 1) * s2


def _store_token_tiles(ref, base_tok, val):
    n = val.shape[0]
    for s in range(SUBLANES):
        ref[pl.ds(base_tok * SUBLANES + s, n, stride=SUBLANES), :] = val[:, LANES * s: LANES * (s + 1)]


def _load_token_tiles(ref, base_tok, n):
    return jnp.concatenate([ref[pl.ds(base_tok * SUBLANES + s, n, stride=SUBLANES), :]
                            for s in range(SUBLANES)], axis=-1)


def _short_conv(u, w, b):
    n = u.shape[0]
    row = lax.broadcasted_iota(I32, u.shape, 0)
    prev = jnp.where(row == 0, 0.0, pltpu.roll(u, 1, 0))
    nxt = jnp.where(row == n - 1, 0.0, pltpu.roll(u, n - 1, 0))
    return prev * w[0:1] + u * w[1:2] + nxt * w[2:3] + b


def _mod_kernel(c_ref, w_ref, b_ref, o_ref):
    c = c_ref[...]
    s = c * _sigmoid(c)
    sh, sl = _split_bf16(s)
    wh, wl = _split_bf16(w_ref[0])
    o_ref[...] = _dot3(sh, sl, wh, wl) + b_ref[0]


def _modulation(cc, w, b, layer):
    r, d = cc.shape
    n = w.shape[2]
    tn = 1536
    return pl.pallas_call(
        _mod_kernel,
        grid=(n // tn,),
        in_specs=[pl.BlockSpec((r, d), lambda j: (0, 0)),
                  pl.BlockSpec((1, d, tn), lambda j: (layer, 0, j)),
                  pl.BlockSpec((1, 1, tn), lambda j: (layer, 0, j))],
        out_specs=pl.BlockSpec((r, tn), lambda j: (0, j)),
        out_shape=jax.ShapeDtypeStruct((r, n), F32),
        compiler_params=_cparams(("parallel",), 48),
        name="modulation",
    )(cc, w, b.reshape(b.shape[0], 1, n))


def _with_ones(v):
    lane = lax.broadcasted_iota(I32, v.shape, 1) % LANES
    return jnp.where(lane >= HEAD_DIM, 1.0, v)


def _even_in_kernel(x_ref, g_ref, sh_ref, sc_ref, w_ref, qg_ref, kg_ref, cs_ref, s1_ref, s2_ref,
                    hy_ref, q_ref, k_ref, v_ref, *, qscale):
    xn = _modnorm(x_ref[0], g_ref[...], sh_ref[0], sc_ref[0]).astype(BF16)
    p = _dot(xn, w_ref[...])
    hw = 3 * A_WIDTH
    hy_ref[0] = p[:, :hw].astype(BF16)
    cs, s1, s2 = cs_ref[...], s1_ref[...], s2_ref[...]
    for h in range(B_Q_HEADS):
        t = _rms(p[:, hw + LANES * h: hw + LANES * (h + 1)], qg_ref[...], HEAD_DIM)
        q_ref[0, :, LANES * h: LANES * (h + 1)] = (_rope(t, cs, s1, s2, HEAD_DIM // 4) * qscale).astype(BF16)
    kw = hw + LANES * B_Q_HEADS
    for h in range(B_KV_HEADS):
        t = _rms(p[:, kw + LANES * h: kw + LANES * (h + 1)], kg_ref[...], HEAD_DIM)
        k_ref[0, :, LANES * h: LANES * (h + 1)] = _rope(t, cs, s1, s2, HEAD_DIM // 4).astype(BF16)
    vw = kw + LANES * B_KV_HEADS
    v_ref[0] = _with_ones(p[:, vw:]).astype(BF16)


def _even_in(x, g, shift, scale, w, qg, kg, tabs, qscale):
    bsz, n, d = x.shape
    tm = min(512, n)
    nw = w.shape[1]
    row = lambda b, t: (b, t, 0)
    per_b = lambda b, t: (b, 0, 0)
    fixed = lambda b, t: (0, 0)
    tab = pl.BlockSpec((tm, LANES), lambda b, t: (t, 0))
    return pl.pallas_call(
        functools.partial(_even_in_kernel, qscale=qscale),
        grid=(bsz, n // tm),
        in_specs=[pl.BlockSpec((1, tm, d), row), pl.BlockSpec((1, d), fixed),
                  pl.BlockSpec((1, 1, d), per_b), pl.BlockSpec((1, 1, d), per_b),
                  pl.BlockSpec((d, nw), fixed), pl.BlockSpec((1, LANES), fixed),
                  pl.BlockSpec((1, LANES), fixed), tab, tab, tab],
        out_specs=[pl.BlockSpec((1, tm, 3 * A_WIDTH), row),
                   pl.BlockSpec((1, tm, LANES * B_Q_HEADS), row),
                   pl.BlockSpec((1, tm, LANES * B_KV_HEADS), row),
                   pl.BlockSpec((1, tm, LANES * B_KV_HEADS), row)],
        out_shape=[jax.ShapeDtypeStruct((bsz, n, 3 * A_WIDTH), BF16),
                   jax.ShapeDtypeStruct((bsz, n, LANES * B_Q_HEADS), BF16),
                   jax.ShapeDtypeStruct((bsz, n, LANES * B_KV_HEADS), BF16),
                   jax.ShapeDtypeStruct((bsz, n, LANES * B_KV_HEADS), BF16)],
        compiler_params=_cparams(("parallel", "parallel"), 48),
        name="even_in",
    )(x, g.reshape(1, d), shift, scale, w, qg, kg, *tabs)


def _odd_in_kernel(x_ref, g_ref, sh_ref, sc_ref, w_ref, qg_ref, kvg_ref, wuq_ref, wuk_ref, wuv_ref,
                   cs_ref, s1_ref, s2_ref, *out_refs, need_q, qscale):
    if need_q:
        u_ref, k_ref, v_ref, q_ref = out_refs
    else:
        u_ref, k_ref, v_ref = out_refs
    xn = _modnorm(x_ref[0], g_ref[...], sh_ref[0], sc_ref[0]).astype(BF16)
    p = _dot(xn, w_ref[...])
    c0, c1, c2 = MLA_Q_RANK, MLA_Q_RANK + MLA_KV_RANK, MLA_Q_RANK + MLA_KV_RANK + LANES
    u_ref[0] = p[:, c2:]
    cs, s1, s2 = cs_ref[...], s1_ref[...], s2_ref[...]
    ckv = _rms(p[:, c0:c1], kvg_ref[...], MLA_KV_RANK).astype(BF16)
    kn = _dot(ckv, wuk_ref[...])
    v_ref[0] = _with_ones(_dot(ckv, wuv_ref[...])).astype(BF16)
    kr = _rope(p[:, c1:c2], cs, s1, s2, MLA_ROPE // 4)
    for h in range(MLA_HEADS):
        k_ref[0, :, LANES * h: LANES * (h + 1)] = (kn[:, LANES * h: LANES * (h + 1)] + kr).astype(BF16)
    if need_q:
        cq = _rms(p[:, :c0], qg_ref[...], MLA_Q_RANK).astype(BF16)
        q = _dot(cq, wuq_ref[...])
        for h in range(MLA_HEADS):
            t = q[:, LANES * h: LANES * (h + 1)]
            q_ref[0, :, LANES * h: LANES * (h + 1)] = (_rope(t, cs, s1, s2, MLA_ROPE // 4) * qscale).astype(BF16)


def _odd_in(x, g, shift, scale, w, qg, kvg, wuq, wuk, wuv, tabs, need_q, qscale):
    bsz, n, d = x.shape
    tm = min(512, n)
    hw = LANES * MLA_HEADS
    row = lambda b, t: (b, t, 0)
    per_b = lambda b, t: (b, 0, 0)
    fixed = lambda b, t: (0, 0)
    tab = pl.BlockSpec((tm, LANES), lambda b, t: (t, 0))
    widths = [S5_WIDTH, hw, hw] + ([hw] if need_q else [])
    return pl.pallas_call(
        functools.partial(_odd_in_kernel, need_q=need_q, qscale=qscale),
        grid=(bsz, n // tm),
        in_specs=[pl.BlockSpec((1, tm, d), row), pl.BlockSpec((1, d), fixed),
                  pl.BlockSpec((1, 1, d), per_b), pl.BlockSpec((1, 1, d), per_b),
                  pl.BlockSpec(w.shape, fixed), pl.BlockSpec((1, MLA_Q_RANK), fixed),
                  pl.BlockSpec((1, MLA_KV_RANK), fixed), pl.BlockSpec(wuq.shape, fixed),
                  pl.BlockSpec(wuk.shape, fixed), pl.BlockSpec(wuv.shape, fixed), tab, tab, tab],
        out_specs=[pl.BlockSpec((1, tm, wd), row) for wd in widths],
        out_shape=[jax.ShapeDtypeStruct((bsz, n, wd), F32 if k == 0 else BF16) for k, wd in enumerate(widths)],
        compiler_params=_cparams(("parallel", "parallel"), 48),
        name="odd_in_q" if need_q else "odd_in",
    )(x, g.reshape(1, d), shift, scale, w, qg, kvg, wuq, wuk, wuv, *tabs)


def _hy_fwd_kernel(wc_ref, ws_ref, u_ref, cw_ref, cb_ref, ka_ref, kb_ref, ya_ref, yb_ref, ubf_ref,
                   *, conv, tf):
    f = pl.program_id(1)

    @pl.when(f == 0)
    def _():
        u = u_ref[0].astype(F32)
        if conv:
            u = _short_conv(u, cw_ref[...], cb_ref[...])
        ubf_ref[...] = u.astype(BF16)

    ub = ubf_ref[...]
    a = _dot(wc_ref[...], ub)
    b = _dot(ws_ref[...], ub)
    ka, kb = ka_ref[...], kb_ref[...]
    first = (lax.broadcasted_iota(I32, a.shape, 0) + f * tf) == 0
    ya_ref[0] = jnp.where(first, a * ka, a * ka - b * kb).astype(BF16)
    yb_ref[0] = jnp.where(first, b * kb, a * kb + b * ka).astype(BF16)


def _hy_fwd(wc, ws, u, col, cw, cb, ka, kb, conv):
    bsz, n, _ = u.shape
    c = A_WIDTH
    tf = min(512, n)
    return pl.pallas_call(
        functools.partial(_hy_fwd_kernel, conv=conv, tf=tf),
        grid=(bsz, n // tf),
        in_specs=[pl.BlockSpec((tf, n), lambda b, f: (f, 0)), pl.BlockSpec((tf, n), lambda b, f: (f, 0)),
                  pl.BlockSpec((1, n, c), lambda b, f: (b, 0, col)),
                  pl.BlockSpec((3, c), lambda b, f: (0, 0)), pl.BlockSpec((1, c), lambda b, f: (0, 0)),
                  pl.BlockSpec((tf, c), lambda b, f: (f, 0)), pl.BlockSpec((tf, c), lambda b, f: (f, 0))],
        out_specs=[pl.BlockSpec((1, tf, c), lambda b, f: (b, f, 0))] * 2,
        out_shape=[jax.ShapeDtypeStruct((bsz, n, c), BF16)] * 2,
        scratch_shapes=[pltpu.VMEM((n, c), BF16)],
        compiler_params=_cparams(("parallel", "arbitrary"), 48),
        name="hyena_fwd",
    )(wc, ws, u, cw, cb, ka, kb)


def _hy_inv_kernel(ci_ref, si_ref, ya_ref, yb_ref, a_ref, g_ref, cwa_ref, cba_ref, cwg_ref, cbg_ref,
                   bias_ref, o_ref, a_sc, g_sc, *, conv_a, tt):
    t = pl.program_id(1)

    @pl.when(t == 0)
    def _():
        a = a_ref[0].astype(F32)
        if conv_a:
            a = _short_conv(a, cwa_ref[...], cba_ref[...])
        a_sc[...] = a
        g_sc[...] = _short_conv(g_ref[0].astype(F32), cwg_ref[...], cbg_ref[...])

    y = _dot(ci_ref[...], ya_ref[0]) + _dot(si_ref[...], yb_ref[0])
    rows = pl.ds(pl.multiple_of(t * tt, tt), tt)
    y = y + a_sc[rows, :] * bias_ref[...]
    o_ref[0] = (g_sc[rows, :] * y).astype(BF16)


def _hy_inv(ci, si, ya, yb, a, a_col, g, g_col, cwa, cba, cwg, cbg, bias, conv_a):
    bsz, n, c = ya.shape
    tt = min(512, n)
    whole = lambda b, t: (b, 0, 0)
    fixed = lambda b, t: (0, 0)
    return pl.pallas_call(
        functools.partial(_hy_inv_kernel, conv_a=conv_a, tt=tt),
        grid=(bsz, n // tt),
        in_specs=[pl.BlockSpec((tt, n), lambda b, t: (t, 0)), pl.BlockSpec((tt, n), lambda b, t: (t, 0)),
                  pl.BlockSpec((1, n, c), whole), pl.BlockSpec((1, n, c), whole),
                  pl.BlockSpec((1, n, c), lambda b, t: (b, 0, a_col)),
                  pl.BlockSpec((1, n, c), lambda b, t: (b, 0, g_col)),
                  pl.BlockSpec((3, c), fixed), pl.BlockSpec((1, c), fixed),
                  pl.BlockSpec((3, c), fixed), pl.BlockSpec((1, c), fixed), pl.BlockSpec((1, c), fixed)],
        out_specs=pl.BlockSpec((1, tt, c), lambda b, t: (b, t, 0)),
        out_shape=jax.ShapeDtypeStruct((bsz, n, c), BF16),
        scratch_shapes=[pltpu.VMEM((n, c), F32), pltpu.VMEM((n, c), F32)],
        compiler_params=_cparams(("parallel", "arbitrary"), 56),
        name="hyena_inv",
    )(ci, si, ya, yb, a, g, cwa, cba, cwg, cbg, bias)


def _dft_mats(n):
    big = 2 * n
    f = jnp.arange(n, dtype=I32)[:, None]
    s = jnp.arange(n, dtype=I32)[None, :]
    step = 64 if n % 64 == 0 else 1
    angle = lambda fr: ((fr[:, None] * s) % big).astype(F32) * (2.0 * math.pi / big)
    a_hi = angle(jnp.arange(n // step, dtype=I32) * step)
    a_lo = angle(jnp.arange(step, dtype=I32))
    c1, s1, c2, s2 = jnp.cos(a_hi)[:, None], jnp.sin(a_hi)[:, None], jnp.cos(a_lo)[None], jnp.sin(a_lo)[None]
    cos = (c1 * c2 - s1 * s2).reshape(n, n)
    sin = (s1 * c2 + c1 * s2).reshape(n, n)
    alt = jnp.where(jnp.arange(n) % 2 == 0, 1.0, -1.0).astype(F32)
    wc = cos
    ws = jnp.where(f == 0, alt[None, :], -sin)
    ci = jnp.where(s == 0, 1.0 / big, (2.0 / big) * cos)
    si = jnp.where(s == 0, alt[:, None] / big, -(2.0 / big) * sin)
    return wc.astype(BF16), ws.astype(BF16), ci.astype(BF16), si.astype(BF16)


def _hyena_filter_spectrum(n, w1, b1, w2, b2, w3, freq):
    t = jnp.arange(n, dtype=F32)
    t_norm = t / max(n - 1, 1)
    bands = jnp.linspace(1e-4, HY_BANDS - 1, HY_BANDS, dtype=F32)
    ang = (2.0 * math.pi / n) * t[:, None] * bands
    z = jnp.concatenate([t_norm[:, None], jnp.cos(ang), jnp.sin(ang)], axis=-1)
    hdn = jnp.sin(freq * (jnp.dot(z, w1, precision=HIGHEST) + b1))
    hdn = jnp.sin(freq * (jnp.dot(hdn, w2, precision=HIGHEST) + b2))
    taps = jnp.dot(hdn, w3, precision=HIGHEST).reshape(n, 2, HY_ORDER, A_WIDTH)
    deltas = jnp.linspace(HY_MIN_DECAY, HY_MAX_DECAY, A_WIDTH, dtype=F32)
    taps = taps * jnp.exp(-t_norm[:, None] * deltas)[:, None, None, :]
    fwd, bwd = taps[:, 0], taps[:, 1]
    circ = jnp.concatenate([fwd, jnp.zeros((1, HY_ORDER, A_WIDTH), F32), bwd[:0:-1]], axis=0)
    circ = circ * lax.rsqrt(jnp.sum(circ * circ, axis=0, keepdims=True) + EPS)
    kf = jnp.fft.rfft(circ, axis=0)
    ka = jnp.real(kf[:n]).astype(F32)
    kb = jnp.concatenate([jnp.real(kf[n:n + 1]), jnp.imag(kf[1:n])], axis=0).astype(F32)
    return ka, kb


def _hyena(hy, cw, cb, filt, fbias):
    n = hy.shape[1]
    c = A_WIDTH
    wc, ws, ci, si = _dft_mats(n)
    ka, kb = _hyena_filter_spectrum(n, *filt)
    cws = [cw[:, i * c:(i + 1) * c] for i in range(3)]
    cbs = [cb[i * c:(i + 1) * c].reshape(1, c) for i in range(3)]
    ya, yb = _hy_fwd(wc, ws, hy, 2, cws[2], cbs[2], ka[:, 0], kb[:, 0], True)
    z = _hy_inv(ci, si, ya, yb, hy, 2, hy, 0, cws[2], cbs[2], cws[0], cbs[0], fbias[0:1], True)
    ya, yb = _hy_fwd(wc, ws, z, 0, cws[2], cbs[2], ka[:, 1], kb[:, 1], False)
    return _hy_inv(ci, si, ya, yb, z, 0, hy, 1, cws[2], cbs[2], cws[1], cbs[1], fbias[1:2], False)


def _attn_kernel(*refs, nseg, hps, kps):
    q_ref = refs[0]
    k_refs = refs[1:1 + nseg]
    v_refs = refs[1 + nseg:1 + 2 * nseg]
    o_ref = refs[1 + 2 * nseg]
    low = lax.broadcasted_iota(I32, (q_ref.shape[1], LANES), 1) < HEAD_DIM
    for pair in range(hps // 2):
        accs = []
        for half in range(2):
            i = 2 * pair + half
            kv = (i * kps) // hps
            q = q_ref[0, :, LANES * i: LANES * (i + 1)]
            ss = [_dot_nt(q, k[0, :, LANES * kv: LANES * (kv + 1)]) for k in k_refs]
            m = ss[0].max(axis=-1, keepdims=True)
            for s in ss[1:]:
                m = jnp.maximum(m, s.max(axis=-1, keepdims=True))
            acc = None
            for s, v in zip(ss, v_refs):
                p = jnp.exp2((s - m).astype(BF16))
                o = _dot(p, v[0, :, LANES * kv: LANES * (kv + 1)])
                acc = o if acc is None else acc + o
            accs.append(acc)
        num = jnp.where(low, accs[0], pltpu.roll(accs[1], HEAD_DIM, 1))
        den = jnp.where(low, pltpu.roll(accs[0], HEAD_DIM, 1), accs[1])
        o_ref[0, :, LANES * pair: LANES * (pair + 1)] = (num / den).astype(BF16)


def _attention(q, ks, vs, hps, kps):
    bsz, lq, qw = q.shape
    nh = qw // LANES
    tq = min(512, lq)
    nseg = len(ks)
    kv_spec = lambda a: pl.BlockSpec((1, a.shape[1], kps * LANES), lambda b, g, t: (b, 0, g))
    return pl.pallas_call(
        functools.partial(_attn_kernel, nseg=nseg, hps=hps, kps=kps),
        grid=(bsz, nh // hps, lq // tq),
        in_specs=[pl.BlockSpec((1, tq, hps * LANES), lambda b, g, t: (b, t, g))]
        + [kv_spec(a) for a in ks] + [kv_spec(a) for a in vs],
        out_specs=pl.BlockSpec((1, tq, hps * HEAD_DIM), lambda b, g, t: (b, t, g)),
        out_shape=jax.ShapeDtypeStruct((bsz, lq, nh * HEAD_DIM), BF16),
        compiler_params=_cparams(("parallel", "parallel", "parallel"), 56),
        name="attention",
    )(q, *ks, *vs)


def _onehot_bf16(cond):
    return jnp.where(cond, 1.0, 0.0).astype(BF16)


def _s5_kernel(uc_ref, ul_ref, w_ref, a_ref, of_ref, or_ref, y_ref, xs_sc, yi_sc, s_sc, x_sc, ys_sc,
               *, bb, ncc, ncl):
    lc, gh, gt = S5_CHUNK, S5_GROUP, S5_TILE_GROUPS
    nc = ncc + ncl
    cw = lc * gh
    hb = gh.bit_length() - 1
    lb = LANES.bit_length() - 1

    for b in range(bb):
        for j in range(lc):
            p, half = divmod(j, 2)
            lanes = slice(half * LANES, (half + 1) * LANES)
            xs_sc[p, b * nc: b * nc + ncc, lanes] = uc_ref[b, pl.ds(j, ncc, stride=lc), :].astype(BF16)
            xs_sc[p, b * nc + ncc: (b + 1) * nc, lanes] = ul_ref[b, pl.ds(j, ncl, stride=lc), :].astype(BF16)

    r_i = lax.broadcasted_iota(I32, (2 * LANES, cw), 0)
    c_i = lax.broadcasted_iota(I32, (2 * LANES, cw), 1)
    same_h = (r_i & (gh - 1)) == (c_i & (gh - 1))
    r_grp = (r_i & (LANES - 1)) >> hb
    c_off = (c_i >> hb) - (r_i >> lb)
    for pr in range(gt // 2):
        rs = []
        for side in range(2):
            q = 2 * pr + side
            u = None
            for p in range(lc // 2):
                sel = _onehot_bf16(same_h & (r_grp == q) & (c_off == 2 * p))
                d = _dot(xs_sc[p], sel)
                u = d if u is None else u + d
            r = _dot(u.astype(BF16), w_ref[q])
            yi_sc[q] = r[:, :cw]
            rs.append(r)
        for kind in range(4):
            lo = cw + kind * LANES
            s_sc[kind, pr] = rs[0][:, lo:lo + LANES] + rs[1][:, lo:lo + LANES]

    a = a_ref[0]
    npair = gt // 2

    def advance(st, c, d):
        rws = pl.ds(c, bb, stride=nc)
        new = []
        for t in range(npair):
            xr, xi = st[2 * t], st[2 * t + 1]
            ar = a[2 * d:2 * d + 1, t * LANES:(t + 1) * LANES]
            ai = a[2 * d + 1:2 * d + 2, t * LANES:(t + 1) * LANES]
            x_sc[2 * d, t, rws, :] = xr
            x_sc[2 * d + 1, t, rws, :] = xi
            new.append(ar * xr - ai * xi + s_sc[2 * d, t, rws, :])
            new.append(ar * xi + ai * xr + s_sc[2 * d + 1, t, rws, :])
        return tuple(new)

    zero = tuple(jnp.zeros((bb, LANES), F32) for _ in range(2 * npair))
    lax.fori_loop(0, nc, lambda c, st: advance(st, c, 0), zero)
    st = lax.fori_loop(0, ncc, lambda i, st: advance(st, ncc - 1 - i, 1), zero)
    lax.fori_loop(0, ncl, lambda i, st: advance(st, nc - 1 - i, 1), st)

    def latent(ref, *idx):
        return jnp.concatenate([ref[idx + (slice(b * nc + ncc, (b + 1) * nc), slice(None))]
                                for b in range(bb)], axis=0)

    for q in range(gt):
        y = latent(yi_sc, q)
        for d, o_ref in ((0, of_ref), (1, or_ref)):
            for part in range(2):
                y = y + _dot(latent(x_sc, 2 * d + part, q // 2).astype(BF16), o_ref[q, part])
        for kt in range(cw // LANES):
            ys_sc[kt, :, q * LANES:(q + 1) * LANES] = y[:, kt * LANES:(kt + 1) * LANES].astype(BF16)

    r2 = lax.broadcasted_iota(I32, (gt * LANES, LANES), 0)
    c2 = lax.broadcasted_iota(I32, (gt * LANES, LANES), 1)
    keep = ((r2 & (gh - 1)) == (c2 & (gh - 1))) & ((r2 >> lb) == (c2 >> hb))
    r_k = (r2 & (LANES - 1)) >> hb
    for k in range(lc):
        kt, kk = divmod(k, gt)
        z = _dot(ys_sc[kt], _onehot_bf16(keep & (r_k == kk)))
        for b in range(bb):
            y_ref[b, pl.ds(k, ncl, stride=lc), :] = z[b * ncl:(b + 1) * ncl, :]


def _s5_scan(u_c, u_l, w, a_tab, of, orr):
    bsz, ncx, c = u_c.shape
    n = u_l.shape[1]
    lc, gt = S5_CHUNK, S5_TILE_GROUPS
    ncc, ncl = ncx // lc, n // lc
    bb = min(4, bsz)
    rows = bb * (ncc + ncl)
    cw = lc * S5_GROUP
    sw = (gt // 2) * LANES
    return pl.pallas_call(
        functools.partial(_s5_kernel, bb=bb, ncc=ncc, ncl=ncl),
        grid=(c // LANES, bsz // bb),
        in_specs=[pl.BlockSpec((bb, ncx, LANES), lambda t, i: (i, 0, t)),
                  pl.BlockSpec((bb, n, LANES), lambda t, i: (i, 0, t)),
                  pl.BlockSpec((gt,) + w.shape[1:], lambda t, i: (t, 0, 0)),
                  pl.BlockSpec((1,) + a_tab.shape[1:], lambda t, i: (t, 0, 0)),
                  pl.BlockSpec((gt,) + of.shape[1:], lambda t, i: (t, 0, 0, 0)),
                  pl.BlockSpec((gt,) + orr.shape[1:], lambda t, i: (t, 0, 0, 0))],
        out_specs=pl.BlockSpec((bb, n, LANES), lambda t, i: (i, 0, t)),
        out_shape=jax.ShapeDtypeStruct((bsz, n, c), F32),
        scratch_shapes=[pltpu.VMEM((lc // 2, rows, 2 * LANES), BF16), pltpu.VMEM((gt, rows, cw), F32),
                        pltpu.VMEM((4, gt // 2, rows, LANES), F32), pltpu.VMEM((4, gt // 2, rows, LANES), F32),
                        pltpu.VMEM((cw // LANES, bb * ncl, gt * LANES), BF16)],
        compiler_params=_cparams(("parallel", "parallel"), 56),
        name="s5_scan",
    )(u_c, u_l, w, a_tab, of, orr)


def _s5_matrices(a_re, a_im, log_dt, b_re, b_im, c_re, c_im, d_skip):
    lc, h, p, g = S5_CHUNK, S5_GROUP, S5_STATE, S5_GROUPS
    dt = jnp.exp(log_dt)[..., None]
    lam_re, lam_im = a_re * dt, a_im * dt
    mag = jnp.exp(lam_re)
    er, ei = mag * jnp.cos(lam_im) - 1.0, mag * jnp.sin(lam_im)
    den = a_re * a_re + a_im * a_im
    co_re = (er * a_re + ei * a_im) / den
    co_im = (ei * a_re - er * a_im) / den
    bb_re = co_re[..., None] * b_re - co_im[..., None] * b_im
    bb_im = co_re[..., None] * b_im + co_im[..., None] * b_re
    m = jnp.arange(lc + 1, dtype=F32)[:, None, None, None]
    pw_mag = jnp.exp(m * lam_re[None])
    pw_re, pw_im = pw_mag * jnp.cos(m * lam_im[None]), pw_mag * jnp.sin(m * lam_im[None])
    ab_re = pw_re[..., None] * bb_re[None] - pw_im[..., None] * bb_im[None]
    ab_im = pw_re[..., None] * bb_im[None] + pw_im[..., None] * bb_re[None]
    kk = (jnp.einsum("dgop,mdgpi->mdgoi", c_re, ab_re[:lc], precision=HIGHEST)
          - jnp.einsum("dgop,mdgpi->mdgoi", c_im, ab_im[:lc], precision=HIGHEST))
    j = jnp.arange(lc)[:, None]
    k = jnp.arange(lc)[None, :]
    lag_f = jnp.clip(k - j, 0, lc - 1)
    lag_r = jnp.clip(j - k, 0, lc - 1)
    tf = jnp.where((k >= j)[:, :, None, None, None], kk[lag_f, 0], 0.0)
    tr = jnp.where((j >= k)[:, :, None, None, None], kk[lag_r, 1], 0.0)
    tw = jnp.transpose(tf + tr, (2, 0, 4, 1, 3))
    dsk = d_skip.reshape(g, h)
    eye = (jnp.eye(lc, dtype=F32)[None, :, None, :, None] * jnp.eye(h, dtype=F32)[None, None, :, None, :]
           * dsk[:, None, :, None, None])
    tw = (tw + eye).reshape(g, lc * h, lc * h)
    idx_f = (lc - 1 - jnp.arange(lc))
    idx_r = jnp.arange(lc)

    odd = (jnp.arange(g) % 2 == 1)

    def place(t, axis):
        z = jnp.zeros_like(t)
        shape = [g] + [1] * (t.ndim - 1)
        return jnp.where(odd.reshape(shape), jnp.concatenate([z, t], axis=axis),
                         jnp.concatenate([t, z], axis=axis))

    def to_state(idx, d):
        re = jnp.transpose(ab_re[idx, d], (1, 0, 3, 2)).reshape(g, lc * h, p)
        im = jnp.transpose(ab_im[idx, d], (1, 0, 3, 2)).reshape(g, lc * h, p)
        return [place(re, 2), place(im, 2)]

    w = jnp.concatenate([tw] + to_state(idx_f, 0) + to_state(idx_r, 1), axis=-1)

    def from_state(idx, d):
        pr, pi = pw_re[idx, d], pw_im[idx, d]
        cr, cim = c_re[d], c_im[d]
        mr = cr[None] * pr[:, :, None, :] - cim[None] * pi[:, :, None, :]
        mi = cr[None] * pi[:, :, None, :] + cim[None] * pr[:, :, None, :]
        rows = lambda m: place(jnp.transpose(m, (1, 3, 0, 2)).reshape(g, p, lc * h), 1)
        return jnp.stack([rows(mr), rows(-mi)], axis=1)

    of = from_state(jnp.arange(lc) + 1, 0)
    orr = from_state(lc - jnp.arange(lc), 1)
    tiles = g // S5_TILE_GROUPS
    a_tab = jnp.stack([pw_re[lc, 0].reshape(tiles, -1), pw_im[lc, 0].reshape(tiles, -1),
                       pw_re[lc, 1].reshape(tiles, -1), pw_im[lc, 1].reshape(tiles, -1)], axis=1)
    return w.astype(BF16), a_tab.astype(F32), of.astype(BF16), orr.astype(BF16)


def _glu_kernel(y_ref, w_ref, b_ref, o_ref):
    y = y_ref[0]
    inner = 0.7978845608028654 * (y + 0.044715 * (y * y * y))
    gl = 0.5 * y * (1.0 + jnp.tanh(inner))
    o_ref[0] = (gl * _sigmoid(_dot(gl.astype(BF16), w_ref[...]) + b_ref[...])).astype(BF16)


def _glu(y, w, b):
    bsz, n, c = y.shape
    tm = min(512, n)
    return pl.pallas_call(
        _glu_kernel,
        grid=(bsz, n // tm),
        in_specs=[pl.BlockSpec((1, tm, c), lambda i, t: (i, t, 0)),
                  pl.BlockSpec((c, c), lambda i, t: (0, 0)), pl.BlockSpec((1, c), lambda i, t: (0, 0))],
        out_specs=pl.BlockSpec((1, tm, c), lambda i, t: (i, t, 0)),
        out_shape=jax.ShapeDtypeStruct((bsz, n, c), BF16),
        compiler_params=_cparams(("parallel", "parallel"), 32),
        name="s5_glu",
    )(y, w, b.reshape(1, c))


def _s5_mixer(u_c, u_l, mats, glu_w, glu_b):
    return _glu(_s5_scan(u_c, u_l, *mats), glu_w, glu_b)


def _route(lg):
    lane = lax.broadcasted_iota(I32, lg.shape, 1).astype(F32)
    neg = jnp.float32(-1e30)
    far = jnp.float32(LANES)
    is_g = lane < N_GROUPS
    lgm = jnp.where(is_g, lg, neg)
    m = lgm.max(axis=-1, keepdims=True)
    gidx = jnp.where(lgm == m, lane, far).min(axis=-1, keepdims=True)
    gp = 1.0 / jnp.where(is_g, jnp.exp(lgm - m), 0.0).sum(axis=-1, keepdims=True)
    lo = N_GROUPS + EXPERTS_PER_GROUP * gidx
    is_e = (lane >= lo) & (lane < lo + EXPERTS_PER_GROUP)
    lem = jnp.where(is_e, lg, neg)
    l0 = lem.max(axis=-1, keepdims=True)
    i0 = jnp.where((lem == l0) & is_e, lane, far).min(axis=-1, keepdims=True)
    lem1 = jnp.where(lane == i0, neg, lem)
    l1 = lem1.max(axis=-1, keepdims=True)
    i1 = jnp.where((lem1 == l1) & is_e & (lane != i0), lane, far).min(axis=-1, keepdims=True)
    e1 = jnp.exp(l1 - l0)
    w0 = 1.0 / (1.0 + e1)
    w1 = e1 * w0
    out = jnp.where(lane == 0, i0 - N_GROUPS, 0.0)
    out = jnp.where(lane == 1, i1 - N_GROUPS, out)
    out = jnp.where(lane == 2, gp * w0, out)
    return jnp.where(lane == 3, gp * w1, out)


def _proj_res_kernel(a_ref, b_ref, wa_ref, wb_ref, x_ref, g1_ref, ng_ref, sh_ref, sc_ref, rh_ref, rl_ref,
                     xo_ref, h_ref, route_ref):
    o = _dot(a_ref[0], wa_ref[...]) + _dot(b_ref[0], wb_ref[...])
    x = x_ref[0] + g1_ref[0] * o
    xo_ref[0] = x
    h = _modnorm(x, ng_ref[...], sh_ref[0], sc_ref[0])
    _store_token_tiles(h_ref, 0, h)
    hh, hl = _split_bf16(h)
    route_ref[0] = _route(_dot3(hh, hl, rh_ref[...], rl_ref[...]))


def _proj_res(a, b, wa, wb, x, g1, ng, shift, scale, rh, rl):
    bsz, n, d = x.shape
    tm = min(512, n)
    row = lambda i, t: (i, t, 0)
    per_b = lambda i, t: (i, 0, 0)
    fixed = lambda i, t: (0, 0)
    return pl.pallas_call(
        _proj_res_kernel,
        grid=(bsz, n // tm),
        in_specs=[pl.BlockSpec((1, tm, a.shape[2]), row), pl.BlockSpec((1, tm, b.shape[2]), row),
                  pl.BlockSpec(wa.shape, fixed), pl.BlockSpec(wb.shape, fixed),
                  pl.BlockSpec((1, tm, d), row), pl.BlockSpec((1, 1, d), per_b), pl.BlockSpec((1, d), fixed),
                  pl.BlockSpec((1, 1, d), per_b), pl.BlockSpec((1, 1, d), per_b),
                  pl.BlockSpec((d, LANES), fixed), pl.BlockSpec((d, LANES), fixed)],
        out_specs=[pl.BlockSpec((1, tm, d), row),
                   pl.BlockSpec((tm * SUBLANES, LANES), lambda i, t: (i * (n // tm) + t, 0)),
                   pl.BlockSpec((1, tm, LANES), row)],
        out_shape=[jax.ShapeDtypeStruct((bsz, n, d), F32),
                   jax.ShapeDtypeStruct((bsz * n * SUBLANES, LANES), F32),
                   jax.ShapeDtypeStruct((bsz, n, LANES), F32)],
        compiler_params=_cparams(("parallel", "parallel"), 40),
        name="proj_res",
    )(a, b, wa, wb, x, g1, ng.reshape(1, d), shift, scale, rh, rl)


def _token_copy(src, src_tok, dst, dst_tok, sem):
    rows = lambda t: pl.ds(pl.multiple_of(t * SUBLANES, SUBLANES), SUBLANES)
    return pltpu.make_async_copy(src.at[rows(src_tok)], dst.at[rows(dst_tok)], sem)


def _tile_copy(src, src_tok, dst, dst_tok, sem):
    n = MOE_TILE * SUBLANES
    rows = lambda t: pl.ds(pl.multiple_of(t * SUBLANES, SUBLANES), n)
    return pltpu.make_async_copy(src.at[rows(src_tok)], dst.at[rows(dst_tok)], sem)


def _experts_kernel(te_ref, src_ref, srcn_ref, dst_ref, h_ref, wg_ref, wu_ref, wd_ref, ys_ref,
                    xbuf, obuf, gsem, ssem):
    del te_ref
    i = pl.program_id(0)
    nt = pl.num_programs(0)
    slot = i % 2
    other = 1 - slot

    def start_gather(idx_ref, s):
        def body(j, c):
            for k in range(2):
                r = 2 * j + k
                _token_copy(h_ref, idx_ref[0, 0, r], xbuf, s * MOE_TILE + r, gsem.at[s]).start(priority=k)
            return c
        lax.fori_loop(0, MOE_TILE // 2, body, 0, unroll=4)

    @pl.when(i == 0)
    def _():
        start_gather(src_ref, 0)

    @pl.when(i + 1 < nt)
    def _():
        start_gather(srcn_ref, other)

    base = slot * MOE_TILE
    _tile_copy(h_ref, 0, xbuf, base, gsem.at[slot]).wait()

    @pl.when(i >= 2)
    def _():
        _tile_copy(obuf, base, ys_ref, 0, ssem.at[slot]).wait()

    x = _load_token_tiles(xbuf, base, MOE_TILE).astype(BF16)
    gt = _dot(x, wg_ref[0, 0].astype(BF16))
    hid = (gt * _sigmoid(gt)) * _dot(x, wu_ref[0, 0].astype(BF16))
    _store_token_tiles(obuf, base, _dot(hid.astype(BF16), wd_ref[0, 0].astype(BF16)))

    def scatter(j, c):
        for k in range(2):
            r = 2 * j + k
            _token_copy(obuf, base + r, ys_ref, dst_ref[0, 0, r], ssem.at[slot]).start(priority=k)
        return c
    lax.fori_loop(0, MOE_TILE // 2, scatter, 0, unroll=4)

    @pl.when(i == nt - 1)
    def _():
        _tile_copy(obuf, base, ys_ref, 0, ssem.at[slot]).wait()
        _tile_copy(obuf, other * MOE_TILE, ys_ref, 0, ssem.at[other]).wait()


def _experts(tile_expert, src_tok, dst_row, h, layer, wg, wu, wd):
    nt = tile_expert.shape[0]
    d = D_MODEL
    wmap = lambda i, te: (layer, te[i], 0, 0)
    idx = lambda a: a.reshape(nt, 1, MOE_TILE)
    smem = lambda f: pl.BlockSpec((1, 1, MOE_TILE), f, memory_space=pltpu.SMEM)
    tile_rows = MOE_TILE * SUBLANES
    return pl.pallas_call(
        _experts_kernel,
        grid_spec=pltpu.PrefetchScalarGridSpec(
            num_scalar_prefetch=1, grid=(nt,),
            in_specs=[smem(lambda i, te: (i, 0, 0)),
                      smem(lambda i, te: (jnp.minimum(i + 1, nt - 1), 0, 0)),
                      smem(lambda i, te: (i, 0, 0)),
                      pl.BlockSpec(memory_space=pl.ANY),
                      pl.BlockSpec((1, 1, d, D_EXPERT), wmap), pl.BlockSpec((1, 1, d, D_EXPERT), wmap),
                      pl.BlockSpec((1, 1, D_EXPERT, d), wmap)],
            out_specs=pl.BlockSpec(memory_space=pl.ANY),
            scratch_shapes=[pltpu.VMEM((2 * tile_rows, LANES), F32), pltpu.VMEM((2 * tile_rows, LANES), F32),
                            pltpu.SemaphoreType.DMA((2,)), pltpu.SemaphoreType.DMA((2,))]),
        out_shape=jax.ShapeDtypeStruct((nt * tile_rows, LANES), F32),
        compiler_params=pltpu.CompilerParams(dimension_semantics=("arbitrary",), vmem_limit_bytes=40 * MIB,
                                             has_side_effects=True),
        name="moe_experts",
    )(tile_expert, idx(src_tok), idx(src_tok), idx(dst_row), h, wg, wu, wd)


def _combine_kernel(route_ref, x_ref, g2_ref, fg_ref, ys_ref, o_ref, *, tc, final):
    rt = route_ref[0]
    g0, g1 = rt[:, 2:3], rt[:, 3:4]
    cols = [g0 * ys_ref[pl.ds(s, tc, stride=2 * SUBLANES), :]
            + g1 * ys_ref[pl.ds(SUBLANES + s, tc, stride=2 * SUBLANES), :] for s in range(SUBLANES)]
    x = x_ref[0] + g2_ref[0] * jnp.concatenate(cols, axis=-1)
    if final:
        x = _rms(x, fg_ref[...], x.shape[-1])
    o_ref[0] = x


def _combine(route, x, g2, fg, ys, tok_off, final):
    bsz, n, d = x.shape
    tc = min(256, n)
    nt = n // tc
    row = lambda i, t: (i, t, 0)
    blk = tok_off // tc
    return pl.pallas_call(
        functools.partial(_combine_kernel, tc=tc, final=final),
        grid=(bsz, nt),
        in_specs=[pl.BlockSpec((1, tc, LANES), row), pl.BlockSpec((1, tc, d), row),
                  pl.BlockSpec((1, 1, d), lambda i, t: (i, 0, 0)), pl.BlockSpec((1, d), lambda i, t: (0, 0)),
                  pl.BlockSpec((2 * tc * SUBLANES, LANES), lambda i, t: (blk + i * nt + t, 0))],
        out_specs=pl.BlockSpec((1, tc, d), row),
        out_shape=jax.ShapeDtypeStruct((bsz, n, d), F32),
        compiler_params=_cparams(("parallel", "parallel"), 32),
        name="moe_combine",
    )(route, x, g2, fg.reshape(1, d), ys)


def _moe_plan(eids):
    a = eids.shape[0]
    order = jnp.argsort(eids, stable=True).astype(I32)
    counts = (eids[:, None] == jnp.arange(N_EXPERTS, dtype=I32)[None, :]).astype(I32).sum(axis=0)
    padded = ((counts + MOE_TILE - 1) // MOE_TILE) * MOE_TILE
    ends = jnp.cumsum(padded)
    first = jnp.cumsum(counts) - counts
    nt = a // MOE_TILE + N_EXPERTS
    starts = jnp.arange(nt, dtype=I32) * MOE_TILE
    tile_expert = jnp.minimum((starts[:, None] >= ends[None, :]).astype(I32).sum(axis=1), N_EXPERTS - 1)
    tile_pos = starts - (ends - padded)[tile_expert]
    tile_real = jnp.where(starts < ends[-1], jnp.clip(counts[tile_expert] - tile_pos, 0, MOE_TILE), 0)
    slot = jnp.arange(nt * MOE_TILE, dtype=I32)
    e = jnp.repeat(tile_expert, MOE_TILE)
    pos = slot - (ends - padded)[e]
    real = (slot % MOE_TILE) < jnp.repeat(tile_real, MOE_TILE)
    assign = order[jnp.clip(first[e] + pos, 0, a - 1)]
    src_tok = jnp.where(real, assign // 2, 0)
    pad_rank = jnp.cumsum(jnp.logical_not(real).astype(I32)) - 1
    dst_row = jnp.where(real, assign, a + pad_rank)
    return src_tok, dst_row, tile_expert.astype(I32)


def _moe(streams, layer, wg, wu, wd, final_g, final):
    eids = jnp.concatenate([s[2][..., :2].reshape(-1) for s in streams]).astype(I32)
    src_tok, dst_row, tile_expert = _moe_plan(eids)
    h = streams[0][1] if len(streams) == 1 else jnp.concatenate([s[1] for s in streams], axis=0)
    ys = _experts(tile_expert, src_tok, dst_row, h, layer, wg, wu, wd)
    outs, off = [], 0
    for x, _, route, g2 in streams:
        outs.append(_combine(route, x, g2, final_g, ys, off, final))
        off += x.shape[0] * x.shape[1]
    return outs


def _rope_tables(n, half, offset, identity):
    cs = jnp.ones((n, LANES), F32)
    z = jnp.zeros((n, LANES), F32)
    if identity:
        return cs, z, z
    rows = (jnp.arange(n, dtype=I32) // GRID_W).astype(F32)
    cols = (jnp.arange(n, dtype=I32) % GRID_W).astype(F32)
    freqs = ROPE_THETA ** (-jnp.arange(half, dtype=F32) / half)
    ar, ac = rows[:, None] * freqs, cols[:, None] * freqs
    zero = jnp.zeros((n, half), F32)
    cos4 = jnp.concatenate([jnp.cos(ar), jnp.cos(ar), jnp.cos(ac), jnp.cos(ac)], axis=1)
    s1 = jnp.concatenate([-jnp.sin(ar), zero, -jnp.sin(ac), zero], axis=1)
    s2 = jnp.concatenate([zero, jnp.sin(ar), zero, jnp.sin(ac)], axis=1)
    place = lambda base, t: lax.dynamic_update_slice(base, t, (0, offset))
    return place(cs, cos4), place(z, s1), place(z, s2)


def _pad_heads(w, nheads, width, left=0):
    k = w.shape[0]
    w = w.reshape(k, nheads, width)
    w = jnp.pad(w, ((0, 0), (0, 0), (left, LANES - width - left)))
    return w.reshape(k, nheads * LANES)


def _lane_pad(v, left=0):
    return jnp.pad(v, (left, LANES - v.shape[0] - left)).reshape(1, LANES)


def kernel(x, c, ctx, c_ctx, w_mod, b_mod, norm_g, e_w_in, e_hy_conv_w, e_hy_conv_b, e_hy_w1, e_hy_b1, e_hy_w2, e_hy_b2, e_hy_w3, e_hy_freq, e_hy_fbias, e_qk_g, e_w_out, o_w_in, o_q_norm_g, o_w_uq, o_kv_norm_g, o_w_ukv, o_s5_a_re, o_s5_a_im, o_s5_log_dt, o_s5_b_re, o_s5_b_im, o_s5_c_re, o_s5_c_im, o_s5_d, o_glu_w, o_glu_b, o_w_out, moe_w_rg, moe_w_re, moe_w_gate, moe_w_up, moe_w_down, final_g):
    bsz, n, d = x.shape
    depth = w_mod.shape[0]
    xc = ctx
    ncx = ctx.shape[1]
    cc = jnp.zeros((bsz + 8, d), F32).at[:bsz].set(c).at[bsz].set(c_ctx)

    for i in range(depth):
        j = i // 2
        last = i == depth - 1
        mods = _modulation(cc, w_mod, b_mod, i)
        lat = [m.reshape(bsz, 1, d) for m in jnp.split(mods[:bsz], 6, axis=-1)]
        cxm = [jnp.broadcast_to(m.reshape(1, 1, d), (bsz, 1, d)) for m in jnp.split(mods[bsz], 6, axis=-1)]

        if i % 2 == 0:
            wq = _pad_heads(e_w_in[j][:, 3 * A_WIDTH:3 * A_WIDTH + B_Q_HEADS * HEAD_DIM], B_Q_HEADS, HEAD_DIM)
            k0 = 3 * A_WIDTH + B_Q_HEADS * HEAD_DIM
            wk = _pad_heads(e_w_in[j][:, k0:k0 + B_KV_HEADS * HEAD_DIM], B_KV_HEADS, HEAD_DIM)
            wv = _pad_heads(e_w_in[j][:, k0 + B_KV_HEADS * HEAD_DIM:], B_KV_HEADS, HEAD_DIM)
            w_in = jnp.concatenate([e_w_in[j][:, :3 * A_WIDTH], wq, wk, wv], axis=1).astype(BF16)
            qg, kg = _lane_pad(e_qk_g[j, 0]), _lane_pad(e_qk_g[j, 1])
            filt = (e_hy_w1[j], e_hy_b1[j], e_hy_w2[j], e_hy_b2[j], e_hy_w3[j], e_hy_freq[j])
            qscale = HEAD_DIM ** -0.5 * LOG2E
            hy_l, q_l, k_l, v_l = _even_in(x, norm_g[i, 0], lat[0], lat[1], w_in, qg, kg,
                                           _rope_tables(n, HEAD_DIM // 4, 0, False), qscale)
            hy_c, q_c, k_c, v_c = _even_in(xc, norm_g[i, 0], cxm[0], cxm[1], w_in, qg, kg,
                                           _rope_tables(ncx, HEAD_DIM // 4, 0, True), qscale)
            rep = B_Q_HEADS // B_KV_HEADS
            att_l = _attention(q_l, [k_c, k_l], [v_c, v_l], rep, 1)
            mix_l = (_hyena(hy_l, e_hy_conv_w[j], e_hy_conv_b[j], filt, e_hy_fbias[j]), att_l)
            if not last:
                att_c = _attention(q_c, [k_c], [v_c], rep, 1)
                mix_c = (_hyena(hy_c, e_hy_conv_w[j], e_hy_conv_b[j], filt, e_hy_fbias[j]), att_c)
            w_out = e_w_out[j]
            split = A_WIDTH
        else:
            w1 = o_w_in[j]
            c0, c1, c2 = MLA_Q_RANK, MLA_Q_RANK + MLA_KV_RANK, MLA_Q_RANK + MLA_KV_RANK + MLA_ROPE
            w_kr = jnp.pad(w1[:, c1:c2], ((0, 0), (MLA_NOPE, LANES - MLA_NOPE - MLA_ROPE)))
            w_in = jnp.concatenate([w1[:, :c1], w_kr, w1[:, c2:]], axis=1).astype(BF16)
            wuq = _pad_heads(o_w_uq[j], MLA_HEADS, MLA_NOPE + MLA_ROPE).astype(BF16)
            wkv = o_w_ukv[j].reshape(MLA_KV_RANK, MLA_HEADS, MLA_NOPE + MLA_V)
            wuk = _pad_heads(wkv[:, :, :MLA_NOPE].reshape(MLA_KV_RANK, -1), MLA_HEADS, MLA_NOPE).astype(BF16)
            wuv = _pad_heads(wkv[:, :, MLA_NOPE:].reshape(MLA_KV_RANK, -1), MLA_HEADS, MLA_V).astype(BF16)
            qg = o_q_norm_g[j].reshape(1, MLA_Q_RANK)
            kvg = o_kv_norm_g[j].reshape(1, MLA_KV_RANK)
            args = (w_in, qg, kvg, wuq, wuk, wuv)
            qscale = (MLA_NOPE + MLA_ROPE) ** -0.5 * LOG2E
            u_l, k_l, v_l, q_l = _odd_in(x, norm_g[i, 0], lat[0], lat[1], *args,
                                         _rope_tables(n, MLA_ROPE // 4, MLA_NOPE, False), True, qscale)
            outs_c = _odd_in(xc, norm_g[i, 0], cxm[0], cxm[1], *args,
                             _rope_tables(ncx, MLA_ROPE // 4, MLA_NOPE, True), not last, qscale)
            u_c, k_c, v_c = outs_c[:3]
            att_l = _attention(q_l, [k_c, k_l], [v_c, v_l], 2, 2)
            mats = _s5_matrices(o_s5_a_re[j], o_s5_a_im[j], o_s5_log_dt[j], o_s5_b_re[j], o_s5_b_im[j],
                                o_s5_c_re[j], o_s5_c_im[j], o_s5_d[j])
            glu_w = o_glu_w[j].astype(BF16)
            mix_l = (att_l, _s5_mixer(u_c, u_l, mats, glu_w, o_glu_b[j]))
            if not last:
                raise NotImplementedError("an odd layer that is not the last needs the context S5 readout")
            w_out = o_w_out[j]
            split = MLA_HEADS * MLA_V

        wa, wb = w_out[:split].astype(BF16), w_out[split:].astype(BF16)
        router = jnp.pad(jnp.concatenate([moe_w_rg[i], moe_w_re[i]], axis=1),
                         ((0, 0), (0, LANES - N_GROUPS - N_EXPERTS)))
        rh = router.astype(BF16)
        rl = (router - rh.astype(F32)).astype(BF16)

        x, h_l, route_l = _proj_res(*mix_l, wa, wb, x, lat[2], norm_g[i, 1], lat[3], lat[4], rh, rl)
        streams = [(x, h_l, route_l, lat[5])]
        if not last:
            xc, h_c, route_c = _proj_res(*mix_c, wa, wb, xc, cxm[2], norm_g[i, 1], cxm[3], cxm[4], rh, rl)
            streams.append((xc, h_c, route_c, cxm[5]))
        outs = _moe(streams, i, moe_w_gate, moe_w_up, moe_w_down, final_g, last)
        x = outs[0]
        if not last:
            xc = outs[1]
    return x
```

```python
import functools
import math

import jax
import jax.numpy as jnp
from jax import lax
from jax.experimental import pallas as pl
from jax.experimental.pallas import tpu as pltpu

F32 = jnp.float32
BF16 = jnp.bfloat16
I32 = jnp.int32

D_MODEL = 1024
GRID_W = 64
HEAD_DIM = 64
A_WIDTH = D_MODEL // 2
B_Q_HEADS = (D_MODEL - A_WIDTH) // HEAD_DIM
B_KV_HEADS = max(1, B_Q_HEADS // 4)
HY_ORDER = 2
HY_BANDS = 16
HY_TARGET = 1e-2
HY_MAX_DECAY = math.log(1.0 / HY_TARGET) / 0.3
HY_MIN_DECAY = math.log(1.0 / HY_TARGET) / 1.5
MLA_HEADS = 8
MLA_NOPE = 64
MLA_ROPE = 32
MLA_V = 64
MLA_Q_RANK = D_MODEL // 4
MLA_KV_RANK = D_MODEL // 8
S5_WIDTH = D_MODEL - MLA_HEADS * MLA_V
S5_GROUP = 16
S5_GROUPS = S5_WIDTH // S5_GROUP
S5_STATE = 64
S5_CHUNK = 16
S5_TILE_GROUPS = 128 // S5_GROUP
N_GROUPS = 4
EXPERTS_PER_GROUP = 8
N_EXPERTS = N_GROUPS * EXPERTS_PER_GROUP
D_EXPERT = D_MODEL // 4
ROPE_THETA = 10000.0
EPS = 1e-6
LOG2E = 1.4426950408889634

LANES = 128
SUBLANES = 8
MOE_TILE = 256
ROW_CHAIN = 256
HIGHEST = lax.Precision.HIGHEST
MIB = 1024 * 1024


def _cparams(sem, vmem_mib):
    return pltpu.CompilerParams(dimension_semantics=sem, vmem_limit_bytes=vmem_mib * MIB)


def _dot(a, b):
    return jnp.dot(a, b, preferred_element_type=F32)


def _dot_nt(a, b):
    return lax.dot_general(a, b, (((1,), (1,)), ((), ())), preferred_element_type=F32)


def _split_bf16(a):
    hi = a.astype(BF16)
    lo = (a - hi.astype(F32)).astype(BF16)
    return hi, lo


def _dot3(ah, al, bh, bl):
    return _dot(ah, bh) + (_dot(ah, bl) + _dot(al, bh))


def _sigmoid(x):
    return 1.0 / (1.0 + jnp.exp(-x))


def _rms(x, g, width):
    ms = jnp.sum(x * x, axis=-1, keepdims=True) * (1.0 / width)
    return x * lax.rsqrt(ms + EPS) * g


def _modnorm(x, g, shift, scale):
    return _rms(x, g, x.shape[-1]) * (1.0 + scale) + shift


def _rope(t, cs, s1, s2, half):
    return t * cs + pltpu.roll(t, LANES - half, 1) * s1 + pltpu.roll(t, half, 1) * s2


def _store_token_tiles(ref, base_tok, val):
    n = val.shape[0]
    for s in range(SUBLANES):
        ref[pl.ds(base_tok * SUBLANES + s, n, stride=SUBLANES), :] = val[:, LANES * s: LANES * (s + 1)]


def _load_token_tiles(ref, base_tok, n):
    return jnp.concatenate([ref[pl.ds(base_tok * SUBLANES + s, n, stride=SUBLANES), :]
                            for s in range(SUBLANES)], axis=-1)


def _short_conv(u, w, b):
    n = u.shape[0]
    row = lax.broadcasted_iota(I32, u.shape, 0)
    prev = jnp.where(row == 0, 0.0, pltpu.roll(u, 1, 0))
    nxt = jnp.where(row == n - 1, 0.0, pltpu.roll(u, n - 1, 0))
    return prev * w[0:1] + u * w[1:2] + nxt * w[2:3] + b


def _mod_kernel(c_ref, w_ref, b_ref, o_ref):
    c = c_ref[...]
    s = c * _sigmoid(c)
    sh, sl = _split_bf16(s)
    wh, wl = _split_bf16(w_ref[0])
    o_ref[...] = _dot3(sh, sl, wh, wl) + b_ref[0]


def _modulation(cc, w, b, layer):
    r, d = cc.shape
    n = w.shape[2]
    tn = 1536
    return pl.pallas_call(
        _mod_kernel,
        grid=(n // tn,),
        in_specs=[pl.BlockSpec((r, d), lambda j: (0, 0)),
                  pl.BlockSpec((1, d, tn), lambda j: (layer, 0, j)),
                  pl.BlockSpec((1, 1, tn), lambda j: (layer, 0, j))],
        out_specs=pl.BlockSpec((r, tn), lambda j: (0, j)),
        out_shape=jax.ShapeDtypeStruct((r, n), F32),
        compiler_params=_cparams(("parallel",), 48),
        name="modulation",
    )(cc, w, b.reshape(b.shape[0], 1, n))


def _with_ones(v):
    lane = lax.broadcasted_iota(I32, v.shape, 1) % LANES
    return jnp.where(lane >= HEAD_DIM, 1.0, v)


def _row_chunks(tm):
    sub = min(ROW_CHAIN, tm)
    return [slice(r, r + sub) for r in range(0, tm, sub)]


def _even_in_kernel(x_ref, g_ref, sh_ref, sc_ref, w_ref, qg_ref, kg_ref, cs_ref, s1_ref, s2_ref,
                    hy_ref, q_ref, k_ref, v_ref, *, qscale):
    hw = 3 * A_WIDTH
    kw = hw + LANES * B_Q_HEADS
    vw = kw + LANES * B_KV_HEADS
    for rows in _row_chunks(x_ref.shape[1]):
        xn = _modnorm(x_ref[0, rows, :], g_ref[...], sh_ref[0], sc_ref[0]).astype(BF16)
        p = _dot(xn, w_ref[...])
        hy_ref[0, rows, :] = p[:, :hw].astype(BF16)
        cs, s1, s2 = cs_ref[rows, :], s1_ref[rows, :], s2_ref[rows, :]
        for h in range(B_Q_HEADS):
            t = _rms(p[:, hw + LANES * h: hw + LANES * (h + 1)], qg_ref[...], HEAD_DIM)
            q_ref[0, rows, LANES * h: LANES * (h + 1)] = (
                _rope(t, cs, s1, s2, HEAD_DIM // 4) * qscale).astype(BF16)
        for h in range(B_KV_HEADS):
            t = _rms(p[:, kw + LANES * h: kw + LANES * (h + 1)], kg_ref[...], HEAD_DIM)
            k_ref[0, rows, LANES * h: LANES * (h + 1)] = _rope(t, cs, s1, s2, HEAD_DIM // 4).astype(BF16)
        v_ref[0, rows, :] = _with_ones(p[:, vw:]).astype(BF16)


def _even_in(x, g, shift, scale, w, qg, kg, tabs, qscale):
    bsz, n, d = x.shape
    tm = min(512, n)
    nw = w.shape[1]
    row = lambda b, t: (b, t, 0)
    per_b = lambda b, t: (b, 0, 0)
    fixed = lambda b, t: (0, 0)
    tab = pl.BlockSpec((tm, LANES), lambda b, t: (t, 0))
    return pl.pallas_call(
        functools.partial(_even_in_kernel, qscale=qscale),
        grid=(bsz, n // tm),
        in_specs=[pl.BlockSpec((1, tm, d), row), pl.BlockSpec((1, d), fixed),
                  pl.BlockSpec((1, 1, d), per_b), pl.BlockSpec((1, 1, d), per_b),
                  pl.BlockSpec((d, nw), fixed), pl.BlockSpec((1, LANES), fixed),
                  pl.BlockSpec((1, LANES), fixed), tab, tab, tab],
        out_specs=[pl.BlockSpec((1, tm, 3 * A_WIDTH), row),
                   pl.BlockSpec((1, tm, LANES * B_Q_HEADS), row),
                   pl.BlockSpec((1, tm, LANES * B_KV_HEADS), row),
                   pl.BlockSpec((1, tm, LANES * B_KV_HEADS), row)],
        out_shape=[jax.ShapeDtypeStruct((bsz, n, 3 * A_WIDTH), BF16),
                   jax.ShapeDtypeStruct((bsz, n, LANES * B_Q_HEADS), BF16),
                   jax.ShapeDtypeStruct((bsz, n, LANES * B_KV_HEADS), BF16),
                   jax.ShapeDtypeStruct((bsz, n, LANES * B_KV_HEADS), BF16)],
        compiler_params=_cparams(("parallel", "parallel"), 48),
        name="even_in",
    )(x, g.reshape(1, d), shift, scale, w, qg, kg, *tabs)


def _odd_in_kernel(x_ref, g_ref, sh_ref, sc_ref, w_ref, qg_ref, kvg_ref, wuq_ref, wuk_ref, wuv_ref,
                   cs_ref, s1_ref, s2_ref, *out_refs, need_q, qscale):
    if need_q:
        u_ref, k_ref, v_ref, q_ref = out_refs
    else:
        u_ref, k_ref, v_ref = out_refs
    c0, c1, c2 = MLA_Q_RANK, MLA_Q_RANK + MLA_KV_RANK, MLA_Q_RANK + MLA_KV_RANK + LANES
    for rows in _row_chunks(x_ref.shape[1]):
        xn = _modnorm(x_ref[0, rows, :], g_ref[...], sh_ref[0], sc_ref[0]).astype(BF16)
        p = _dot(xn, w_ref[...])
        u_ref[0, rows, :] = p[:, c2:]
        cs, s1, s2 = cs_ref[rows, :], s1_ref[rows, :], s2_ref[rows, :]
        ckv = _rms(p[:, c0:c1], kvg_ref[...], MLA_KV_RANK).astype(BF16)
        kn = _dot(ckv, wuk_ref[...])
        v_ref[0, rows, :] = _with_ones(_dot(ckv, wuv_ref[...])).astype(BF16)
        kr = _rope(p[:, c1:c2], cs, s1, s2, MLA_ROPE // 4)
        for h in range(MLA_HEADS):
            k_ref[0, rows, LANES * h: LANES * (h + 1)] = (kn[:, LANES * h: LANES * (h + 1)] + kr).astype(BF16)
        if need_q:
            cq = _rms(p[:, :c0], qg_ref[...], MLA_Q_RANK).astype(BF16)
            q = _dot(cq, wuq_ref[...])
            for h in range(MLA_HEADS):
                t = q[:, LANES * h: LANES * (h + 1)]
                q_ref[0, rows, LANES * h: LANES * (h + 1)] = (
                    _rope(t, cs, s1, s2, MLA_ROPE // 4) * qscale).astype(BF16)


def _odd_in(x, g, shift, scale, w, qg, kvg, wuq, wuk, wuv, tabs, need_q, qscale):
    bsz, n, d = x.shape
    tm = min(512, n)
    hw = LANES * MLA_HEADS
    row = lambda b, t: (b, t, 0)
    per_b = lambda b, t: (b, 0, 0)
    fixed = lambda b, t: (0, 0)
    tab = pl.BlockSpec((tm, LANES), lambda b, t: (t, 0))
    widths = [S5_WIDTH, hw, hw] + ([hw] if need_q else [])
    return pl.pallas_call(
        functools.partial(_odd_in_kernel, need_q=need_q, qscale=qscale),
        grid=(bsz, n // tm),
        in_specs=[pl.BlockSpec((1, tm, d), row), pl.BlockSpec((1, d), fixed),
                  pl.BlockSpec((1, 1, d), per_b), pl.BlockSpec((1, 1, d), per_b),
                  pl.BlockSpec(w.shape, fixed), pl.BlockSpec((1, MLA_Q_RANK), fixed),
                  pl.BlockSpec((1, MLA_KV_RANK), fixed), pl.BlockSpec(wuq.shape, fixed),
                  pl.BlockSpec(wuk.shape, fixed), pl.BlockSpec(wuv.shape, fixed), tab, tab, tab],
        out_specs=[pl.BlockSpec((1, tm, wd), row) for wd in widths],
        out_shape=[jax.ShapeDtypeStruct((bsz, n, wd), F32 if k == 0 else BF16) for k, wd in enumerate(widths)],
        compiler_params=_cparams(("parallel", "parallel"), 48),
        name="odd_in_q" if need_q else "odd_in",
    )(x, g.reshape(1, d), shift, scale, w, qg, kvg, wuq, wuk, wuv, *tabs)


def _hy_fwd_kernel(wc_ref, ws_ref, u_ref, cw_ref, cb_ref, ka_ref, kb_ref, ya_ref, yb_ref, ubf_ref,
                   *, conv, tf):
    f = pl.program_id(1)

    @pl.when(f == 0)
    def _():
        u = u_ref[0].astype(F32)
        if conv:
            u = _short_conv(u, cw_ref[...], cb_ref[...])
        ubf_ref[...] = u.astype(BF16)

    ub = ubf_ref[...]
    a = _dot(wc_ref[...], ub)
    b = _dot(ws_ref[...], ub)
    ka, kb = ka_ref[...], kb_ref[...]
    first = (lax.broadcasted_iota(I32, a.shape, 0) + f * tf) == 0
    ya_ref[0] = jnp.where(first, a * ka, a * ka - b * kb).astype(BF16)
    yb_ref[0] = jnp.where(first, b * kb, a * kb + b * ka).astype(BF16)


def _hy_fwd(wc, ws, u, col, cw, cb, ka, kb, conv):
    bsz, n, _ = u.shape
    c = A_WIDTH
    tf = min(512, n)
    return pl.pallas_call(
        functools.partial(_hy_fwd_kernel, conv=conv, tf=tf),
        grid=(bsz, n // tf),
        in_specs=[pl.BlockSpec((tf, n), lambda b, f: (f, 0)), pl.BlockSpec((tf, n), lambda b, f: (f, 0)),
                  pl.BlockSpec((1, n, c), lambda b, f: (b, 0, col)),
                  pl.BlockSpec((3, c), lambda b, f: (0, 0)), pl.BlockSpec((1, c), lambda b, f: (0, 0)),
                  pl.BlockSpec((tf, c), lambda b, f: (f, 0)), pl.BlockSpec((tf, c), lambda b, f: (f, 0))],
        out_specs=[pl.BlockSpec((1, tf, c), lambda b, f: (b, f, 0))] * 2,
        out_shape=[jax.ShapeDtypeStruct((bsz, n, c), BF16)] * 2,
        scratch_shapes=[pltpu.VMEM((n, c), BF16)],
        compiler_params=_cparams(("parallel", "arbitrary"), 48),
        name="hyena_fwd",
    )(wc, ws, u, cw, cb, ka, kb)


def _hy_inv_kernel(ci_ref, si_ref, ya_ref, yb_ref, a_ref, g_ref, cwa_ref, cba_ref, cwg_ref, cbg_ref,
                   bias_ref, o_ref, a_sc, g_sc, *, conv_a, tt):
    t = pl.program_id(1)

    @pl.when(t == 0)
    def _():
        a = a_ref[0].astype(F32)
        if conv_a:
            a = _short_conv(a, cwa_ref[...], cba_ref[...])
        a_sc[...] = a
        g_sc[...] = _short_conv(g_ref[0].astype(F32), cwg_ref[...], cbg_ref[...])

    y = _dot(ci_ref[...], ya_ref[0]) + _dot(si_ref[...], yb_ref[0])
    rows = pl.ds(pl.multiple_of(t * tt, tt), tt)
    y = y + a_sc[rows, :] * bias_ref[...]
    o_ref[0] = (g_sc[rows, :] * y).astype(BF16)


def _hy_inv(ci, si, ya, yb, a, a_col, g, g_col, cwa, cba, cwg, cbg, bias, conv_a):
    bsz, n, c = ya.shape
    tt = min(512, n)
    whole = lambda b, t: (b, 0, 0)
    fixed = lambda b, t: (0, 0)
    return pl.pallas_call(
        functools.partial(_hy_inv_kernel, conv_a=conv_a, tt=tt),
        grid=(bsz, n // tt),
        in_specs=[pl.BlockSpec((tt, n), lambda b, t: (t, 0)), pl.BlockSpec((tt, n), lambda b, t: (t, 0)),
                  pl.BlockSpec((1, n, c), whole), pl.BlockSpec((1, n, c), whole),
                  pl.BlockSpec((1, n, c), lambda b, t: (b, 0, a_col)),
                  pl.BlockSpec((1, n, c), lambda b, t: (b, 0, g_col)),
                  pl.BlockSpec((3, c), fixed), pl.BlockSpec((1, c), fixed),
                  pl.BlockSpec((3, c), fixed), pl.BlockSpec((1, c), fixed), pl.BlockSpec((1, c), fixed)],
        out_specs=pl.BlockSpec((1, tt, c), lambda b, t: (b, t, 0)),
        out_shape=jax.ShapeDtypeStruct((bsz, n, c), BF16),
        scratch_shapes=[pltpu.VMEM((n, c), F32), pltpu.VMEM((n, c), F32)],
        compiler_params=_cparams(("parallel", "arbitrary"), 56),
        name="hyena_inv",
    )(ci, si, ya, yb, a, g, cwa, cba, cwg, cbg, bias)


def _dft_mats(n):
    big = 2 * n
    f = jnp.arange(n, dtype=I32)[:, None]
    s = jnp.arange(n, dtype=I32)[None, :]
    step = 64 if n % 64 == 0 else 1
    angle = lambda fr: ((fr[:, None] * s) % big).astype(F32) * (2.0 * math.pi / big)
    a_hi = angle(jnp.arange(n // step, dtype=I32) * step)
    a_lo = angle(jnp.arange(step, dtype=I32))
    c1, s1, c2, s2 = jnp.cos(a_hi)[:, None], jnp.sin(a_hi)[:, None], jnp.cos(a_lo)[None], jnp.sin(a_lo)[None]
    cos = (c1 * c2 - s1 * s2).reshape(n, n)
    sin = (s1 * c2 + c1 * s2).reshape(n, n)
    alt = jnp.where(jnp.arange(n) % 2 == 0, 1.0, -1.0).astype(F32)
    wc = cos
    ws = jnp.where(f == 0, alt[None, :], -sin)
    ci = jnp.where(s == 0, 1.0 / big, (2.0 / big) * cos)
    si = jnp.where(s == 0, alt[:, None] / big, -(2.0 / big) * sin)
    return wc.astype(BF16), ws.astype(BF16), ci.astype(BF16), si.astype(BF16)


def _hyena_filter_spectrum(n, w1, b1, w2, b2, w3, freq):
    t = jnp.arange(n, dtype=F32)
    t_norm = t / max(n - 1, 1)
    bands = jnp.linspace(1e-4, HY_BANDS - 1, HY_BANDS, dtype=F32)
    ang = (2.0 * math.pi / n) * t[:, None] * bands
    z = jnp.concatenate([t_norm[:, None], jnp.cos(ang), jnp.sin(ang)], axis=-1)
    hdn = jnp.sin(freq * (jnp.dot(z, w1, precision=HIGHEST) + b1))
    hdn = jnp.sin(freq * (jnp.dot(hdn, w2, precision=HIGHEST) + b2))
    taps = jnp.dot(hdn, w3, precision=HIGHEST).reshape(n, 2, HY_ORDER, A_WIDTH)
    deltas = jnp.linspace(HY_MIN_DECAY, HY_MAX_DECAY, A_WIDTH, dtype=F32)
    taps = taps * jnp.exp(-t_norm[:, None] * deltas)[:, None, None, :]
    fwd, bwd = taps[:, 0], taps[:, 1]
    circ = jnp.concatenate([fwd, jnp.zeros((1, HY_ORDER, A_WIDTH), F32), bwd[:0:-1]], axis=0)
    circ = circ * lax.rsqrt(jnp.sum(circ * circ, axis=0, keepdims=True) + EPS)
    kf = jnp.fft.rfft(circ, axis=0)
    ka = jnp.real(kf[:n]).astype(F32)
    kb = jnp.concatenate([jnp.real(kf[n:n + 1]), jnp.imag(kf[1:n])], axis=0).astype(F32)
    return ka, kb


def _hyena(hy, cw, cb, filt, fbias):
    n = hy.shape[1]
    c = A_WIDTH
    wc, ws, ci, si = _dft_mats(n)
    ka, kb = _hyena_filter_spectrum(n, *filt)
    cws = [cw[:, i * c:(i + 1) * c] for i in range(3)]
    cbs = [cb[i * c:(i + 1) * c].reshape(1, c) for i in range(3)]
    ya, yb = _hy_fwd(wc, ws, hy, 2, cws[2], cbs[2], ka[:, 0], kb[:, 0], True)
    z = _hy_inv(ci, si, ya, yb, hy, 2, hy, 0, cws[2], cbs[2], cws[0], cbs[0], fbias[0:1], True)
    ya, yb = _hy_fwd(wc, ws, z, 0, cws[2], cbs[2], ka[:, 1], kb[:, 1], False)
    return _hy_inv(ci, si, ya, yb, z, 0, hy, 1, cws[2], cbs[2], cws[1], cbs[1], fbias[1:2], False)


def _attn_kernel(*refs, nseg, hps, kps):
    q_ref = refs[0]
    k_refs = refs[1:1 + nseg]
    v_refs = refs[1 + nseg:1 + 2 * nseg]
    o_ref = refs[1 + 2 * nseg]
    low = lax.broadcasted_iota(I32, (q_ref.shape[1], LANES), 1) < HEAD_DIM
    for pair in range(hps // 2):
        accs = []
        for half in range(2):
            i = 2 * pair + half
            kv = (i * kps) // hps
            q = q_ref[0, :, LANES * i: LANES * (i + 1)]
            ss = [_dot_nt(q, k[0, :, LANES * kv: LANES * (kv + 1)]) for k in k_refs]
            m = ss[0].max(axis=-1, keepdims=True)
            for s in ss[1:]:
                m = jnp.maximum(m, s.max(axis=-1, keepdims=True))
            acc = None
            for s, v in zip(ss, v_refs):
                p = jnp.exp2((s - m).astype(BF16))
                o = _dot(p, v[0, :, LANES * kv: LANES * (kv + 1)])
                acc = o if acc is None else acc + o
            accs.append(acc)
        num = jnp.where(low, accs[0], pltpu.roll(accs[1], HEAD_DIM, 1))
        den = jnp.where(low, pltpu.roll(accs[0], HEAD_DIM, 1), accs[1])
        o_ref[0, :, LANES * pair: LANES * (pair + 1)] = (num / den).astype(BF16)


def _attention(q, ks, vs, hps, kps):
    bsz, lq, qw = q.shape
    nh = qw // LANES
    tq = min(512, lq)
    nseg = len(ks)
    kv_spec = lambda a: pl.BlockSpec((1, a.shape[1], kps * LANES), lambda b, g, t: (b, 0, g))
    return pl.pallas_call(
        functools.partial(_attn_kernel, nseg=nseg, hps=hps, kps=kps),
        grid=(bsz, nh // hps, lq // tq),
        in_specs=[pl.BlockSpec((1, tq, hps * LANES), lambda b, g, t: (b, t, g))]
        + [kv_spec(a) for a in ks] + [kv_spec(a) for a in vs],
        out_specs=pl.BlockSpec((1, tq, hps * HEAD_DIM), lambda b, g, t: (b, t, g)),
        out_shape=jax.ShapeDtypeStruct((bsz, lq, nh * HEAD_DIM), BF16),
        compiler_params=_cparams(("parallel", "parallel", "parallel"), 56),
        name="attention",
    )(q, *ks, *vs)


def _onehot_bf16(cond):
    return jnp.where(cond, 1.0, 0.0).astype(BF16)


def _s5_kernel(uc_ref, ul_ref, w_ref, a_ref, of_ref, or_ref, y_ref, xs_sc, yi_sc, s_sc, x_sc, ys_sc,
               *, bb, ncc, ncl):
    lc, gh, gt = S5_CHUNK, S5_GROUP, S5_TILE_GROUPS
    nc = ncc + ncl
    cw = lc * gh
    hb = gh.bit_length() - 1
    lb = LANES.bit_length() - 1

    for b in range(bb):
        for j in range(lc):
            p, half = divmod(j, 2)
            lanes = slice(half * LANES, (half + 1) * LANES)
            xs_sc[p, b * nc: b * nc + ncc, lanes] = uc_ref[b, pl.ds(j, ncc, stride=lc), :].astype(BF16)
            xs_sc[p, b * nc + ncc: (b + 1) * nc, lanes] = ul_ref[b, pl.ds(j, ncl, stride=lc), :].astype(BF16)

    r_i = lax.broadcasted_iota(I32, (2 * LANES, cw), 0)
    c_i = lax.broadcasted_iota(I32, (2 * LANES, cw), 1)
    same_h = (r_i & (gh - 1)) == (c_i & (gh - 1))
    r_grp = (r_i & (LANES - 1)) >> hb
    c_off = (c_i >> hb) - (r_i >> lb)
    for pr in range(gt // 2):
        rs = []
        for side in range(2):
            q = 2 * pr + side
            u = None
            for p in range(lc // 2):
                sel = _onehot_bf16(same_h & (r_grp == q) & (c_off == 2 * p))
                d = _dot(xs_sc[p], sel)
                u = d if u is None else u + d
            r = _dot(u.astype(BF16), w_ref[q])
            yi_sc[q] = r[:, :cw]
            rs.append(r)
        for kind in range(4):
            lo = cw + kind * LANES
            s_sc[kind, pr] = rs[0][:, lo:lo + LANES] + rs[1][:, lo:lo + LANES]

    a = a_ref[0]
    npair = gt // 2

    def advance(st, c, d):
        rws = pl.ds(c, bb, stride=nc)
        new = []
        for t in range(npair):
            xr, xi = st[2 * t], st[2 * t + 1]
            ar = a[2 * d:2 * d + 1, t * LANES:(t + 1) * LANES]
            ai = a[2 * d + 1:2 * d + 2, t * LANES:(t + 1) * LANES]
            x_sc[2 * d, t, rws, :] = xr
            x_sc[2 * d + 1, t, rws, :] = xi
            new.append(ar * xr - ai * xi + s_sc[2 * d, t, rws, :])
            new.append(ar * xi + ai * xr + s_sc[2 * d + 1, t, rws, :])
        return tuple(new)

    zero = tuple(jnp.zeros((bb, LANES), F32) for _ in range(2 * npair))
    lax.fori_loop(0, nc, lambda c, st: advance(st, c, 0), zero)
    st = lax.fori_loop(0, ncc, lambda i, st: advance(st, ncc - 1 - i, 1), zero)
    lax.fori_loop(0, ncl, lambda i, st: advance(st, nc - 1 - i, 1), st)

    def latent(ref, *idx):
        return jnp.concatenate([ref[idx + (slice(b * nc + ncc, (b + 1) * nc), slice(None))]
                                for b in range(bb)], axis=0)

    for q in range(gt):
        y = latent(yi_sc, q)
        for d, o_ref in ((0, of_ref), (1, or_ref)):
            for part in range(2):
                y = y + _dot(latent(x_sc, 2 * d + part, q // 2).astype(BF16), o_ref[q, part])
        for kt in range(cw // LANES):
            ys_sc[kt, :, q * LANES:(q + 1) * LANES] = y[:, kt * LANES:(kt + 1) * LANES].astype(BF16)

    r2 = lax.broadcasted_iota(I32, (gt * LANES, LANES), 0)
    c2 = lax.broadcasted_iota(I32, (gt * LANES, LANES), 1)
    keep = ((r2 & (gh - 1)) == (c2 & (gh - 1))) & ((r2 >> lb) == (c2 >> hb))
    r_k = (r2 & (LANES - 1)) >> hb
    for k in range(lc):
        kt, kk = divmod(k, gt)
        z = _dot(ys_sc[kt], _onehot_bf16(keep & (r_k == kk)))
        for b in range(bb):
            y_ref[b, pl.ds(k, ncl, stride=lc), :] = z[b * ncl:(b + 1) * ncl, :]


def _s5_scan(u_c, u_l, w, a_tab, of, orr):
    bsz, ncx, c = u_c.shape
    n = u_l.shape[1]
    lc, gt = S5_CHUNK, S5_TILE_GROUPS
    ncc, ncl = ncx // lc, n // lc
    bb = min(4, bsz)
    rows = bb * (ncc + ncl)
    cw = lc * S5_GROUP
    sw = (gt // 2) * LANES
    return pl.pallas_call(
        functools.partial(_s5_kernel, bb=bb, ncc=ncc, ncl=ncl),
        grid=(c // LANES, bsz // bb),
        in_specs=[pl.BlockSpec((bb, ncx, LANES), lambda t, i: (i, 0, t)),
                  pl.BlockSpec((bb, n, LANES), lambda t, i: (i, 0, t)),
                  pl.BlockSpec((gt,) + w.shape[1:], lambda t, i: (t, 0, 0)),
                  pl.BlockSpec((1,) + a_tab.shape[1:], lambda t, i: (t, 0, 0)),
                  pl.BlockSpec((gt,) + of.shape[1:], lambda t, i: (t, 0, 0, 0)),
                  pl.BlockSpec((gt,) + orr.shape[1:], lambda t, i: (t, 0, 0, 0))],
        out_specs=pl.BlockSpec((bb, n, LANES), lambda t, i: (i, 0, t)),
        out_shape=jax.ShapeDtypeStruct((bsz, n, c), F32),
        scratch_shapes=[pltpu.VMEM((lc // 2, rows, 2 * LANES), BF16), pltpu.VMEM((gt, rows, cw), F32),
                        pltpu.VMEM((4, gt // 2, rows, LANES), F32), pltpu.VMEM((4, gt // 2, rows, LANES), F32),
                        pltpu.VMEM((cw // LANES, bb * ncl, gt * LANES), BF16)],
        compiler_params=_cparams(("parallel", "parallel"), 56),
        name="s5_scan",
    )(u_c, u_l, w, a_tab, of, orr)


def _s5_matrices(a_re, a_im, log_dt, b_re, b_im, c_re, c_im, d_skip):
    lc, h, p, g = S5_CHUNK, S5_GROUP, S5_STATE, S5_GROUPS
    dt = jnp.exp(log_dt)[..., None]
    lam_re, lam_im = a_re * dt, a_im * dt
    mag = jnp.exp(lam_re)
    er, ei = mag * jnp.cos(lam_im) - 1.0, mag * jnp.sin(lam_im)
    den = a_re * a_re + a_im * a_im
    co_re = (er * a_re + ei * a_im) / den
    co_im = (ei * a_re - er * a_im) / den
    bb_re = co_re[..., None] * b_re - co_im[..., None] * b_im
    bb_im = co_re[..., None] * b_im + co_im[..., None] * b_re
    m = jnp.arange(lc + 1, dtype=F32)[:, None, None, None]
    pw_mag = jnp.exp(m * lam_re[None])
    pw_re, pw_im = pw_mag * jnp.cos(m * lam_im[None]), pw_mag * jnp.sin(m * lam_im[None])
    ab_re = pw_re[..., None] * bb_re[None] - pw_im[..., None] * bb_im[None]
    ab_im = pw_re[..., None] * bb_im[None] + pw_im[..., None] * bb_re[None]
    kk = (jnp.einsum("dgop,mdgpi->mdgoi", c_re, ab_re[:lc], precision=HIGHEST)
          - jnp.einsum("dgop,mdgpi->mdgoi", c_im, ab_im[:lc], precision=HIGHEST))
    j = jnp.arange(lc)[:, None]
    k = jnp.arange(lc)[None, :]
    lag = (k - j)[:, :, None, None, None]
    tf = sum(jnp.where(lag == m, kk[m, 0][None, None], 0.0) for m in range(lc))
    tr = sum(jnp.where(lag == -m, kk[m, 1][None, None], 0.0) for m in range(lc))
    tw = jnp.transpose(tf + tr, (2, 0, 4, 1, 3))
    dsk = d_skip.reshape(g, h)
    eye = (jnp.eye(lc, dtype=F32)[None, :, None, :, None] * jnp.eye(h, dtype=F32)[None, None, :, None, :]
           * dsk[:, None, :, None, None])
    tw = (tw + eye).reshape(g, lc * h, lc * h)
    idx_f = slice(lc - 1, None, -1)
    idx_r = slice(0, lc)

    odd = (jnp.arange(g) % 2 == 1)

    def place(t, axis):
        z = jnp.zeros_like(t)
        shape = [g] + [1] * (t.ndim - 1)
        return jnp.where(odd.reshape(shape), jnp.concatenate([z, t], axis=axis),
                         jnp.concatenate([t, z], axis=axis))

    def to_state(idx, d):
        re = jnp.transpose(ab_re[idx, d], (1, 0, 3, 2)).reshape(g, lc * h, p)
        im = jnp.transpose(ab_im[idx, d], (1, 0, 3, 2)).reshape(g, lc * h, p)
        return [place(re, 2), place(im, 2)]

    w = jnp.concatenate([tw] + to_state(idx_f, 0) + to_state(idx_r, 1), axis=-1)

    def from_state(idx, d):
        pr, pi = pw_re[idx, d], pw_im[idx, d]
        cr, cim = c_re[d], c_im[d]
        mr = cr[None] * pr[:, :, None, :] - cim[None] * pi[:, :, None, :]
        mi = cr[None] * pi[:, :, None, :] + cim[None] * pr[:, :, None, :]
        rows = lambda m: place(jnp.transpose(m, (1, 3, 0, 2)).reshape(g, p, lc * h), 1)
        return jnp.stack([rows(mr), rows(-mi)], axis=1)

    of = from_state(slice(1, lc + 1), 0)
    orr = from_state(slice(lc, 0, -1), 1)
    tiles = g // S5_TILE_GROUPS
    a_tab = jnp.stack([pw_re[lc, 0].reshape(tiles, -1), pw_im[lc, 0].reshape(tiles, -1),
                       pw_re[lc, 1].reshape(tiles, -1), pw_im[lc, 1].reshape(tiles, -1)], axis=1)
    return w.astype(BF16), a_tab.astype(F32), of.astype(BF16), orr.astype(BF16)


def _glu_kernel(y_ref, w_ref, b_ref, o_ref):
    y = y_ref[0]
    inner = 0.7978845608028654 * (y + 0.044715 * (y * y * y))
    gl = 0.5 * y * (1.0 + jnp.tanh(inner))
    o_ref[0] = (gl * _sigmoid(_dot(gl.astype(BF16), w_ref[...]) + b_ref[...])).astype(BF16)


def _glu(y, w, b):
    bsz, n, c = y.shape
    tm = min(512, n)
    return pl.pallas_call(
        _glu_kernel,
        grid=(bsz, n // tm),
        in_specs=[pl.BlockSpec((1, tm, c), lambda i, t: (i, t, 0)),
                  pl.BlockSpec((c, c), lambda i, t: (0, 0)), pl.BlockSpec((1, c), lambda i, t: (0, 0))],
        out_specs=pl.BlockSpec((1, tm, c), lambda i, t: (i, t, 0)),
        out_shape=jax.ShapeDtypeStruct((bsz, n, c), BF16),
        compiler_params=_cparams(("parallel", "parallel"), 32),
        name="s5_glu",
    )(y, w, b.reshape(1, c))


def _s5_mixer(u_c, u_l, mats, glu_w, glu_b):
    return _glu(_s5_scan(u_c, u_l, *mats), glu_w, glu_b)


def _route(lg):
    lane = lax.broadcasted_iota(I32, lg.shape, 1).astype(F32)
    neg = jnp.float32(-1e30)
    far = jnp.float32(LANES)
    is_g = lane < N_GROUPS
    lgm = jnp.where(is_g, lg, neg)
    m = lgm.max(axis=-1, keepdims=True)
    gidx = jnp.where(lgm == m, lane, far).min(axis=-1, keepdims=True)
    gp = 1.0 / jnp.where(is_g, jnp.exp(lgm - m), 0.0).sum(axis=-1, keepdims=True)
    lo = N_GROUPS + EXPERTS_PER_GROUP * gidx
    is_e = (lane >= lo) & (lane < lo + EXPERTS_PER_GROUP)
    lem = jnp.where(is_e, lg, neg)
    l0 = lem.max(axis=-1, keepdims=True)
    i0 = jnp.where((lem == l0) & is_e, lane, far).min(axis=-1, keepdims=True)
    lem1 = jnp.where(lane == i0, neg, lem)
    l1 = lem1.max(axis=-1, keepdims=True)
    i1 = jnp.where((lem1 == l1) & is_e & (lane != i0), lane, far).min(axis=-1, keepdims=True)
    e1 = jnp.exp(l1 - l0)
    w0 = 1.0 / (1.0 + e1)
    w1 = e1 * w0
    out = jnp.where(lane == 0, i0 - N_GROUPS, 0.0)
    out = jnp.where(lane == 1, i1 - N_GROUPS, out)
    out = jnp.where(lane == 2, gp * w0, out)
    return jnp.where(lane == 3, gp * w1, out)


def _proj_res_kernel(a_ref, b_ref, wa_ref, wb_ref, x_ref, g1_ref, ng_ref, sh_ref, sc_ref, rh_ref, rl_ref,
                     xo_ref, h_ref, route_ref):
    for rows in _row_chunks(x_ref.shape[1]):
        o = _dot(a_ref[0, rows, :], wa_ref[...]) + _dot(b_ref[0, rows, :], wb_ref[...])
        x = x_ref[0, rows, :] + g1_ref[0] * o
        xo_ref[0, rows, :] = x
        h = _modnorm(x, ng_ref[...], sh_ref[0], sc_ref[0])
        _store_token_tiles(h_ref, rows.start, h)
        hh, hl = _split_bf16(h)
        route_ref[0, rows, :] = _route(_dot3(hh, hl, rh_ref[...], rl_ref[...]))


def _proj_res(a, b, wa, wb, x, g1, ng, shift, scale, rh, rl):
    bsz, n, d = x.shape
    tm = min(512, n)
    row = lambda i, t: (i, t, 0)
    per_b = lambda i, t: (i, 0, 0)
    fixed = lambda i, t: (0, 0)
    return pl.pallas_call(
        _proj_res_kernel,
        grid=(bsz, n // tm),
        in_specs=[pl.BlockSpec((1, tm, a.shape[2]), row), pl.BlockSpec((1, tm, b.shape[2]), row),
                  pl.BlockSpec(wa.shape, fixed), pl.BlockSpec(wb.shape, fixed),
                  pl.BlockSpec((1, tm, d), row), pl.BlockSpec((1, 1, d), per_b), pl.BlockSpec((1, d), fixed),
                  pl.BlockSpec((1, 1, d), per_b), pl.BlockSpec((1, 1, d), per_b),
                  pl.BlockSpec((d, LANES), fixed), pl.BlockSpec((d, LANES), fixed)],
        out_specs=[pl.BlockSpec((1, tm, d), row),
                   pl.BlockSpec((tm * SUBLANES, LANES), lambda i, t: (i * (n // tm) + t, 0)),
                   pl.BlockSpec((1, tm, LANES), row)],
        out_shape=[jax.ShapeDtypeStruct((bsz, n, d), F32),
                   jax.ShapeDtypeStruct((bsz * n * SUBLANES, LANES), F32),
                   jax.ShapeDtypeStruct((bsz, n, LANES), F32)],
        compiler_params=_cparams(("parallel", "parallel"), 40),
        name="proj_res",
    )(a, b, wa, wb, x, g1, ng.reshape(1, d), shift, scale, rh, rl)


def _token_copy(src, src_tok, dst, dst_tok, sem):
    rows = lambda t: pl.ds(pl.multiple_of(t * SUBLANES, SUBLANES), SUBLANES)
    return pltpu.make_async_copy(src.at[rows(src_tok)], dst.at[rows(dst_tok)], sem)


def _tile_copy(src, src_tok, dst, dst_tok, sem):
    n = MOE_TILE * SUBLANES
    rows = lambda t: pl.ds(pl.multiple_of(t * SUBLANES, SUBLANES), n)
    return pltpu.make_async_copy(src.at[rows(src_tok)], dst.at[rows(dst_tok)], sem)


def _experts_kernel(te_ref, ts_ref, tr_ref, tp_ref, win_ref, winn_ref, h_ref, wg_ref, wu_ref, wd_ref, ys_ref,
                    xbuf, obuf, gsem, ssem, *, n_assign):
    del te_ref
    i = pl.program_id(0)
    nt = pl.num_programs(0)
    slot = i % 2
    other = 1 - slot
    nxt = jnp.minimum(i + 1, nt - 1)
    within = lambda start: start & (MOE_TILE - 1)

    def start_gather(w_ref, off, s):
        def body(j, c):
            for k in range(2):
                r = 2 * j + k
                tok = lax.shift_right_logical(w_ref[0, 0, off + r], 1)
                _token_copy(h_ref, tok, xbuf, s * MOE_TILE + r, gsem.at[s]).start(priority=k)
            return c
        lax.fori_loop(0, MOE_TILE // 2, body, 0, unroll=4)

    @pl.when(i == 0)
    def _():
        start_gather(win_ref, within(ts_ref[0]), 0)

    @pl.when(i + 1 < nt)
    def _():
        start_gather(winn_ref, within(ts_ref[nxt]), other)

    base = slot * MOE_TILE
    _tile_copy(h_ref, 0, xbuf, base, gsem.at[slot]).wait()

    @pl.when(i >= 2)
    def _():
        _tile_copy(obuf, base, ys_ref, 0, ssem.at[slot]).wait()

    x = _load_token_tiles(xbuf, base, MOE_TILE).astype(BF16)
    gt = _dot(x, wg_ref[0, 0].astype(BF16))
    hid = (gt * _sigmoid(gt)) * _dot(x, wu_ref[0, 0].astype(BF16))
    _store_token_tiles(obuf, base, _dot(hid.astype(BF16), wd_ref[0, 0].astype(BF16)))

    off, real = within(ts_ref[i]), tr_ref[i]
    spare = n_assign + tp_ref[i] - real

    def scatter(j, c):
        for k in range(2):
            r = 2 * j + k
            dst = jnp.where(r < real, win_ref[0, 0, off + r], spare + r)
            _token_copy(obuf, base + r, ys_ref, dst, ssem.at[slot]).start(priority=k)
        return c
    lax.fori_loop(0, MOE_TILE // 2, scatter, 0, unroll=4)

    @pl.when(i == nt - 1)
    def _():
        _tile_copy(obuf, base, ys_ref, 0, ssem.at[slot]).wait()
        _tile_copy(obuf, other * MOE_TILE, ys_ref, 0, ssem.at[other]).wait()


def _experts(plan, h, n_assign, layer, wg, wu, wd):
    windows, tile_expert, tile_start, tile_real, tile_padbase = plan
    nt = tile_expert.shape[0]
    d = D_MODEL
    wmap = lambda i, te, ts, tr, tp: (layer, te[i], 0, 0)
    log_tile = MOE_TILE.bit_length() - 1
    smem = lambda f: pl.BlockSpec((1, 1, 2 * MOE_TILE), f, memory_space=pltpu.SMEM)
    tile_rows = MOE_TILE * SUBLANES
    return pl.pallas_call(
        functools.partial(_experts_kernel, n_assign=n_assign),
        grid_spec=pltpu.PrefetchScalarGridSpec(
            num_scalar_prefetch=4, grid=(nt,),
            in_specs=[smem(lambda i, te, ts, tr, tp: (lax.shift_right_logical(ts[i], log_tile), 0, 0)),
                      smem(lambda i, te, ts, tr, tp: (
                          lax.shift_right_logical(ts[jnp.minimum(i + 1, nt - 1)], log_tile), 0, 0)),
                      pl.BlockSpec(memory_space=pl.ANY),
                      pl.BlockSpec((1, 1, d, D_EXPERT), wmap), pl.BlockSpec((1, 1, d, D_EXPERT), wmap),
                      pl.BlockSpec((1, 1, D_EXPERT, d), wmap)],
            out_specs=pl.BlockSpec(memory_space=pl.ANY),
            scratch_shapes=[pltpu.VMEM((2 * tile_rows, LANES), F32), pltpu.VMEM((2 * tile_rows, LANES), F32),
                            pltpu.SemaphoreType.DMA((2,)), pltpu.SemaphoreType.DMA((2,))]),
        out_shape=jax.ShapeDtypeStruct((nt * tile_rows, LANES), F32),
        compiler_params=pltpu.CompilerParams(dimension_semantics=("arbitrary",), vmem_limit_bytes=40 * MIB,
                                             has_side_effects=True),
        name="moe_experts",
    )(tile_expert, tile_start, tile_real, tile_padbase, windows[:, None, :], windows[:, None, :], h, wg, wu, wd)


def _combine_kernel(route_ref, x_ref, g2_ref, fg_ref, ys_ref, o_ref, *, tc, final):
    rt = route_ref[0]
    g0, g1 = rt[:, 2:3], rt[:, 3:4]
    cols = [g0 * ys_ref[pl.ds(s, tc, stride=2 * SUBLANES), :]
            + g1 * ys_ref[pl.ds(SUBLANES + s, tc, stride=2 * SUBLANES), :] for s in range(SUBLANES)]
    x = x_ref[0] + g2_ref[0] * jnp.concatenate(cols, axis=-1)
    if final:
        x = _rms(x, fg_ref[...], x.shape[-1])
    o_ref[0] = x


def _combine(route, x, g2, fg, ys, tok_off, final):
    bsz, n, d = x.shape
    tc = min(256, n)
    nt = n // tc
    row = lambda i, t: (i, t, 0)
    blk = tok_off // tc
    return pl.pallas_call(
        functools.partial(_combine_kernel, tc=tc, final=final),
        grid=(bsz, nt),
        in_specs=[pl.BlockSpec((1, tc, LANES), row), pl.BlockSpec((1, tc, d), row),
                  pl.BlockSpec((1, 1, d), lambda i, t: (i, 0, 0)), pl.BlockSpec((1, d), lambda i, t: (0, 0)),
                  pl.BlockSpec((2 * tc * SUBLANES, LANES), lambda i, t: (blk + i * nt + t, 0))],
        out_specs=pl.BlockSpec((1, tc, d), row),
        out_shape=jax.ShapeDtypeStruct((bsz, n, d), F32),
        compiler_params=_cparams(("parallel", "parallel"), 32),
        name="moe_combine",
    )(route, x, g2, fg.reshape(1, d), ys)


def _moe_plan(eids):
    a = eids.shape[0]
    order = jnp.argsort(eids, stable=True).astype(I32)
    experts = jnp.arange(N_EXPERTS, dtype=I32)
    counts = (eids[:, None] == experts[None, :]).astype(I32).sum(axis=0)
    tiles_e = (counts + MOE_TILE - 1) // MOE_TILE
    tile_end = jnp.cumsum(tiles_e)
    first = jnp.cumsum(counts) - counts
    nt = a // MOE_TILE + N_EXPERTS
    tidx = jnp.arange(nt, dtype=I32)
    tile_expert = jnp.minimum((tidx[:, None] >= tile_end[None, :]).astype(I32).sum(axis=1), N_EXPERTS - 1)
    pick = lambda table: ((tile_expert[:, None] == experts[None, :]).astype(I32) * table[None, :]).sum(axis=1)
    tile_pos = (tidx - pick(tile_end - tiles_e)) * MOE_TILE
    used = tidx < tile_end[-1]
    tile_real = jnp.where(used, jnp.clip(pick(counts) - tile_pos, 0, MOE_TILE), 0)
    tile_start = jnp.where(used, pick(first) + tile_pos, 0)
    pads = MOE_TILE - tile_real
    tile_padbase = jnp.cumsum(pads) - pads
    rows = jnp.concatenate([order, jnp.zeros((MOE_TILE,), I32)]).reshape(-1, MOE_TILE)
    windows = jnp.concatenate([rows[:-1], rows[1:]], axis=1)
    return windows, tile_expert, tile_start, tile_real, tile_padbase


def _moe(streams, layer, wg, wu, wd, final_g, final):
    eids = jnp.concatenate([s[2][..., :2].reshape(-1) for s in streams]).astype(I32)
    h = streams[0][1] if len(streams) == 1 else jnp.concatenate([s[1] for s in streams], axis=0)
    ys = _experts(_moe_plan(eids), h, eids.shape[0], layer, wg, wu, wd)
    outs, off = [], 0
    for x, _, route, g2 in streams:
        outs.append(_combine(route, x, g2, final_g, ys, off, final))
        off += x.shape[0] * x.shape[1]
    return outs


def _rope_tables(n, half, offset, identity):
    cs = jnp.ones((n, LANES), F32)
    z = jnp.zeros((n, LANES), F32)
    if identity:
        return cs, z, z
    rows = (jnp.arange(n, dtype=I32) // GRID_W).astype(F32)
    cols = (jnp.arange(n, dtype=I32) % GRID_W).astype(F32)
    freqs = ROPE_THETA ** (-jnp.arange(half, dtype=F32) / half)
    ar, ac = rows[:, None] * freqs, cols[:, None] * freqs
    zero = jnp.zeros((n, half), F32)
    cos4 = jnp.concatenate([jnp.cos(ar), jnp.cos(ar), jnp.cos(ac), jnp.cos(ac)], axis=1)
    s1 = jnp.concatenate([-jnp.sin(ar), zero, -jnp.sin(ac), zero], axis=1)
    s2 = jnp.concatenate([zero, jnp.sin(ar), zero, jnp.sin(ac)], axis=1)
    place = lambda base, t: lax.dynamic_update_slice(base, t, (0, offset))
    return place(cs, cos4), place(z, s1), place(z, s2)


def _pad_heads(w, nheads, width, left=0):
    k = w.shape[0]
    w = w.reshape(k, nheads, width)
    w = jnp.pad(w, ((0, 0), (0, 0), (left, LANES - width - left)))
    return w.reshape(k, nheads * LANES)


def _lane_pad(v, left=0):
    return jnp.pad(v, (left, LANES - v.shape[0] - left)).reshape(1, LANES)


def kernel(x, c, ctx, c_ctx, w_mod, b_mod, norm_g, e_w_in, e_hy_conv_w, e_hy_conv_b, e_hy_w1, e_hy_b1, e_hy_w2, e_hy_b2, e_hy_w3, e_hy_freq, e_hy_fbias, e_qk_g, e_w_out, o_w_in, o_q_norm_g, o_w_uq, o_kv_norm_g, o_w_ukv, o_s5_a_re, o_s5_a_im, o_s5_log_dt, o_s5_b_re, o_s5_b_im, o_s5_c_re, o_s5_c_im, o_s5_d, o_glu_w, o_glu_b, o_w_out, moe_w_rg, moe_w_re, moe_w_gate, moe_w_up, moe_w_down, final_g):
    bsz, n, d = x.shape
    depth = w_mod.shape[0]
    xc = ctx
    ncx = ctx.shape[1]
    cc = jnp.zeros((bsz + 8, d), F32).at[:bsz].set(c).at[bsz].set(c_ctx)

    for i in range(depth):
        j = i // 2
        last = i == depth - 1
        mods = _modulation(cc, w_mod, b_mod, i)
        lat = [m.reshape(bsz, 1, d) for m in jnp.split(mods[:bsz], 6, axis=-1)]
        cxm = [jnp.broadcast_to(m.reshape(1, 1, d), (bsz, 1, d)) for m in jnp.split(mods[bsz], 6, axis=-1)]

        if i % 2 == 0:
            wq = _pad_heads(e_w_in[j][:, 3 * A_WIDTH:3 * A_WIDTH + B_Q_HEADS * HEAD_DIM], B_Q_HEADS, HEAD_DIM)
            k0 = 3 * A_WIDTH + B_Q_HEADS * HEAD_DIM
            wk = _pad_heads(e_w_in[j][:, k0:k0 + B_KV_HEADS * HEAD_DIM], B_KV_HEADS, HEAD_DIM)
            wv = _pad_heads(e_w_in[j][:, k0 + B_KV_HEADS * HEAD_DIM:], B_KV_HEADS, HEAD_DIM)
            w_in = jnp.concatenate([e_w_in[j][:, :3 * A_WIDTH], wq, wk, wv], axis=1).astype(BF16)
            qg, kg = _lane_pad(e_qk_g[j, 0]), _lane_pad(e_qk_g[j, 1])
            filt = (e_hy_w1[j], e_hy_b1[j], e_hy_w2[j], e_hy_b2[j], e_hy_w3[j], e_hy_freq[j])
            qscale = HEAD_DIM ** -0.5 * LOG2E
            hy_l, q_l, k_l, v_l = _even_in(x, norm_g[i, 0], lat[0], lat[1], w_in, qg, kg,
                                           _rope_tables(n, HEAD_DIM // 4, 0, False), qscale)
            hy_c, q_c, k_c, v_c = _even_in(xc, norm_g[i, 0], cxm[0], cxm[1], w_in, qg, kg,
                                           _rope_tables(ncx, HEAD_DIM // 4, 0, True), qscale)
            rep = B_Q_HEADS // B_KV_HEADS
            att_l = _attention(q_l, [k_c, k_l], [v_c, v_l], rep, 1)
            mix_l = (_hyena(hy_l, e_hy_conv_w[j], e_hy_conv_b[j], filt, e_hy_fbias[j]), att_l)
            if not last:
                att_c = _attention(q_c, [k_c], [v_c], rep, 1)
                mix_c = (_hyena(hy_c, e_hy_conv_w[j], e_hy_conv_b[j], filt, e_hy_fbias[j]), att_c)
            w_out = e_w_out[j]
            split = A_WIDTH
        else:
            w1 = o_w_in[j]
            c0, c1, c2 = MLA_Q_RANK, MLA_Q_RANK + MLA_KV_RANK, MLA_Q_RANK + MLA_KV_RANK + MLA_ROPE
            w_kr = jnp.pad(w1[:, c1:c2], ((0, 0), (MLA_NOPE, LANES - MLA_NOPE - MLA_ROPE)))
            w_in = jnp.concatenate([w1[:, :c1], w_kr, w1[:, c2:]], axis=1).astype(BF16)
            wuq = _pad_heads(o_w_uq[j], MLA_HEADS, MLA_NOPE + MLA_ROPE).astype(BF16)
            wkv = o_w_ukv[j].reshape(MLA_KV_RANK, MLA_HEADS, MLA_NOPE + MLA_V)
            wuk = _pad_heads(wkv[:, :, :MLA_NOPE].reshape(MLA_KV_RANK, -1), MLA_HEADS, MLA_NOPE).astype(BF16)
            wuv = _pad_heads(wkv[:, :, MLA_NOPE:].reshape(MLA_KV_RANK, -1), MLA_HEADS, MLA_V).astype(BF16)
            qg = o_q_norm_g[j].reshape(1, MLA_Q_RANK)
            kvg = o_kv_norm_g[j].reshape(1, MLA_KV_RANK)
            args = (w_in, qg, kvg, wuq, wuk, wuv)
            qscale = (MLA_NOPE + MLA_ROPE) ** -0.5 * LOG2E
            u_l, k_l, v_l, q_l = _odd_in(x, norm_g[i, 0], lat[0], lat[1], *args,
                                         _rope_tables(n, MLA_ROPE // 4, MLA_NOPE, False), True, qscale)
            outs_c = _odd_in(xc, norm_g[i, 0], cxm[0], cxm[1], *args,
                             _rope_tables(ncx, MLA_ROPE // 4, MLA_NOPE, True), not last, qscale)
            u_c, k_c, v_c = outs_c[:3]
            att_l = _attention(q_l, [k_c, k_l], [v_c, v_l], 2, 2)
            mats = _s5_matrices(o_s5_a_re[j], o_s5_a_im[j], o_s5_log_dt[j], o_s5_b_re[j], o_s5_b_im[j],
                                o_s5_c_re[j], o_s5_c_im[j], o_s5_d[j])
            glu_w = o_glu_w[j].astype(BF16)
            mix_l = (att_l, _s5_mixer(u_c, u_l, mats, glu_w, o_glu_b[j]))
            if not last:
                raise NotImplementedError("an odd layer that is not the last needs the context S5 readout")
            w_out = o_w_out[j]
            split = MLA_HEADS * MLA_V

        wa, wb = w_out[:split].astype(BF16), w_out[split:].astype(BF16)
        router = jnp.pad(jnp.concatenate([moe_w_rg[i], moe_w_re[i]], axis=1),
                         ((0, 0), (0, LANES - N_GROUPS - N_EXPERTS)))
        rh = router.astype(BF16)
        rl = (router - rh.astype(F32)).astype(BF16)

        x, h_l, route_l = _proj_res(*mix_l, wa, wb, x, lat[2], norm_g[i, 1], lat[3], lat[4], rh, rl)
        streams = [(x, h_l, route_l, lat[5])]
        if not last:
            xc, h_c, route_c = _proj_res(*mix_c, wa, wb, xc, cxm[2], norm_g[i, 1], cxm[3], cxm[4], rh, rl)
            streams.append((xc, h_c, route_c, cxm[5]))
        outs = _moe(streams, i, moe_w_gate, moe_w_up, moe_w_down, final_g, last)
        x = outs[0]
        if not last:
            xc = outs[1]
    return x
```

```python
import functools
import math

import jax
import jax.numpy as jnp
from jax import lax
from jax.experimental import pallas as pl
from jax.experimental.pallas import tpu as pltpu

F32 = jnp.float32
BF16 = jnp.bfloat16
I32 = jnp.int32

D_MODEL = 1024
GRID_W = 64
HEAD_DIM = 64
A_WIDTH = D_MODEL // 2
B_Q_HEADS = (D_MODEL - A_WIDTH) // HEAD_DIM
B_KV_HEADS = max(1, B_Q_HEADS // 4)
HY_ORDER = 2
HY_BANDS = 16
HY_TARGET = 1e-2
HY_MAX_DECAY = math.log(1.0 / HY_TARGET) / 0.3
HY_MIN_DECAY = math.log(1.0 / HY_TARGET) / 1.5
MLA_HEADS = 8
MLA_NOPE = 64
MLA_ROPE = 32
MLA_V = 64
MLA_Q_RANK = D_MODEL // 4
MLA_KV_RANK = D_MODEL // 8
S5_WIDTH = D_MODEL - MLA_HEADS * MLA_V
S5_GROUP = 16
S5_GROUPS = S5_WIDTH // S5_GROUP
S5_STATE = 64
S5_CHUNK = 16
S5_TILE_GROUPS = 128 // S5_GROUP
N_GROUPS = 4
EXPERTS_PER_GROUP = 8
N_EXPERTS = N_GROUPS * EXPERTS_PER_GROUP
D_EXPERT = D_MODEL // 4
ROPE_THETA = 10000.0
EPS = 1e-6
LOG2E = 1.4426950408889634

LANES = 128
SUBLANES = 8
MOE_TILE = 256
ROW_CHAIN = 256
HIGHEST = lax.Precision.HIGHEST
MIB = 1024 * 1024


def _cparams(sem, vmem_mib):
    return pltpu.CompilerParams(dimension_semantics=sem, vmem_limit_bytes=vmem_mib * MIB)


def _dot(a, b):
    return jnp.dot(a, b, preferred_element_type=F32)


def _dot_nt(a, b):
    return lax.dot_general(a, b, (((1,), (1,)), ((), ())), preferred_element_type=F32)


def _split_bf16(a):
    hi = a.astype(BF16)
    lo = (a - hi.astype(F32)).astype(BF16)
    return hi, lo


def _dot3(ah, al, bh, bl):
    return _dot(ah, bh) + (_dot(ah, bl) + _dot(al, bh))


def _sigmoid(x):
    return 1.0 / (1.0 + jnp.exp(-x))


def _rms(x, g, width):
    ms = jnp.sum(x * x, axis=-1, keepdims=True) * (1.0 / width)
    return x * lax.rsqrt(ms + EPS) * g


def _modnorm(x, g, shift, scale):
    return _rms(x, g, x.shape[-1]) * (1.0 + scale) + shift


def _rope(t, cs, s1, s2, half):
    return t * cs + pltpu.roll(t, LANES - half, 1) * s1 + pltpu.roll(t, half, 1) * s2


def _store_token_tiles(ref, base_tok, val):
    n = val.shape[0]
    for s in range(SUBLANES):
        ref[pl.ds(base_tok * SUBLANES + s, n, stride=SUBLANES), :] = val[:, LANES * s: LANES * (s + 1)]


def _load_token_tiles(ref, base_tok, n):
    return jnp.concatenate([ref[pl.ds(base_tok * SUBLANES + s, n, stride=SUBLANES), :]
                            for s in range(SUBLANES)], axis=-1)


def _short_conv(u, w, b):
    n = u.shape[0]
    row = lax.broadcasted_iota(I32, u.shape, 0)
    prev = jnp.where(row == 0, 0.0, pltpu.roll(u, 1, 0))
    nxt = jnp.where(row == n - 1, 0.0, pltpu.roll(u, n - 1, 0))
    return prev * w[0:1] + u * w[1:2] + nxt * w[2:3] + b


def _mod_kernel(c_ref, w_ref, b_ref, o_ref):
    c = c_ref[...]
    s = c * _sigmoid(c)
    sh, sl = _split_bf16(s)
    wh, wl = _split_bf16(w_ref[0])
    o_ref[...] = _dot3(sh, sl, wh, wl) + b_ref[0]


def _modulation(cc, w, b, layer):
    r, d = cc.shape
    n = w.shape[2]
    tn = 1536
    return pl.pallas_call(
        _mod_kernel,
        grid=(n // tn,),
        in_specs=[pl.BlockSpec((r, d), lambda j: (0, 0)),
                  pl.BlockSpec((1, d, tn), lambda j: (layer, 0, j)),
                  pl.BlockSpec((1, 1, tn), lambda j: (layer, 0, j))],
        out_specs=pl.BlockSpec((r, tn), lambda j: (0, j)),
        out_shape=jax.ShapeDtypeStruct((r, n), F32),
        compiler_params=_cparams(("parallel",), 48),
        name="modulation",
    )(cc, w, b.reshape(b.shape[0], 1, n))


def _with_ones(v):
    lane = lax.broadcasted_iota(I32, v.shape, 1) % LANES
    return jnp.where(lane >= HEAD_DIM, 1.0, v)


def _row_chunks(tm):
    sub = min(ROW_CHAIN, tm)
    return [slice(r, r + sub) for r in range(0, tm, sub)]


def _even_in_kernel(x_ref, g_ref, sh_ref, sc_ref, w_ref, qg_ref, kg_ref, cs_ref, s1_ref, s2_ref,
                    hy_ref, q_ref, k_ref, v_ref, *, qscale):
    hw = 3 * A_WIDTH
    kw = hw + LANES * B_Q_HEADS
    vw = kw + LANES * B_KV_HEADS
    for rows in _row_chunks(x_ref.shape[1]):
        xn = _modnorm(x_ref[0, rows, :], g_ref[...], sh_ref[0], sc_ref[0]).astype(BF16)
        p = _dot(xn, w_ref[...])
        hy_ref[0, rows, :] = p[:, :hw].astype(BF16)
        cs, s1, s2 = cs_ref[rows, :], s1_ref[rows, :], s2_ref[rows, :]
        for h in range(B_Q_HEADS):
            t = _rms(p[:, hw + LANES * h: hw + LANES * (h + 1)], qg_ref[...], HEAD_DIM)
            q_ref[0, rows, LANES * h: LANES * (h + 1)] = (
                _rope(t, cs, s1, s2, HEAD_DIM // 4) * qscale).astype(BF16)
        for h in range(B_KV_HEADS):
            t = _rms(p[:, kw + LANES * h: kw + LANES * (h + 1)], kg_ref[...], HEAD_DIM)
            k_ref[0, rows, LANES * h: LANES * (h + 1)] = _rope(t, cs, s1, s2, HEAD_DIM // 4).astype(BF16)
        v_ref[0, rows, :] = _with_ones(p[:, vw:]).astype(BF16)


def _even_in(x, g, shift, scale, w, qg, kg, tabs, qscale):
    bsz, n, d = x.shape
    tm = min(512, n)
    nw = w.shape[1]
    row = lambda b, t: (b, t, 0)
    per_b = lambda b, t: (b, 0, 0)
    fixed = lambda b, t: (0, 0)
    tab = pl.BlockSpec((tm, LANES), lambda b, t: (t, 0))
    return pl.pallas_call(
        functools.partial(_even_in_kernel, qscale=qscale),
        grid=(bsz, n // tm),
        in_specs=[pl.BlockSpec((1, tm, d), row), pl.BlockSpec((1, d), fixed),
                  pl.BlockSpec((1, 1, d), per_b), pl.BlockSpec((1, 1, d), per_b),
                  pl.BlockSpec((d, nw), fixed), pl.BlockSpec((1, LANES), fixed),
                  pl.BlockSpec((1, LANES), fixed), tab, tab, tab],
        out_specs=[pl.BlockSpec((1, tm, 3 * A_WIDTH), row),
                   pl.BlockSpec((1, tm, LANES * B_Q_HEADS), row),
                   pl.BlockSpec((1, tm, LANES * B_KV_HEADS), row),
                   pl.BlockSpec((1, tm, LANES * B_KV_HEADS), row)],
        out_shape=[jax.ShapeDtypeStruct((bsz, n, 3 * A_WIDTH), BF16),
                   jax.ShapeDtypeStruct((bsz, n, LANES * B_Q_HEADS), BF16),
                   jax.ShapeDtypeStruct((bsz, n, LANES * B_KV_HEADS), BF16),
                   jax.ShapeDtypeStruct((bsz, n, LANES * B_KV_HEADS), BF16)],
        compiler_params=_cparams(("parallel", "parallel"), 48),
        name="even_in",
    )(x, g.reshape(1, d), shift, scale, w, qg, kg, *tabs)


def _odd_in_kernel(x_ref, g_ref, sh_ref, sc_ref, w_ref, qg_ref, kvg_ref, wuq_ref, wuk_ref, wuv_ref,
                   cs_ref, s1_ref, s2_ref, *out_refs, need_q, qscale):
    if need_q:
        u_ref, k_ref, v_ref, q_ref = out_refs
    else:
        u_ref, k_ref, v_ref = out_refs
    c0, c1, c2 = MLA_Q_RANK, MLA_Q_RANK + MLA_KV_RANK, MLA_Q_RANK + MLA_KV_RANK + LANES
    for rows in _row_chunks(x_ref.shape[1]):
        xn = _modnorm(x_ref[0, rows, :], g_ref[...], sh_ref[0], sc_ref[0]).astype(BF16)
        p = _dot(xn, w_ref[...])
        u_ref[0, rows, :] = p[:, c2:]
        cs, s1, s2 = cs_ref[rows, :], s1_ref[rows, :], s2_ref[rows, :]
        ckv = _rms(p[:, c0:c1], kvg_ref[...], MLA_KV_RANK).astype(BF16)
        kn = _dot(ckv, wuk_ref[...])
        v_ref[0, rows, :] = _with_ones(_dot(ckv, wuv_ref[...])).astype(BF16)
        kr = _rope(p[:, c1:c2], cs, s1, s2, MLA_ROPE // 4)
        for h in range(MLA_HEADS):
            k_ref[0, rows, LANES * h: LANES * (h + 1)] = (kn[:, LANES * h: LANES * (h + 1)] + kr).astype(BF16)
        if need_q:
            cq = _rms(p[:, :c0], qg_ref[...], MLA_Q_RANK).astype(BF16)
            q = _dot(cq, wuq_ref[...])
            for h in range(MLA_HEADS):
                t = q[:, LANES * h: LANES * (h + 1)]
                q_ref[0, rows, LANES * h: LANES * (h + 1)] = (
                    _rope(t, cs, s1, s2, MLA_ROPE // 4) * qscale).astype(BF16)


def _odd_in(x, g, shift, scale, w, qg, kvg, wuq, wuk, wuv, tabs, need_q, qscale):
    bsz, n, d = x.shape
    tm = min(512, n)
    hw = LANES * MLA_HEADS
    row = lambda b, t: (b, t, 0)
    per_b = lambda b, t: (b, 0, 0)
    fixed = lambda b, t: (0, 0)
    tab = pl.BlockSpec((tm, LANES), lambda b, t: (t, 0))
    widths = [S5_WIDTH, hw, hw] + ([hw] if need_q else [])
    return pl.pallas_call(
        functools.partial(_odd_in_kernel, need_q=need_q, qscale=qscale),
        grid=(bsz, n // tm),
        in_specs=[pl.BlockSpec((1, tm, d), row), pl.BlockSpec((1, d), fixed),
                  pl.BlockSpec((1, 1, d), per_b), pl.BlockSpec((1, 1, d), per_b),
                  pl.BlockSpec(w.shape, fixed), pl.BlockSpec((1, MLA_Q_RANK), fixed),
                  pl.BlockSpec((1, MLA_KV_RANK), fixed), pl.BlockSpec(wuq.shape, fixed),
                  pl.BlockSpec(wuk.shape, fixed), pl.BlockSpec(wuv.shape, fixed), tab, tab, tab],
        out_specs=[pl.BlockSpec((1, tm, wd), row) for wd in widths],
        out_shape=[jax.ShapeDtypeStruct((bsz, n, wd), F32 if k == 0 else BF16) for k, wd in enumerate(widths)],
        compiler_params=_cparams(("parallel", "parallel"), 48),
        name="odd_in_q" if need_q else "odd_in",
    )(x, g.reshape(1, d), shift, scale, w, qg, kvg, wuq, wuk, wuv, *tabs)


def _hy_fwd_kernel(wc_ref, ws_ref, u_ref, cw_ref, cb_ref, ka_ref, kb_ref, ya_ref, yb_ref, ubf_ref,
                   *, conv, tf):
    f = pl.program_id(1)

    @pl.when(f == 0)
    def _():
        u = u_ref[0].astype(F32)
        if conv:
            u = _short_conv(u, cw_ref[...], cb_ref[...])
        ubf_ref[...] = u.astype(BF16)

    ub = ubf_ref[...]
    a = _dot(wc_ref[...], ub)
    b = _dot(ws_ref[...], ub)
    ka, kb = ka_ref[...], kb_ref[...]
    first = (lax.broadcasted_iota(I32, a.shape, 0) + f * tf) == 0
    ya_ref[0] = jnp.where(first, a * ka, a * ka - b * kb).astype(BF16)
    yb_ref[0] = jnp.where(first, b * kb, a * kb + b * ka).astype(BF16)


def _hy_fwd(wc, ws, u, col, cw, cb, ka, kb, conv):
    bsz, n, _ = u.shape
    c = A_WIDTH
    tf = min(512, n)
    return pl.pallas_call(
        functools.partial(_hy_fwd_kernel, conv=conv, tf=tf),
        grid=(bsz, n // tf),
        in_specs=[pl.BlockSpec((tf, n), lambda b, f: (f, 0)), pl.BlockSpec((tf, n), lambda b, f: (f, 0)),
                  pl.BlockSpec((1, n, c), lambda b, f: (b, 0, col)),
                  pl.BlockSpec((3, c), lambda b, f: (0, 0)), pl.BlockSpec((1, c), lambda b, f: (0, 0)),
                  pl.BlockSpec((tf, c), lambda b, f: (f, 0)), pl.BlockSpec((tf, c), lambda b, f: (f, 0))],
        out_specs=[pl.BlockSpec((1, tf, c), lambda b, f: (b, f, 0))] * 2,
        out_shape=[jax.ShapeDtypeStruct((bsz, n, c), BF16)] * 2,
        scratch_shapes=[pltpu.VMEM((n, c), BF16)],
        compiler_params=_cparams(("parallel", "arbitrary"), 48),
        name="hyena_fwd",
    )(wc, ws, u, cw, cb, ka, kb)


def _hy_inv_kernel(ci_ref, si_ref, ya_ref, yb_ref, a_ref, g_ref, cwa_ref, cba_ref, cwg_ref, cbg_ref,
                   bias_ref, o_ref, a_sc, g_sc, *, conv_a, tt):
    t = pl.program_id(1)

    @pl.when(t == 0)
    def _():
        a = a_ref[0].astype(F32)
        if conv_a:
            a = _short_conv(a, cwa_ref[...], cba_ref[...])
        a_sc[...] = a
        g_sc[...] = _short_conv(g_ref[0].astype(F32), cwg_ref[...], cbg_ref[...])

    y = _dot(ci_ref[...], ya_ref[0]) + _dot(si_ref[...], yb_ref[0])
    rows = pl.ds(pl.multiple_of(t * tt, tt), tt)
    y = y + a_sc[rows, :] * bias_ref[...]
    o_ref[0] = (g_sc[rows, :] * y).astype(BF16)


def _hy_inv(ci, si, ya, yb, a, a_col, g, g_col, cwa, cba, cwg, cbg, bias, conv_a):
    bsz, n, c = ya.shape
    tt = min(512, n)
    whole = lambda b, t: (b, 0, 0)
    fixed = lambda b, t: (0, 0)
    return pl.pallas_call(
        functools.partial(_hy_inv_kernel, conv_a=conv_a, tt=tt),
        grid=(bsz, n // tt),
        in_specs=[pl.BlockSpec((tt, n), lambda b, t: (t, 0)), pl.BlockSpec((tt, n), lambda b, t: (t, 0)),
                  pl.BlockSpec((1, n, c), whole), pl.BlockSpec((1, n, c), whole),
                  pl.BlockSpec((1, n, c), lambda b, t: (b, 0, a_col)),
                  pl.BlockSpec((1, n, c), lambda b, t: (b, 0, g_col)),
                  pl.BlockSpec((3, c), fixed), pl.BlockSpec((1, c), fixed),
                  pl.BlockSpec((3, c), fixed), pl.BlockSpec((1, c), fixed), pl.BlockSpec((1, c), fixed)],
        out_specs=pl.BlockSpec((1, tt, c), lambda b, t: (b, t, 0)),
        out_shape=jax.ShapeDtypeStruct((bsz, n, c), BF16),
        scratch_shapes=[pltpu.VMEM((n, c), F32), pltpu.VMEM((n, c), F32)],
        compiler_params=_cparams(("parallel", "arbitrary"), 56),
        name="hyena_inv",
    )(ci, si, ya, yb, a, g, cwa, cba, cwg, cbg, bias)


def _dft_mats(n):
    big = 2 * n
    f = jnp.arange(n, dtype=I32)[:, None]
    s = jnp.arange(n, dtype=I32)[None, :]
    step = 64 if n % 64 == 0 else 1
    angle = lambda fr: ((fr[:, None] * s) % big).astype(F32) * (2.0 * math.pi / big)
    a_hi = angle(jnp.arange(n // step, dtype=I32) * step)
    a_lo = angle(jnp.arange(step, dtype=I32))
    c1, s1, c2, s2 = jnp.cos(a_hi)[:, None], jnp.sin(a_hi)[:, None], jnp.cos(a_lo)[None], jnp.sin(a_lo)[None]
    cos = (c1 * c2 - s1 * s2).reshape(n, n)
    sin = (s1 * c2 + c1 * s2).reshape(n, n)
    alt = jnp.where(jnp.arange(n) % 2 == 0, 1.0, -1.0).astype(F32)
    wc = cos
    ws = jnp.where(f == 0, alt[None, :], -sin)
    ci = jnp.where(s == 0, 1.0 / big, (2.0 / big) * cos)
    si = jnp.where(s == 0, alt[:, None] / big, -(2.0 / big) * sin)
    return (wc, ws), (wc.astype(BF16), ws.astype(BF16), ci.astype(BF16), si.astype(BF16))


def _spectrum_kernel(wc_ref, ws_ref, sh_ref, sl_ref, dh_ref, dl_ref, nyq_ref, nrm_ref, ka_ref, kb_ref, *, tf):
    ch, cl = _split_bf16(wc_ref[...])
    sh, sl = _split_bf16(ws_ref[...])
    a = _dot3(ch, cl, sh_ref[...], sl_ref[...])
    b = _dot3(sh, sl, dh_ref[...], dl_ref[...])
    first = (lax.broadcasted_iota(I32, a.shape, 0) + pl.program_id(1) * tf) == 0
    ka_ref[...] = a * nrm_ref[...]
    kb_ref[...] = jnp.where(first, nyq_ref[...], b) * nrm_ref[...]


def _filter_spectrum(tables, fwd, bwd):
    wc, ws = tables
    n, c = fwd.shape
    bwd = bwd.at[0].set(0.0)
    nrm = lax.rsqrt(jnp.sum(fwd * fwd, axis=0) + jnp.sum(bwd * bwd, axis=0) + EPS).reshape(1, c)
    s, d = fwd + bwd, fwd - bwd
    alt = jnp.where(jnp.arange(n) % 2 == 0, 1.0, -1.0).astype(F32)
    nyq = jnp.sum(alt[:, None] * s, axis=0).reshape(1, c)
    sh, sl = _split_bf16(s)
    dh, dl = _split_bf16(d)
    tf = min(256, n)
    tc = A_WIDTH
    tab = pl.BlockSpec((tf, n), lambda j, f: (f, 0))
    tap = pl.BlockSpec((n, tc), lambda j, f: (0, j))
    vec = pl.BlockSpec((1, tc), lambda j, f: (0, j))
    out = pl.BlockSpec((tf, tc), lambda j, f: (f, j))
    return pl.pallas_call(
        functools.partial(_spectrum_kernel, tf=tf),
        grid=(c // tc, n // tf),
        in_specs=[tab, tab, tap, tap, tap, tap, vec, vec],
        out_specs=[out, out],
        out_shape=[jax.ShapeDtypeStruct((n, c), F32)] * 2,
        compiler_params=_cparams(("parallel", "parallel"), 48),
        name="hyena_filter_spectrum",
    )(wc, ws, sh, sl, dh, dl, nyq, nrm)


def _hyena_filter_spectrum(n, tables, w1, b1, w2, b2, w3, freq):
    t = jnp.arange(n, dtype=F32)
    t_norm = t / max(n - 1, 1)
    bands = jnp.linspace(1e-4, HY_BANDS - 1, HY_BANDS, dtype=F32)
    ang = (2.0 * math.pi / n) * t[:, None] * bands
    z = jnp.concatenate([t_norm[:, None], jnp.cos(ang), jnp.sin(ang)], axis=-1)
    hdn = jnp.sin(freq * (jnp.dot(z, w1, precision=HIGHEST) + b1))
    hdn = jnp.sin(freq * (jnp.dot(hdn, w2, precision=HIGHEST) + b2))
    taps = jnp.dot(hdn, w3, precision=HIGHEST).reshape(n, 2, HY_ORDER, A_WIDTH)
    deltas = jnp.linspace(HY_MIN_DECAY, HY_MAX_DECAY, A_WIDTH, dtype=F32)
    taps = taps * jnp.exp(-t_norm[:, None] * deltas)[:, None, None, :]
    width = HY_ORDER * A_WIDTH
    ka, kb = _filter_spectrum(tables, taps[:, 0].reshape(n, width), taps[:, 1].reshape(n, width))
    return ka.reshape(n, HY_ORDER, A_WIDTH), kb.reshape(n, HY_ORDER, A_WIDTH)


def _hyena(hy, cw, cb, filt, fbias):
    n = hy.shape[1]
    c = A_WIDTH
    tables, (wc, ws, ci, si) = _dft_mats(n)
    ka, kb = _hyena_filter_spectrum(n, tables, *filt)
    cws = [cw[:, i * c:(i + 1) * c] for i in range(3)]
    cbs = [cb[i * c:(i + 1) * c].reshape(1, c) for i in range(3)]
    ya, yb = _hy_fwd(wc, ws, hy, 2, cws[2], cbs[2], ka[:, 0], kb[:, 0], True)
    z = _hy_inv(ci, si, ya, yb, hy, 2, hy, 0, cws[2], cbs[2], cws[0], cbs[0], fbias[0:1], True)
    ya, yb = _hy_fwd(wc, ws, z, 0, cws[2], cbs[2], ka[:, 1], kb[:, 1], False)
    return _hy_inv(ci, si, ya, yb, z, 0, hy, 1, cws[2], cbs[2], cws[1], cbs[1], fbias[1:2], False)


def _attn_kernel(*refs, nseg, hps, kps):
    q_ref = refs[0]
    k_refs = refs[1:1 + nseg]
    v_refs = refs[1 + nseg:1 + 2 * nseg]
    o_ref = refs[1 + 2 * nseg]
    low = lax.broadcasted_iota(I32, (q_ref.shape[1], LANES), 1) < HEAD_DIM
    for pair in range(hps // 2):
        accs = []
        for half in range(2):
            i = 2 * pair + half
            kv = (i * kps) // hps
            q = q_ref[0, :, LANES * i: LANES * (i + 1)]
            ss = [_dot_nt(q, k[0, :, LANES * kv: LANES * (kv + 1)]) for k in k_refs]
            m = ss[0].max(axis=-1, keepdims=True)
            for s in ss[1:]:
                m = jnp.maximum(m, s.max(axis=-1, keepdims=True))
            acc = None
            for s, v in zip(ss, v_refs):
                p = jnp.exp2((s - m).astype(BF16))
                o = _dot(p, v[0, :, LANES * kv: LANES * (kv + 1)])
                acc = o if acc is None else acc + o
            accs.append(acc)
        num = jnp.where(low, accs[0], pltpu.roll(accs[1], HEAD_DIM, 1))
        den = jnp.where(low, pltpu.roll(accs[0], HEAD_DIM, 1), accs[1])
        o_ref[0, :, LANES * pair: LANES * (pair + 1)] = (num / den).astype(BF16)


def _attention(q, ks, vs, hps, kps):
    bsz, lq, qw = q.shape
    nh = qw // LANES
    tq = min(512, lq)
    nseg = len(ks)
    kv_spec = lambda a: pl.BlockSpec((1, a.shape[1], kps * LANES), lambda b, g, t: (b, 0, g))
    return pl.pallas_call(
        functools.partial(_attn_kernel, nseg=nseg, hps=hps, kps=kps),
        grid=(bsz, nh // hps, lq // tq),
        in_specs=[pl.BlockSpec((1, tq, hps * LANES), lambda b, g, t: (b, t, g))]
        + [kv_spec(a) for a in ks] + [kv_spec(a) for a in vs],
        out_specs=pl.BlockSpec((1, tq, hps * HEAD_DIM), lambda b, g, t: (b, t, g)),
        out_shape=jax.ShapeDtypeStruct((bsz, lq, nh * HEAD_DIM), BF16),
        compiler_params=_cparams(("parallel", "parallel", "parallel"), 56),
        name="attention",
    )(q, *ks, *vs)


def _onehot_bf16(cond):
    return jnp.where(cond, 1.0, 0.0).astype(BF16)


def _s5_kernel(uc_ref, ul_ref, w_ref, a_ref, of_ref, or_ref, y_ref, xs_sc, yi_sc, s_sc, x_sc, ys_sc,
               *, bb, ncc, ncl):
    lc, gh, gt = S5_CHUNK, S5_GROUP, S5_TILE_GROUPS
    nc = ncc + ncl
    cw = lc * gh
    hb = gh.bit_length() - 1
    lb = LANES.bit_length() - 1

    for b in range(bb):
        for j in range(lc):
            p, half = divmod(j, 2)
            lanes = slice(half * LANES, (half + 1) * LANES)
            xs_sc[p, b * nc: b * nc + ncc, lanes] = uc_ref[b, pl.ds(j, ncc, stride=lc), :].astype(BF16)
            xs_sc[p, b * nc + ncc: (b + 1) * nc, lanes] = ul_ref[b, pl.ds(j, ncl, stride=lc), :].astype(BF16)

    r_i = lax.broadcasted_iota(I32, (2 * LANES, cw), 0)
    c_i = lax.broadcasted_iota(I32, (2 * LANES, cw), 1)
    same_h = (r_i & (gh - 1)) == (c_i & (gh - 1))
    r_grp = (r_i & (LANES - 1)) >> hb
    c_off = (c_i >> hb) - (r_i >> lb)
    for pr in range(gt // 2):
        rs = []
        for side in range(2):
            q = 2 * pr + side
            u = None
            for p in range(lc // 2):
                sel = _onehot_bf16(same_h & (r_grp == q) & (c_off == 2 * p))
                d = _dot(xs_sc[p], sel)
                u = d if u is None else u + d
            r = _dot(u.astype(BF16), w_ref[q])
            yi_sc[q] = r[:, :cw]
            rs.append(r)
        for kind in range(4):
            lo = cw + kind * LANES
            s_sc[kind, pr] = rs[0][:, lo:lo + LANES] + rs[1][:, lo:lo + LANES]

    a = a_ref[0]
    npair = gt // 2

    def advance(st, c, d):
        rws = pl.ds(c, bb, stride=nc)
        new = []
        for t in range(npair):
            xr, xi = st[2 * t], st[2 * t + 1]
            ar = a[2 * d:2 * d + 1, t * LANES:(t + 1) * LANES]
            ai = a[2 * d + 1:2 * d + 2, t * LANES:(t + 1) * LANES]
            x_sc[2 * d, t, rws, :] = xr
            x_sc[2 * d + 1, t, rws, :] = xi
            new.append(ar * xr - ai * xi + s_sc[2 * d, t, rws, :])
            new.append(ar * xi + ai * xr + s_sc[2 * d + 1, t, rws, :])
        return tuple(new)

    zero = tuple(jnp.zeros((bb, LANES), F32) for _ in range(2 * npair))
    lax.fori_loop(0, nc, lambda c, st: advance(st, c, 0), zero)
    st = lax.fori_loop(0, ncc, lambda i, st: advance(st, ncc - 1 - i, 1), zero)
    lax.fori_loop(0, ncl, lambda i, st: advance(st, nc - 1 - i, 1), st)

    def latent(ref, *idx):
        return jnp.concatenate([ref[idx + (slice(b * nc + ncc, (b + 1) * nc), slice(None))]
                                for b in range(bb)], axis=0)

    for q in range(gt):
        y = latent(yi_sc, q)
        for d, o_ref in ((0, of_ref), (1, or_ref)):
            for part in range(2):
                y = y + _dot(latent(x_sc, 2 * d + part, q // 2).astype(BF16), o_ref[q, part])
        for kt in range(cw // LANES):
            ys_sc[kt, :, q * LANES:(q + 1) * LANES] = y[:, kt * LANES:(kt + 1) * LANES].astype(BF16)

    r2 = lax.broadcasted_iota(I32, (gt * LANES, LANES), 0)
    c2 = lax.broadcasted_iota(I32, (gt * LANES, LANES), 1)
    keep = ((r2 & (gh - 1)) == (c2 & (gh - 1))) & ((r2 >> lb) == (c2 >> hb))
    r_k = (r2 & (LANES - 1)) >> hb
    for k in range(0, lc, 2):
        kt, kk = divmod(k, gt)
        sel = jnp.concatenate([_onehot_bf16(keep & (r_k == kk)), _onehot_bf16(keep & (r_k == kk + 1))], axis=1)
        z = _dot(ys_sc[kt], sel)
        for b in range(bb):
            y_ref[b, pl.ds(k, ncl, stride=lc), :] = z[b * ncl:(b + 1) * ncl, :LANES]
            y_ref[b, pl.ds(k + 1, ncl, stride=lc), :] = z[b * ncl:(b + 1) * ncl, LANES:]


def _s5_scan(u_c, u_l, w, a_tab, of, orr):
    bsz, ncx, c = u_c.shape
    n = u_l.shape[1]
    lc, gt = S5_CHUNK, S5_TILE_GROUPS
    ncc, ncl = ncx // lc, n // lc
    bb = min(4, bsz)
    rows = bb * (ncc + ncl)
    cw = lc * S5_GROUP
    sw = (gt // 2) * LANES
    return pl.pallas_call(
        functools.partial(_s5_kernel, bb=bb, ncc=ncc, ncl=ncl),
        grid=(c // LANES, bsz // bb),
        in_specs=[pl.BlockSpec((bb, ncx, LANES), lambda t, i: (i, 0, t)),
                  pl.BlockSpec((bb, n, LANES), lambda t, i: (i, 0, t)),
                  pl.BlockSpec((gt,) + w.shape[1:], lambda t, i: (t, 0, 0)),
                  pl.BlockSpec((1,) + a_tab.shape[1:], lambda t, i: (t, 0, 0)),
                  pl.BlockSpec((gt,) + of.shape[1:], lambda t, i: (t, 0, 0, 0)),
                  pl.BlockSpec((gt,) + orr.shape[1:], lambda t, i: (t, 0, 0, 0))],
        out_specs=pl.BlockSpec((bb, n, LANES), lambda t, i: (i, 0, t)),
        out_shape=jax.ShapeDtypeStruct((bsz, n, c), F32),
        scratch_shapes=[pltpu.VMEM((lc // 2, rows, 2 * LANES), BF16), pltpu.VMEM((gt, rows, cw), F32),
                        pltpu.VMEM((4, gt // 2, rows, LANES), F32), pltpu.VMEM((4, gt // 2, rows, LANES), F32),
                        pltpu.VMEM((cw // LANES, bb * ncl, gt * LANES), BF16)],
        compiler_params=_cparams(("parallel", "parallel"), 56),
        name="s5_scan",
    )(u_c, u_l, w, a_tab, of, orr)


def _s5_matrices(a_re, a_im, log_dt, b_re, b_im, c_re, c_im, d_skip):
    lc, h, p, g = S5_CHUNK, S5_GROUP, S5_STATE, S5_GROUPS
    dt = jnp.exp(log_dt)[..., None]
    lam_re, lam_im = a_re * dt, a_im * dt
    mag = jnp.exp(lam_re)
    er, ei = mag * jnp.cos(lam_im) - 1.0, mag * jnp.sin(lam_im)
    den = a_re * a_re + a_im * a_im
    co_re = (er * a_re + ei * a_im) / den
    co_im = (ei * a_re - er * a_im) / den
    bb_re = co_re[..., None] * b_re - co_im[..., None] * b_im
    bb_im = co_re[..., None] * b_im + co_im[..., None] * b_re
    m = jnp.arange(lc + 1, dtype=F32)[:, None, None, None]
    pw_mag = jnp.exp(m * lam_re[None])
    pw_re, pw_im = pw_mag * jnp.cos(m * lam_im[None]), pw_mag * jnp.sin(m * lam_im[None])
    ab_re = pw_re[..., None] * bb_re[None] - pw_im[..., None] * bb_im[None]
    ab_im = pw_re[..., None] * bb_im[None] + pw_im[..., None] * bb_re[None]
    kk = (jnp.einsum("dgop,mdgpi->mdgoi", c_re, ab_re[:lc], precision=HIGHEST)
          - jnp.einsum("dgop,mdgpi->mdgoi", c_im, ab_im[:lc], precision=HIGHEST))
    j = jnp.arange(lc)[:, None]
    k = jnp.arange(lc)[None, :]
    def toeplitz(resp):
        period = 2 * lc + 1
        x = jnp.concatenate([jnp.zeros_like(resp), resp, jnp.zeros_like(resp[:1])], axis=0)
        z = jnp.tile(x, (lc,) + (1,) * (resp.ndim - 1))[:lc * (period - 1)]
        return z.reshape((lc, period - 1) + resp.shape[1:])[:, lc:]

    tf = toeplitz(kk[:, 0])
    tr = jnp.swapaxes(toeplitz(kk[:, 1]), 0, 1)
    tw = jnp.transpose(tf + tr, (2, 0, 4, 1, 3))
    dsk = d_skip.reshape(g, h)
    eye = (jnp.eye(lc, dtype=F32)[None, :, None, :, None] * jnp.eye(h, dtype=F32)[None, None, :, None, :]
           * dsk[:, None, :, None, None])
    tw = (tw + eye).reshape(g, lc * h, lc * h)
    idx_f = slice(lc - 1, None, -1)
    idx_r = slice(0, lc)

    odd = (jnp.arange(g) % 2 == 1)

    def place(t, axis):
        z = jnp.zeros_like(t)
        shape = [g] + [1] * (t.ndim - 1)
        return jnp.where(odd.reshape(shape), jnp.concatenate([z, t], axis=axis),
                         jnp.concatenate([t, z], axis=axis))

    def to_state(idx, d):
        re = jnp.transpose(ab_re[idx, d], (1, 0, 3, 2)).reshape(g, lc * h, p)
        im = jnp.transpose(ab_im[idx, d], (1, 0, 3, 2)).reshape(g, lc * h, p)
        return [place(re, 2), place(im, 2)]

    w = jnp.concatenate([tw] + to_state(idx_f, 0) + to_state(idx_r, 1), axis=-1)

    def from_state(idx, d):
        pr, pi = pw_re[idx, d], pw_im[idx, d]
        cr, cim = c_re[d], c_im[d]
        mr = cr[None] * pr[:, :, None, :] - cim[None] * pi[:, :, None, :]
        mi = cr[None] * pi[:, :, None, :] + cim[None] * pr[:, :, None, :]
        rows = lambda m: place(jnp.transpose(m, (1, 3, 0, 2)).reshape(g, p, lc * h), 1)
        return jnp.stack([rows(mr), rows(-mi)], axis=1)

    of = from_state(slice(1, lc + 1), 0)
    orr = from_state(slice(lc, 0, -1), 1)
    tiles = g // S5_TILE_GROUPS
    a_tab = jnp.stack([pw_re[lc, 0].reshape(tiles, -1), pw_im[lc, 0].reshape(tiles, -1),
                       pw_re[lc, 1].reshape(tiles, -1), pw_im[lc, 1].reshape(tiles, -1)], axis=1)
    return w.astype(BF16), a_tab.astype(F32), of.astype(BF16), orr.astype(BF16)


def _glu_kernel(y_ref, w_ref, b_ref, o_ref):
    y = y_ref[0]
    inner = 0.7978845608028654 * (y + 0.044715 * (y * y * y))
    gl = 0.5 * y * (1.0 + jnp.tanh(inner))
    o_ref[0] = (gl * _sigmoid(_dot(gl.astype(BF16), w_ref[...]) + b_ref[...])).astype(BF16)


def _glu(y, w, b):
    bsz, n, c = y.shape
    tm = min(512, n)
    return pl.pallas_call(
        _glu_kernel,
        grid=(bsz, n // tm),
        in_specs=[pl.BlockSpec((1, tm, c), lambda i, t: (i, t, 0)),
                  pl.BlockSpec((c, c), lambda i, t: (0, 0)), pl.BlockSpec((1, c), lambda i, t: (0, 0))],
        out_specs=pl.BlockSpec((1, tm, c), lambda i, t: (i, t, 0)),
        out_shape=jax.ShapeDtypeStruct((bsz, n, c), BF16),
        compiler_params=_cparams(("parallel", "parallel"), 32),
        name="s5_glu",
    )(y, w, b.reshape(1, c))


def _s5_mixer(u_c, u_l, mats, glu_w, glu_b):
    return _glu(_s5_scan(u_c, u_l, *mats), glu_w, glu_b)


def _route(lg):
    lane = lax.broadcasted_iota(I32, lg.shape, 1).astype(F32)
    neg = jnp.float32(-1e30)
    far = jnp.float32(LANES)
    is_g = lane < N_GROUPS
    lgm = jnp.where(is_g, lg, neg)
    m = lgm.max(axis=-1, keepdims=True)
    gidx = jnp.where(lgm == m, lane, far).min(axis=-1, keepdims=True)
    gp = 1.0 / jnp.where(is_g, jnp.exp(lgm - m), 0.0).sum(axis=-1, keepdims=True)
    lo = N_GROUPS + EXPERTS_PER_GROUP * gidx
    is_e = (lane >= lo) & (lane < lo + EXPERTS_PER_GROUP)
    lem = jnp.where(is_e, lg, neg)
    l0 = lem.max(axis=-1, keepdims=True)
    i0 = jnp.where((lem == l0) & is_e, lane, far).min(axis=-1, keepdims=True)
    lem1 = jnp.where(lane == i0, neg, lem)
    l1 = lem1.max(axis=-1, keepdims=True)
    i1 = jnp.where((lem1 == l1) & is_e & (lane != i0), lane, far).min(axis=-1, keepdims=True)
    e1 = jnp.exp(l1 - l0)
    w0 = 1.0 / (1.0 + e1)
    w1 = e1 * w0
    out = jnp.where(lane == 0, i0 - N_GROUPS, 0.0)
    out = jnp.where(lane == 1, i1 - N_GROUPS, out)
    out = jnp.where(lane == 2, gp * w0, out)
    return jnp.where(lane == 3, gp * w1, out)


def _proj_res_kernel(a_ref, b_ref, wa_ref, wb_ref, x_ref, g1_ref, ng_ref, sh_ref, sc_ref, rh_ref, rl_ref,
                     xo_ref, h_ref, route_ref):
    for rows in _row_chunks(x_ref.shape[1]):
        o = _dot(a_ref[0, rows, :], wa_ref[...]) + _dot(b_ref[0, rows, :], wb_ref[...])
        x = x_ref[0, rows, :] + g1_ref[0] * o
        xo_ref[0, rows, :] = x
        h = _modnorm(x, ng_ref[...], sh_ref[0], sc_ref[0])
        _store_token_tiles(h_ref, rows.start, h)
        hh, hl = _split_bf16(h)
        route_ref[0, rows, :] = _route(_dot3(hh, hl, rh_ref[...], rl_ref[...]))


def _proj_res(a, b, wa, wb, x, g1, ng, shift, scale, rh, rl):
    bsz, n, d = x.shape
    tm = min(512, n)
    row = lambda i, t: (i, t, 0)
    per_b = lambda i, t: (i, 0, 0)
    fixed = lambda i, t: (0, 0)
    return pl.pallas_call(
        _proj_res_kernel,
        grid=(bsz, n // tm),
        in_specs=[pl.BlockSpec((1, tm, a.shape[2]), row), pl.BlockSpec((1, tm, b.shape[2]), row),
                  pl.BlockSpec(wa.shape, fixed), pl.BlockSpec(wb.shape, fixed),
                  pl.BlockSpec((1, tm, d), row), pl.BlockSpec((1, 1, d), per_b), pl.BlockSpec((1, d), fixed),
                  pl.BlockSpec((1, 1, d), per_b), pl.BlockSpec((1, 1, d), per_b),
                  pl.BlockSpec((d, LANES), fixed), pl.BlockSpec((d, LANES), fixed)],
        out_specs=[pl.BlockSpec((1, tm, d), row),
                   pl.BlockSpec((tm * SUBLANES, LANES), lambda i, t: (i * (n // tm) + t, 0)),
                   pl.BlockSpec((1, tm, LANES), row)],
        out_shape=[jax.ShapeDtypeStruct((bsz, n, d), F32),
                   jax.ShapeDtypeStruct((bsz * n * SUBLANES, LANES), F32),
                   jax.ShapeDtypeStruct((bsz, n, LANES), F32)],
        compiler_params=_cparams(("parallel", "parallel"), 40),
        name="proj_res",
    )(a, b, wa, wb, x, g1, ng.reshape(1, d), shift, scale, rh, rl)


def _token_copy(src, src_tok, dst, dst_tok, sem):
    rows = lambda t: pl.ds(pl.multiple_of(t * SUBLANES, SUBLANES), SUBLANES)
    return pltpu.make_async_copy(src.at[rows(src_tok)], dst.at[rows(dst_tok)], sem)


def _tile_copy(src, src_tok, dst, dst_tok, sem):
    n = MOE_TILE * SUBLANES
    rows = lambda t: pl.ds(pl.multiple_of(t * SUBLANES, SUBLANES), n)
    return pltpu.make_async_copy(src.at[rows(src_tok)], dst.at[rows(dst_tok)], sem)


def _experts_kernel(te_ref, ts_ref, tr_ref, tp_ref, win_ref, winn_ref, h_ref, wg_ref, wu_ref, wd_ref, ys_ref,
                    xbuf, obuf, gsem, ssem, *, n_assign):
    del te_ref
    i = pl.program_id(0)
    nt = pl.num_programs(0)
    slot = i % 2
    other = 1 - slot
    nxt = jnp.minimum(i + 1, nt - 1)
    within = lambda start: start & (MOE_TILE - 1)

    def start_gather(w_ref, off, s):
        def body(j, c):
            for k in range(2):
                r = 2 * j + k
                tok = lax.shift_right_logical(w_ref[0, 0, off + r], 1)
                _token_copy(h_ref, tok, xbuf, s * MOE_TILE + r, gsem.at[s]).start(priority=k)
            return c
        lax.fori_loop(0, MOE_TILE // 2, body, 0, unroll=4)

    @pl.when(i == 0)
    def _():
        start_gather(win_ref, within(ts_ref[0]), 0)

    @pl.when(i + 1 < nt)
    def _():
        start_gather(winn_ref, within(ts_ref[nxt]), other)

    base = slot * MOE_TILE
    _tile_copy(h_ref, 0, xbuf, base, gsem.at[slot]).wait()

    @pl.when(i >= 2)
    def _():
        _tile_copy(obuf, base, ys_ref, 0, ssem.at[slot]).wait()

    x = _load_token_tiles(xbuf, base, MOE_TILE).astype(BF16)
    gt = _dot(x, wg_ref[0, 0].astype(BF16))
    hid = (gt * _sigmoid(gt)) * _dot(x, wu_ref[0, 0].astype(BF16))
    _store_token_tiles(obuf, base, _dot(hid.astype(BF16), wd_ref[0, 0].astype(BF16)))

    off, real = within(ts_ref[i]), tr_ref[i]
    spare = n_assign + tp_ref[i] - real

    def scatter(j, c):
        for k in range(2):
            r = 2 * j + k
            dst = jnp.where(r < real, win_ref[0, 0, off + r], spare + r)
            _token_copy(obuf, base + r, ys_ref, dst, ssem.at[slot]).start(priority=k)
        return c
    lax.fori_loop(0, MOE_TILE // 2, scatter, 0, unroll=4)

    @pl.when(i == nt - 1)
    def _():
        _tile_copy(obuf, base, ys_ref, 0, ssem.at[slot]).wait()
        _tile_copy(obuf, other * MOE_TILE, ys_ref, 0, ssem.at[other]).wait()


def _experts(plan, h, n_assign, layer, wg, wu, wd):
    windows, tile_expert, tile_start, tile_real, tile_padbase = plan
    nt = tile_expert.shape[0]
    d = D_MODEL
    wmap = lambda i, te, ts, tr, tp: (layer, te[i], 0, 0)
    log_tile = MOE_TILE.bit_length() - 1
    smem = lambda f: pl.BlockSpec((1, 1, 2 * MOE_TILE), f, memory_space=pltpu.SMEM)
    tile_rows = MOE_TILE * SUBLANES
    return pl.pallas_call(
        functools.partial(_experts_kernel, n_assign=n_assign),
        grid_spec=pltpu.PrefetchScalarGridSpec(
            num_scalar_prefetch=4, grid=(nt,),
            in_specs=[smem(lambda i, te, ts, tr, tp: (lax.shift_right_logical(ts[i], log_tile), 0, 0)),
                      smem(lambda i, te, ts, tr, tp: (
                          lax.shift_right_logical(ts[jnp.minimum(i + 1, nt - 1)], log_tile), 0, 0)),
                      pl.BlockSpec(memory_space=pl.ANY),
                      pl.BlockSpec((1, 1, d, D_EXPERT), wmap), pl.BlockSpec((1, 1, d, D_EXPERT), wmap),
                      pl.BlockSpec((1, 1, D_EXPERT, d), wmap)],
            out_specs=pl.BlockSpec(memory_space=pl.ANY),
            scratch_shapes=[pltpu.VMEM((2 * tile_rows, LANES), F32), pltpu.VMEM((2 * tile_rows, LANES), F32),
                            pltpu.SemaphoreType.DMA((2,)), pltpu.SemaphoreType.DMA((2,))]),
        out_shape=jax.ShapeDtypeStruct((nt * tile_rows, LANES), F32),
        compiler_params=pltpu.CompilerParams(dimension_semantics=("arbitrary",), vmem_limit_bytes=40 * MIB,
                                             has_side_effects=True),
        name="moe_experts",
    )(tile_expert, tile_start, tile_real, tile_padbase, windows[:, None, :], windows[:, None, :], h, wg, wu, wd)


def _combine_kernel(route_ref, x_ref, g2_ref, fg_ref, ys_ref, o_ref, *, tc, final):
    rt = route_ref[0]
    g0, g1 = rt[:, 2:3], rt[:, 3:4]
    cols = [g0 * ys_ref[pl.ds(s, tc, stride=2 * SUBLANES), :]
            + g1 * ys_ref[pl.ds(SUBLANES + s, tc, stride=2 * SUBLANES), :] for s in range(SUBLANES)]
    x = x_ref[0] + g2_ref[0] * jnp.concatenate(cols, axis=-1)
    if final:
        x = _rms(x, fg_ref[...], x.shape[-1])
    o_ref[0] = x


def _combine(route, x, g2, fg, ys, tok_off, final):
    bsz, n, d = x.shape
    tc = min(256, n)
    nt = n // tc
    row = lambda i, t: (i, t, 0)
    blk = tok_off // tc
    return pl.pallas_call(
        functools.partial(_combine_kernel, tc=tc, final=final),
        grid=(bsz, nt),
        in_specs=[pl.BlockSpec((1, tc, LANES), row), pl.BlockSpec((1, tc, d), row),
                  pl.BlockSpec((1, 1, d), lambda i, t: (i, 0, 0)), pl.BlockSpec((1, d), lambda i, t: (0, 0)),
                  pl.BlockSpec((2 * tc * SUBLANES, LANES), lambda i, t: (blk + i * nt + t, 0))],
        out_specs=pl.BlockSpec((1, tc, d), row),
        out_shape=jax.ShapeDtypeStruct((bsz, n, d), F32),
        compiler_params=_cparams(("parallel", "parallel"), 32),
        name="moe_combine",
    )(route, x, g2, fg.reshape(1, d), ys)


def _moe_plan(eids):
    a = eids.shape[0]
    order = jnp.argsort(eids, stable=True).astype(I32)
    experts = jnp.arange(N_EXPERTS, dtype=I32)
    counts = (eids[:, None] == experts[None, :]).astype(I32).sum(axis=0)
    tiles_e = (counts + MOE_TILE - 1) // MOE_TILE
    tile_end = jnp.cumsum(tiles_e)
    first = jnp.cumsum(counts) - counts
    nt = a // MOE_TILE + N_EXPERTS
    tidx = jnp.arange(nt, dtype=I32)
    tile_expert = jnp.minimum((tidx[:, None] >= tile_end[None, :]).astype(I32).sum(axis=1), N_EXPERTS - 1)
    pick = lambda table: ((tile_expert[:, None] == experts[None, :]).astype(I32) * table[None, :]).sum(axis=1)
    tile_pos = (tidx - pick(tile_end - tiles_e)) * MOE_TILE
    used = tidx < tile_end[-1]
    tile_real = jnp.where(used, jnp.clip(pick(counts) - tile_pos, 0, MOE_TILE), 0)
    tile_start = jnp.where(used, pick(first) + tile_pos, 0)
    pads = MOE_TILE - tile_real
    tile_padbase = jnp.cumsum(pads) - pads
    rows = jnp.concatenate([order, jnp.zeros((MOE_TILE,), I32)]).reshape(-1, MOE_TILE)
    windows = jnp.concatenate([rows[:-1], rows[1:]], axis=1)
    return windows, tile_expert, tile_start, tile_real, tile_padbase


def _moe(streams, layer, wg, wu, wd, final_g, final):
    eids = jnp.concatenate([s[2][..., :2].reshape(-1) for s in streams]).astype(I32)
    h = streams[0][1] if len(streams) == 1 else jnp.concatenate([s[1] for s in streams], axis=0)
    ys = _experts(_moe_plan(eids), h, eids.shape[0], layer, wg, wu, wd)
    outs, off = [], 0
    for x, _, route, g2 in streams:
        outs.append(_combine(route, x, g2, final_g, ys, off, final))
        off += x.shape[0] * x.shape[1]
    return outs


def _rope_tables(n, half, offset, identity):
    cs = jnp.ones((n, LANES), F32)
    z = jnp.zeros((n, LANES), F32)
    if identity:
        return cs, z, z
    rows = (jnp.arange(n, dtype=I32) // GRID_W).astype(F32)
    cols = (jnp.arange(n, dtype=I32) % GRID_W).astype(F32)
    freqs = ROPE_THETA ** (-jnp.arange(half, dtype=F32) / half)
    ar, ac = rows[:, None] * freqs, cols[:, None] * freqs
    zero = jnp.zeros((n, half), F32)
    cos4 = jnp.concatenate([jnp.cos(ar), jnp.cos(ar), jnp.cos(ac), jnp.cos(ac)], axis=1)
    s1 = jnp.concatenate([-jnp.sin(ar), zero, -jnp.sin(ac), zero], axis=1)
    s2 = jnp.concatenate([zero, jnp.sin(ar), zero, jnp.sin(ac)], axis=1)
    place = lambda base, t: lax.dynamic_update_slice(base, t, (0, offset))
    return place(cs, cos4), place(z, s1), place(z, s2)


def _pad_heads(w, nheads, width, left=0):
    k = w.shape[0]
    w = w.reshape(k, nheads, width)
    w = jnp.pad(w, ((0, 0), (0, 0), (left, LANES - width - left)))
    return w.reshape(k, nheads * LANES)


def _lane_pad(v, left=0):
    return jnp.pad(v, (left, LANES - v.shape[0] - left)).reshape(1, LANES)


def kernel(x, c, ctx, c_ctx, w_mod, b_mod, norm_g, e_w_in, e_hy_conv_w, e_hy_conv_b, e_hy_w1, e_hy_b1, e_hy_w2, e_hy_b2, e_hy_w3, e_hy_freq, e_hy_fbias, e_qk_g, e_w_out, o_w_in, o_q_norm_g, o_w_uq, o_kv_norm_g, o_w_ukv, o_s5_a_re, o_s5_a_im, o_s5_log_dt, o_s5_b_re, o_s5_b_im, o_s5_c_re, o_s5_c_im, o_s5_d, o_glu_w, o_glu_b, o_w_out, moe_w_rg, moe_w_re, moe_w_gate, moe_w_up, moe_w_down, final_g):
    bsz, n, d = x.shape
    depth = w_mod.shape[0]
    xc = ctx
    ncx = ctx.shape[1]
    cc = jnp.zeros((bsz + 8, d), F32).at[:bsz].set(c).at[bsz].set(c_ctx)

    for i in range(depth):
        j = i // 2
        last = i == depth - 1
        mods = _modulation(cc, w_mod, b_mod, i)
        lat = [m.reshape(bsz, 1, d) for m in jnp.split(mods[:bsz], 6, axis=-1)]
        cxm = [jnp.broadcast_to(m.reshape(1, 1, d), (bsz, 1, d)) for m in jnp.split(mods[bsz], 6, axis=-1)]

        if i % 2 == 0:
            wq = _pad_heads(e_w_in[j][:, 3 * A_WIDTH:3 * A_WIDTH + B_Q_HEADS * HEAD_DIM], B_Q_HEADS, HEAD_DIM)
            k0 = 3 * A_WIDTH + B_Q_HEADS * HEAD_DIM
            wk = _pad_heads(e_w_in[j][:, k0:k0 + B_KV_HEADS * HEAD_DIM], B_KV_HEADS, HEAD_DIM)
            wv = _pad_heads(e_w_in[j][:, k0 + B_KV_HEADS * HEAD_DIM:], B_KV_HEADS, HEAD_DIM)
            w_in = jnp.concatenate([e_w_in[j][:, :3 * A_WIDTH], wq, wk, wv], axis=1).astype(BF16)
            qg, kg = _lane_pad(e_qk_g[j, 0]), _lane_pad(e_qk_g[j, 1])
            filt = (e_hy_w1[j], e_hy_b1[j], e_hy_w2[j], e_hy_b2[j], e_hy_w3[j], e_hy_freq[j])
            qscale = HEAD_DIM ** -0.5 * LOG2E
            hy_l, q_l, k_l, v_l = _even_in(x, norm_g[i, 0], lat[0], lat[1], w_in, qg, kg,
                                           _rope_tables(n, HEAD_DIM // 4, 0, False), qscale)
            hy_c, q_c, k_c, v_c = _even_in(xc, norm_g[i, 0], cxm[0], cxm[1], w_in, qg, kg,
                                           _rope_tables(ncx, HEAD_DIM // 4, 0, True), qscale)
            rep = B_Q_HEADS // B_KV_HEADS
            att_l = _attention(q_l, [k_c, k_l], [v_c, v_l], rep, 1)
            mix_l = (_hyena(hy_l, e_hy_conv_w[j], e_hy_conv_b[j], filt, e_hy_fbias[j]), att_l)
            if not last:
                att_c = _attention(q_c, [k_c], [v_c], rep, 1)
                mix_c = (_hyena(hy_c, e_hy_conv_w[j], e_hy_conv_b[j], filt, e_hy_fbias[j]), att_c)
            w_out = e_w_out[j]
            split = A_WIDTH
        else:
            w1 = o_w_in[j]
            c0, c1, c2 = MLA_Q_RANK, MLA_Q_RANK + MLA_KV_RANK, MLA_Q_RANK + MLA_KV_RANK + MLA_ROPE
            w_kr = jnp.pad(w1[:, c1:c2], ((0, 0), (MLA_NOPE, LANES - MLA_NOPE - MLA_ROPE)))
            w_in = jnp.concatenate([w1[:, :c1], w_kr, w1[:, c2:]], axis=1).astype(BF16)
            wuq = _pad_heads(o_w_uq[j], MLA_HEADS, MLA_NOPE + MLA_ROPE).astype(BF16)
            wkv = o_w_ukv[j].reshape(MLA_KV_RANK, MLA_HEADS, MLA_NOPE + MLA_V)
            wuk = _pad_heads(wkv[:, :, :MLA_NOPE].reshape(MLA_KV_RANK, -1), MLA_HEADS, MLA_NOPE).astype(BF16)
            wuv = _pad_heads(wkv[:, :, MLA_NOPE:].reshape(MLA_KV_RANK, -1), MLA_HEADS, MLA_V).astype(BF16)
            qg = o_q_norm_g[j].reshape(1, MLA_Q_RANK)
            kvg = o_kv_norm_g[j].reshape(1, MLA_KV_RANK)
            args = (w_in, qg, kvg, wuq, wuk, wuv)
            qscale = (MLA_NOPE + MLA_ROPE) ** -0.5 * LOG2E
            u_l, k_l, v_l, q_l = _odd_in(x, norm_g[i, 0], lat[0], lat[1], *args,
                                         _rope_tables(n, MLA_ROPE // 4, MLA_NOPE, False), True, qscale)
            outs_c = _odd_in(xc, norm_g[i, 0], cxm[0], cxm[1], *args,
                             _rope_tables(ncx, MLA_ROPE // 4, MLA_NOPE, True), not last, qscale)
            u_c, k_c, v_c = outs_c[:3]
            att_l = _attention(q_l, [k_c, k_l], [v_c, v_l], 2, 2)
            mats = _s5_matrices(o_s5_a_re[j], o_s5_a_im[j], o_s5_log_dt[j], o_s5_b_re[j], o_s5_b_im[j],
                                o_s5_c_re[j], o_s5_c_im[j], o_s5_d[j])
            glu_w = o_glu_w[j].astype(BF16)
            mix_l = (att_l, _s5_mixer(u_c, u_l, mats, glu_w, o_glu_b[j]))
            if not last:
                raise NotImplementedError("an odd layer that is not the last needs the context S5 readout")
            w_out = o_w_out[j]
            split = MLA_HEADS * MLA_V

        wa, wb = w_out[:split].astype(BF16), w_out[split:].astype(BF16)
        router = jnp.pad(jnp.concatenate([moe_w_rg[i], moe_w_re[i]], axis=1),
                         ((0, 0), (0, LANES - N_GROUPS - N_EXPERTS)))
        rh = router.astype(BF16)
        rl = (router - rh.astype(F32)).astype(BF16)

        x, h_l, route_l = _proj_res(*mix_l, wa, wb, x, lat[2], norm_g[i, 1], lat[3], lat[4], rh, rl)
        streams = [(x, h_l, route_l, lat[5])]
        if not last:
            xc, h_c, route_c = _proj_res(*mix_c, wa, wb, xc, cxm[2], norm_g[i, 1], cxm[3], cxm[4], rh, rl)
            streams.append((xc, h_c, route_c, cxm[5]))
        outs = _moe(streams, i, moe_w_gate, moe_w_up, moe_w_down, final_g, last)
        x = outs[0]
        if not last:
            xc = outs[1]
    return x
```

```python
import functools
import math

import jax
import jax.numpy as jnp
from jax import lax
from jax.experimental import pallas as pl
from jax.experimental.pallas import tpu as pltpu

F32 = jnp.float32
BF16 = jnp.bfloat16
I32 = jnp.int32

D_MODEL = 1024
GRID_W = 64
HEAD_DIM = 64
A_WIDTH = D_MODEL // 2
B_Q_HEADS = (D_MODEL - A_WIDTH) // HEAD_DIM
B_KV_HEADS = max(1, B_Q_HEADS // 4)
HY_ORDER = 2
HY_BANDS = 16
HY_TARGET = 1e-2
HY_MAX_DECAY = math.log(1.0 / HY_TARGET) / 0.3
HY_MIN_DECAY = math.log(1.0 / HY_TARGET) / 1.5
MLA_HEADS = 8
MLA_NOPE = 64
MLA_ROPE = 32
MLA_V = 64
MLA_Q_RANK = D_MODEL // 4
MLA_KV_RANK = D_MODEL // 8
S5_WIDTH = D_MODEL - MLA_HEADS * MLA_V
S5_GROUP = 16
S5_GROUPS = S5_WIDTH // S5_GROUP
S5_STATE = 64
S5_CHUNK = 16
S5_TILE_GROUPS = 128 // S5_GROUP
N_GROUPS = 4
EXPERTS_PER_GROUP = 8
N_EXPERTS = N_GROUPS * EXPERTS_PER_GROUP
D_EXPERT = D_MODEL // 4
ROPE_THETA = 10000.0
EPS = 1e-6
LOG2E = 1.4426950408889634

LANES = 128
SUBLANES = 8
MOE_TILE = 256
ROW_CHAIN = 256
HIGHEST = lax.Precision.HIGHEST
MIB = 1024 * 1024


def _cparams(sem, vmem_mib):
    return pltpu.CompilerParams(dimension_semantics=sem, vmem_limit_bytes=vmem_mib * MIB)


def _dot(a, b):
    return jnp.dot(a, b, preferred_element_type=F32)


def _dot_nt(a, b):
    return lax.dot_general(a, b, (((1,), (1,)), ((), ())), preferred_element_type=F32)


def _split_bf16(a):
    hi = a.astype(BF16)
    lo = (a - hi.astype(F32)).astype(BF16)
    return hi, lo


def _dot3(ah, al, bh, bl):
    return _dot(ah, bh) + (_dot(ah, bl) + _dot(al, bh))


def _sigmoid(x):
    return 1.0 / (1.0 + jnp.exp(-x))


def _rms(x, g, width):
    ms = jnp.sum(x * x, axis=-1, keepdims=True) * (1.0 / width)
    return x * lax.rsqrt(ms + EPS) * g


def _modnorm(x, g, shift, scale):
    return _rms(x, g, x.shape[-1]) * (1.0 + scale) + shift


def _rope(t, cs, s1, s2, half):
    return t * cs + pltpu.roll(t, LANES - half, 1) * s1 + pltpu.roll(t, half, 1) * s2


def _store_token_tiles(ref, base_tok, val):
    n = val.shape[0]
    for s in range(SUBLANES):
        ref[pl.ds(base_tok * SUBLANES + s, n, stride=SUBLANES), :] = val[:, LANES * s: LANES * (s + 1)]


def _load_token_tiles(ref, base_tok, n):
    return jnp.concatenate([ref[pl.ds(base_tok * SUBLANES + s, n, stride=SUBLANES), :]
                            for s in range(SUBLANES)], axis=-1)


def _short_conv(u, w, b):
    n = u.shape[0]
    row = lax.broadcasted_iota(I32, u.shape, 0)
    prev = jnp.where(row == 0, 0.0, pltpu.roll(u, 1, 0))
    nxt = jnp.where(row == n - 1, 0.0, pltpu.roll(u, n - 1, 0))
    return prev * w[0:1] + u * w[1:2] + nxt * w[2:3] + b


def _mod_kernel(c_ref, w_ref, b_ref, o_ref):
    c = c_ref[...]
    s = c * _sigmoid(c)
    sh, sl = _split_bf16(s)
    wh, wl = _split_bf16(w_ref[0])
    o_ref[...] = _dot3(sh, sl, wh, wl) + b_ref[0]


def _modulation(cc, w, b, layer):
    r, d = cc.shape
    n = w.shape[2]
    tn = 1536
    return pl.pallas_call(
        _mod_kernel,
        grid=(n // tn,),
        in_specs=[pl.BlockSpec((r, d), lambda j: (0, 0)),
                  pl.BlockSpec((1, d, tn), lambda j: (layer, 0, j)),
                  pl.BlockSpec((1, 1, tn), lambda j: (layer, 0, j))],
        out_specs=pl.BlockSpec((r, tn), lambda j: (0, j)),
        out_shape=jax.ShapeDtypeStruct((r, n), F32),
        compiler_params=_cparams(("parallel",), 48),
        name="modulation",
    )(cc, w, b.reshape(b.shape[0], 1, n))


def _with_ones(v):
    lane = lax.broadcasted_iota(I32, v.shape, 1) % LANES
    return jnp.where(lane >= HEAD_DIM, 1.0, v)


def _row_chunks(tm):
    sub = min(ROW_CHAIN, tm)
    return [slice(r, r + sub) for r in range(0, tm, sub)]


def _even_in_kernel(x_ref, g_ref, sh_ref, sc_ref, w_ref, qg_ref, kg_ref, cs_ref, s1_ref, s2_ref,
                    hy_ref, q_ref, k_ref, v_ref, *, qscale):
    hw = 3 * A_WIDTH
    kw = hw + LANES * B_Q_HEADS
    vw = kw + LANES * B_KV_HEADS
    for rows in _row_chunks(x_ref.shape[1]):
        xn = _modnorm(x_ref[0, rows, :], g_ref[...], sh_ref[0], sc_ref[0]).astype(BF16)
        p = _dot(xn, w_ref[...])
        hy_ref[0, rows, :] = p[:, :hw].astype(BF16)
        cs, s1, s2 = cs_ref[rows, :], s1_ref[rows, :], s2_ref[rows, :]
        for h in range(B_Q_HEADS):
            t = _rms(p[:, hw + LANES * h: hw + LANES * (h + 1)], qg_ref[...], HEAD_DIM)
            q_ref[0, rows, LANES * h: LANES * (h + 1)] = (
                _rope(t, cs, s1, s2, HEAD_DIM // 4) * qscale).astype(BF16)
        for h in range(B_KV_HEADS):
            t = _rms(p[:, kw + LANES * h: kw + LANES * (h + 1)], kg_ref[...], HEAD_DIM)
            k_ref[0, rows, LANES * h: LANES * (h + 1)] = _rope(t, cs, s1, s2, HEAD_DIM // 4).astype(BF16)
        v_ref[0, rows, :] = _with_ones(p[:, vw:]).astype(BF16)


def _even_in(x, g, shift, scale, w, qg, kg, tabs, qscale):
    bsz, n, d = x.shape
    tm = min(512, n)
    nw = w.shape[1]
    row = lambda b, t: (b, t, 0)
    per_b = lambda b, t: (b, 0, 0)
    fixed = lambda b, t: (0, 0)
    tab = pl.BlockSpec((tm, LANES), lambda b, t: (t, 0))
    return pl.pallas_call(
        functools.partial(_even_in_kernel, qscale=qscale),
        grid=(bsz, n // tm),
        in_specs=[pl.BlockSpec((1, tm, d), row), pl.BlockSpec((1, d), fixed),
                  pl.BlockSpec((1, 1, d), per_b), pl.BlockSpec((1, 1, d), per_b),
                  pl.BlockSpec((d, nw), fixed), pl.BlockSpec((1, LANES), fixed),
                  pl.BlockSpec((1, LANES), fixed), tab, tab, tab],
        out_specs=[pl.BlockSpec((1, tm, 3 * A_WIDTH), row),
                   pl.BlockSpec((1, tm, LANES * B_Q_HEADS), row),
                   pl.BlockSpec((1, tm, LANES * B_KV_HEADS), row),
                   pl.BlockSpec((1, tm, LANES * B_KV_HEADS), row)],
        out_shape=[jax.ShapeDtypeStruct((bsz, n, 3 * A_WIDTH), BF16),
                   jax.ShapeDtypeStruct((bsz, n, LANES * B_Q_HEADS), BF16),
                   jax.ShapeDtypeStruct((bsz, n, LANES * B_KV_HEADS), BF16),
                   jax.ShapeDtypeStruct((bsz, n, LANES * B_KV_HEADS), BF16)],
        compiler_params=_cparams(("parallel", "parallel"), 48),
        name="even_in",
    )(x, g.reshape(1, d), shift, scale, w, qg, kg, *tabs)


def _odd_in_kernel(x_ref, g_ref, sh_ref, sc_ref, w_ref, qg_ref, kvg_ref, wuq_ref, wuk_ref, wuv_ref,
                   cs_ref, s1_ref, s2_ref, *out_refs, need_q, qscale):
    if need_q:
        u_ref, k_ref, v_ref, q_ref = out_refs
    else:
        u_ref, k_ref, v_ref = out_refs
    c0, c1, c2 = MLA_Q_RANK, MLA_Q_RANK + MLA_KV_RANK, MLA_Q_RANK + MLA_KV_RANK + LANES
    for rows in _row_chunks(x_ref.shape[1]):
        xn = _modnorm(x_ref[0, rows, :], g_ref[...], sh_ref[0], sc_ref[0]).astype(BF16)
        p = _dot(xn, w_ref[...])
        u_ref[0, rows, :] = p[:, c2:]
        cs, s1, s2 = cs_ref[rows, :], s1_ref[rows, :], s2_ref[rows, :]
        ckv = _rms(p[:, c0:c1], kvg_ref[...], MLA_KV_RANK).astype(BF16)
        kn = _dot(ckv, wuk_ref[...])
        v_ref[0, rows, :] = _with_ones(_dot(ckv, wuv_ref[...])).astype(BF16)
        kr = _rope(p[:, c1:c2], cs, s1, s2, MLA_ROPE // 4)
        for h in range(MLA_HEADS):
            k_ref[0, rows, LANES * h: LANES * (h + 1)] = (kn[:, LANES * h: LANES * (h + 1)] + kr).astype(BF16)
        if need_q:
            cq = _rms(p[:, :c0], qg_ref[...], MLA_Q_RANK).astype(BF16)
            q = _dot(cq, wuq_ref[...])
            for h in range(MLA_HEADS):
                t = q[:, LANES * h: LANES * (h + 1)]
                q_ref[0, rows, LANES * h: LANES * (h + 1)] = (
                    _rope(t, cs, s1, s2, MLA_ROPE // 4) * qscale).astype(BF16)


def _odd_in(x, g, shift, scale, w, qg, kvg, wuq, wuk, wuv, tabs, need_q, qscale):
    bsz, n, d = x.shape
    tm = min(512, n)
    hw = LANES * MLA_HEADS
    row = lambda b, t: (b, t, 0)
    per_b = lambda b, t: (b, 0, 0)
    fixed = lambda b, t: (0, 0)
    tab = pl.BlockSpec((tm, LANES), lambda b, t: (t, 0))
    widths = [S5_WIDTH, hw, hw] + ([hw] if need_q else [])
    return pl.pallas_call(
        functools.partial(_odd_in_kernel, need_q=need_q, qscale=qscale),
        grid=(bsz, n // tm),
        in_specs=[pl.BlockSpec((1, tm, d), row), pl.BlockSpec((1, d), fixed),
                  pl.BlockSpec((1, 1, d), per_b), pl.BlockSpec((1, 1, d), per_b),
                  pl.BlockSpec(w.shape, fixed), pl.BlockSpec((1, MLA_Q_RANK), fixed),
                  pl.BlockSpec((1, MLA_KV_RANK), fixed), pl.BlockSpec(wuq.shape, fixed),
                  pl.BlockSpec(wuk.shape, fixed), pl.BlockSpec(wuv.shape, fixed), tab, tab, tab],
        out_specs=[pl.BlockSpec((1, tm, wd), row) for wd in widths],
        out_shape=[jax.ShapeDtypeStruct((bsz, n, wd), F32 if k == 0 else BF16) for k, wd in enumerate(widths)],
        compiler_params=_cparams(("parallel", "parallel"), 48),
        name="odd_in_q" if need_q else "odd_in",
    )(x, g.reshape(1, d), shift, scale, w, qg, kvg, wuq, wuk, wuv, *tabs)


def _hy_fwd_kernel(wc_ref, ws_ref, u_ref, cw_ref, cb_ref, ka_ref, kb_ref, ya_ref, yb_ref, ubf_ref,
                   *, conv, tf):
    f = pl.program_id(1)

    @pl.when(f == 0)
    def _():
        u = u_ref[0].astype(F32)
        if conv:
            u = _short_conv(u, cw_ref[...], cb_ref[...])
        ubf_ref[...] = u.astype(BF16)

    ub = ubf_ref[...]
    a = _dot(wc_ref[...], ub)
    b = _dot(ws_ref[...], ub)
    ka, kb = ka_ref[...], kb_ref[...]
    first = (lax.broadcasted_iota(I32, a.shape, 0) + f * tf) == 0
    ya_ref[0] = jnp.where(first, a * ka, a * ka - b * kb).astype(BF16)
    yb_ref[0] = jnp.where(first, b * kb, a * kb + b * ka).astype(BF16)


def _hy_fwd(wc, ws, u, col, cw, cb, ka, kb, conv):
    bsz, n, _ = u.shape
    c = A_WIDTH
    tf = min(512, n)
    return pl.pallas_call(
        functools.partial(_hy_fwd_kernel, conv=conv, tf=tf),
        grid=(bsz, n // tf),
        in_specs=[pl.BlockSpec((tf, n), lambda b, f: (f, 0)), pl.BlockSpec((tf, n), lambda b, f: (f, 0)),
                  pl.BlockSpec((1, n, c), lambda b, f: (b, 0, col)),
                  pl.BlockSpec((3, c), lambda b, f: (0, 0)), pl.BlockSpec((1, c), lambda b, f: (0, 0)),
                  pl.BlockSpec((tf, c), lambda b, f: (f, 0)), pl.BlockSpec((tf, c), lambda b, f: (f, 0))],
        out_specs=[pl.BlockSpec((1, tf, c), lambda b, f: (b, f, 0))] * 2,
        out_shape=[jax.ShapeDtypeStruct((bsz, n, c), BF16)] * 2,
        scratch_shapes=[pltpu.VMEM((n, c), BF16)],
        compiler_params=_cparams(("parallel", "arbitrary"), 48),
        name="hyena_fwd",
    )(wc, ws, u, cw, cb, ka, kb)


def _hy_inv_kernel(ci_ref, si_ref, ya_ref, yb_ref, a_ref, g_ref, cwa_ref, cba_ref, cwg_ref, cbg_ref,
                   bias_ref, o_ref, a_sc, g_sc, *, conv_a, tt):
    t = pl.program_id(1)

    @pl.when(t == 0)
    def _():
        a = a_ref[0].astype(F32)
        if conv_a:
            a = _short_conv(a, cwa_ref[...], cba_ref[...])
        a_sc[...] = a
        g_sc[...] = _short_conv(g_ref[0].astype(F32), cwg_ref[...], cbg_ref[...])

    y = _dot(ci_ref[...], ya_ref[0]) + _dot(si_ref[...], yb_ref[0])
    rows = pl.ds(pl.multiple_of(t * tt, tt), tt)
    y = y + a_sc[rows, :] * bias_ref[...]
    o_ref[0] = (g_sc[rows, :] * y).astype(BF16)


def _hy_inv(ci, si, ya, yb, a, a_col, g, g_col, cwa, cba, cwg, cbg, bias, conv_a):
    bsz, n, c = ya.shape
    tt = min(512, n)
    whole = lambda b, t: (b, 0, 0)
    fixed = lambda b, t: (0, 0)
    return pl.pallas_call(
        functools.partial(_hy_inv_kernel, conv_a=conv_a, tt=tt),
        grid=(bsz, n // tt),
        in_specs=[pl.BlockSpec((tt, n), lambda b, t: (t, 0)), pl.BlockSpec((tt, n), lambda b, t: (t, 0)),
                  pl.BlockSpec((1, n, c), whole), pl.BlockSpec((1, n, c), whole),
                  pl.BlockSpec((1, n, c), lambda b, t: (b, 0, a_col)),
                  pl.BlockSpec((1, n, c), lambda b, t: (b, 0, g_col)),
                  pl.BlockSpec((3, c), fixed), pl.BlockSpec((1, c), fixed),
                  pl.BlockSpec((3, c), fixed), pl.BlockSpec((1, c), fixed), pl.BlockSpec((1, c), fixed)],
        out_specs=pl.BlockSpec((1, tt, c), lambda b, t: (b, t, 0)),
        out_shape=jax.ShapeDtypeStruct((bsz, n, c), BF16),
        scratch_shapes=[pltpu.VMEM((n, c), F32), pltpu.VMEM((n, c), F32)],
        compiler_params=_cparams(("parallel", "arbitrary"), 56),
        name="hyena_inv",
    )(ci, si, ya, yb, a, g, cwa, cba, cwg, cbg, bias)


def _dft_mats(n):
    big = 2 * n
    f = jnp.arange(n, dtype=I32)[:, None]
    s = jnp.arange(n, dtype=I32)[None, :]
    step = 64 if n % 64 == 0 else 1
    angle = lambda fr: ((fr[:, None] * s) % big).astype(F32) * (2.0 * math.pi / big)
    a_hi = angle(jnp.arange(n // step, dtype=I32) * step)
    a_lo = angle(jnp.arange(step, dtype=I32))
    c1, s1, c2, s2 = jnp.cos(a_hi)[:, None], jnp.sin(a_hi)[:, None], jnp.cos(a_lo)[None], jnp.sin(a_lo)[None]
    cos = (c1 * c2 - s1 * s2).reshape(n, n)
    sin = (s1 * c2 + c1 * s2).reshape(n, n)
    alt = jnp.where(jnp.arange(n) % 2 == 0, 1.0, -1.0).astype(F32)
    wc = cos
    ws = jnp.where(f == 0, alt[None, :], -sin)
    ci = jnp.where(s == 0, 1.0 / big, (2.0 / big) * cos)
    si = jnp.where(s == 0, alt[:, None] / big, -(2.0 / big) * sin)
    return (wc, ws), (wc.astype(BF16), ws.astype(BF16), ci.astype(BF16), si.astype(BF16))


def _spectrum_kernel(wc_ref, ws_ref, sh_ref, sl_ref, dh_ref, dl_ref, nyq_ref, nrm_ref, ka_ref, kb_ref, *, tf):
    ch, cl = _split_bf16(wc_ref[...])
    sh, sl = _split_bf16(ws_ref[...])
    a = _dot3(ch, cl, sh_ref[...], sl_ref[...])
    b = _dot3(sh, sl, dh_ref[...], dl_ref[...])
    first = (lax.broadcasted_iota(I32, a.shape, 0) + pl.program_id(1) * tf) == 0
    ka_ref[...] = a * nrm_ref[...]
    kb_ref[...] = jnp.where(first, nyq_ref[...], b) * nrm_ref[...]


def _filter_spectrum(tables, fwd, bwd):
    wc, ws = tables
    n, c = fwd.shape
    bwd = bwd.at[0].set(0.0)
    nrm = lax.rsqrt(jnp.sum(fwd * fwd, axis=0) + jnp.sum(bwd * bwd, axis=0) + EPS).reshape(1, c)
    s, d = fwd + bwd, fwd - bwd
    alt = jnp.where(jnp.arange(n) % 2 == 0, 1.0, -1.0).astype(F32)
    nyq = jnp.sum(alt[:, None] * s, axis=0).reshape(1, c)
    sh, sl = _split_bf16(s)
    dh, dl = _split_bf16(d)
    tf = min(256, n)
    tc = A_WIDTH
    tab = pl.BlockSpec((tf, n), lambda j, f: (f, 0))
    tap = pl.BlockSpec((n, tc), lambda j, f: (0, j))
    vec = pl.BlockSpec((1, tc), lambda j, f: (0, j))
    out = pl.BlockSpec((tf, tc), lambda j, f: (f, j))
    return pl.pallas_call(
        functools.partial(_spectrum_kernel, tf=tf),
        grid=(c // tc, n // tf),
        in_specs=[tab, tab, tap, tap, tap, tap, vec, vec],
        out_specs=[out, out],
        out_shape=[jax.ShapeDtypeStruct((n, c), F32)] * 2,
        compiler_params=_cparams(("parallel", "parallel"), 48),
        name="hyena_filter_spectrum",
    )(wc, ws, sh, sl, dh, dl, nyq, nrm)


def _hyena_filter_spectrum(n, tables, w1, b1, w2, b2, w3, freq):
    t = jnp.arange(n, dtype=F32)
    t_norm = t / max(n - 1, 1)
    bands = jnp.linspace(1e-4, HY_BANDS - 1, HY_BANDS, dtype=F32)
    ang = (2.0 * math.pi / n) * t[:, None] * bands
    z = jnp.concatenate([t_norm[:, None], jnp.cos(ang), jnp.sin(ang)], axis=-1)
    hdn = jnp.sin(freq * (jnp.dot(z, w1, precision=HIGHEST) + b1))
    hdn = jnp.sin(freq * (jnp.dot(hdn, w2, precision=HIGHEST) + b2))
    taps = jnp.dot(hdn, w3, precision=HIGHEST).reshape(n, 2, HY_ORDER, A_WIDTH)
    deltas = jnp.linspace(HY_MIN_DECAY, HY_MAX_DECAY, A_WIDTH, dtype=F32)
    taps = taps * jnp.exp(-t_norm[:, None] * deltas)[:, None, None, :]
    width = HY_ORDER * A_WIDTH
    ka, kb = _filter_spectrum(tables, taps[:, 0].reshape(n, width), taps[:, 1].reshape(n, width))
    return ka.reshape(n, HY_ORDER, A_WIDTH), kb.reshape(n, HY_ORDER, A_WIDTH)


def _hyena(hy, cw, cb, filt, fbias):
    n = hy.shape[1]
    c = A_WIDTH
    tables, (wc, ws, ci, si) = _dft_mats(n)
    ka, kb = _hyena_filter_spectrum(n, tables, *filt)
    cws = [cw[:, i * c:(i + 1) * c] for i in range(3)]
    cbs = [cb[i * c:(i + 1) * c].reshape(1, c) for i in range(3)]
    ya, yb = _hy_fwd(wc, ws, hy, 2, cws[2], cbs[2], ka[:, 0], kb[:, 0], True)
    z = _hy_inv(ci, si, ya, yb, hy, 2, hy, 0, cws[2], cbs[2], cws[0], cbs[0], fbias[0:1], True)
    ya, yb = _hy_fwd(wc, ws, z, 0, cws[2], cbs[2], ka[:, 1], kb[:, 1], False)
    return _hy_inv(ci, si, ya, yb, z, 0, hy, 1, cws[2], cbs[2], cws[1], cbs[1], fbias[1:2], False)


def _attn_kernel(*refs, nseg, hps, kps):
    q_ref = refs[0]
    k_refs = refs[1:1 + nseg]
    v_refs = refs[1 + nseg:1 + 2 * nseg]
    o_ref = refs[1 + 2 * nseg]
    low = lax.broadcasted_iota(I32, (q_ref.shape[1], LANES), 1) < HEAD_DIM
    for pair in range(hps // 2):
        accs = []
        for half in range(2):
            i = 2 * pair + half
            kv = (i * kps) // hps
            q = q_ref[0, :, LANES * i: LANES * (i + 1)]
            ss = [_dot_nt(q, k[0, :, LANES * kv: LANES * (kv + 1)]) for k in k_refs]
            m = ss[0].max(axis=-1, keepdims=True)
            for s in ss[1:]:
                m = jnp.maximum(m, s.max(axis=-1, keepdims=True))
            acc = None
            for s, v in zip(ss, v_refs):
                p = jnp.exp2((s - m).astype(BF16))
                o = _dot(p, v[0, :, LANES * kv: LANES * (kv + 1)])
                acc = o if acc is None else acc + o
            accs.append(acc)
        num = jnp.where(low, accs[0], pltpu.roll(accs[1], HEAD_DIM, 1))
        den = jnp.where(low, pltpu.roll(accs[0], HEAD_DIM, 1), accs[1])
        o_ref[0, :, LANES * pair: LANES * (pair + 1)] = (num / den).astype(BF16)


def _attention(q, ks, vs, hps, kps):
    bsz, lq, qw = q.shape
    nh = qw // LANES
    tq = min(512, lq)
    nseg = len(ks)
    kv_spec = lambda a: pl.BlockSpec((1, a.shape[1], kps * LANES), lambda b, g, t: (b, 0, g))
    return pl.pallas_call(
        functools.partial(_attn_kernel, nseg=nseg, hps=hps, kps=kps),
        grid=(bsz, nh // hps, lq // tq),
        in_specs=[pl.BlockSpec((1, tq, hps * LANES), lambda b, g, t: (b, t, g))]
        + [kv_spec(a) for a in ks] + [kv_spec(a) for a in vs],
        out_specs=pl.BlockSpec((1, tq, hps * HEAD_DIM), lambda b, g, t: (b, t, g)),
        out_shape=jax.ShapeDtypeStruct((bsz, lq, nh * HEAD_DIM), BF16),
        compiler_params=_cparams(("parallel", "parallel", "parallel"), 56),
        name="attention",
    )(q, *ks, *vs)


def _onehot_bf16(cond):
    return jnp.where(cond, 1.0, 0.0).astype(BF16)


def _s5_kernel(uc_ref, ul_ref, w_ref, a_ref, of_ref, or_ref, y_ref, xs_sc, yi_sc, s_sc, x_sc, ys_sc,
               *, bb, ncc, ncl):
    lc, gh, gt = S5_CHUNK, S5_GROUP, S5_TILE_GROUPS
    nc = ncc + ncl
    cw = lc * gh
    hb = gh.bit_length() - 1
    lb = LANES.bit_length() - 1

    for b in range(bb):
        for j in range(lc):
            p, half = divmod(j, 2)
            lanes = slice(half * LANES, (half + 1) * LANES)
            xs_sc[p, b * nc: b * nc + ncc, lanes] = uc_ref[b, pl.ds(j, ncc, stride=lc), :].astype(BF16)
            xs_sc[p, b * nc + ncc: (b + 1) * nc, lanes] = ul_ref[b, pl.ds(j, ncl, stride=lc), :].astype(BF16)

    r_i = lax.broadcasted_iota(I32, (2 * LANES, cw), 0)
    c_i = lax.broadcasted_iota(I32, (2 * LANES, cw), 1)
    same_h = (r_i & (gh - 1)) == (c_i & (gh - 1))
    r_grp = (r_i & (LANES - 1)) >> hb
    c_off = (c_i >> hb) - (r_i >> lb)
    for pr in range(gt // 2):
        rs = []
        for side in range(2):
            q = 2 * pr + side
            u = None
            for p in range(lc // 2):
                sel = _onehot_bf16(same_h & (r_grp == q) & (c_off == 2 * p))
                d = _dot(xs_sc[p], sel)
                u = d if u is None else u + d
            r = _dot(u.astype(BF16), w_ref[q])
            yi_sc[q] = r[:, :cw]
            rs.append(r)
        for kind in range(4):
            lo = cw + kind * LANES
            s_sc[kind, pr] = rs[0][:, lo:lo + LANES] + rs[1][:, lo:lo + LANES]

    a = a_ref[0]
    npair = gt // 2

    def advance(st, c, d):
        rws = pl.ds(c, bb, stride=nc)
        new = []
        for t in range(npair):
            xr, xi = st[2 * t], st[2 * t + 1]
            ar = a[2 * d:2 * d + 1, t * LANES:(t + 1) * LANES]
            ai = a[2 * d + 1:2 * d + 2, t * LANES:(t + 1) * LANES]
            x_sc[2 * d, t, rws, :] = xr
            x_sc[2 * d + 1, t, rws, :] = xi
            new.append(ar * xr - ai * xi + s_sc[2 * d, t, rws, :])
            new.append(ar * xi + ai * xr + s_sc[2 * d + 1, t, rws, :])
        return tuple(new)

    zero = tuple(jnp.zeros((bb, LANES), F32) for _ in range(2 * npair))
    lax.fori_loop(0, nc, lambda c, st: advance(st, c, 0), zero)
    st = lax.fori_loop(0, ncc, lambda i, st: advance(st, ncc - 1 - i, 1), zero)
    lax.fori_loop(0, ncl, lambda i, st: advance(st, nc - 1 - i, 1), st)

    def latent(ref, *idx):
        return jnp.concatenate([ref[idx + (slice(b * nc + ncc, (b + 1) * nc), slice(None))]
                                for b in range(bb)], axis=0)

    for q in range(gt):
        y = latent(yi_sc, q)
        for d, o_ref in ((0, of_ref), (1, or_ref)):
            state = jnp.concatenate([latent(x_sc, 2 * d, q // 2), latent(x_sc, 2 * d + 1, q // 2)], axis=1)
            y = y + _dot(state.astype(BF16), o_ref[q])
        for kt in range(cw // LANES):
            ys_sc[kt, :, q * LANES:(q + 1) * LANES] = y[:, kt * LANES:(kt + 1) * LANES].astype(BF16)

    r2 = lax.broadcasted_iota(I32, (gt * LANES, LANES), 0)
    c2 = lax.broadcasted_iota(I32, (gt * LANES, LANES), 1)
    keep = ((r2 & (gh - 1)) == (c2 & (gh - 1))) & ((r2 >> lb) == (c2 >> hb))
    r_k = (r2 & (LANES - 1)) >> hb
    for k in range(0, lc, 2):
        kt, kk = divmod(k, gt)
        sel = jnp.concatenate([_onehot_bf16(keep & (r_k == kk)), _onehot_bf16(keep & (r_k == kk + 1))], axis=1)
        z = _dot(ys_sc[kt], sel)
        for b in range(bb):
            y_ref[b, pl.ds(k, ncl, stride=lc), :] = z[b * ncl:(b + 1) * ncl, :LANES]
            y_ref[b, pl.ds(k + 1, ncl, stride=lc), :] = z[b * ncl:(b + 1) * ncl, LANES:]


def _s5_scan(u_c, u_l, w, a_tab, of, orr):
    bsz, ncx, c = u_c.shape
    n = u_l.shape[1]
    lc, gt = S5_CHUNK, S5_TILE_GROUPS
    ncc, ncl = ncx // lc, n // lc
    bb = min(4, bsz)
    rows = bb * (ncc + ncl)
    cw = lc * S5_GROUP
    sw = (gt // 2) * LANES
    return pl.pallas_call(
        functools.partial(_s5_kernel, bb=bb, ncc=ncc, ncl=ncl),
        grid=(c // LANES, bsz // bb),
        in_specs=[pl.BlockSpec((bb, ncx, LANES), lambda t, i: (i, 0, t)),
                  pl.BlockSpec((bb, n, LANES), lambda t, i: (i, 0, t)),
                  pl.BlockSpec((gt,) + w.shape[1:], lambda t, i: (t, 0, 0)),
                  pl.BlockSpec((1,) + a_tab.shape[1:], lambda t, i: (t, 0, 0)),
                  pl.BlockSpec((gt,) + of.shape[1:], lambda t, i: (t, 0, 0)),
                  pl.BlockSpec((gt,) + orr.shape[1:], lambda t, i: (t, 0, 0))],
        out_specs=pl.BlockSpec((bb, n, LANES), lambda t, i: (i, 0, t)),
        out_shape=jax.ShapeDtypeStruct((bsz, n, c), F32),
        scratch_shapes=[pltpu.VMEM((lc // 2, rows, 2 * LANES), BF16), pltpu.VMEM((gt, rows, cw), F32),
                        pltpu.VMEM((4, gt // 2, rows, LANES), F32), pltpu.VMEM((4, gt // 2, rows, LANES), F32),
                        pltpu.VMEM((cw // LANES, bb * ncl, gt * LANES), BF16)],
        compiler_params=_cparams(("parallel", "parallel"), 56),
        name="s5_scan",
    )(u_c, u_l, w, a_tab, of, orr)


def _s5_matrices(a_re, a_im, log_dt, b_re, b_im, c_re, c_im, d_skip):
    lc, h, p, g = S5_CHUNK, S5_GROUP, S5_STATE, S5_GROUPS
    dt = jnp.exp(log_dt)[..., None]
    lam_re, lam_im = a_re * dt, a_im * dt
    mag = jnp.exp(lam_re)
    er, ei = mag * jnp.cos(lam_im) - 1.0, mag * jnp.sin(lam_im)
    den = a_re * a_re + a_im * a_im
    co_re = (er * a_re + ei * a_im) / den
    co_im = (ei * a_re - er * a_im) / den
    bb_re = co_re[..., None] * b_re - co_im[..., None] * b_im
    bb_im = co_re[..., None] * b_im + co_im[..., None] * b_re
    m = jnp.arange(lc + 1, dtype=F32)[:, None, None, None]
    pw_mag = jnp.exp(m * lam_re[None])
    pw_re, pw_im = pw_mag * jnp.cos(m * lam_im[None]), pw_mag * jnp.sin(m * lam_im[None])
    ab_re = pw_re[..., None] * bb_re[None] - pw_im[..., None] * bb_im[None]
    ab_im = pw_re[..., None] * bb_im[None] + pw_im[..., None] * bb_re[None]
    kk = (jnp.einsum("dgop,mdgpi->mdgoi", c_re, ab_re[:lc], precision=HIGHEST)
          - jnp.einsum("dgop,mdgpi->mdgoi", c_im, ab_im[:lc], precision=HIGHEST))
    cw = lc * h

    def shifted_rows(x):
        period = 2 * cw + h
        z = jnp.tile(x, (1, 1, lc))[:, :, :lc * (period - h)]
        return jnp.swapaxes(z.reshape(g, h, lc, period - h)[..., cw:], 1, 2)

    resp = jnp.transpose(kk, (1, 2, 4, 0, 3))
    zeros = lambda w: jnp.zeros((g, h, w), F32)
    fwd = resp[0].reshape(g, h, cw)
    rev = resp[1][:, :, ::-1].reshape(g, h, cw)
    tf = shifted_rows(jnp.concatenate([zeros(cw), fwd, zeros(h)], axis=-1))
    tr = shifted_rows(jnp.concatenate([zeros(h), rev, zeros(cw)], axis=-1))
    lane = jnp.arange(cw, dtype=I32)[None, None, None, :]
    diag = (jnp.arange(lc, dtype=I32)[None, :, None, None] * h + jnp.arange(h, dtype=I32)[None, None, :, None])
    dsk = d_skip.reshape(g, 1, h, 1)
    tw = (tf + tr + jnp.where(lane == diag, dsk, 0.0)).reshape(g, cw, cw)
    idx_f = slice(lc - 1, None, -1)
    idx_r = slice(0, lc)

    odd = (jnp.arange(g) % 2 == 1)

    def place(t, axis):
        z = jnp.zeros_like(t)
        shape = [g] + [1] * (t.ndim - 1)
        return jnp.where(odd.reshape(shape), jnp.concatenate([z, t], axis=axis),
                         jnp.concatenate([t, z], axis=axis))

    def to_state(idx, d):
        re = jnp.transpose(ab_re[idx, d], (1, 0, 3, 2)).reshape(g, lc * h, p)
        im = jnp.transpose(ab_im[idx, d], (1, 0, 3, 2)).reshape(g, lc * h, p)
        return [place(re, 2), place(im, 2)]

    w = jnp.concatenate([tw] + to_state(idx_f, 0) + to_state(idx_r, 1), axis=-1)

    def from_state(idx, d):
        pr, pi = pw_re[idx, d], pw_im[idx, d]
        cr, cim = c_re[d], c_im[d]
        mr = cr[None] * pr[:, :, None, :] - cim[None] * pi[:, :, None, :]
        mi = cr[None] * pi[:, :, None, :] + cim[None] * pr[:, :, None, :]
        rows = lambda m: place(jnp.transpose(m, (1, 3, 0, 2)).reshape(g, p, lc * h), 1)
        return jnp.concatenate([rows(mr), rows(-mi)], axis=1)

    of = from_state(slice(1, lc + 1), 0)
    orr = from_state(slice(lc, 0, -1), 1)
    tiles = g // S5_TILE_GROUPS
    a_tab = jnp.stack([pw_re[lc, 0].reshape(tiles, -1), pw_im[lc, 0].reshape(tiles, -1),
                       pw_re[lc, 1].reshape(tiles, -1), pw_im[lc, 1].reshape(tiles, -1)], axis=1)
    return w.astype(BF16), a_tab.astype(F32), of.astype(BF16), orr.astype(BF16)


def _glu_kernel(y_ref, w_ref, b_ref, o_ref):
    y = y_ref[0]
    inner = 0.7978845608028654 * (y + 0.044715 * (y * y * y))
    gl = 0.5 * y * (1.0 + jnp.tanh(inner))
    o_ref[0] = (gl * _sigmoid(_dot(gl.astype(BF16), w_ref[...]) + b_ref[...])).astype(BF16)


def _glu(y, w, b):
    bsz, n, c = y.shape
    tm = min(512, n)
    return pl.pallas_call(
        _glu_kernel,
        grid=(bsz, n // tm),
        in_specs=[pl.BlockSpec((1, tm, c), lambda i, t: (i, t, 0)),
                  pl.BlockSpec((c, c), lambda i, t: (0, 0)), pl.BlockSpec((1, c), lambda i, t: (0, 0))],
        out_specs=pl.BlockSpec((1, tm, c), lambda i, t: (i, t, 0)),
        out_shape=jax.ShapeDtypeStruct((bsz, n, c), BF16),
        compiler_params=_cparams(("parallel", "parallel"), 32),
        name="s5_glu",
    )(y, w, b.reshape(1, c))


def _s5_mixer(u_c, u_l, mats, glu_w, glu_b):
    return _glu(_s5_scan(u_c, u_l, *mats), glu_w, glu_b)


def _route(lg):
    lane = lax.broadcasted_iota(I32, lg.shape, 1).astype(F32)
    neg = jnp.float32(-1e30)
    far = jnp.float32(LANES)
    is_g = lane < N_GROUPS
    lgm = jnp.where(is_g, lg, neg)
    m = lgm.max(axis=-1, keepdims=True)
    gidx = jnp.where(lgm == m, lane, far).min(axis=-1, keepdims=True)
    gp = 1.0 / jnp.where(is_g, jnp.exp(lgm - m), 0.0).sum(axis=-1, keepdims=True)
    lo = N_GROUPS + EXPERTS_PER_GROUP * gidx
    is_e = (lane >= lo) & (lane < lo + EXPERTS_PER_GROUP)
    lem = jnp.where(is_e, lg, neg)
    l0 = lem.max(axis=-1, keepdims=True)
    i0 = jnp.where((lem == l0) & is_e, lane, far).min(axis=-1, keepdims=True)
    lem1 = jnp.where(lane == i0, neg, lem)
    l1 = lem1.max(axis=-1, keepdims=True)
    i1 = jnp.where((lem1 == l1) & is_e & (lane != i0), lane, far).min(axis=-1, keepdims=True)
    e1 = jnp.exp(l1 - l0)
    w0 = 1.0 / (1.0 + e1)
    w1 = e1 * w0
    out = jnp.where(lane == 0, i0 - N_GROUPS, 0.0)
    out = jnp.where(lane == 1, i1 - N_GROUPS, out)
    out = jnp.where(lane == 2, gp * w0, out)
    return jnp.where(lane == 3, gp * w1, out)


def _proj_res_kernel(a_ref, b_ref, wa_ref, wb_ref, x_ref, g1_ref, ng_ref, sh_ref, sc_ref, rh_ref, rl_ref,
                     xo_ref, h_ref, route_ref):
    for rows in _row_chunks(x_ref.shape[1]):
        o = _dot(a_ref[0, rows, :], wa_ref[...]) + _dot(b_ref[0, rows, :], wb_ref[...])
        x = x_ref[0, rows, :] + g1_ref[0] * o
        xo_ref[0, rows, :] = x
        h = _modnorm(x, ng_ref[...], sh_ref[0], sc_ref[0])
        _store_token_tiles(h_ref, rows.start, h)
        hh, hl = _split_bf16(h)
        route_ref[0, rows, :] = _route(_dot3(hh, hl, rh_ref[...], rl_ref[...]))


def _proj_res(a, b, wa, wb, x, g1, ng, shift, scale, rh, rl):
    bsz, n, d = x.shape
    tm = min(512, n)
    row = lambda i, t: (i, t, 0)
    per_b = lambda i, t: (i, 0, 0)
    fixed = lambda i, t: (0, 0)
    return pl.pallas_call(
        _proj_res_kernel,
        grid=(bsz, n // tm),
        in_specs=[pl.BlockSpec((1, tm, a.shape[2]), row), pl.BlockSpec((1, tm, b.shape[2]), row),
                  pl.BlockSpec(wa.shape, fixed), pl.BlockSpec(wb.shape, fixed),
                  pl.BlockSpec((1, tm, d), row), pl.BlockSpec((1, 1, d), per_b), pl.BlockSpec((1, d), fixed),
                  pl.BlockSpec((1, 1, d), per_b), pl.BlockSpec((1, 1, d), per_b),
                  pl.BlockSpec((d, LANES), fixed), pl.BlockSpec((d, LANES), fixed)],
        out_specs=[pl.BlockSpec((1, tm, d), row),
                   pl.BlockSpec((tm * SUBLANES, LANES), lambda i, t: (i * (n // tm) + t, 0)),
                   pl.BlockSpec((1, tm, LANES), row)],
        out_shape=[jax.ShapeDtypeStruct((bsz, n, d), F32),
                   jax.ShapeDtypeStruct((bsz * n * SUBLANES, LANES), F32),
                   jax.ShapeDtypeStruct((bsz, n, LANES), F32)],
        compiler_params=_cparams(("parallel", "parallel"), 40),
        name="proj_res",
    )(a, b, wa, wb, x, g1, ng.reshape(1, d), shift, scale, rh, rl)


def _token_copy(src, src_tok, dst, dst_tok, sem):
    rows = lambda t: pl.ds(pl.multiple_of(t * SUBLANES, SUBLANES), SUBLANES)
    return pltpu.make_async_copy(src.at[rows(src_tok)], dst.at[rows(dst_tok)], sem)


def _tile_copy(src, src_tok, dst, dst_tok, sem):
    n = MOE_TILE * SUBLANES
    rows = lambda t: pl.ds(pl.multiple_of(t * SUBLANES, SUBLANES), n)
    return pltpu.make_async_copy(src.at[rows(src_tok)], dst.at[rows(dst_tok)], sem)


def _experts_kernel(te_ref, ts_ref, tr_ref, tp_ref, win_ref, winn_ref, h_ref, wg_ref, wu_ref, wd_ref, ys_ref,
                    xbuf, obuf, gsem, ssem, *, n_assign):
    del te_ref
    i = pl.program_id(0)
    nt = pl.num_programs(0)
    slot = i % 2
    other = 1 - slot
    nxt = jnp.minimum(i + 1, nt - 1)
    within = lambda start: start & (MOE_TILE - 1)

    def start_gather(w_ref, off, s):
        def body(j, c):
            for k in range(2):
                r = 2 * j + k
                tok = lax.shift_right_logical(w_ref[0, 0, off + r], 1)
                _token_copy(h_ref, tok, xbuf, s * MOE_TILE + r, gsem.at[s]).start(priority=k)
            return c
        lax.fori_loop(0, MOE_TILE // 2, body, 0, unroll=4)

    @pl.when(i == 0)
    def _():
        start_gather(win_ref, within(ts_ref[0]), 0)

    @pl.when(i + 1 < nt)
    def _():
        start_gather(winn_ref, within(ts_ref[nxt]), other)

    base = slot * MOE_TILE
    _tile_copy(h_ref, 0, xbuf, base, gsem.at[slot]).wait()

    @pl.when(i >= 2)
    def _():
        _tile_copy(obuf, base, ys_ref, 0, ssem.at[slot]).wait()

    x = _load_token_tiles(xbuf, base, MOE_TILE).astype(BF16)
    gt = _dot(x, wg_ref[0, 0].astype(BF16))
    hid = (gt * _sigmoid(gt)) * _dot(x, wu_ref[0, 0].astype(BF16))
    _store_token_tiles(obuf, base, _dot(hid.astype(BF16), wd_ref[0, 0].astype(BF16)))

    off, real = within(ts_ref[i]), tr_ref[i]
    spare = n_assign + tp_ref[i] - real

    def scatter(j, c):
        for k in range(2):
            r = 2 * j + k
            dst = jnp.where(r < real, win_ref[0, 0, off + r], spare + r)
            _token_copy(obuf, base + r, ys_ref, dst, ssem.at[slot]).start(priority=k)
        return c
    lax.fori_loop(0, MOE_TILE // 2, scatter, 0, unroll=4)

    @pl.when(i == nt - 1)
    def _():
        _tile_copy(obuf, base, ys_ref, 0, ssem.at[slot]).wait()
        _tile_copy(obuf, other * MOE_TILE, ys_ref, 0, ssem.at[other]).wait()


def _experts(plan, h, n_assign, layer, wg, wu, wd):
    windows, tile_expert, tile_start, tile_real, tile_padbase = plan
    nt = tile_expert.shape[0]
    d = D_MODEL
    wmap = lambda i, te, ts, tr, tp: (layer, te[i], 0, 0)
    log_tile = MOE_TILE.bit_length() - 1
    smem = lambda f: pl.BlockSpec((1, 1, 2 * MOE_TILE), f, memory_space=pltpu.SMEM)
    tile_rows = MOE_TILE * SUBLANES
    return pl.pallas_call(
        functools.partial(_experts_kernel, n_assign=n_assign),
        grid_spec=pltpu.PrefetchScalarGridSpec(
            num_scalar_prefetch=4, grid=(nt,),
            in_specs=[smem(lambda i, te, ts, tr, tp: (lax.shift_right_logical(ts[i], log_tile), 0, 0)),
                      smem(lambda i, te, ts, tr, tp: (
                          lax.shift_right_logical(ts[jnp.minimum(i + 1, nt - 1)], log_tile), 0, 0)),
                      pl.BlockSpec(memory_space=pl.ANY),
                      pl.BlockSpec((1, 1, d, D_EXPERT), wmap), pl.BlockSpec((1, 1, d, D_EXPERT), wmap),
                      pl.BlockSpec((1, 1, D_EXPERT, d), wmap)],
            out_specs=pl.BlockSpec(memory_space=pl.ANY),
            scratch_shapes=[pltpu.VMEM((2 * tile_rows, LANES), F32), pltpu.VMEM((2 * tile_rows, LANES), F32),
                            pltpu.SemaphoreType.DMA((2,)), pltpu.SemaphoreType.DMA((2,))]),
        out_shape=jax.ShapeDtypeStruct((nt * tile_rows, LANES), F32),
        compiler_params=pltpu.CompilerParams(dimension_semantics=("arbitrary",), vmem_limit_bytes=40 * MIB,
                                             has_side_effects=True),
        name="moe_experts",
    )(tile_expert, tile_start, tile_real, tile_padbase, windows[:, None, :], windows[:, None, :], h, wg, wu, wd)


def _combine_kernel(route_ref, x_ref, g2_ref, fg_ref, ys_ref, o_ref, *, tc, final):
    rt = route_ref[0]
    g0, g1 = rt[:, 2:3], rt[:, 3:4]
    cols = [g0 * ys_ref[pl.ds(s, tc, stride=2 * SUBLANES), :]
            + g1 * ys_ref[pl.ds(SUBLANES + s, tc, stride=2 * SUBLANES), :] for s in range(SUBLANES)]
    x = x_ref[0] + g2_ref[0] * jnp.concatenate(cols, axis=-1)
    if final:
        x = _rms(x, fg_ref[...], x.shape[-1])
    o_ref[0] = x


def _combine(route, x, g2, fg, ys, tok_off, final):
    bsz, n, d = x.shape
    tc = min(256, n)
    nt = n // tc
    row = lambda i, t: (i, t, 0)
    blk = tok_off // tc
    return pl.pallas_call(
        functools.partial(_combine_kernel, tc=tc, final=final),
        grid=(bsz, nt),
        in_specs=[pl.BlockSpec((1, tc, LANES), row), pl.BlockSpec((1, tc, d), row),
                  pl.BlockSpec((1, 1, d), lambda i, t: (i, 0, 0)), pl.BlockSpec((1, d), lambda i, t: (0, 0)),
                  pl.BlockSpec((2 * tc * SUBLANES, LANES), lambda i, t: (blk + i * nt + t, 0))],
        out_specs=pl.BlockSpec((1, tc, d), row),
        out_shape=jax.ShapeDtypeStruct((bsz, n, d), F32),
        compiler_params=_cparams(("parallel", "parallel"), 32),
        name="moe_combine",
    )(route, x, g2, fg.reshape(1, d), ys)


def _moe_plan(eids):
    a = eids.shape[0]
    order = jnp.argsort(eids, stable=True).astype(I32)
    experts = jnp.arange(N_EXPERTS, dtype=I32)
    counts = (eids[:, None] == experts[None, :]).astype(I32).sum(axis=0)
    tiles_e = (counts + MOE_TILE - 1) // MOE_TILE
    tile_end = jnp.cumsum(tiles_e)
    first = jnp.cumsum(counts) - counts
    nt = a // MOE_TILE + N_EXPERTS
    tidx = jnp.arange(nt, dtype=I32)
    tile_expert = jnp.minimum((tidx[:, None] >= tile_end[None, :]).astype(I32).sum(axis=1), N_EXPERTS - 1)
    pick = lambda table: ((tile_expert[:, None] == experts[None, :]).astype(I32) * table[None, :]).sum(axis=1)
    tile_pos = (tidx - pick(tile_end - tiles_e)) * MOE_TILE
    used = tidx < tile_end[-1]
    tile_real = jnp.where(used, jnp.clip(pick(counts) - tile_pos, 0, MOE_TILE), 0)
    tile_start = jnp.where(used, pick(first) + tile_pos, 0)
    pads = MOE_TILE - tile_real
    tile_padbase = jnp.cumsum(pads) - pads
    rows = jnp.concatenate([order, jnp.zeros((MOE_TILE,), I32)]).reshape(-1, MOE_TILE)
    windows = jnp.concatenate([rows[:-1], rows[1:]], axis=1)
    return windows, tile_expert, tile_start, tile_real, tile_padbase


def _moe(streams, layer, wg, wu, wd, final_g, final):
    eids = jnp.concatenate([s[2][..., :2].reshape(-1) for s in streams]).astype(I32)
    h = streams[0][1] if len(streams) == 1 else jnp.concatenate([s[1] for s in streams], axis=0)
    ys = _experts(_moe_plan(eids), h, eids.shape[0], layer, wg, wu, wd)
    outs, off = [], 0
    for x, _, route, g2 in streams:
        outs.append(_combine(route, x, g2, final_g, ys, off, final))
        off += x.shape[0] * x.shape[1]
    return outs


def _rope_tables(n, half, offset, identity):
    cs = jnp.ones((n, LANES), F32)
    z = jnp.zeros((n, LANES), F32)
    if identity:
        return cs, z, z
    rows = (jnp.arange(n, dtype=I32) // GRID_W).astype(F32)
    cols = (jnp.arange(n, dtype=I32) % GRID_W).astype(F32)
    freqs = ROPE_THETA ** (-jnp.arange(half, dtype=F32) / half)
    ar, ac = rows[:, None] * freqs, cols[:, None] * freqs
    zero = jnp.zeros((n, half), F32)
    cos4 = jnp.concatenate([jnp.cos(ar), jnp.cos(ar), jnp.cos(ac), jnp.cos(ac)], axis=1)
    s1 = jnp.concatenate([-jnp.sin(ar), zero, -jnp.sin(ac), zero], axis=1)
    s2 = jnp.concatenate([zero, jnp.sin(ar), zero, jnp.sin(ac)], axis=1)
    place = lambda base, t: lax.dynamic_update_slice(base, t, (0, offset))
    return place(cs, cos4), place(z, s1), place(z, s2)


def _pad_heads(w, nheads, width, left=0):
    k = w.shape[0]
    w = w.reshape(k, nheads, width)
    w = jnp.pad(w, ((0, 0), (0, 0), (left, LANES - width - left)))
    return w.reshape(k, nheads * LANES)


def _lane_pad(v, left=0):
    return jnp.pad(v, (left, LANES - v.shape[0] - left)).reshape(1, LANES)


def kernel(x, c, ctx, c_ctx, w_mod, b_mod, norm_g, e_w_in, e_hy_conv_w, e_hy_conv_b, e_hy_w1, e_hy_b1, e_hy_w2, e_hy_b2, e_hy_w3, e_hy_freq, e_hy_fbias, e_qk_g, e_w_out, o_w_in, o_q_norm_g, o_w_uq, o_kv_norm_g, o_w_ukv, o_s5_a_re, o_s5_a_im, o_s5_log_dt, o_s5_b_re, o_s5_b_im, o_s5_c_re, o_s5_c_im, o_s5_d, o_glu_w, o_glu_b, o_w_out, moe_w_rg, moe_w_re, moe_w_gate, moe_w_up, moe_w_down, final_g):
    bsz, n, d = x.shape
    depth = w_mod.shape[0]
    xc = ctx
    ncx = ctx.shape[1]
    cc = jnp.zeros((bsz + 8, d), F32).at[:bsz].set(c).at[bsz].set(c_ctx)

    for i in range(depth):
        j = i // 2
        last = i == depth - 1
        mods = _modulation(cc, w_mod, b_mod, i)
        lat = [m.reshape(bsz, 1, d) for m in jnp.split(mods[:bsz], 6, axis=-1)]
        cxm = [jnp.broadcast_to(m.reshape(1, 1, d), (bsz, 1, d)) for m in jnp.split(mods[bsz], 6, axis=-1)]

        if i % 2 == 0:
            wq = _pad_heads(e_w_in[j][:, 3 * A_WIDTH:3 * A_WIDTH + B_Q_HEADS * HEAD_DIM], B_Q_HEADS, HEAD_DIM)
            k0 = 3 * A_WIDTH + B_Q_HEADS * HEAD_DIM
            wk = _pad_heads(e_w_in[j][:, k0:k0 + B_KV_HEADS * HEAD_DIM], B_KV_HEADS, HEAD_DIM)
            wv = _pad_heads(e_w_in[j][:, k0 + B_KV_HEADS * HEAD_DIM:], B_KV_HEADS, HEAD_DIM)
            w_in = jnp.concatenate([e_w_in[j][:, :3 * A_WIDTH], wq, wk, wv], axis=1).astype(BF16)
            qg, kg = _lane_pad(e_qk_g[j, 0]), _lane_pad(e_qk_g[j, 1])
            filt = (e_hy_w1[j], e_hy_b1[j], e_hy_w2[j], e_hy_b2[j], e_hy_w3[j], e_hy_freq[j])
            qscale = HEAD_DIM ** -0.5 * LOG2E
            hy_l, q_l, k_l, v_l = _even_in(x, norm_g[i, 0], lat[0], lat[1], w_in, qg, kg,
                                           _rope_tables(n, HEAD_DIM // 4, 0, False), qscale)
            hy_c, q_c, k_c, v_c = _even_in(xc, norm_g[i, 0], cxm[0], cxm[1], w_in, qg, kg,
                                           _rope_tables(ncx, HEAD_DIM // 4, 0, True), qscale)
            rep = B_Q_HEADS // B_KV_HEADS
            att_l = _attention(q_l, [k_c, k_l], [v_c, v_l], rep, 1)
            mix_l = (_hyena(hy_l, e_hy_conv_w[j], e_hy_conv_b[j], filt, e_hy_fbias[j]), att_l)
            if not last:
                att_c = _attention(q_c, [k_c], [v_c], rep, 1)
                mix_c = (_hyena(hy_c, e_hy_conv_w[j], e_hy_conv_b[j], filt, e_hy_fbias[j]), att_c)
            w_out = e_w_out[j]
            split = A_WIDTH
        else:
            w1 = o_w_in[j]
            c0, c1, c2 = MLA_Q_RANK, MLA_Q_RANK + MLA_KV_RANK, MLA_Q_RANK + MLA_KV_RANK + MLA_ROPE
            w_kr = jnp.pad(w1[:, c1:c2], ((0, 0), (MLA_NOPE, LANES - MLA_NOPE - MLA_ROPE)))
            w_in = jnp.concatenate([w1[:, :c1], w_kr, w1[:, c2:]], axis=1).astype(BF16)
            wuq = _pad_heads(o_w_uq[j], MLA_HEADS, MLA_NOPE + MLA_ROPE).astype(BF16)
            wkv = o_w_ukv[j].reshape(MLA_KV_RANK, MLA_HEADS, MLA_NOPE + MLA_V)
            wuk = _pad_heads(wkv[:, :, :MLA_NOPE].reshape(MLA_KV_RANK, -1), MLA_HEADS, MLA_NOPE).astype(BF16)
            wuv = _pad_heads(wkv[:, :, MLA_NOPE:].reshape(MLA_KV_RANK, -1), MLA_HEADS, MLA_V).astype(BF16)
            qg = o_q_norm_g[j].reshape(1, MLA_Q_RANK)
            kvg = o_kv_norm_g[j].reshape(1, MLA_KV_RANK)
            args = (w_in, qg, kvg, wuq, wuk, wuv)
            qscale = (MLA_NOPE + MLA_ROPE) ** -0.5 * LOG2E
            u_l, k_l, v_l, q_l = _odd_in(x, norm_g[i, 0], lat[0], lat[1], *args,
                                         _rope_tables(n, MLA_ROPE // 4, MLA_NOPE, False), True, qscale)
            outs_c = _odd_in(xc, norm_g[i, 0], cxm[0], cxm[1], *args,
                             _rope_tables(ncx, MLA_ROPE // 4, MLA_NOPE, True), not last, qscale)
            u_c, k_c, v_c = outs_c[:3]
            att_l = _attention(q_l, [k_c, k_l], [v_c, v_l], 2, 2)
            mats = _s5_matrices(o_s5_a_re[j], o_s5_a_im[j], o_s5_log_dt[j], o_s5_b_re[j], o_s5_b_im[j],
                                o_s5_c_re[j], o_s5_c_im[j], o_s5_d[j])
            glu_w = o_glu_w[j].astype(BF16)
            mix_l = (att_l, _s5_mixer(u_c, u_l, mats, glu_w, o_glu_b[j]))
            if not last:
                raise NotImplementedError("an odd layer that is not the last needs the context S5 readout")
            w_out = o_w_out[j]
            split = MLA_HEADS * MLA_V

        wa, wb = w_out[:split].astype(BF16), w_out[split:].astype(BF16)
        router = jnp.pad(jnp.concatenate([moe_w_rg[i], moe_w_re[i]], axis=1),
                         ((0, 0), (0, LANES - N_GROUPS - N_EXPERTS)))
        rh = router.astype(BF16)
        rl = (router - rh.astype(F32)).astype(BF16)

        x, h_l, route_l = _proj_res(*mix_l, wa, wb, x, lat[2], norm_g[i, 1], lat[3], lat[4], rh, rl)
        streams = [(x, h_l, route_l, lat[5])]
        if not last:
            xc, h_c, route_c = _proj_res(*mix_c, wa, wb, xc, cxm[2], norm_g[i, 1], cxm[3], cxm[4], rh, rl)
            streams.append((xc, h_c, route_c, cxm[5]))
        outs = _moe(streams, i, moe_w_gate, moe_w_up, moe_w_down, final_g, last)
        x = outs[0]
        if not last:
            xc = outs[1]
    return x
```

```python
import functools
import math

import jax
import jax.numpy as jnp
from jax import lax
from jax.experimental import pallas as pl
from jax.experimental.pallas import tpu as pltpu

F32 = jnp.float32
BF16 = jnp.bfloat16
I32 = jnp.int32

D_MODEL = 1024
GRID_W = 64
HEAD_DIM = 64
A_WIDTH = D_MODEL // 2
B_Q_HEADS = (D_MODEL - A_WIDTH) // HEAD_DIM
B_KV_HEADS = max(1, B_Q_HEADS // 4)
HY_ORDER = 2
HY_BANDS = 16
HY_TARGET = 1e-2
HY_MAX_DECAY = math.log(1.0 / HY_TARGET) / 0.3
HY_MIN_DECAY = math.log(1.0 / HY_TARGET) / 1.5
MLA_HEADS = 8
MLA_NOPE = 64
MLA_ROPE = 32
MLA_V = 64
MLA_Q_RANK = D_MODEL // 4
MLA_KV_RANK = D_MODEL // 8
S5_WIDTH = D_MODEL - MLA_HEADS * MLA_V
S5_GROUP = 16
S5_GROUPS = S5_WIDTH // S5_GROUP
S5_STATE = 64
S5_CHUNK = 16
S5_TILE_GROUPS = 128 // S5_GROUP
N_GROUPS = 4
EXPERTS_PER_GROUP = 8
N_EXPERTS = N_GROUPS * EXPERTS_PER_GROUP
D_EXPERT = D_MODEL // 4
ROPE_THETA = 10000.0
EPS = 1e-6
LOG2E = 1.4426950408889634

LANES = 128
SUBLANES = 8
MOE_TILE = 256
ROW_CHAIN = 256
HIGHEST = lax.Precision.HIGHEST
MIB = 1024 * 1024


def _cparams(sem, vmem_mib):
    return pltpu.CompilerParams(dimension_semantics=sem, vmem_limit_bytes=vmem_mib * MIB)


def _dot(a, b):
    return jnp.dot(a, b, preferred_element_type=F32)


def _dot_nt(a, b):
    return lax.dot_general(a, b, (((1,), (1,)), ((), ())), preferred_element_type=F32)


def _split_bf16(a):
    hi = a.astype(BF16)
    lo = (a - hi.astype(F32)).astype(BF16)
    return hi, lo


def _dot3(ah, al, bh, bl):
    return _dot(ah, bh) + (_dot(ah, bl) + _dot(al, bh))


def _sigmoid(x):
    return 1.0 / (1.0 + jnp.exp(-x))


def _rms(x, g, width):
    ms = jnp.sum(x * x, axis=-1, keepdims=True) * (1.0 / width)
    return x * lax.rsqrt(ms + EPS) * g


def _modnorm(x, g, shift, scale):
    return _rms(x, g, x.shape[-1]) * (1.0 + scale) + shift


def _rope(t, cs, s1, s2, half):
    return t * cs + pltpu.roll(t, LANES - half, 1) * s1 + pltpu.roll(t, half, 1) * s2


def _store_token_tiles(ref, base_tok, val):
    n = val.shape[0]
    for s in range(SUBLANES):
        ref[pl.ds(base_tok * SUBLANES + s, n, stride=SUBLANES), :] = val[:, LANES * s: LANES * (s + 1)]


def _load_token_tiles(ref, base_tok, n):
    return jnp.concatenate([ref[pl.ds(base_tok * SUBLANES + s, n, stride=SUBLANES), :]
                            for s in range(SUBLANES)], axis=-1)


def _short_conv(u, w, b):
    n = u.shape[0]
    row = lax.broadcasted_iota(I32, u.shape, 0)
    prev = jnp.where(row == 0, 0.0, pltpu.roll(u, 1, 0))
    nxt = jnp.where(row == n - 1, 0.0, pltpu.roll(u, n - 1, 0))
    return prev * w[0:1] + u * w[1:2] + nxt * w[2:3] + b


def _mod_kernel(c_ref, w_ref, b_ref, o_ref):
    c = c_ref[...]
    s = c * _sigmoid(c)
    sh, sl = _split_bf16(s)
    wh, wl = _split_bf16(w_ref[0])
    o_ref[...] = _dot3(sh, sl, wh, wl) + b_ref[0]


def _modulation(cc, w, b, layer):
    r, d = cc.shape
    n = w.shape[2]
    tn = 1536
    return pl.pallas_call(
        _mod_kernel,
        grid=(n // tn,),
        in_specs=[pl.BlockSpec((r, d), lambda j: (0, 0)),
                  pl.BlockSpec((1, d, tn), lambda j: (layer, 0, j)),
                  pl.BlockSpec((1, 1, tn), lambda j: (layer, 0, j))],
        out_specs=pl.BlockSpec((r, tn), lambda j: (0, j)),
        out_shape=jax.ShapeDtypeStruct((r, n), F32),
        compiler_params=_cparams(("parallel",), 48),
        name="modulation",
    )(cc, w, b.reshape(b.shape[0], 1, n))


def _with_ones(v):
    lane = lax.broadcasted_iota(I32, v.shape, 1) % LANES
    return jnp.where(lane >= HEAD_DIM, 1.0, v)


def _row_chunks(tm):
    sub = min(ROW_CHAIN, tm)
    return [slice(r, r + sub) for r in range(0, tm, sub)]


def _even_in_kernel(x_ref, g_ref, sh_ref, sc_ref, w_ref, qg_ref, kg_ref, cs_ref, s1_ref, s2_ref,
                    hy_ref, q_ref, k_ref, v_ref, *, qscale):
    hw = 3 * A_WIDTH
    kw = hw + LANES * B_Q_HEADS
    vw = kw + LANES * B_KV_HEADS
    for rows in _row_chunks(x_ref.shape[1]):
        xn = _modnorm(x_ref[0, rows, :], g_ref[...], sh_ref[0], sc_ref[0]).astype(BF16)
        p = _dot(xn, w_ref[...])
        hy_ref[0, rows, :] = p[:, :hw].astype(BF16)
        cs, s1, s2 = cs_ref[rows, :], s1_ref[rows, :], s2_ref[rows, :]
        for h in range(B_Q_HEADS):
            t = _rms(p[:, hw + LANES * h: hw + LANES * (h + 1)], qg_ref[...], HEAD_DIM)
            q_ref[0, rows, LANES * h: LANES * (h + 1)] = (
                _rope(t, cs, s1, s2, HEAD_DIM // 4) * qscale).astype(BF16)
        for h in range(B_KV_HEADS):
            t = _rms(p[:, kw + LANES * h: kw + LANES * (h + 1)], kg_ref[...], HEAD_DIM)
            k_ref[0, rows, LANES * h: LANES * (h + 1)] = _rope(t, cs, s1, s2, HEAD_DIM // 4).astype(BF16)
        v_ref[0, rows, :] = _with_ones(p[:, vw:]).astype(BF16)


def _even_in(x, g, shift, scale, w, qg, kg, tabs, qscale):
    bsz, n, d = x.shape
    tm = min(512, n)
    nw = w.shape[1]
    row = lambda b, t: (b, t, 0)
    per_b = lambda b, t: (b, 0, 0)
    fixed = lambda b, t: (0, 0)
    tab = pl.BlockSpec((tm, LANES), lambda b, t: (t, 0))
    return pl.pallas_call(
        functools.partial(_even_in_kernel, qscale=qscale),
        grid=(bsz, n // tm),
        in_specs=[pl.BlockSpec((1, tm, d), row), pl.BlockSpec((1, d), fixed),
                  pl.BlockSpec((1, 1, d), per_b), pl.BlockSpec((1, 1, d), per_b),
                  pl.BlockSpec((d, nw), fixed), pl.BlockSpec((1, LANES), fixed),
                  pl.BlockSpec((1, LANES), fixed), tab, tab, tab],
        out_specs=[pl.BlockSpec((1, tm, 3 * A_WIDTH), row),
                   pl.BlockSpec((1, tm, LANES * B_Q_HEADS), row),
                   pl.BlockSpec((1, tm, LANES * B_KV_HEADS), row),
                   pl.BlockSpec((1, tm, LANES * B_KV_HEADS), row)],
        out_shape=[jax.ShapeDtypeStruct((bsz, n, 3 * A_WIDTH), BF16),
                   jax.ShapeDtypeStruct((bsz, n, LANES * B_Q_HEADS), BF16),
                   jax.ShapeDtypeStruct((bsz, n, LANES * B_KV_HEADS), BF16),
                   jax.ShapeDtypeStruct((bsz, n, LANES * B_KV_HEADS), BF16)],
        compiler_params=_cparams(("parallel", "parallel"), 48),
        name="even_in",
    )(x, g.reshape(1, d), shift, scale, w, qg, kg, *tabs)


def _odd_in_kernel(x_ref, g_ref, sh_ref, sc_ref, w_ref, qg_ref, kvg_ref, wuq_ref, wuk_ref, wuv_ref,
                   cs_ref, s1_ref, s2_ref, *out_refs, need_q, qscale):
    if need_q:
        u_ref, k_ref, v_ref, q_ref = out_refs
    else:
        u_ref, k_ref, v_ref = out_refs
    c0, c1, c2 = MLA_Q_RANK, MLA_Q_RANK + MLA_KV_RANK, MLA_Q_RANK + MLA_KV_RANK + LANES
    for rows in _row_chunks(x_ref.shape[1]):
        xn = _modnorm(x_ref[0, rows, :], g_ref[...], sh_ref[0], sc_ref[0]).astype(BF16)
        p = _dot(xn, w_ref[...])
        u_ref[0, rows, :] = p[:, c2:]
        cs, s1, s2 = cs_ref[rows, :], s1_ref[rows, :], s2_ref[rows, :]
        ckv = _rms(p[:, c0:c1], kvg_ref[...], MLA_KV_RANK).astype(BF16)
        kn = _dot(ckv, wuk_ref[...])
        v_ref[0, rows, :] = _with_ones(_dot(ckv, wuv_ref[...])).astype(BF16)
        kr = _rope(p[:, c1:c2], cs, s1, s2, MLA_ROPE // 4)
        for h in range(MLA_HEADS):
            k_ref[0, rows, LANES * h: LANES * (h + 1)] = (kn[:, LANES * h: LANES * (h + 1)] + kr).astype(BF16)
        if need_q:
            cq = _rms(p[:, :c0], qg_ref[...], MLA_Q_RANK).astype(BF16)
            q = _dot(cq, wuq_ref[...])
            for h in range(MLA_HEADS):
                t = q[:, LANES * h: LANES * (h + 1)]
                q_ref[0, rows, LANES * h: LANES * (h + 1)] = (
                    _rope(t, cs, s1, s2, MLA_ROPE // 4) * qscale).astype(BF16)


def _odd_in(x, g, shift, scale, w, qg, kvg, wuq, wuk, wuv, tabs, need_q, qscale):
    bsz, n, d = x.shape
    tm = min(512, n)
    hw = LANES * MLA_HEADS
    row = lambda b, t: (b, t, 0)
    per_b = lambda b, t: (b, 0, 0)
    fixed = lambda b, t: (0, 0)
    tab = pl.BlockSpec((tm, LANES), lambda b, t: (t, 0))
    widths = [S5_WIDTH, hw, hw] + ([hw] if need_q else [])
    return pl.pallas_call(
        functools.partial(_odd_in_kernel, need_q=need_q, qscale=qscale),
        grid=(bsz, n // tm),
        in_specs=[pl.BlockSpec((1, tm, d), row), pl.BlockSpec((1, d), fixed),
                  pl.BlockSpec((1, 1, d), per_b), pl.BlockSpec((1, 1, d), per_b),
                  pl.BlockSpec(w.shape, fixed), pl.BlockSpec((1, MLA_Q_RANK), fixed),
                  pl.BlockSpec((1, MLA_KV_RANK), fixed), pl.BlockSpec(wuq.shape, fixed),
                  pl.BlockSpec(wuk.shape, fixed), pl.BlockSpec(wuv.shape, fixed), tab, tab, tab],
        out_specs=[pl.BlockSpec((1, tm, wd), row) for wd in widths],
        out_shape=[jax.ShapeDtypeStruct((bsz, n, wd), F32 if k == 0 else BF16) for k, wd in enumerate(widths)],
        compiler_params=_cparams(("parallel", "parallel"), 48),
        name="odd_in_q" if need_q else "odd_in",
    )(x, g.reshape(1, d), shift, scale, w, qg, kvg, wuq, wuk, wuv, *tabs)


def _hy_fwd_kernel(wc_ref, ws_ref, u_ref, cw_ref, cb_ref, ka_ref, kb_ref, ya_ref, yb_ref, ubf_ref,
                   *, conv, tf):
    f = pl.program_id(1)

    @pl.when(f == 0)
    def _():
        u = u_ref[0].astype(F32)
        if conv:
            u = _short_conv(u, cw_ref[...], cb_ref[...])
        ubf_ref[...] = u.astype(BF16)

    ub = ubf_ref[...]
    a = _dot(wc_ref[...], ub)
    b = _dot(ws_ref[...], ub)
    ka, kb = ka_ref[...], kb_ref[...]
    first = (lax.broadcasted_iota(I32, a.shape, 0) + f * tf) == 0
    ya_ref[0] = jnp.where(first, a * ka, a * ka - b * kb).astype(BF16)
    yb_ref[0] = jnp.where(first, b * kb, a * kb + b * ka).astype(BF16)


def _hy_fwd(wc, ws, u, col, cw, cb, ka, kb, conv):
    bsz, n, _ = u.shape
    c = A_WIDTH
    tf = min(1024, n)
    return pl.pallas_call(
        functools.partial(_hy_fwd_kernel, conv=conv, tf=tf),
        grid=(bsz, n // tf),
        in_specs=[pl.BlockSpec((tf, n), lambda b, f: (f, 0)), pl.BlockSpec((tf, n), lambda b, f: (f, 0)),
                  pl.BlockSpec((1, n, c), lambda b, f: (b, 0, col)),
                  pl.BlockSpec((3, c), lambda b, f: (0, 0)), pl.BlockSpec((1, c), lambda b, f: (0, 0)),
                  pl.BlockSpec((tf, c), lambda b, f: (f, 0)), pl.BlockSpec((tf, c), lambda b, f: (f, 0))],
        out_specs=[pl.BlockSpec((1, tf, c), lambda b, f: (b, f, 0))] * 2,
        out_shape=[jax.ShapeDtypeStruct((bsz, n, c), BF16)] * 2,
        scratch_shapes=[pltpu.VMEM((n, c), BF16)],
        compiler_params=_cparams(("parallel", "arbitrary"), 56),
        name="hyena_fwd",
    )(wc, ws, u, cw, cb, ka, kb)


def _hy_inv_kernel(ci_ref, si_ref, ya_ref, yb_ref, a_ref, g_ref, cwa_ref, cba_ref, cwg_ref, cbg_ref,
                   bias_ref, o_ref, a_sc, g_sc, *, conv_a, tt):
    t = pl.program_id(1)

    @pl.when(t == 0)
    def _():
        a = a_ref[0].astype(F32)
        if conv_a:
            a = _short_conv(a, cwa_ref[...], cba_ref[...])
        a_sc[...] = a
        g_sc[...] = _short_conv(g_ref[0].astype(F32), cwg_ref[...], cbg_ref[...])

    y = _dot(ci_ref[...], ya_ref[0]) + _dot(si_ref[...], yb_ref[0])
    rows = pl.ds(pl.multiple_of(t * tt, tt), tt)
    y = y + a_sc[rows, :] * bias_ref[...]
    o_ref[0] = (g_sc[rows, :] * y).astype(BF16)


def _hy_inv(ci, si, ya, yb, a, a_col, g, g_col, cwa, cba, cwg, cbg, bias, conv_a):
    bsz, n, c = ya.shape
    tt = min(512, n)
    whole = lambda b, t: (b, 0, 0)
    fixed = lambda b, t: (0, 0)
    return pl.pallas_call(
        functools.partial(_hy_inv_kernel, conv_a=conv_a, tt=tt),
        grid=(bsz, n // tt),
        in_specs=[pl.BlockSpec((tt, n), lambda b, t: (t, 0)), pl.BlockSpec((tt, n), lambda b, t: (t, 0)),
                  pl.BlockSpec((1, n, c), whole), pl.BlockSpec((1, n, c), whole),
                  pl.BlockSpec((1, n, c), lambda b, t: (b, 0, a_col)),
                  pl.BlockSpec((1, n, c), lambda b, t: (b, 0, g_col)),
                  pl.BlockSpec((3, c), fixed), pl.BlockSpec((1, c), fixed),
                  pl.BlockSpec((3, c), fixed), pl.BlockSpec((1, c), fixed), pl.BlockSpec((1, c), fixed)],
        out_specs=pl.BlockSpec((1, tt, c), lambda b, t: (b, t, 0)),
        out_shape=jax.ShapeDtypeStruct((bsz, n, c), BF16),
        scratch_shapes=[pltpu.VMEM((n, c), F32), pltpu.VMEM((n, c), F32)],
        compiler_params=_cparams(("parallel", "arbitrary"), 56),
        name="hyena_inv",
    )(ci, si, ya, yb, a, g, cwa, cba, cwg, cbg, bias)


def _dft_mats(n):
    big = 2 * n
    f = jnp.arange(n, dtype=I32)[:, None]
    s = jnp.arange(n, dtype=I32)[None, :]
    step = 64 if n % 64 == 0 else 1
    angle = lambda fr: ((fr[:, None] * s) % big).astype(F32) * (2.0 * math.pi / big)
    a_hi = angle(jnp.arange(n // step, dtype=I32) * step)
    a_lo = angle(jnp.arange(step, dtype=I32))
    c1, s1, c2, s2 = jnp.cos(a_hi)[:, None], jnp.sin(a_hi)[:, None], jnp.cos(a_lo)[None], jnp.sin(a_lo)[None]
    cos = (c1 * c2 - s1 * s2).reshape(n, n)
    sin = (s1 * c2 + c1 * s2).reshape(n, n)
    alt = jnp.where(jnp.arange(n) % 2 == 0, 1.0, -1.0).astype(F32)
    wc = cos
    ws = jnp.where(f == 0, alt[None, :], -sin)
    ci = jnp.where(s == 0, 1.0 / big, (2.0 / big) * cos)
    si = jnp.where(s == 0, alt[:, None] / big, -(2.0 / big) * sin)
    return (wc, ws), (wc.astype(BF16), ws.astype(BF16), ci.astype(BF16), si.astype(BF16))


def _spectrum_kernel(wc_ref, ws_ref, sh_ref, sl_ref, dh_ref, dl_ref, nyq_ref, nrm_ref, ka_ref, kb_ref, *, tf):
    ch, cl = _split_bf16(wc_ref[...])
    sh, sl = _split_bf16(ws_ref[...])
    a = _dot3(ch, cl, sh_ref[...], sl_ref[...])
    b = _dot3(sh, sl, dh_ref[...], dl_ref[...])
    first = (lax.broadcasted_iota(I32, a.shape, 0) + pl.program_id(1) * tf) == 0
    ka_ref[...] = a * nrm_ref[...]
    kb_ref[...] = jnp.where(first, nyq_ref[...], b) * nrm_ref[...]


def _filter_spectrum(tables, fwd, bwd):
    wc, ws = tables
    n, c = fwd.shape
    bwd = bwd.at[0].set(0.0)
    nrm = lax.rsqrt(jnp.sum(fwd * fwd, axis=0) + jnp.sum(bwd * bwd, axis=0) + EPS).reshape(1, c)
    s, d = fwd + bwd, fwd - bwd
    alt = jnp.where(jnp.arange(n) % 2 == 0, 1.0, -1.0).astype(F32)
    nyq = jnp.sum(alt[:, None] * s, axis=0).reshape(1, c)
    sh, sl = _split_bf16(s)
    dh, dl = _split_bf16(d)
    tf = min(256, n)
    tc = A_WIDTH
    tab = pl.BlockSpec((tf, n), lambda j, f: (f, 0))
    tap = pl.BlockSpec((n, tc), lambda j, f: (0, j))
    vec = pl.BlockSpec((1, tc), lambda j, f: (0, j))
    out = pl.BlockSpec((tf, tc), lambda j, f: (f, j))
    return pl.pallas_call(
        functools.partial(_spectrum_kernel, tf=tf),
        grid=(c // tc, n // tf),
        in_specs=[tab, tab, tap, tap, tap, tap, vec, vec],
        out_specs=[out, out],
        out_shape=[jax.ShapeDtypeStruct((n, c), F32)] * 2,
        compiler_params=_cparams(("parallel", "parallel"), 48),
        name="hyena_filter_spectrum",
    )(wc, ws, sh, sl, dh, dl, nyq, nrm)


def _hyena_filter_spectrum(n, tables, w1, b1, w2, b2, w3, freq):
    t = jnp.arange(n, dtype=F32)
    t_norm = t / max(n - 1, 1)
    bands = jnp.linspace(1e-4, HY_BANDS - 1, HY_BANDS, dtype=F32)
    ang = (2.0 * math.pi / n) * t[:, None] * bands
    z = jnp.concatenate([t_norm[:, None], jnp.cos(ang), jnp.sin(ang)], axis=-1)
    hdn = jnp.sin(freq * (jnp.dot(z, w1, precision=HIGHEST) + b1))
    hdn = jnp.sin(freq * (jnp.dot(hdn, w2, precision=HIGHEST) + b2))
    taps = jnp.dot(hdn, w3, precision=HIGHEST).reshape(n, 2, HY_ORDER, A_WIDTH)
    deltas = jnp.linspace(HY_MIN_DECAY, HY_MAX_DECAY, A_WIDTH, dtype=F32)
    taps = taps * jnp.exp(-t_norm[:, None] * deltas)[:, None, None, :]
    width = HY_ORDER * A_WIDTH
    ka, kb = _filter_spectrum(tables, taps[:, 0].reshape(n, width), taps[:, 1].reshape(n, width))
    return ka.reshape(n, HY_ORDER, A_WIDTH), kb.reshape(n, HY_ORDER, A_WIDTH)


def _hyena(hy, cw, cb, filt, fbias):
    n = hy.shape[1]
    c = A_WIDTH
    tables, (wc, ws, ci, si) = _dft_mats(n)
    ka, kb = _hyena_filter_spectrum(n, tables, *filt)
    cws = [cw[:, i * c:(i + 1) * c] for i in range(3)]
    cbs = [cb[i * c:(i + 1) * c].reshape(1, c) for i in range(3)]
    ya, yb = _hy_fwd(wc, ws, hy, 2, cws[2], cbs[2], ka[:, 0], kb[:, 0], True)
    z = _hy_inv(ci, si, ya, yb, hy, 2, hy, 0, cws[2], cbs[2], cws[0], cbs[0], fbias[0:1], True)
    ya, yb = _hy_fwd(wc, ws, z, 0, cws[2], cbs[2], ka[:, 1], kb[:, 1], False)
    return _hy_inv(ci, si, ya, yb, z, 0, hy, 1, cws[2], cbs[2], cws[1], cbs[1], fbias[1:2], False)


def _attn_kernel(*refs, nseg, hps, kps):
    q_ref = refs[0]
    k_refs = refs[1:1 + nseg]
    v_refs = refs[1 + nseg:1 + 2 * nseg]
    o_ref = refs[1 + 2 * nseg]
    low = lax.broadcasted_iota(I32, (q_ref.shape[1], LANES), 1) < HEAD_DIM
    for pair in range(hps // 2):
        accs = []
        for half in range(2):
            i = 2 * pair + half
            kv = (i * kps) // hps
            q = q_ref[0, :, LANES * i: LANES * (i + 1)]
            ss = [_dot_nt(q, k[0, :, LANES * kv: LANES * (kv + 1)]) for k in k_refs]
            m = ss[0].max(axis=-1, keepdims=True)
            for s in ss[1:]:
                m = jnp.maximum(m, s.max(axis=-1, keepdims=True))
            acc = None
            for s, v in zip(ss, v_refs):
                p = jnp.exp2((s - m).astype(BF16))
                o = _dot(p, v[0, :, LANES * kv: LANES * (kv + 1)])
                acc = o if acc is None else acc + o
            accs.append(acc)
        num = jnp.where(low, accs[0], pltpu.roll(accs[1], HEAD_DIM, 1))
        den = jnp.where(low, pltpu.roll(accs[0], HEAD_DIM, 1), accs[1])
        o_ref[0, :, LANES * pair: LANES * (pair + 1)] = (num / den).astype(BF16)


def _attention(q, ks, vs, hps, kps):
    bsz, lq, qw = q.shape
    nh = qw // LANES
    tq = min(512, lq)
    nseg = len(ks)
    kv_spec = lambda a: pl.BlockSpec((1, a.shape[1], kps * LANES), lambda b, g, t: (b, 0, g))
    return pl.pallas_call(
        functools.partial(_attn_kernel, nseg=nseg, hps=hps, kps=kps),
        grid=(bsz, nh // hps, lq // tq),
        in_specs=[pl.BlockSpec((1, tq, hps * LANES), lambda b, g, t: (b, t, g))]
        + [kv_spec(a) for a in ks] + [kv_spec(a) for a in vs],
        out_specs=pl.BlockSpec((1, tq, hps * HEAD_DIM), lambda b, g, t: (b, t, g)),
        out_shape=jax.ShapeDtypeStruct((bsz, lq, nh * HEAD_DIM), BF16),
        compiler_params=_cparams(("parallel", "parallel", "parallel"), 56),
        name="attention",
    )(q, *ks, *vs)


def _onehot_bf16(cond):
    return jnp.where(cond, 1.0, 0.0).astype(BF16)


def _s5_kernel(uc_ref, ul_ref, w_ref, a_ref, of_ref, or_ref, y_ref, xs_sc, yi_sc, s_sc, x_sc, ys_sc,
               *, bb, ncc, ncl):
    lc, gh, gt = S5_CHUNK, S5_GROUP, S5_TILE_GROUPS
    nc = ncc + ncl
    cw = lc * gh
    hb = gh.bit_length() - 1
    lb = LANES.bit_length() - 1

    for b in range(bb):
        for j in range(lc):
            p, half = divmod(j, 2)
            lanes = slice(half * LANES, (half + 1) * LANES)
            xs_sc[p, b * nc: b * nc + ncc, lanes] = uc_ref[b, pl.ds(j, ncc, stride=lc), :].astype(BF16)
            xs_sc[p, b * nc + ncc: (b + 1) * nc, lanes] = ul_ref[b, pl.ds(j, ncl, stride=lc), :].astype(BF16)

    r_i = lax.broadcasted_iota(I32, (2 * LANES, cw), 0)
    c_i = lax.broadcasted_iota(I32, (2 * LANES, cw), 1)
    same_h = (r_i & (gh - 1)) == (c_i & (gh - 1))
    r_grp = (r_i & (LANES - 1)) >> hb
    c_off = (c_i >> hb) - (r_i >> lb)
    for pr in range(gt // 2):
        rs = []
        for side in range(2):
            q = 2 * pr + side
            u = None
            for p in range(lc // 2):
                sel = _onehot_bf16(same_h & (r_grp == q) & (c_off == 2 * p))
                d = _dot(xs_sc[p], sel)
                u = d if u is None else u + d
            r = _dot(u.astype(BF16), w_ref[q])
            yi_sc[q] = r[:, :cw]
            rs.append(r)
        for kind in range(4):
            lo = cw + kind * LANES
            s_sc[kind, pr] = rs[0][:, lo:lo + LANES] + rs[1][:, lo:lo + LANES]

    a = a_ref[0]
    npair = gt // 2

    def advance(st, c, d):
        rws = pl.ds(c, bb, stride=nc)
        new = []
        for t in range(npair):
            xr, xi = st[2 * t], st[2 * t + 1]
            ar = a[2 * d:2 * d + 1, t * LANES:(t + 1) * LANES]
            ai = a[2 * d + 1:2 * d + 2, t * LANES:(t + 1) * LANES]
            x_sc[2 * d, t, rws, :] = xr
            x_sc[2 * d + 1, t, rws, :] = xi
            new.append(ar * xr - ai * xi + s_sc[2 * d, t, rws, :])
            new.append(ar * xi + ai * xr + s_sc[2 * d + 1, t, rws, :])
        return tuple(new)

    zero = tuple(jnp.zeros((bb, LANES), F32) for _ in range(2 * npair))
    lax.fori_loop(0, nc, lambda c, st: advance(st, c, 0), zero)
    st = lax.fori_loop(0, ncc, lambda i, st: advance(st, ncc - 1 - i, 1), zero)
    lax.fori_loop(0, ncl, lambda i, st: advance(st, nc - 1 - i, 1), st)

    def latent(ref, *idx):
        return jnp.concatenate([ref[idx + (slice(b * nc + ncc, (b + 1) * nc), slice(None))]
                                for b in range(bb)], axis=0)

    for q in range(gt):
        y = latent(yi_sc, q)
        for d, o_ref in ((0, of_ref), (1, or_ref)):
            state = jnp.concatenate([latent(x_sc, 2 * d, q // 2), latent(x_sc, 2 * d + 1, q // 2)], axis=1)
            y = y + _dot(state.astype(BF16), o_ref[q])
        for kt in range(cw // LANES):
            ys_sc[kt, :, q * LANES:(q + 1) * LANES] = y[:, kt * LANES:(kt + 1) * LANES].astype(BF16)

    r2 = lax.broadcasted_iota(I32, (gt * LANES, LANES), 0)
    c2 = lax.broadcasted_iota(I32, (gt * LANES, LANES), 1)
    keep = ((r2 & (gh - 1)) == (c2 & (gh - 1))) & ((r2 >> lb) == (c2 >> hb))
    r_k = (r2 & (LANES - 1)) >> hb
    for k in range(0, lc, 2):
        kt, kk = divmod(k, gt)
        sel = jnp.concatenate([_onehot_bf16(keep & (r_k == kk)), _onehot_bf16(keep & (r_k == kk + 1))], axis=1)
        z = _dot(ys_sc[kt], sel)
        for b in range(bb):
            y_ref[b, pl.ds(k, ncl, stride=lc), :] = z[b * ncl:(b + 1) * ncl, :LANES]
            y_ref[b, pl.ds(k + 1, ncl, stride=lc), :] = z[b * ncl:(b + 1) * ncl, LANES:]


def _s5_scan(u_c, u_l, w, a_tab, of, orr):
    bsz, ncx, c = u_c.shape
    n = u_l.shape[1]
    lc, gt = S5_CHUNK, S5_TILE_GROUPS
    ncc, ncl = ncx // lc, n // lc
    bb = min(4, bsz)
    rows = bb * (ncc + ncl)
    cw = lc * S5_GROUP
    sw = (gt // 2) * LANES
    return pl.pallas_call(
        functools.partial(_s5_kernel, bb=bb, ncc=ncc, ncl=ncl),
        grid=(c // LANES, bsz // bb),
        in_specs=[pl.BlockSpec((bb, ncx, LANES), lambda t, i: (i, 0, t)),
                  pl.BlockSpec((bb, n, LANES), lambda t, i: (i, 0, t)),
                  pl.BlockSpec((gt,) + w.shape[1:], lambda t, i: (t, 0, 0)),
                  pl.BlockSpec((1,) + a_tab.shape[1:], lambda t, i: (t, 0, 0)),
                  pl.BlockSpec((gt,) + of.shape[1:], lambda t, i: (t, 0, 0)),
                  pl.BlockSpec((gt,) + orr.shape[1:], lambda t, i: (t, 0, 0))],
        out_specs=pl.BlockSpec((bb, n, LANES), lambda t, i: (i, 0, t)),
        out_shape=jax.ShapeDtypeStruct((bsz, n, c), F32),
        scratch_shapes=[pltpu.VMEM((lc // 2, rows, 2 * LANES), BF16), pltpu.VMEM((gt, rows, cw), F32),
                        pltpu.VMEM((4, gt // 2, rows, LANES), F32), pltpu.VMEM((4, gt // 2, rows, LANES), F32),
                        pltpu.VMEM((cw // LANES, bb * ncl, gt * LANES), BF16)],
        compiler_params=_cparams(("parallel", "parallel"), 56),
        name="s5_scan",
    )(u_c, u_l, w, a_tab, of, orr)


def _s5_matrices(a_re, a_im, log_dt, b_re, b_im, c_re, c_im, d_skip):
    lc, h, p, g = S5_CHUNK, S5_GROUP, S5_STATE, S5_GROUPS
    dt = jnp.exp(log_dt)[..., None]
    lam_re, lam_im = a_re * dt, a_im * dt
    mag = jnp.exp(lam_re)
    er, ei = mag * jnp.cos(lam_im) - 1.0, mag * jnp.sin(lam_im)
    den = a_re * a_re + a_im * a_im
    co_re = (er * a_re + ei * a_im) / den
    co_im = (ei * a_re - er * a_im) / den
    bb_re = co_re[..., None] * b_re - co_im[..., None] * b_im
    bb_im = co_re[..., None] * b_im + co_im[..., None] * b_re
    m = jnp.arange(lc + 1, dtype=F32)[:, None, None, None]
    pw_mag = jnp.exp(m * lam_re[None])
    pw_re, pw_im = pw_mag * jnp.cos(m * lam_im[None]), pw_mag * jnp.sin(m * lam_im[None])
    ab_re = pw_re[..., None] * bb_re[None] - pw_im[..., None] * bb_im[None]
    ab_im = pw_re[..., None] * bb_im[None] + pw_im[..., None] * bb_re[None]
    kk = (jnp.einsum("dgop,mdgpi->mdgoi", c_re, ab_re[:lc], precision=HIGHEST)
          - jnp.einsum("dgop,mdgpi->mdgoi", c_im, ab_im[:lc], precision=HIGHEST))
    cw = lc * h

    def shifted_rows(x):
        period = 2 * cw + h
        z = jnp.tile(x, (1, 1, lc))[:, :, :lc * (period - h)]
        return jnp.swapaxes(z.reshape(g, h, lc, period - h)[..., cw:], 1, 2)

    resp = jnp.transpose(kk, (1, 2, 4, 0, 3))
    zeros = lambda w: jnp.zeros((g, h, w), F32)
    fwd = resp[0].reshape(g, h, cw)
    rev = resp[1][:, :, ::-1].reshape(g, h, cw)
    tf = shifted_rows(jnp.concatenate([zeros(cw), fwd, zeros(h)], axis=-1))
    tr = shifted_rows(jnp.concatenate([zeros(h), rev, zeros(cw)], axis=-1))
    lane = jnp.arange(cw, dtype=I32)[None, None, None, :]
    diag = (jnp.arange(lc, dtype=I32)[None, :, None, None] * h + jnp.arange(h, dtype=I32)[None, None, :, None])
    dsk = d_skip.reshape(g, 1, h, 1)
    tw = (tf + tr + jnp.where(lane == diag, dsk, 0.0)).reshape(g, cw, cw)
    idx_f = slice(lc - 1, None, -1)
    idx_r = slice(0, lc)

    odd = (jnp.arange(g) % 2 == 1)

    def place(t, axis):
        z = jnp.zeros_like(t)
        shape = [g] + [1] * (t.ndim - 1)
        return jnp.where(odd.reshape(shape), jnp.concatenate([z, t], axis=axis),
                         jnp.concatenate([t, z], axis=axis))

    def to_state(idx, d):
        re = jnp.transpose(ab_re[idx, d], (1, 0, 3, 2)).reshape(g, lc * h, p)
        im = jnp.transpose(ab_im[idx, d], (1, 0, 3, 2)).reshape(g, lc * h, p)
        return [place(re, 2), place(im, 2)]

    w = jnp.concatenate([tw] + to_state(idx_f, 0) + to_state(idx_r, 1), axis=-1)

    def from_state(idx, d):
        pr, pi = pw_re[idx, d], pw_im[idx, d]
        cr, cim = c_re[d], c_im[d]
        mr = cr[None] * pr[:, :, None, :] - cim[None] * pi[:, :, None, :]
        mi = cr[None] * pi[:, :, None, :] + cim[None] * pr[:, :, None, :]
        rows = lambda m: place(jnp.transpose(m, (1, 3, 0, 2)).reshape(g, p, lc * h), 1)
        return jnp.concatenate([rows(mr), rows(-mi)], axis=1)

    of = from_state(slice(1, lc + 1), 0)
    orr = from_state(slice(lc, 0, -1), 1)
    tiles = g // S5_TILE_GROUPS
    a_tab = jnp.stack([pw_re[lc, 0].reshape(tiles, -1), pw_im[lc, 0].reshape(tiles, -1),
                       pw_re[lc, 1].reshape(tiles, -1), pw_im[lc, 1].reshape(tiles, -1)], axis=1)
    return w.astype(BF16), a_tab.astype(F32), of.astype(BF16), orr.astype(BF16)


def _glu_kernel(y_ref, w_ref, b_ref, o_ref):
    y = y_ref[0]
    inner = 0.7978845608028654 * (y + 0.044715 * (y * y * y))
    gl = 0.5 * y * (1.0 + jnp.tanh(inner))
    o_ref[0] = (gl * _sigmoid(_dot(gl.astype(BF16), w_ref[...]) + b_ref[...])).astype(BF16)


def _glu(y, w, b):
    bsz, n, c = y.shape
    tm = min(512, n)
    return pl.pallas_call(
        _glu_kernel,
        grid=(bsz, n // tm),
        in_specs=[pl.BlockSpec((1, tm, c), lambda i, t: (i, t, 0)),
                  pl.BlockSpec((c, c), lambda i, t: (0, 0)), pl.BlockSpec((1, c), lambda i, t: (0, 0))],
        out_specs=pl.BlockSpec((1, tm, c), lambda i, t: (i, t, 0)),
        out_shape=jax.ShapeDtypeStruct((bsz, n, c), BF16),
        compiler_params=_cparams(("parallel", "parallel"), 32),
        name="s5_glu",
    )(y, w, b.reshape(1, c))


def _s5_mixer(u_c, u_l, mats, glu_w, glu_b):
    return _glu(_s5_scan(u_c, u_l, *mats), glu_w, glu_b)


def _route(lg):
    lane = lax.broadcasted_iota(I32, lg.shape, 1).astype(F32)
    neg = jnp.float32(-1e30)
    far = jnp.float32(LANES)
    is_g = lane < N_GROUPS
    lgm = jnp.where(is_g, lg, neg)
    m = lgm.max(axis=-1, keepdims=True)
    gidx = jnp.where(lgm == m, lane, far).min(axis=-1, keepdims=True)
    gp = 1.0 / jnp.where(is_g, jnp.exp(lgm - m), 0.0).sum(axis=-1, keepdims=True)
    lo = N_GROUPS + EXPERTS_PER_GROUP * gidx
    is_e = (lane >= lo) & (lane < lo + EXPERTS_PER_GROUP)
    lem = jnp.where(is_e, lg, neg)
    l0 = lem.max(axis=-1, keepdims=True)
    i0 = jnp.where((lem == l0) & is_e, lane, far).min(axis=-1, keepdims=True)
    lem1 = jnp.where(lane == i0, neg, lem)
    l1 = lem1.max(axis=-1, keepdims=True)
    i1 = jnp.where((lem1 == l1) & is_e & (lane != i0), lane, far).min(axis=-1, keepdims=True)
    e1 = jnp.exp(l1 - l0)
    w0 = 1.0 / (1.0 + e1)
    w1 = e1 * w0
    out = jnp.where(lane == 0, i0 - N_GROUPS, 0.0)
    out = jnp.where(lane == 1, i1 - N_GROUPS, out)
    out = jnp.where(lane == 2, gp * w0, out)
    return jnp.where(lane == 3, gp * w1, out)


def _proj_res_kernel(a_ref, b_ref, wa_ref, wb_ref, x_ref, g1_ref, ng_ref, sh_ref, sc_ref, rh_ref, rl_ref,
                     xo_ref, h_ref, route_ref):
    for rows in _row_chunks(x_ref.shape[1]):
        o = _dot(a_ref[0, rows, :], wa_ref[...]) + _dot(b_ref[0, rows, :], wb_ref[...])
        x = x_ref[0, rows, :] + g1_ref[0] * o
        xo_ref[0, rows, :] = x
        h = _modnorm(x, ng_ref[...], sh_ref[0], sc_ref[0])
        _store_token_tiles(h_ref, rows.start, h)
        hh, hl = _split_bf16(h)
        route_ref[0, rows, :] = _route(_dot3(hh, hl, rh_ref[...], rl_ref[...]))


def _proj_res(a, b, wa, wb, x, g1, ng, shift, scale, rh, rl):
    bsz, n, d = x.shape
    tm = min(512, n)
    row = lambda i, t: (i, t, 0)
    per_b = lambda i, t: (i, 0, 0)
    fixed = lambda i, t: (0, 0)
    return pl.pallas_call(
        _proj_res_kernel,
        grid=(bsz, n // tm),
        in_specs=[pl.BlockSpec((1, tm, a.shape[2]), row), pl.BlockSpec((1, tm, b.shape[2]), row),
                  pl.BlockSpec(wa.shape, fixed), pl.BlockSpec(wb.shape, fixed),
                  pl.BlockSpec((1, tm, d), row), pl.BlockSpec((1, 1, d), per_b), pl.BlockSpec((1, d), fixed),
                  pl.BlockSpec((1, 1, d), per_b), pl.BlockSpec((1, 1, d), per_b),
                  pl.BlockSpec((d, LANES), fixed), pl.BlockSpec((d, LANES), fixed)],
        out_specs=[pl.BlockSpec((1, tm, d), row),
                   pl.BlockSpec((tm * SUBLANES, LANES), lambda i, t: (i * (n // tm) + t, 0)),
                   pl.BlockSpec((1, tm, LANES), row)],
        out_shape=[jax.ShapeDtypeStruct((bsz, n, d), F32),
                   jax.ShapeDtypeStruct((bsz * n * SUBLANES, LANES), F32),
                   jax.ShapeDtypeStruct((bsz, n, LANES), F32)],
        compiler_params=_cparams(("parallel", "parallel"), 40),
        name="proj_res",
    )(a, b, wa, wb, x, g1, ng.reshape(1, d), shift, scale, rh, rl)


def _token_copy(src, src_tok, dst, dst_tok, sem):
    rows = lambda t: pl.ds(pl.multiple_of(t * SUBLANES, SUBLANES), SUBLANES)
    return pltpu.make_async_copy(src.at[rows(src_tok)], dst.at[rows(dst_tok)], sem)


def _tile_copy(src, src_tok, dst, dst_tok, sem):
    n = MOE_TILE * SUBLANES
    rows = lambda t: pl.ds(pl.multiple_of(t * SUBLANES, SUBLANES), n)
    return pltpu.make_async_copy(src.at[rows(src_tok)], dst.at[rows(dst_tok)], sem)


def _experts_kernel(te_ref, ts_ref, tr_ref, tp_ref, win_ref, winn_ref, h_ref, wg_ref, wu_ref, wd_ref, ys_ref,
                    xbuf, obuf, gsem, ssem, *, n_assign):
    del te_ref
    i = pl.program_id(0)
    nt = pl.num_programs(0)
    slot = i % 2
    other = 1 - slot
    nxt = jnp.minimum(i + 1, nt - 1)
    within = lambda start: start & (MOE_TILE - 1)

    def start_gather(w_ref, off, s):
        def body(j, c):
            for k in range(2):
                r = 2 * j + k
                tok = lax.shift_right_logical(w_ref[0, 0, off + r], 1)
                _token_copy(h_ref, tok, xbuf, s * MOE_TILE + r, gsem.at[s]).start(priority=k)
            return c
        lax.fori_loop(0, MOE_TILE // 2, body, 0, unroll=4)

    @pl.when(i == 0)
    def _():
        start_gather(win_ref, within(ts_ref[0]), 0)

    @pl.when(i + 1 < nt)
    def _():
        start_gather(winn_ref, within(ts_ref[nxt]), other)

    base = slot * MOE_TILE
    _tile_copy(h_ref, 0, xbuf, base, gsem.at[slot]).wait()

    @pl.when(i >= 2)
    def _():
        _tile_copy(obuf, base, ys_ref, 0, ssem.at[slot]).wait()

    x = _load_token_tiles(xbuf, base, MOE_TILE).astype(BF16)
    gt = _dot(x, wg_ref[0, 0].astype(BF16))
    hid = (gt * _sigmoid(gt)) * _dot(x, wu_ref[0, 0].astype(BF16))
    _store_token_tiles(obuf, base, _dot(hid.astype(BF16), wd_ref[0, 0].astype(BF16)))

    off, real = within(ts_ref[i]), tr_ref[i]
    spare = n_assign + tp_ref[i] - real

    def scatter(j, c):
        for k in range(2):
            r = 2 * j + k
            dst = jnp.where(r < real, win_ref[0, 0, off + r], spare + r)
            _token_copy(obuf, base + r, ys_ref, dst, ssem.at[slot]).start(priority=k)
        return c
    lax.fori_loop(0, MOE_TILE // 2, scatter, 0, unroll=4)

    @pl.when(i == nt - 1)
    def _():
        _tile_copy(obuf, base, ys_ref, 0, ssem.at[slot]).wait()
        _tile_copy(obuf, other * MOE_TILE, ys_ref, 0, ssem.at[other]).wait()


def _experts(plan, h, n_assign, layer, wg, wu, wd):
    windows, tile_expert, tile_start, tile_real, tile_padbase = plan
    nt = tile_expert.shape[0]
    d = D_MODEL
    wmap = lambda i, te, ts, tr, tp: (layer, te[i], 0, 0)
    log_tile = MOE_TILE.bit_length() - 1
    smem = lambda f: pl.BlockSpec((1, 1, 2 * MOE_TILE), f, memory_space=pltpu.SMEM)
    tile_rows = MOE_TILE * SUBLANES
    return pl.pallas_call(
        functools.partial(_experts_kernel, n_assign=n_assign),
        grid_spec=pltpu.PrefetchScalarGridSpec(
            num_scalar_prefetch=4, grid=(nt,),
            in_specs=[smem(lambda i, te, ts, tr, tp: (lax.shift_right_logical(ts[i], log_tile), 0, 0)),
                      smem(lambda i, te, ts, tr, tp: (
                          lax.shift_right_logical(ts[jnp.minimum(i + 1, nt - 1)], log_tile), 0, 0)),
                      pl.BlockSpec(memory_space=pl.ANY),
                      pl.BlockSpec((1, 1, d, D_EXPERT), wmap), pl.BlockSpec((1, 1, d, D_EXPERT), wmap),
                      pl.BlockSpec((1, 1, D_EXPERT, d), wmap)],
            out_specs=pl.BlockSpec(memory_space=pl.ANY),
            scratch_shapes=[pltpu.VMEM((2 * tile_rows, LANES), F32), pltpu.VMEM((2 * tile_rows, LANES), F32),
                            pltpu.SemaphoreType.DMA((2,)), pltpu.SemaphoreType.DMA((2,))]),
        out_shape=jax.ShapeDtypeStruct((nt * tile_rows, LANES), F32),
        compiler_params=pltpu.CompilerParams(dimension_semantics=("arbitrary",), vmem_limit_bytes=40 * MIB,
                                             has_side_effects=True),
        name="moe_experts",
    )(tile_expert, tile_start, tile_real, tile_padbase, windows[:, None, :], windows[:, None, :], h, wg, wu, wd)


def _combine_kernel(route_ref, x_ref, g2_ref, fg_ref, ys_ref, o_ref, *, tc, final):
    rt = route_ref[0]
    g0, g1 = rt[:, 2:3], rt[:, 3:4]
    cols = [g0 * ys_ref[pl.ds(s, tc, stride=2 * SUBLANES), :]
            + g1 * ys_ref[pl.ds(SUBLANES + s, tc, stride=2 * SUBLANES), :] for s in range(SUBLANES)]
    x = x_ref[0] + g2_ref[0] * jnp.concatenate(cols, axis=-1)
    if final:
        x = _rms(x, fg_ref[...], x.shape[-1])
    o_ref[0] = x


def _combine(route, x, g2, fg, ys, tok_off, final):
    bsz, n, d = x.shape
    tc = min(256, n)
    nt = n // tc
    row = lambda i, t: (i, t, 0)
    blk = tok_off // tc
    return pl.pallas_call(
        functools.partial(_combine_kernel, tc=tc, final=final),
        grid=(bsz, nt),
        in_specs=[pl.BlockSpec((1, tc, LANES), row), pl.BlockSpec((1, tc, d), row),
                  pl.BlockSpec((1, 1, d), lambda i, t: (i, 0, 0)), pl.BlockSpec((1, d), lambda i, t: (0, 0)),
                  pl.BlockSpec((2 * tc * SUBLANES, LANES), lambda i, t: (blk + i * nt + t, 0))],
        out_specs=pl.BlockSpec((1, tc, d), row),
        out_shape=jax.ShapeDtypeStruct((bsz, n, d), F32),
        compiler_params=_cparams(("parallel", "parallel"), 32),
        name="moe_combine",
    )(route, x, g2, fg.reshape(1, d), ys)


def _moe_plan(eids):
    a = eids.shape[0]
    bits = max(a - 1, 1).bit_length()
    order = jnp.sort(eids * (1 << bits) + jnp.arange(a, dtype=I32)) & ((1 << bits) - 1)
    experts = jnp.arange(N_EXPERTS, dtype=I32)
    counts = (eids[:, None] == experts[None, :]).astype(I32).sum(axis=0)
    tiles_e = (counts + MOE_TILE - 1) // MOE_TILE
    tile_end = jnp.cumsum(tiles_e)
    first = jnp.cumsum(counts) - counts
    nt = a // MOE_TILE + N_EXPERTS
    tidx = jnp.arange(nt, dtype=I32)
    tile_expert = jnp.minimum((tidx[:, None] >= tile_end[None, :]).astype(I32).sum(axis=1), N_EXPERTS - 1)
    pick = lambda table: ((tile_expert[:, None] == experts[None, :]).astype(I32) * table[None, :]).sum(axis=1)
    tile_pos = (tidx - pick(tile_end - tiles_e)) * MOE_TILE
    used = tidx < tile_end[-1]
    tile_real = jnp.where(used, jnp.clip(pick(counts) - tile_pos, 0, MOE_TILE), 0)
    tile_start = jnp.where(used, pick(first) + tile_pos, 0)
    pads = MOE_TILE - tile_real
    tile_padbase = jnp.cumsum(pads) - pads
    rows = jnp.concatenate([order, jnp.zeros((MOE_TILE,), I32)]).reshape(-1, MOE_TILE)
    windows = jnp.concatenate([rows[:-1], rows[1:]], axis=1)
    return windows, tile_expert, tile_start, tile_real, tile_padbase


def _moe(streams, layer, wg, wu, wd, final_g, final):
    eids = jnp.concatenate([s[2][..., :2].reshape(-1) for s in streams]).astype(I32)
    h = streams[0][1] if len(streams) == 1 else jnp.concatenate([s[1] for s in streams], axis=0)
    ys = _experts(_moe_plan(eids), h, eids.shape[0], layer, wg, wu, wd)
    outs, off = [], 0
    for x, _, route, g2 in streams:
        outs.append(_combine(route, x, g2, final_g, ys, off, final))
        off += x.shape[0] * x.shape[1]
    return outs


def _rope_tables(n, half, offset, identity):
    cs = jnp.ones((n, LANES), F32)
    z = jnp.zeros((n, LANES), F32)
    if identity:
        return cs, z, z
    rows = (jnp.arange(n, dtype=I32) // GRID_W).astype(F32)
    cols = (jnp.arange(n, dtype=I32) % GRID_W).astype(F32)
    freqs = ROPE_THETA ** (-jnp.arange(half, dtype=F32) / half)
    ar, ac = rows[:, None] * freqs, cols[:, None] * freqs
    zero = jnp.zeros((n, half), F32)
    cos4 = jnp.concatenate([jnp.cos(ar), jnp.cos(ar), jnp.cos(ac), jnp.cos(ac)], axis=1)
    s1 = jnp.concatenate([-jnp.sin(ar), zero, -jnp.sin(ac), zero], axis=1)
    s2 = jnp.concatenate([zero, jnp.sin(ar), zero, jnp.sin(ac)], axis=1)
    place = lambda base, t: lax.dynamic_update_slice(base, t, (0, offset))
    return place(cs, cos4), place(z, s1), place(z, s2)


def _pad_heads(w, nheads, width, left=0):
    k = w.shape[0]
    w = w.reshape(k, nheads, width)
    w = jnp.pad(w, ((0, 0), (0, 0), (left, LANES - width - left)))
    return w.reshape(k, nheads * LANES)


def _lane_pad(v, left=0):
    return jnp.pad(v, (left, LANES - v.shape[0] - left)).reshape(1, LANES)


def kernel(x, c, ctx, c_ctx, w_mod, b_mod, norm_g, e_w_in, e_hy_conv_w, e_hy_conv_b, e_hy_w1, e_hy_b1, e_hy_w2, e_hy_b2, e_hy_w3, e_hy_freq, e_hy_fbias, e_qk_g, e_w_out, o_w_in, o_q_norm_g, o_w_uq, o_kv_norm_g, o_w_ukv, o_s5_a_re, o_s5_a_im, o_s5_log_dt, o_s5_b_re, o_s5_b_im, o_s5_c_re, o_s5_c_im, o_s5_d, o_glu_w, o_glu_b, o_w_out, moe_w_rg, moe_w_re, moe_w_gate, moe_w_up, moe_w_down, final_g):
    bsz, n, d = x.shape
    depth = w_mod.shape[0]
    xc = ctx
    ncx = ctx.shape[1]
    cc = jnp.zeros((bsz + 8, d), F32).at[:bsz].set(c).at[bsz].set(c_ctx)

    for i in range(depth):
        j = i // 2
        last = i == depth - 1
        mods = _modulation(cc, w_mod, b_mod, i)
        lat = [m.reshape(bsz, 1, d) for m in jnp.split(mods[:bsz], 6, axis=-1)]
        cxm = [jnp.broadcast_to(m.reshape(1, 1, d), (bsz, 1, d)) for m in jnp.split(mods[bsz], 6, axis=-1)]

        if i % 2 == 0:
            wq = _pad_heads(e_w_in[j][:, 3 * A_WIDTH:3 * A_WIDTH + B_Q_HEADS * HEAD_DIM], B_Q_HEADS, HEAD_DIM)
            k0 = 3 * A_WIDTH + B_Q_HEADS * HEAD_DIM
            wk = _pad_heads(e_w_in[j][:, k0:k0 + B_KV_HEADS * HEAD_DIM], B_KV_HEADS, HEAD_DIM)
            wv = _pad_heads(e_w_in[j][:, k0 + B_KV_HEADS * HEAD_DIM:], B_KV_HEADS, HEAD_DIM)
            w_in = jnp.concatenate([e_w_in[j][:, :3 * A_WIDTH], wq, wk, wv], axis=1).astype(BF16)
            qg, kg = _lane_pad(e_qk_g[j, 0]), _lane_pad(e_qk_g[j, 1])
            filt = (e_hy_w1[j], e_hy_b1[j], e_hy_w2[j], e_hy_b2[j], e_hy_w3[j], e_hy_freq[j])
            qscale = HEAD_DIM ** -0.5 * LOG2E
            hy_l, q_l, k_l, v_l = _even_in(x, norm_g[i, 0], lat[0], lat[1], w_in, qg, kg,
                                           _rope_tables(n, HEAD_DIM // 4, 0, False), qscale)
            hy_c, q_c, k_c, v_c = _even_in(xc, norm_g[i, 0], cxm[0], cxm[1], w_in, qg, kg,
                                           _rope_tables(ncx, HEAD_DIM // 4, 0, True), qscale)
            rep = B_Q_HEADS // B_KV_HEADS
            att_l = _attention(q_l, [k_c, k_l], [v_c, v_l], rep, 1)
            mix_l = (_hyena(hy_l, e_hy_conv_w[j], e_hy_conv_b[j], filt, e_hy_fbias[j]), att_l)
            if not last:
                att_c = _attention(q_c, [k_c], [v_c], rep, 1)
                mix_c = (_hyena(hy_c, e_hy_conv_w[j], e_hy_conv_b[j], filt, e_hy_fbias[j]), att_c)
            w_out = e_w_out[j]
            split = A_WIDTH
        else:
            w1 = o_w_in[j]
            c0, c1, c2 = MLA_Q_RANK, MLA_Q_RANK + MLA_KV_RANK, MLA_Q_RANK + MLA_KV_RANK + MLA_ROPE
            w_kr = jnp.pad(w1[:, c1:c2], ((0, 0), (MLA_NOPE, LANES - MLA_NOPE - MLA_ROPE)))
            w_in = jnp.concatenate([w1[:, :c1], w_kr, w1[:, c2:]], axis=1).astype(BF16)
            wuq = _pad_heads(o_w_uq[j], MLA_HEADS, MLA_NOPE + MLA_ROPE).astype(BF16)
            wkv = o_w_ukv[j].reshape(MLA_KV_RANK, MLA_HEADS, MLA_NOPE + MLA_V)
            wuk = _pad_heads(wkv[:, :, :MLA_NOPE].reshape(MLA_KV_RANK, -1), MLA_HEADS, MLA_NOPE).astype(BF16)
            wuv = _pad_heads(wkv[:, :, MLA_NOPE:].reshape(MLA_KV_RANK, -1), MLA_HEADS, MLA_V).astype(BF16)
            qg = o_q_norm_g[j].reshape(1, MLA_Q_RANK)
            kvg = o_kv_norm_g[j].reshape(1, MLA_KV_RANK)
            args = (w_in, qg, kvg, wuq, wuk, wuv)
            qscale = (MLA_NOPE + MLA_ROPE) ** -0.5 * LOG2E
            u_l, k_l, v_l, q_l = _odd_in(x, norm_g[i, 0], lat[0], lat[1], *args,
                                         _rope_tables(n, MLA_ROPE // 4, MLA_NOPE, False), True, qscale)
            outs_c = _odd_in(xc, norm_g[i, 0], cxm[0], cxm[1], *args,
                             _rope_tables(ncx, MLA_ROPE // 4, MLA_NOPE, True), not last, qscale)
            u_c, k_c, v_c = outs_c[:3]
            att_l = _attention(q_l, [k_c, k_l], [v_c, v_l], 2, 2)
            mats = _s5_matrices(o_s5_a_re[j], o_s5_a_im[j], o_s5_log_dt[j], o_s5_b_re[j], o_s5_b_im[j],
                                o_s5_c_re[j], o_s5_c_im[j], o_s5_d[j])
            glu_w = o_glu_w[j].astype(BF16)
            mix_l = (att_l, _s5_mixer(u_c, u_l, mats, glu_w, o_glu_b[j]))
            if not last:
                raise NotImplementedError("an odd layer that is not the last needs the context S5 readout")
            w_out = o_w_out[j]
            split = MLA_HEADS * MLA_V

        wa, wb = w_out[:split].astype(BF16), w_out[split:].astype(BF16)
        router = jnp.pad(jnp.concatenate([moe_w_rg[i], moe_w_re[i]], axis=1),
                         ((0, 0), (0, LANES - N_GROUPS - N_EXPERTS)))
        rh = router.astype(BF16)
        rl = (router - rh.astype(F32)).astype(BF16)

        x, h_l, route_l = _proj_res(*mix_l, wa, wb, x, lat[2], norm_g[i, 1], lat[3], lat[4], rh, rl)
        streams = [(x, h_l, route_l, lat[5])]
        if not last:
            xc, h_c, route_c = _proj_res(*mix_c, wa, wb, xc, cxm[2], norm_g[i, 1], cxm[3], cxm[4], rh, rl)
            streams.append((xc, h_c, route_c, cxm[5]))
        outs = _moe(streams, i, moe_w_gate, moe_w_up, moe_w_down, final_g, last)
        x = outs[0]
        if not last:
            xc = outs[1]
    return x
```

```python
import functools
import math

import jax
import jax.numpy as jnp
from jax import lax
from jax.experimental import pallas as pl
from jax.experimental.pallas import tpu as pltpu

F32 = jnp.float32
BF16 = jnp.bfloat16
I32 = jnp.int32

D_MODEL = 1024
GRID_W = 64
HEAD_DIM = 64
A_WIDTH = D_MODEL // 2
B_Q_HEADS = (D_MODEL - A_WIDTH) // HEAD_DIM
B_KV_HEADS = max(1, B_Q_HEADS // 4)
HY_ORDER = 2
HY_BANDS = 16
HY_TARGET = 1e-2
HY_MAX_DECAY = math.log(1.0 / HY_TARGET) / 0.3
HY_MIN_DECAY = math.log(1.0 / HY_TARGET) / 1.5
MLA_HEADS = 8
MLA_NOPE = 64
MLA_ROPE = 32
MLA_V = 64
MLA_Q_RANK = D_MODEL // 4
MLA_KV_RANK = D_MODEL // 8
S5_WIDTH = D_MODEL - MLA_HEADS * MLA_V
S5_GROUP = 16
S5_GROUPS = S5_WIDTH // S5_GROUP
S5_STATE = 64
S5_CHUNK = 16
S5_TILE_GROUPS = 128 // S5_GROUP
N_GROUPS = 4
EXPERTS_PER_GROUP = 8
N_EXPERTS = N_GROUPS * EXPERTS_PER_GROUP
D_EXPERT = D_MODEL // 4
ROPE_THETA = 10000.0
EPS = 1e-6
LOG2E = 1.4426950408889634

LANES = 128
SUBLANES = 8
MOE_TILE = 256
ROW_CHAIN = 256
HIGHEST = lax.Precision.HIGHEST
MIB = 1024 * 1024


def _cparams(sem, vmem_mib):
    return pltpu.CompilerParams(dimension_semantics=sem, vmem_limit_bytes=vmem_mib * MIB)


def _dot(a, b):
    return jnp.dot(a, b, preferred_element_type=F32)


def _dot_nt(a, b):
    return lax.dot_general(a, b, (((1,), (1,)), ((), ())), preferred_element_type=F32)


def _split_bf16(a):
    hi = a.astype(BF16)
    lo = (a - hi.astype(F32)).astype(BF16)
    return hi, lo


def _dot3(ah, al, bh, bl):
    return _dot(ah, bh) + (_dot(ah, bl) + _dot(al, bh))


def _sigmoid(x):
    return 1.0 / (1.0 + jnp.exp(-x))


def _rms(x, g, width):
    ms = jnp.sum(x * x, axis=-1, keepdims=True) * (1.0 / width)
    return x * lax.rsqrt(ms + EPS) * g


def _modnorm(x, g, shift, scale):
    return _rms(x, g, x.shape[-1]) * (1.0 + scale) + shift


def _rope(t, cs, s1, s2, half):
    return t * cs + pltpu.roll(t, LANES - half, 1) * s1 + pltpu.roll(t, half, 1) * s2


def _store_token_tiles(ref, base_tok, val):
    n = val.shape[0]
    for s in range(SUBLANES):
        ref[pl.ds(base_tok * SUBLANES + s, n, stride=SUBLANES), :] = val[:, LANES * s: LANES * (s + 1)]


def _load_token_tiles(ref, base_tok, n):
    return jnp.concatenate([ref[pl.ds(base_tok * SUBLANES + s, n, stride=SUBLANES), :]
                            for s in range(SUBLANES)], axis=-1)


def _short_conv(u, w, b):
    n = u.shape[0]
    row = lax.broadcasted_iota(I32, u.shape, 0)
    prev = jnp.where(row == 0, 0.0, pltpu.roll(u, 1, 0))
    nxt = jnp.where(row == n - 1, 0.0, pltpu.roll(u, n - 1, 0))
    return prev * w[0:1] + u * w[1:2] + nxt * w[2:3] + b


def _mod_kernel(c_ref, w_ref, b_ref, o_ref):
    c = c_ref[...]
    s = c * _sigmoid(c)
    sh, sl = _split_bf16(s)
    wh, wl = _split_bf16(w_ref[0])
    o_ref[...] = _dot3(sh, sl, wh, wl) + b_ref[0]


def _modulation(cc, w, b, layer):
    r, d = cc.shape
    n = w.shape[2]
    tn = 1536
    return pl.pallas_call(
        _mod_kernel,
        grid=(n // tn,),
        in_specs=[pl.BlockSpec((r, d), lambda j: (0, 0)),
                  pl.BlockSpec((1, d, tn), lambda j: (layer, 0, j)),
                  pl.BlockSpec((1, 1, tn), lambda j: (layer, 0, j))],
        out_specs=pl.BlockSpec((r, tn), lambda j: (0, j)),
        out_shape=jax.ShapeDtypeStruct((r, n), F32),
        compiler_params=_cparams(("parallel",), 48),
        name="modulation",
    )(cc, w, b.reshape(b.shape[0], 1, n))


def _with_ones(v):
    lane = lax.broadcasted_iota(I32, v.shape, 1) % LANES
    return jnp.where(lane >= HEAD_DIM, 1.0, v)


def _row_chunks(tm):
    sub = min(ROW_CHAIN, tm)
    return [slice(r, r + sub) for r in range(0, tm, sub)]


def _even_in_kernel(x_ref, g_ref, sh_ref, sc_ref, w_ref, qg_ref, kg_ref, cs_ref, s1_ref, s2_ref,
                    hy_ref, q_ref, k_ref, v_ref, *, qscale):
    hw = 3 * A_WIDTH
    kw = hw + LANES * B_Q_HEADS
    vw = kw + LANES * B_KV_HEADS
    for rows in _row_chunks(x_ref.shape[1]):
        xn = _modnorm(x_ref[0, rows, :], g_ref[...], sh_ref[0], sc_ref[0]).astype(BF16)
        p = _dot(xn, w_ref[...])
        hy_ref[0, rows, :] = p[:, :hw].astype(BF16)
        cs, s1, s2 = cs_ref[rows, :], s1_ref[rows, :], s2_ref[rows, :]
        for h in range(B_Q_HEADS):
            t = _rms(p[:, hw + LANES * h: hw + LANES * (h + 1)], qg_ref[...], HEAD_DIM)
            q_ref[0, rows, LANES * h: LANES * (h + 1)] = (
                _rope(t, cs, s1, s2, HEAD_DIM // 4) * qscale).astype(BF16)
        for h in range(B_KV_HEADS):
            t = _rms(p[:, kw + LANES * h: kw + LANES * (h + 1)], kg_ref[...], HEAD_DIM)
            k_ref[0, rows, LANES * h: LANES * (h + 1)] = _rope(t, cs, s1, s2, HEAD_DIM // 4).astype(BF16)
        v_ref[0, rows, :] = _with_ones(p[:, vw:]).astype(BF16)


def _even_in(x, g, shift, scale, w, qg, kg, tabs, qscale):
    bsz, n, d = x.shape
    tm = min(512, n)
    nw = w.shape[1]
    row = lambda b, t: (b, t, 0)
    per_b = lambda b, t: (b, 0, 0)
    fixed = lambda b, t: (0, 0)
    tab = pl.BlockSpec((tm, LANES), lambda b, t: (t, 0))
    return pl.pallas_call(
        functools.partial(_even_in_kernel, qscale=qscale),
        grid=(bsz, n // tm),
        in_specs=[pl.BlockSpec((1, tm, d), row), pl.BlockSpec((1, d), fixed),
                  pl.BlockSpec((1, 1, d), per_b), pl.BlockSpec((1, 1, d), per_b),
                  pl.BlockSpec((d, nw), fixed), pl.BlockSpec((1, LANES), fixed),
                  pl.BlockSpec((1, LANES), fixed), tab, tab, tab],
        out_specs=[pl.BlockSpec((1, tm, 3 * A_WIDTH), row),
                   pl.BlockSpec((1, tm, LANES * B_Q_HEADS), row),
                   pl.BlockSpec((1, tm, LANES * B_KV_HEADS), row),
                   pl.BlockSpec((1, tm, LANES * B_KV_HEADS), row)],
        out_shape=[jax.ShapeDtypeStruct((bsz, n, 3 * A_WIDTH), BF16),
                   jax.ShapeDtypeStruct((bsz, n, LANES * B_Q_HEADS), BF16),
                   jax.ShapeDtypeStruct((bsz, n, LANES * B_KV_HEADS), BF16),
                   jax.ShapeDtypeStruct((bsz, n, LANES * B_KV_HEADS), BF16)],
        compiler_params=_cparams(("parallel", "parallel"), 48),
        name="even_in",
    )(x, g.reshape(1, d), shift, scale, w, qg, kg, *tabs)


def _odd_in_kernel(x_ref, g_ref, sh_ref, sc_ref, w_ref, qg_ref, kvg_ref, wuq_ref, wuk_ref, wuv_ref,
                   cs_ref, s1_ref, s2_ref, *out_refs, need_q, qscale):
    if need_q:
        u_ref, k_ref, v_ref, q_ref = out_refs
    else:
        u_ref, k_ref, v_ref = out_refs
    c0, c1, c2 = MLA_Q_RANK, MLA_Q_RANK + MLA_KV_RANK, MLA_Q_RANK + MLA_KV_RANK + LANES
    for rows in _row_chunks(x_ref.shape[1]):
        xn = _modnorm(x_ref[0, rows, :], g_ref[...], sh_ref[0], sc_ref[0]).astype(BF16)
        p = _dot(xn, w_ref[...])
        u_ref[0, rows, :] = p[:, c2:]
        cs, s1, s2 = cs_ref[rows, :], s1_ref[rows, :], s2_ref[rows, :]
        ckv = _rms(p[:, c0:c1], kvg_ref[...], MLA_KV_RANK).astype(BF16)
        kn = _dot(ckv, wuk_ref[...])
        v_ref[0, rows, :] = _with_ones(_dot(ckv, wuv_ref[...])).astype(BF16)
        kr = _rope(p[:, c1:c2], cs, s1, s2, MLA_ROPE // 4)
        for h in range(MLA_HEADS):
            k_ref[0, rows, LANES * h: LANES * (h + 1)] = (kn[:, LANES * h: LANES * (h + 1)] + kr).astype(BF16)
        if need_q:
            cq = _rms(p[:, :c0], qg_ref[...], MLA_Q_RANK).astype(BF16)
            q = _dot(cq, wuq_ref[...])
            for h in range(MLA_HEADS):
                t = q[:, LANES * h: LANES * (h + 1)]
                q_ref[0, rows, LANES * h: LANES * (h + 1)] = (
                    _rope(t, cs, s1, s2, MLA_ROPE // 4) * qscale).astype(BF16)


def _odd_in(x, g, shift, scale, w, qg, kvg, wuq, wuk, wuv, tabs, need_q, qscale):
    bsz, n, d = x.shape
    tm = min(512, n)
    hw = LANES * MLA_HEADS
    row = lambda b, t: (b, t, 0)
    per_b = lambda b, t: (b, 0, 0)
    fixed = lambda b, t: (0, 0)
    tab = pl.BlockSpec((tm, LANES), lambda b, t: (t, 0))
    widths = [S5_WIDTH, hw, hw] + ([hw] if need_q else [])
    return pl.pallas_call(
        functools.partial(_odd_in_kernel, need_q=need_q, qscale=qscale),
        grid=(bsz, n // tm),
        in_specs=[pl.BlockSpec((1, tm, d), row), pl.BlockSpec((1, d), fixed),
                  pl.BlockSpec((1, 1, d), per_b), pl.BlockSpec((1, 1, d), per_b),
                  pl.BlockSpec(w.shape, fixed), pl.BlockSpec((1, MLA_Q_RANK), fixed),
                  pl.BlockSpec((1, MLA_KV_RANK), fixed), pl.BlockSpec(wuq.shape, fixed),
                  pl.BlockSpec(wuk.shape, fixed), pl.BlockSpec(wuv.shape, fixed), tab, tab, tab],
        out_specs=[pl.BlockSpec((1, tm, wd), row) for wd in widths],
        out_shape=[jax.ShapeDtypeStruct((bsz, n, wd), F32 if k == 0 else BF16) for k, wd in enumerate(widths)],
        compiler_params=_cparams(("parallel", "parallel"), 48),
        name="odd_in_q" if need_q else "odd_in",
    )(x, g.reshape(1, d), shift, scale, w, qg, kvg, wuq, wuk, wuv, *tabs)


def _hy_fwd_kernel(wc_ref, ws_ref, u_ref, cw_ref, cb_ref, ka_ref, kb_ref, ya_ref, yb_ref, ubf_ref,
                   *, conv, tf):
    f = pl.program_id(1)

    @pl.when(f == 0)
    def _():
        u = u_ref[0].astype(F32)
        if conv:
            u = _short_conv(u, cw_ref[...], cb_ref[...])
        ubf_ref[...] = u.astype(BF16)

    ub = ubf_ref[...]
    a = _dot(wc_ref[...], ub)
    b = _dot(ws_ref[...], ub)
    ka, kb = ka_ref[...], kb_ref[...]
    first = (lax.broadcasted_iota(I32, a.shape, 0) + f * tf) == 0
    ya_ref[0] = jnp.where(first, a * ka, a * ka - b * kb).astype(BF16)
    yb_ref[0] = jnp.where(first, b * kb, a * kb + b * ka).astype(BF16)


def _hy_fwd(wc, ws, u, col, cw, cb, ka, kb, conv):
    bsz, n, _ = u.shape
    c = A_WIDTH
    tf = min(1024, n)
    return pl.pallas_call(
        functools.partial(_hy_fwd_kernel, conv=conv, tf=tf),
        grid=(bsz, n // tf),
        in_specs=[pl.BlockSpec((tf, n), lambda b, f: (f, 0)), pl.BlockSpec((tf, n), lambda b, f: (f, 0)),
                  pl.BlockSpec((1, n, c), lambda b, f: (b, 0, col)),
                  pl.BlockSpec((3, c), lambda b, f: (0, 0)), pl.BlockSpec((1, c), lambda b, f: (0, 0)),
                  pl.BlockSpec((tf, c), lambda b, f: (f, 0)), pl.BlockSpec((tf, c), lambda b, f: (f, 0))],
        out_specs=[pl.BlockSpec((1, tf, c), lambda b, f: (b, f, 0))] * 2,
        out_shape=[jax.ShapeDtypeStruct((bsz, n, c), BF16)] * 2,
        scratch_shapes=[pltpu.VMEM((n, c), BF16)],
        compiler_params=_cparams(("parallel", "arbitrary"), 56),
        name="hyena_fwd",
    )(wc, ws, u, cw, cb, ka, kb)


def _hy_inv_kernel(ci_ref, si_ref, ya_ref, yb_ref, a_ref, g_ref, cwa_ref, cba_ref, cwg_ref, cbg_ref,
                   bias_ref, o_ref, a_sc, g_sc, *, conv_a, tt):
    t = pl.program_id(1)

    @pl.when(t == 0)
    def _():
        a = a_ref[0].astype(F32)
        if conv_a:
            a = _short_conv(a, cwa_ref[...], cba_ref[...])
        a_sc[...] = a
        g_sc[...] = _short_conv(g_ref[0].astype(F32), cwg_ref[...], cbg_ref[...])

    y = _dot(ci_ref[...], ya_ref[0]) + _dot(si_ref[...], yb_ref[0])
    rows = pl.ds(pl.multiple_of(t * tt, tt), tt)
    y = y + a_sc[rows, :] * bias_ref[...]
    o_ref[0] = (g_sc[rows, :] * y).astype(BF16)


def _hy_inv(ci, si, ya, yb, a, a_col, g, g_col, cwa, cba, cwg, cbg, bias, conv_a):
    bsz, n, c = ya.shape
    tt = min(512, n)
    whole = lambda b, t: (b, 0, 0)
    fixed = lambda b, t: (0, 0)
    return pl.pallas_call(
        functools.partial(_hy_inv_kernel, conv_a=conv_a, tt=tt),
        grid=(bsz, n // tt),
        in_specs=[pl.BlockSpec((tt, n), lambda b, t: (t, 0)), pl.BlockSpec((tt, n), lambda b, t: (t, 0)),
                  pl.BlockSpec((1, n, c), whole), pl.BlockSpec((1, n, c), whole),
                  pl.BlockSpec((1, n, c), lambda b, t: (b, 0, a_col)),
                  pl.BlockSpec((1, n, c), lambda b, t: (b, 0, g_col)),
                  pl.BlockSpec((3, c), fixed), pl.BlockSpec((1, c), fixed),
                  pl.BlockSpec((3, c), fixed), pl.BlockSpec((1, c), fixed), pl.BlockSpec((1, c), fixed)],
        out_specs=pl.BlockSpec((1, tt, c), lambda b, t: (b, t, 0)),
        out_shape=jax.ShapeDtypeStruct((bsz, n, c), BF16),
        scratch_shapes=[pltpu.VMEM((n, c), F32), pltpu.VMEM((n, c), F32)],
        compiler_params=_cparams(("parallel", "arbitrary"), 56),
        name="hyena_inv",
    )(ci, si, ya, yb, a, g, cwa, cba, cwg, cbg, bias)


def _dft_mats(n):
    big = 2 * n
    f = jnp.arange(n, dtype=I32)[:, None]
    s = jnp.arange(n, dtype=I32)[None, :]
    step = 64 if n % 64 == 0 else 1
    angle = lambda fr: ((fr[:, None] * s) % big).astype(F32) * (2.0 * math.pi / big)
    a_hi = angle(jnp.arange(n // step, dtype=I32) * step)
    a_lo = angle(jnp.arange(step, dtype=I32))
    c1, s1, c2, s2 = jnp.cos(a_hi)[:, None], jnp.sin(a_hi)[:, None], jnp.cos(a_lo)[None], jnp.sin(a_lo)[None]
    cos = (c1 * c2 - s1 * s2).reshape(n, n)
    sin = (s1 * c2 + c1 * s2).reshape(n, n)
    alt = jnp.where(jnp.arange(n) % 2 == 0, 1.0, -1.0).astype(F32)
    wc = cos
    ws = jnp.where(f == 0, alt[None, :], -sin)
    ci = jnp.where(s == 0, 1.0 / big, (2.0 / big) * cos)
    si = jnp.where(s == 0, alt[:, None] / big, -(2.0 / big) * sin)
    return (wc, ws), (wc.astype(BF16), ws.astype(BF16), ci.astype(BF16), si.astype(BF16))


def _spectrum_kernel(wc_ref, ws_ref, sh_ref, sl_ref, dh_ref, dl_ref, nyq_ref, nrm_ref, ka_ref, kb_ref, *, tf):
    ch, cl = _split_bf16(wc_ref[...])
    sh, sl = _split_bf16(ws_ref[...])
    a = _dot3(ch, cl, sh_ref[...], sl_ref[...])
    b = _dot3(sh, sl, dh_ref[...], dl_ref[...])
    first = (lax.broadcasted_iota(I32, a.shape, 0) + pl.program_id(1) * tf) == 0
    ka_ref[...] = a * nrm_ref[...]
    kb_ref[...] = jnp.where(first, nyq_ref[...], b) * nrm_ref[...]


def _filter_spectrum(tables, fwd, bwd):
    wc, ws = tables
    n, c = fwd.shape
    bwd = bwd.at[0].set(0.0)
    nrm = lax.rsqrt(jnp.sum(fwd * fwd, axis=0) + jnp.sum(bwd * bwd, axis=0) + EPS).reshape(1, c)
    s, d = fwd + bwd, fwd - bwd
    alt = jnp.where(jnp.arange(n) % 2 == 0, 1.0, -1.0).astype(F32)
    nyq = jnp.sum(alt[:, None] * s, axis=0).reshape(1, c)
    sh, sl = _split_bf16(s)
    dh, dl = _split_bf16(d)
    tf = min(256, n)
    tc = A_WIDTH
    tab = pl.BlockSpec((tf, n), lambda j, f: (f, 0))
    tap = pl.BlockSpec((n, tc), lambda j, f: (0, j))
    vec = pl.BlockSpec((1, tc), lambda j, f: (0, j))
    out = pl.BlockSpec((tf, tc), lambda j, f: (f, j))
    return pl.pallas_call(
        functools.partial(_spectrum_kernel, tf=tf),
        grid=(c // tc, n // tf),
        in_specs=[tab, tab, tap, tap, tap, tap, vec, vec],
        out_specs=[out, out],
        out_shape=[jax.ShapeDtypeStruct((n, c), F32)] * 2,
        compiler_params=_cparams(("parallel", "parallel"), 48),
        name="hyena_filter_spectrum",
    )(wc, ws, sh, sl, dh, dl, nyq, nrm)


def _hyena_filter_spectrum(n, tables, w1, b1, w2, b2, w3, freq):
    t = jnp.arange(n, dtype=F32)
    t_norm = t / max(n - 1, 1)
    bands = jnp.linspace(1e-4, HY_BANDS - 1, HY_BANDS, dtype=F32)
    ang = (2.0 * math.pi / n) * t[:, None] * bands
    z = jnp.concatenate([t_norm[:, None], jnp.cos(ang), jnp.sin(ang)], axis=-1)
    hdn = jnp.sin(freq * (jnp.dot(z, w1, precision=HIGHEST) + b1))
    hdn = jnp.sin(freq * (jnp.dot(hdn, w2, precision=HIGHEST) + b2))
    taps = jnp.dot(hdn, w3, precision=HIGHEST).reshape(n, 2, HY_ORDER, A_WIDTH)
    deltas = jnp.linspace(HY_MIN_DECAY, HY_MAX_DECAY, A_WIDTH, dtype=F32)
    taps = taps * jnp.exp(-t_norm[:, None] * deltas)[:, None, None, :]
    width = HY_ORDER * A_WIDTH
    ka, kb = _filter_spectrum(tables, taps[:, 0].reshape(n, width), taps[:, 1].reshape(n, width))
    return ka.reshape(n, HY_ORDER, A_WIDTH), kb.reshape(n, HY_ORDER, A_WIDTH)


def _hyena(hy, cw, cb, filt, fbias):
    n = hy.shape[1]
    c = A_WIDTH
    tables, (wc, ws, ci, si) = _dft_mats(n)
    ka, kb = _hyena_filter_spectrum(n, tables, *filt)
    cws = [cw[:, i * c:(i + 1) * c] for i in range(3)]
    cbs = [cb[i * c:(i + 1) * c].reshape(1, c) for i in range(3)]
    ya, yb = _hy_fwd(wc, ws, hy, 2, cws[2], cbs[2], ka[:, 0], kb[:, 0], True)
    z = _hy_inv(ci, si, ya, yb, hy, 2, hy, 0, cws[2], cbs[2], cws[0], cbs[0], fbias[0:1], True)
    ya, yb = _hy_fwd(wc, ws, z, 0, cws[2], cbs[2], ka[:, 1], kb[:, 1], False)
    return _hy_inv(ci, si, ya, yb, z, 0, hy, 1, cws[2], cbs[2], cws[1], cbs[1], fbias[1:2], False)


def _attn_kernel(*refs, nseg, hps, kps):
    q_ref = refs[0]
    k_refs = refs[1:1 + nseg]
    v_refs = refs[1 + nseg:1 + 2 * nseg]
    o_ref = refs[1 + 2 * nseg]
    low = lax.broadcasted_iota(I32, (q_ref.shape[1], LANES), 1) < HEAD_DIM
    for pair in range(hps // 2):
        accs = []
        for half in range(2):
            i = 2 * pair + half
            kv = (i * kps) // hps
            q = q_ref[0, :, LANES * i: LANES * (i + 1)]
            ss = [_dot_nt(q, k[0, :, LANES * kv: LANES * (kv + 1)]) for k in k_refs]
            m = ss[0].max(axis=-1, keepdims=True)
            for s in ss[1:]:
                m = jnp.maximum(m, s.max(axis=-1, keepdims=True))
            acc = None
            for s, v in zip(ss, v_refs):
                p = jnp.exp2((s - m).astype(BF16))
                o = _dot(p, v[0, :, LANES * kv: LANES * (kv + 1)])
                acc = o if acc is None else acc + o
            accs.append(acc)
        num = jnp.where(low, accs[0], pltpu.roll(accs[1], HEAD_DIM, 1))
        den = jnp.where(low, pltpu.roll(accs[0], HEAD_DIM, 1), accs[1])
        o_ref[0, :, LANES * pair: LANES * (pair + 1)] = (num / den).astype(BF16)


def _attention(q, ks, vs, hps, kps):
    bsz, lq, qw = q.shape
    nh = qw // LANES
    tq = min(512, lq)
    nseg = len(ks)
    kv_spec = lambda a: pl.BlockSpec((1, a.shape[1], kps * LANES), lambda b, g, t: (b, 0, g))
    return pl.pallas_call(
        functools.partial(_attn_kernel, nseg=nseg, hps=hps, kps=kps),
        grid=(bsz, nh // hps, lq // tq),
        in_specs=[pl.BlockSpec((1, tq, hps * LANES), lambda b, g, t: (b, t, g))]
        + [kv_spec(a) for a in ks] + [kv_spec(a) for a in vs],
        out_specs=pl.BlockSpec((1, tq, hps * HEAD_DIM), lambda b, g, t: (b, t, g)),
        out_shape=jax.ShapeDtypeStruct((bsz, lq, nh * HEAD_DIM), BF16),
        compiler_params=_cparams(("parallel", "parallel", "parallel"), 56),
        name="attention",
    )(q, *ks, *vs)


def _onehot_bf16(cond):
    return jnp.where(cond, 1.0, 0.0).astype(BF16)


def _s5_kernel(uc_ref, ul_ref, w_ref, a_ref, of_ref, or_ref, y_ref, xs_sc, yi_sc, s_sc, x_sc, ys_sc,
               *, bb, ncc, ncl):
    lc, gh, gt = S5_CHUNK, S5_GROUP, S5_TILE_GROUPS
    nc = ncc + ncl
    cw = lc * gh
    hb = gh.bit_length() - 1
    lb = LANES.bit_length() - 1

    for b in range(bb):
        for j in range(lc):
            p, half = divmod(j, 2)
            lanes = slice(half * LANES, (half + 1) * LANES)
            xs_sc[p, b * nc: b * nc + ncc, lanes] = uc_ref[b, pl.ds(j, ncc, stride=lc), :].astype(BF16)
            xs_sc[p, b * nc + ncc: (b + 1) * nc, lanes] = ul_ref[b, pl.ds(j, ncl, stride=lc), :].astype(BF16)

    r_i = lax.broadcasted_iota(I32, (2 * LANES, cw), 0)
    c_i = lax.broadcasted_iota(I32, (2 * LANES, cw), 1)
    same_h = (r_i & (gh - 1)) == (c_i & (gh - 1))
    r_grp = (r_i & (LANES - 1)) >> hb
    c_off = (c_i >> hb) - (r_i >> lb)
    for pr in range(gt // 2):
        rs = []
        for side in range(2):
            q = 2 * pr + side
            u = None
            for p in range(lc // 2):
                sel = _onehot_bf16(same_h & (r_grp == q) & (c_off == 2 * p))
                d = _dot(xs_sc[p], sel)
                u = d if u is None else u + d
            r = _dot(u.astype(BF16), w_ref[q])
            yi_sc[q] = r[:, :cw]
            rs.append(r)
        for kind in range(4):
            lo = cw + kind * LANES
            s_sc[kind, pr] = rs[0][:, lo:lo + LANES] + rs[1][:, lo:lo + LANES]

    a = a_ref[0]
    npair = gt // 2

    def advance(st, c, d):
        rws = pl.ds(c, bb, stride=nc)
        new = []
        for t in range(npair):
            xr, xi = st[2 * t], st[2 * t + 1]
            ar = a[2 * d:2 * d + 1, t * LANES:(t + 1) * LANES]
            ai = a[2 * d + 1:2 * d + 2, t * LANES:(t + 1) * LANES]
            x_sc[2 * d, t, rws, :] = xr
            x_sc[2 * d + 1, t, rws, :] = xi
            new.append(ar * xr - ai * xi + s_sc[2 * d, t, rws, :])
            new.append(ar * xi + ai * xr + s_sc[2 * d + 1, t, rws, :])
        return tuple(new)

    zero = tuple(jnp.zeros((bb, LANES), F32) for _ in range(2 * npair))
    lax.fori_loop(0, nc, lambda c, st: advance(st, c, 0), zero)
    st = lax.fori_loop(0, ncc, lambda i, st: advance(st, ncc - 1 - i, 1), zero)
    lax.fori_loop(0, ncl, lambda i, st: advance(st, nc - 1 - i, 1), st)

    def latent(ref, *idx):
        return jnp.concatenate([ref[idx + (slice(b * nc + ncc, (b + 1) * nc), slice(None))]
                                for b in range(bb)], axis=0)

    for q in range(gt):
        y = latent(yi_sc, q)
        for d, o_ref in ((0, of_ref), (1, or_ref)):
            state = jnp.concatenate([latent(x_sc, 2 * d, q // 2), latent(x_sc, 2 * d + 1, q // 2)], axis=1)
            y = y + _dot(state.astype(BF16), o_ref[q])
        for kt in range(cw // LANES):
            ys_sc[kt, :, q * LANES:(q + 1) * LANES] = y[:, kt * LANES:(kt + 1) * LANES].astype(BF16)

    r2 = lax.broadcasted_iota(I32, (gt * LANES, LANES), 0)
    c2 = lax.broadcasted_iota(I32, (gt * LANES, LANES), 1)
    keep = ((r2 & (gh - 1)) == (c2 & (gh - 1))) & ((r2 >> lb) == (c2 >> hb))
    r_k = (r2 & (LANES - 1)) >> hb
    for k in range(0, lc, 2):
        kt, kk = divmod(k, gt)
        sel = jnp.concatenate([_onehot_bf16(keep & (r_k == kk)), _onehot_bf16(keep & (r_k == kk + 1))], axis=1)
        z = _dot(ys_sc[kt], sel)
        for b in range(bb):
            y_ref[b, pl.ds(k, ncl, stride=lc), :] = z[b * ncl:(b + 1) * ncl, :LANES]
            y_ref[b, pl.ds(k + 1, ncl, stride=lc), :] = z[b * ncl:(b + 1) * ncl, LANES:]


def _s5_scan(u_c, u_l, w, a_tab, of, orr):
    bsz, ncx, c = u_c.shape
    n = u_l.shape[1]
    lc, gt = S5_CHUNK, S5_TILE_GROUPS
    ncc, ncl = ncx // lc, n // lc
    bb = min(4, bsz)
    rows = bb * (ncc + ncl)
    cw = lc * S5_GROUP
    sw = (gt // 2) * LANES
    return pl.pallas_call(
        functools.partial(_s5_kernel, bb=bb, ncc=ncc, ncl=ncl),
        grid=(c // LANES, bsz // bb),
        in_specs=[pl.BlockSpec((bb, ncx, LANES), lambda t, i: (i, 0, t)),
                  pl.BlockSpec((bb, n, LANES), lambda t, i: (i, 0, t)),
                  pl.BlockSpec((gt,) + w.shape[1:], lambda t, i: (t, 0, 0)),
                  pl.BlockSpec((1,) + a_tab.shape[1:], lambda t, i: (t, 0, 0)),
                  pl.BlockSpec((gt,) + of.shape[1:], lambda t, i: (t, 0, 0)),
                  pl.BlockSpec((gt,) + orr.shape[1:], lambda t, i: (t, 0, 0))],
        out_specs=pl.BlockSpec((bb, n, LANES), lambda t, i: (i, 0, t)),
        out_shape=jax.ShapeDtypeStruct((bsz, n, c), F32),
        scratch_shapes=[pltpu.VMEM((lc // 2, rows, 2 * LANES), BF16), pltpu.VMEM((gt, rows, cw), F32),
                        pltpu.VMEM((4, gt // 2, rows, LANES), F32), pltpu.VMEM((4, gt // 2, rows, LANES), F32),
                        pltpu.VMEM((cw // LANES, bb * ncl, gt * LANES), BF16)],
        compiler_params=_cparams(("parallel", "parallel"), 56),
        name="s5_scan",
    )(u_c, u_l, w, a_tab, of, orr)


def _s5_matrices(a_re, a_im, log_dt, b_re, b_im, c_re, c_im, d_skip):
    lc, h, p, g = S5_CHUNK, S5_GROUP, S5_STATE, S5_GROUPS
    dt = jnp.exp(log_dt)[..., None]
    lam_re, lam_im = a_re * dt, a_im * dt
    mag = jnp.exp(lam_re)
    er, ei = mag * jnp.cos(lam_im) - 1.0, mag * jnp.sin(lam_im)
    den = a_re * a_re + a_im * a_im
    co_re = (er * a_re + ei * a_im) / den
    co_im = (ei * a_re - er * a_im) / den
    bb_re = co_re[..., None] * b_re - co_im[..., None] * b_im
    bb_im = co_re[..., None] * b_im + co_im[..., None] * b_re
    m = jnp.arange(lc + 1, dtype=F32)[:, None, None, None]
    pw_mag = jnp.exp(m * lam_re[None])
    pw_re, pw_im = pw_mag * jnp.cos(m * lam_im[None]), pw_mag * jnp.sin(m * lam_im[None])
    ab_re = pw_re[..., None] * bb_re[None] - pw_im[..., None] * bb_im[None]
    ab_im = pw_re[..., None] * bb_im[None] + pw_im[..., None] * bb_re[None]
    kk = (jnp.einsum("dgop,mdgpi->mdgoi", c_re, ab_re[:lc], precision=HIGHEST)
          - jnp.einsum("dgop,mdgpi->mdgoi", c_im, ab_im[:lc], precision=HIGHEST))
    cw = lc * h

    def shifted_rows(x):
        period = 2 * cw + h
        z = jnp.tile(x, (1, 1, lc))[:, :, :lc * (period - h)]
        return jnp.swapaxes(z.reshape(g, h, lc, period - h)[..., cw:], 1, 2)

    resp = jnp.transpose(kk, (1, 2, 4, 0, 3))
    zeros = lambda w: jnp.zeros((g, h, w), F32)
    fwd = resp[0].reshape(g, h, cw)
    rev = resp[1][:, :, ::-1].reshape(g, h, cw)
    tf = shifted_rows(jnp.concatenate([zeros(cw), fwd, zeros(h)], axis=-1))
    tr = shifted_rows(jnp.concatenate([zeros(h), rev, zeros(cw)], axis=-1))
    lane = jnp.arange(cw, dtype=I32)[None, None, None, :]
    diag = (jnp.arange(lc, dtype=I32)[None, :, None, None] * h + jnp.arange(h, dtype=I32)[None, None, :, None])
    dsk = d_skip.reshape(g, 1, h, 1)
    tw = (tf + tr + jnp.where(lane == diag, dsk, 0.0)).reshape(g, cw, cw)
    idx_f = slice(lc - 1, None, -1)
    idx_r = slice(0, lc)

    odd = (jnp.arange(g) % 2 == 1)

    def place(t, axis):
        z = jnp.zeros_like(t)
        shape = [g] + [1] * (t.ndim - 1)
        return jnp.where(odd.reshape(shape), jnp.concatenate([z, t], axis=axis),
                         jnp.concatenate([t, z], axis=axis))

    def to_state(idx, d):
        re = jnp.transpose(ab_re[idx, d], (1, 0, 3, 2)).reshape(g, lc * h, p)
        im = jnp.transpose(ab_im[idx, d], (1, 0, 3, 2)).reshape(g, lc * h, p)
        return [place(re, 2), place(im, 2)]

    w = jnp.concatenate([tw] + to_state(idx_f, 0) + to_state(idx_r, 1), axis=-1)

    def from_state(idx, d):
        pr, pi = pw_re[idx, d], pw_im[idx, d]
        cr, cim = c_re[d], c_im[d]
        mr = cr[None] * pr[:, :, None, :] - cim[None] * pi[:, :, None, :]
        mi = cr[None] * pi[:, :, None, :] + cim[None] * pr[:, :, None, :]
        rows = lambda m: place(jnp.transpose(m, (1, 3, 0, 2)).reshape(g, p, lc * h), 1)
        return jnp.concatenate([rows(mr), rows(-mi)], axis=1)

    of = from_state(slice(1, lc + 1), 0)
    orr = from_state(slice(lc, 0, -1), 1)
    tiles = g // S5_TILE_GROUPS
    a_tab = jnp.stack([pw_re[lc, 0].reshape(tiles, -1), pw_im[lc, 0].reshape(tiles, -1),
                       pw_re[lc, 1].reshape(tiles, -1), pw_im[lc, 1].reshape(tiles, -1)], axis=1)
    return w.astype(BF16), a_tab.astype(F32), of.astype(BF16), orr.astype(BF16)


def _glu_kernel(y_ref, w_ref, b_ref, o_ref):
    y = y_ref[0]
    inner = 0.7978845608028654 * (y + 0.044715 * (y * y * y))
    gl = 0.5 * y * (1.0 + jnp.tanh(inner))
    o_ref[0] = (gl * _sigmoid(_dot(gl.astype(BF16), w_ref[...]) + b_ref[...])).astype(BF16)


def _glu(y, w, b):
    bsz, n, c = y.shape
    tm = min(512, n)
    return pl.pallas_call(
        _glu_kernel,
        grid=(bsz, n // tm),
        in_specs=[pl.BlockSpec((1, tm, c), lambda i, t: (i, t, 0)),
                  pl.BlockSpec((c, c), lambda i, t: (0, 0)), pl.BlockSpec((1, c), lambda i, t: (0, 0))],
        out_specs=pl.BlockSpec((1, tm, c), lambda i, t: (i, t, 0)),
        out_shape=jax.ShapeDtypeStruct((bsz, n, c), BF16),
        compiler_params=_cparams(("parallel", "parallel"), 32),
        name="s5_glu",
    )(y, w, b.reshape(1, c))


def _s5_mixer(u_c, u_l, mats, glu_w, glu_b):
    return _glu(_s5_scan(u_c, u_l, *mats), glu_w, glu_b)


def _route(lg):
    lane = lax.broadcasted_iota(I32, lg.shape, 1).astype(F32)
    neg = jnp.float32(-1e30)
    far = jnp.float32(LANES)
    is_g = lane < N_GROUPS
    lgm = jnp.where(is_g, lg, neg)
    m = lgm.max(axis=-1, keepdims=True)
    gidx = jnp.where(lgm == m, lane, far).min(axis=-1, keepdims=True)
    gp = 1.0 / jnp.where(is_g, jnp.exp(lgm - m), 0.0).sum(axis=-1, keepdims=True)
    lo = N_GROUPS + EXPERTS_PER_GROUP * gidx
    is_e = (lane >= lo) & (lane < lo + EXPERTS_PER_GROUP)
    lem = jnp.where(is_e, lg, neg)
    l0 = lem.max(axis=-1, keepdims=True)
    i0 = jnp.where((lem == l0) & is_e, lane, far).min(axis=-1, keepdims=True)
    lem1 = jnp.where(lane == i0, neg, lem)
    l1 = lem1.max(axis=-1, keepdims=True)
    i1 = jnp.where((lem1 == l1) & is_e & (lane != i0), lane, far).min(axis=-1, keepdims=True)
    e1 = jnp.exp(l1 - l0)
    w0 = 1.0 / (1.0 + e1)
    w1 = e1 * w0
    out = jnp.where(lane == 0, i0 - N_GROUPS, 0.0)
    out = jnp.where(lane == 1, i1 - N_GROUPS, out)
    out = jnp.where(lane == 2, gp * w0, out)
    return jnp.where(lane == 3, gp * w1, out)


def _proj_res_kernel(a_ref, b_ref, wa_ref, wb_ref, x_ref, g1_ref, ng_ref, sh_ref, sc_ref, rh_ref, rl_ref,
                     xo_ref, h_ref, route_ref):
    for rows in _row_chunks(x_ref.shape[1]):
        o = _dot(a_ref[0, rows, :], wa_ref[...]) + _dot(b_ref[0, rows, :], wb_ref[...])
        x = x_ref[0, rows, :] + g1_ref[0] * o
        xo_ref[0, rows, :] = x
        h = _modnorm(x, ng_ref[...], sh_ref[0], sc_ref[0])
        _store_token_tiles(h_ref, rows.start, h)
        hh, hl = _split_bf16(h)
        route_ref[0, rows, :] = _route(_dot3(hh, hl, rh_ref[...], rl_ref[...]))


def _proj_res(a, b, wa, wb, x, g1, ng, shift, scale, rh, rl):
    bsz, n, d = x.shape
    tm = min(512, n)
    row = lambda i, t: (i, t, 0)
    per_b = lambda i, t: (i, 0, 0)
    fixed = lambda i, t: (0, 0)
    return pl.pallas_call(
        _proj_res_kernel,
        grid=(bsz, n // tm),
        in_specs=[pl.BlockSpec((1, tm, a.shape[2]), row), pl.BlockSpec((1, tm, b.shape[2]), row),
                  pl.BlockSpec(wa.shape, fixed), pl.BlockSpec(wb.shape, fixed),
                  pl.BlockSpec((1, tm, d), row), pl.BlockSpec((1, 1, d), per_b), pl.BlockSpec((1, d), fixed),
                  pl.BlockSpec((1, 1, d), per_b), pl.BlockSpec((1, 1, d), per_b),
                  pl.BlockSpec((d, LANES), fixed), pl.BlockSpec((d, LANES), fixed)],
        out_specs=[pl.BlockSpec((1, tm, d), row),
                   pl.BlockSpec((tm * SUBLANES, LANES), lambda i, t: (i * (n // tm) + t, 0)),
                   pl.BlockSpec((1, tm, LANES), row)],
        out_shape=[jax.ShapeDtypeStruct((bsz, n, d), F32),
                   jax.ShapeDtypeStruct((bsz * n * SUBLANES, LANES), F32),
                   jax.ShapeDtypeStruct((bsz, n, LANES), F32)],
        compiler_params=_cparams(("parallel", "parallel"), 40),
        name="proj_res",
    )(a, b, wa, wb, x, g1, ng.reshape(1, d), shift, scale, rh, rl)


def _token_copy(src, src_tok, dst, dst_tok, sem):
    rows = lambda t: pl.ds(pl.multiple_of(t * SUBLANES, SUBLANES), SUBLANES)
    return pltpu.make_async_copy(src.at[rows(src_tok)], dst.at[rows(dst_tok)], sem)


def _tile_copy(src, src_tok, dst, dst_tok, sem):
    n = MOE_TILE * SUBLANES
    rows = lambda t: pl.ds(pl.multiple_of(t * SUBLANES, SUBLANES), n)
    return pltpu.make_async_copy(src.at[rows(src_tok)], dst.at[rows(dst_tok)], sem)


def _experts_kernel(te_ref, ts_ref, tr_ref, tp_ref, win_ref, winn_ref, h_ref, wg_ref, wu_ref, wd_ref, ys_ref,
                    xbuf, obuf, gsem, ssem, *, n_assign):
    del te_ref
    i = pl.program_id(0)
    nt = pl.num_programs(0)
    slot = i % 2
    other = 1 - slot
    nxt = jnp.minimum(i + 1, nt - 1)
    within = lambda start: start & (MOE_TILE - 1)

    def start_gather(w_ref, off, s):
        def body(j, c):
            for k in range(2):
                r = 2 * j + k
                tok = lax.shift_right_logical(w_ref[0, 0, off + r], 1)
                _token_copy(h_ref, tok, xbuf, s * MOE_TILE + r, gsem.at[s]).start(priority=k)
            return c
        lax.fori_loop(0, MOE_TILE // 2, body, 0, unroll=4)

    @pl.when(i == 0)
    def _():
        start_gather(win_ref, within(ts_ref[0]), 0)

    @pl.when(i + 1 < nt)
    def _():
        start_gather(winn_ref, within(ts_ref[nxt]), other)

    base = slot * MOE_TILE
    _tile_copy(h_ref, 0, xbuf, base, gsem.at[slot]).wait()

    @pl.when(i >= 2)
    def _():
        _tile_copy(obuf, base, ys_ref, 0, ssem.at[slot]).wait()

    x = _load_token_tiles(xbuf, base, MOE_TILE).astype(BF16)
    gt = _dot(x, wg_ref[0, 0].astype(BF16))
    hid = (gt * _sigmoid(gt)) * _dot(x, wu_ref[0, 0].astype(BF16))
    _store_token_tiles(obuf, base, _dot(hid.astype(BF16), wd_ref[0, 0].astype(BF16)))

    off, real = within(ts_ref[i]), tr_ref[i]
    spare = n_assign + tp_ref[i] - real

    def scatter(j, c):
        for k in range(2):
            r = 2 * j + k
            dst = jnp.where(r < real, win_ref[0, 0, off + r], spare + r)
            _token_copy(obuf, base + r, ys_ref, dst, ssem.at[slot]).start(priority=k)
        return c
    lax.fori_loop(0, MOE_TILE // 2, scatter, 0, unroll=4)

    @pl.when(i == nt - 1)
    def _():
        _tile_copy(obuf, base, ys_ref, 0, ssem.at[slot]).wait()
        _tile_copy(obuf, other * MOE_TILE, ys_ref, 0, ssem.at[other]).wait()


def _experts(plan, h, n_assign, layer, wg, wu, wd):
    windows, tile_expert, tile_start, tile_real, tile_padbase = plan
    nt = tile_expert.shape[0]
    d = D_MODEL
    wmap = lambda i, te, ts, tr, tp: (layer, te[i], 0, 0)
    log_tile = MOE_TILE.bit_length() - 1
    smem = lambda f: pl.BlockSpec((1, 1, 2 * MOE_TILE), f, memory_space=pltpu.SMEM)
    tile_rows = MOE_TILE * SUBLANES
    return pl.pallas_call(
        functools.partial(_experts_kernel, n_assign=n_assign),
        grid_spec=pltpu.PrefetchScalarGridSpec(
            num_scalar_prefetch=4, grid=(nt,),
            in_specs=[smem(lambda i, te, ts, tr, tp: (lax.shift_right_logical(ts[i], log_tile), 0, 0)),
                      smem(lambda i, te, ts, tr, tp: (
                          lax.shift_right_logical(ts[jnp.minimum(i + 1, nt - 1)], log_tile), 0, 0)),
                      pl.BlockSpec(memory_space=pl.ANY),
                      pl.BlockSpec((1, 1, d, D_EXPERT), wmap), pl.BlockSpec((1, 1, d, D_EXPERT), wmap),
                      pl.BlockSpec((1, 1, D_EXPERT, d), wmap)],
            out_specs=pl.BlockSpec(memory_space=pl.ANY),
            scratch_shapes=[pltpu.VMEM((2 * tile_rows, LANES), F32), pltpu.VMEM((2 * tile_rows, LANES), F32),
                            pltpu.SemaphoreType.DMA((2,)), pltpu.SemaphoreType.DMA((2,))]),
        out_shape=jax.ShapeDtypeStruct((nt * tile_rows, LANES), F32),
        compiler_params=pltpu.CompilerParams(dimension_semantics=("arbitrary",), vmem_limit_bytes=40 * MIB,
                                             has_side_effects=True),
        name="moe_experts",
    )(tile_expert, tile_start, tile_real, tile_padbase, windows[:, None, :], windows[:, None, :], h, wg, wu, wd)


def _combine_kernel(route_ref, x_ref, g2_ref, fg_ref, ys_ref, o_ref, *, tc, final):
    rt = route_ref[0]
    g0, g1 = rt[:, 2:3], rt[:, 3:4]
    cols = [g0 * ys_ref[pl.ds(s, tc, stride=2 * SUBLANES), :]
            + g1 * ys_ref[pl.ds(SUBLANES + s, tc, stride=2 * SUBLANES), :] for s in range(SUBLANES)]
    x = x_ref[0] + g2_ref[0] * jnp.concatenate(cols, axis=-1)
    if final:
        x = _rms(x, fg_ref[...], x.shape[-1])
    o_ref[0] = x


def _combine(route, x, g2, fg, ys, tok_off, final):
    bsz, n, d = x.shape
    tc = min(256, n)
    nt = n // tc
    row = lambda i, t: (i, t, 0)
    blk = tok_off // tc
    return pl.pallas_call(
        functools.partial(_combine_kernel, tc=tc, final=final),
        grid=(bsz, nt),
        in_specs=[pl.BlockSpec((1, tc, LANES), row), pl.BlockSpec((1, tc, d), row),
                  pl.BlockSpec((1, 1, d), lambda i, t: (i, 0, 0)), pl.BlockSpec((1, d), lambda i, t: (0, 0)),
                  pl.BlockSpec((2 * tc * SUBLANES, LANES), lambda i, t: (blk + i * nt + t, 0))],
        out_specs=pl.BlockSpec((1, tc, d), row),
        out_shape=jax.ShapeDtypeStruct((bsz, n, d), F32),
        compiler_params=_cparams(("parallel", "parallel"), 32),
        name="moe_combine",
    )(route, x, g2, fg.reshape(1, d), ys)


def _moe_plan(eids):
    a = eids.shape[0]
    bits = max(a - 1, 1).bit_length()
    order = jnp.sort(eids * (1 << bits) + jnp.arange(a, dtype=I32), stable=False) & ((1 << bits) - 1)
    experts = jnp.arange(N_EXPERTS, dtype=I32)
    counts = (eids[:, None] == experts[None, :]).astype(I32).sum(axis=0)
    tiles_e = (counts + MOE_TILE - 1) // MOE_TILE
    tile_end = jnp.cumsum(tiles_e)
    first = jnp.cumsum(counts) - counts
    nt = a // MOE_TILE + N_EXPERTS
    tidx = jnp.arange(nt, dtype=I32)
    tile_expert = jnp.minimum((tidx[:, None] >= tile_end[None, :]).astype(I32).sum(axis=1), N_EXPERTS - 1)
    pick = lambda table: ((tile_expert[:, None] == experts[None, :]).astype(I32) * table[None, :]).sum(axis=1)
    tile_pos = (tidx - pick(tile_end - tiles_e)) * MOE_TILE
    used = tidx < tile_end[-1]
    tile_real = jnp.where(used, jnp.clip(pick(counts) - tile_pos, 0, MOE_TILE), 0)
    tile_start = jnp.where(used, pick(first) + tile_pos, 0)
    pads = MOE_TILE - tile_real
    tile_padbase = jnp.cumsum(pads) - pads
    rows = jnp.concatenate([order, jnp.zeros((MOE_TILE,), I32)]).reshape(-1, MOE_TILE)
    windows = jnp.concatenate([rows[:-1], rows[1:]], axis=1)
    return windows, tile_expert, tile_start, tile_real, tile_padbase


def _moe(streams, layer, wg, wu, wd, final_g, final):
    eids = jnp.concatenate([s[2][..., :2].reshape(-1) for s in streams]).astype(I32)
    h = streams[0][1] if len(streams) == 1 else jnp.concatenate([s[1] for s in streams], axis=0)
    ys = _experts(_moe_plan(eids), h, eids.shape[0], layer, wg, wu, wd)
    outs, off = [], 0
    for x, _, route, g2 in streams:
        outs.append(_combine(route, x, g2, final_g, ys, off, final))
        off += x.shape[0] * x.shape[1]
    return outs


def _rope_tables(n, half, offset, identity):
    cs = jnp.ones((n, LANES), F32)
    z = jnp.zeros((n, LANES), F32)
    if identity:
        return cs, z, z
    rows = (jnp.arange(n, dtype=I32) // GRID_W).astype(F32)
    cols = (jnp.arange(n, dtype=I32) % GRID_W).astype(F32)
    freqs = ROPE_THETA ** (-jnp.arange(half, dtype=F32) / half)
    ar, ac = rows[:, None] * freqs, cols[:, None] * freqs
    zero = jnp.zeros((n, half), F32)
    cos4 = jnp.concatenate([jnp.cos(ar), jnp.cos(ar), jnp.cos(ac), jnp.cos(ac)], axis=1)
    s1 = jnp.concatenate([-jnp.sin(ar), zero, -jnp.sin(ac), zero], axis=1)
    s2 = jnp.concatenate([zero, jnp.sin(ar), zero, jnp.sin(ac)], axis=1)
    place = lambda base, t: lax.dynamic_update_slice(base, t, (0, offset))
    return place(cs, cos4), place(z, s1), place(z, s2)


def _pad_heads(w, nheads, width, left=0):
    k = w.shape[0]
    w = w.reshape(k, nheads, width)
    w = jnp.pad(w, ((0, 0), (0, 0), (left, LANES - width - left)))
    return w.reshape(k, nheads * LANES)


def _lane_pad(v, left=0):
    return jnp.pad(v, (left, LANES - v.shape[0] - left)).reshape(1, LANES)


def kernel(x, c, ctx, c_ctx, w_mod, b_mod, norm_g, e_w_in, e_hy_conv_w, e_hy_conv_b, e_hy_w1, e_hy_b1, e_hy_w2, e_hy_b2, e_hy_w3, e_hy_freq, e_hy_fbias, e_qk_g, e_w_out, o_w_in, o_q_norm_g, o_w_uq, o_kv_norm_g, o_w_ukv, o_s5_a_re, o_s5_a_im, o_s5_log_dt, o_s5_b_re, o_s5_b_im, o_s5_c_re, o_s5_c_im, o_s5_d, o_glu_w, o_glu_b, o_w_out, moe_w_rg, moe_w_re, moe_w_gate, moe_w_up, moe_w_down, final_g):
    bsz, n, d = x.shape
    depth = w_mod.shape[0]
    xc = ctx
    ncx = ctx.shape[1]
    cc = jnp.zeros((bsz + 8, d), F32).at[:bsz].set(c).at[bsz].set(c_ctx)

    for i in range(depth):
        j = i // 2
        last = i == depth - 1
        mods = _modulation(cc, w_mod, b_mod, i)
        lat = [m.reshape(bsz, 1, d) for m in jnp.split(mods[:bsz], 6, axis=-1)]
        cxm = [jnp.broadcast_to(m.reshape(1, 1, d), (bsz, 1, d)) for m in jnp.split(mods[bsz], 6, axis=-1)]

        if i % 2 == 0:
            wq = _pad_heads(e_w_in[j][:, 3 * A_WIDTH:3 * A_WIDTH + B_Q_HEADS * HEAD_DIM], B_Q_HEADS, HEAD_DIM)
            k0 = 3 * A_WIDTH + B_Q_HEADS * HEAD_DIM
            wk = _pad_heads(e_w_in[j][:, k0:k0 + B_KV_HEADS * HEAD_DIM], B_KV_HEADS, HEAD_DIM)
            wv = _pad_heads(e_w_in[j][:, k0 + B_KV_HEADS * HEAD_DIM:], B_KV_HEADS, HEAD_DIM)
            w_in = jnp.concatenate([e_w_in[j][:, :3 * A_WIDTH], wq, wk, wv], axis=1).astype(BF16)
            qg, kg = _lane_pad(e_qk_g[j, 0]), _lane_pad(e_qk_g[j, 1])
            filt = (e_hy_w1[j], e_hy_b1[j], e_hy_w2[j], e_hy_b2[j], e_hy_w3[j], e_hy_freq[j])
            qscale = HEAD_DIM ** -0.5 * LOG2E
            hy_l, q_l, k_l, v_l = _even_in(x, norm_g[i, 0], lat[0], lat[1], w_in, qg, kg,
                                           _rope_tables(n, HEAD_DIM // 4, 0, False), qscale)
            hy_c, q_c, k_c, v_c = _even_in(xc, norm_g[i, 0], cxm[0], cxm[1], w_in, qg, kg,
                                           _rope_tables(ncx, HEAD_DIM // 4, 0, True), qscale)
            rep = B_Q_HEADS // B_KV_HEADS
            att_l = _attention(q_l, [k_c, k_l], [v_c, v_l], rep, 1)
            mix_l = (_hyena(hy_l, e_hy_conv_w[j], e_hy_conv_b[j], filt, e_hy_fbias[j]), att_l)
            if not last:
                att_c = _attention(q_c, [k_c], [v_c], rep, 1)
                mix_c = (_hyena(hy_c, e_hy_conv_w[j], e_hy_conv_b[j], filt, e_hy_fbias[j]), att_c)
            w_out = e_w_out[j]
            split = A_WIDTH
        else:
            w1 = o_w_in[j]
            c0, c1, c2 = MLA_Q_RANK, MLA_Q_RANK + MLA_KV_RANK, MLA_Q_RANK + MLA_KV_RANK + MLA_ROPE
            w_kr = jnp.pad(w1[:, c1:c2], ((0, 0), (MLA_NOPE, LANES - MLA_NOPE - MLA_ROPE)))
            w_in = jnp.concatenate([w1[:, :c1], w_kr, w1[:, c2:]], axis=1).astype(BF16)
            wuq = _pad_heads(o_w_uq[j], MLA_HEADS, MLA_NOPE + MLA_ROPE).astype(BF16)
            wkv = o_w_ukv[j].reshape(MLA_KV_RANK, MLA_HEADS, MLA_NOPE + MLA_V)
            wuk = _pad_heads(wkv[:, :, :MLA_NOPE].reshape(MLA_KV_RANK, -1), MLA_HEADS, MLA_NOPE).astype(BF16)
            wuv = _pad_heads(wkv[:, :, MLA_NOPE:].reshape(MLA_KV_RANK, -1), MLA_HEADS, MLA_V).astype(BF16)
            qg = o_q_norm_g[j].reshape(1, MLA_Q_RANK)
            kvg = o_kv_norm_g[j].reshape(1, MLA_KV_RANK)
            args = (w_in, qg, kvg, wuq, wuk, wuv)
            qscale = (MLA_NOPE + MLA_ROPE) ** -0.5 * LOG2E
            u_l, k_l, v_l, q_l = _odd_in(x, norm_g[i, 0], lat[0], lat[1], *args,
                                         _rope_tables(n, MLA_ROPE // 4, MLA_NOPE, False), True, qscale)
            outs_c = _odd_in(xc, norm_g[i, 0], cxm[0], cxm[1], *args,
                             _rope_tables(ncx, MLA_ROPE // 4, MLA_NOPE, True), not last, qscale)
            u_c, k_c, v_c = outs_c[:3]
            att_l = _attention(q_l, [k_c, k_l], [v_c, v_l], 4, 4)
            mats = _s5_matrices(o_s5_a_re[j], o_s5_a_im[j], o_s5_log_dt[j], o_s5_b_re[j], o_s5_b_im[j],
                                o_s5_c_re[j], o_s5_c_im[j], o_s5_d[j])
            glu_w = o_glu_w[j].astype(BF16)
            mix_l = (att_l, _s5_mixer(u_c, u_l, mats, glu_w, o_glu_b[j]))
            if not last:
                raise NotImplementedError("an odd layer that is not the last needs the context S5 readout")
            w_out = o_w_out[j]
            split = MLA_HEADS * MLA_V

        wa, wb = w_out[:split].astype(BF16), w_out[split:].astype(BF16)
        router = jnp.pad(jnp.concatenate([moe_w_rg[i], moe_w_re[i]], axis=1),
                         ((0, 0), (0, LANES - N_GROUPS - N_EXPERTS)))
        rh = router.astype(BF16)
        rl = (router - rh.astype(F32)).astype(BF16)

        x, h_l, route_l = _proj_res(*mix_l, wa, wb, x, lat[2], norm_g[i, 1], lat[3], lat[4], rh, rl)
        streams = [(x, h_l, route_l, lat[5])]
        if not last:
            xc, h_c, route_c = _proj_res(*mix_c, wa, wb, xc, cxm[2], norm_g[i, 1], cxm[3], cxm[4], rh, rl)
            streams.append((xc, h_c, route_c, cxm[5]))
        outs = _moe(streams, i, moe_w_gate, moe_w_up, moe_w_down, final_g, last)
        x = outs[0]
        if not last:
            xc = outs[1]
    return x
```

```python
import functools
import math

import jax
import jax.numpy as jnp
from jax import lax
from jax.experimental import pallas as pl
from jax.experimental.pallas import tpu as pltpu

F32 = jnp.float32
BF16 = jnp.bfloat16
I32 = jnp.int32

D_MODEL = 1024
GRID_W = 64
HEAD_DIM = 64
A_WIDTH = D_MODEL // 2
B_Q_HEADS = (D_MODEL - A_WIDTH) // HEAD_DIM
B_KV_HEADS = max(1, B_Q_HEADS // 4)
HY_ORDER = 2
HY_BANDS = 16
HY_TARGET = 1e-2
HY_MAX_DECAY = math.log(1.0 / HY_TARGET) / 0.3
HY_MIN_DECAY = math.log(1.0 / HY_TARGET) / 1.5
MLA_HEADS = 8
MLA_NOPE = 64
MLA_ROPE = 32
MLA_V = 64
MLA_Q_RANK = D_MODEL // 4
MLA_KV_RANK = D_MODEL // 8
S5_WIDTH = D_MODEL - MLA_HEADS * MLA_V
S5_GROUP = 16
S5_GROUPS = S5_WIDTH // S5_GROUP
S5_STATE = 64
S5_CHUNK = 16
S5_TILE_GROUPS = 128 // S5_GROUP
N_GROUPS = 4
EXPERTS_PER_GROUP = 8
N_EXPERTS = N_GROUPS * EXPERTS_PER_GROUP
D_EXPERT = D_MODEL // 4
ROPE_THETA = 10000.0
EPS = 1e-6
LOG2E = 1.4426950408889634

LANES = 128
SUBLANES = 8
MOE_TILE = 256
ROW_CHAIN = 256
HIGHEST = lax.Precision.HIGHEST
MIB = 1024 * 1024


def _cparams(sem, vmem_mib):
    return pltpu.CompilerParams(dimension_semantics=sem, vmem_limit_bytes=vmem_mib * MIB)


def _dot(a, b):
    return jnp.dot(a, b, preferred_element_type=F32)


def _dot_nt(a, b):
    return lax.dot_general(a, b, (((1,), (1,)), ((), ())), preferred_element_type=F32)


def _split_bf16(a):
    hi = a.astype(BF16)
    lo = (a - hi.astype(F32)).astype(BF16)
    return hi, lo


def _dot3(ah, al, bh, bl):
    return _dot(ah, bh) + (_dot(ah, bl) + _dot(al, bh))


def _sigmoid(x):
    return 1.0 / (1.0 + jnp.exp(-x))


def _rms(x, g, width):
    ms = jnp.sum(x * x, axis=-1, keepdims=True) * (1.0 / width)
    return x * lax.rsqrt(ms + EPS) * g


def _modnorm(x, g, shift, scale):
    return _rms(x, g, x.shape[-1]) * (1.0 + scale) + shift


def _rope(t, cs, s1, s2, half):
    return t * cs + pltpu.roll(t, LANES - half, 1) * s1 + pltpu.roll(t, half, 1) * s2


def _store_token_tiles(ref, base_tok, val):
    n = val.shape[0]
    for s in range(SUBLANES):
        ref[pl.ds(base_tok * SUBLANES + s, n, stride=SUBLANES), :] = val[:, LANES * s: LANES * (s + 1)]


def _load_token_tiles(ref, base_tok, n):
    return jnp.concatenate([ref[pl.ds(base_tok * SUBLANES + s, n, stride=SUBLANES), :]
                            for s in range(SUBLANES)], axis=-1)


def _short_conv(u, w, b):
    n = u.shape[0]
    row = lax.broadcasted_iota(I32, u.shape, 0)
    prev = jnp.where(row == 0, 0.0, pltpu.roll(u, 1, 0))
    nxt = jnp.where(row == n - 1, 0.0, pltpu.roll(u, n - 1, 0))
    return prev * w[0:1] + u * w[1:2] + nxt * w[2:3] + b


def _mod_kernel(c_ref, w_ref, b_ref, o_ref):
    c = c_ref[...]
    s = c * _sigmoid(c)
    sh, sl = _split_bf16(s)
    wh, wl = _split_bf16(w_ref[0])
    o_ref[...] = _dot3(sh, sl, wh, wl) + b_ref[0]


def _modulation(cc, w, b, layer):
    r, d = cc.shape
    n = w.shape[2]
    tn = 1536
    return pl.pallas_call(
        _mod_kernel,
        grid=(n // tn,),
        in_specs=[pl.BlockSpec((r, d), lambda j: (0, 0)),
                  pl.BlockSpec((1, d, tn), lambda j: (layer, 0, j)),
                  pl.BlockSpec((1, 1, tn), lambda j: (layer, 0, j))],
        out_specs=pl.BlockSpec((r, tn), lambda j: (0, j)),
        out_shape=jax.ShapeDtypeStruct((r, n), F32),
        compiler_params=_cparams(("parallel",), 48),
        name="modulation",
    )(cc, w, b.reshape(b.shape[0], 1, n))


def _with_ones(v):
    lane = lax.broadcasted_iota(I32, v.shape, 1) % LANES
    return jnp.where(lane >= HEAD_DIM, 1.0, v)


def _row_chunks(tm):
    sub = min(ROW_CHAIN, tm)
    return [slice(r, r + sub) for r in range(0, tm, sub)]


def _even_in_kernel(x_ref, g_ref, sh_ref, sc_ref, w_ref, qg_ref, kg_ref, cs_ref, s1_ref, s2_ref,
                    hy_ref, q_ref, k_ref, v_ref, *, qscale):
    hw = 3 * A_WIDTH
    kw = hw + LANES * B_Q_HEADS
    vw = kw + LANES * B_KV_HEADS
    for rows in _row_chunks(x_ref.shape[1]):
        xn = _modnorm(x_ref[0, rows, :], g_ref[...], sh_ref[0], sc_ref[0]).astype(BF16)
        p = _dot(xn, w_ref[...])
        hy_ref[0, rows, :] = p[:, :hw].astype(BF16)
        cs, s1, s2 = cs_ref[rows, :], s1_ref[rows, :], s2_ref[rows, :]
        for h in range(B_Q_HEADS):
            t = _rms(p[:, hw + LANES * h: hw + LANES * (h + 1)], qg_ref[...], HEAD_DIM)
            q_ref[0, rows, LANES * h: LANES * (h + 1)] = (
                _rope(t, cs, s1, s2, HEAD_DIM // 4) * qscale).astype(BF16)
        for h in range(B_KV_HEADS):
            t = _rms(p[:, kw + LANES * h: kw + LANES * (h + 1)], kg_ref[...], HEAD_DIM)
            k_ref[0, rows, LANES * h: LANES * (h + 1)] = _rope(t, cs, s1, s2, HEAD_DIM // 4).astype(BF16)
        v_ref[0, rows, :] = _with_ones(p[:, vw:]).astype(BF16)


def _even_in(x, g, shift, scale, w, qg, kg, tabs, qscale):
    bsz, n, d = x.shape
    tm = min(512, n)
    nw = w.shape[1]
    row = lambda b, t: (b, t, 0)
    per_b = lambda b, t: (b, 0, 0)
    fixed = lambda b, t: (0, 0)
    tab = pl.BlockSpec((tm, LANES), lambda b, t: (t, 0))
    return pl.pallas_call(
        functools.partial(_even_in_kernel, qscale=qscale),
        grid=(bsz, n // tm),
        in_specs=[pl.BlockSpec((1, tm, d), row), pl.BlockSpec((1, d), fixed),
                  pl.BlockSpec((1, 1, d), per_b), pl.BlockSpec((1, 1, d), per_b),
                  pl.BlockSpec((d, nw), fixed), pl.BlockSpec((1, LANES), fixed),
                  pl.BlockSpec((1, LANES), fixed), tab, tab, tab],
        out_specs=[pl.BlockSpec((1, tm, 3 * A_WIDTH), row),
                   pl.BlockSpec((1, tm, LANES * B_Q_HEADS), row),
                   pl.BlockSpec((1, tm, LANES * B_KV_HEADS), row),
                   pl.BlockSpec((1, tm, LANES * B_KV_HEADS), row)],
        out_shape=[jax.ShapeDtypeStruct((bsz, n, 3 * A_WIDTH), BF16),
                   jax.ShapeDtypeStruct((bsz, n, LANES * B_Q_HEADS), BF16),
                   jax.ShapeDtypeStruct((bsz, n, LANES * B_KV_HEADS), BF16),
                   jax.ShapeDtypeStruct((bsz, n, LANES * B_KV_HEADS), BF16)],
        compiler_params=_cparams(("parallel", "parallel"), 48),
        name="even_in",
    )(x, g.reshape(1, d), shift, scale, w, qg, kg, *tabs)


def _odd_in_kernel(x_ref, g_ref, sh_ref, sc_ref, w_ref, qg_ref, kvg_ref, wuq_ref, wuk_ref, wuv_ref,
                   cs_ref, s1_ref, s2_ref, *out_refs, need_q, qscale):
    if need_q:
        u_ref, k_ref, v_ref, q_ref = out_refs
    else:
        u_ref, k_ref, v_ref = out_refs
    c0, c1, c2 = MLA_Q_RANK, MLA_Q_RANK + MLA_KV_RANK, MLA_Q_RANK + MLA_KV_RANK + LANES
    for rows in _row_chunks(x_ref.shape[1]):
        xn = _modnorm(x_ref[0, rows, :], g_ref[...], sh_ref[0], sc_ref[0]).astype(BF16)
        p = _dot(xn, w_ref[...])
        u_ref[0, rows, :] = p[:, c2:]
        cs, s1, s2 = cs_ref[rows, :], s1_ref[rows, :], s2_ref[rows, :]
        ckv = _rms(p[:, c0:c1], kvg_ref[...], MLA_KV_RANK).astype(BF16)
        kn = _dot(ckv, wuk_ref[...])
        v_ref[0, rows, :] = _with_ones(_dot(ckv, wuv_ref[...])).astype(BF16)
        kr = _rope(p[:, c1:c2], cs, s1, s2, MLA_ROPE // 4)
        for h in range(MLA_HEADS):
            k_ref[0, rows, LANES * h: LANES * (h + 1)] = (kn[:, LANES * h: LANES * (h + 1)] + kr).astype(BF16)
        if need_q:
            cq = _rms(p[:, :c0], qg_ref[...], MLA_Q_RANK).astype(BF16)
            q = _dot(cq, wuq_ref[...])
            for h in range(MLA_HEADS):
                t = q[:, LANES * h: LANES * (h + 1)]
                q_ref[0, rows, LANES * h: LANES * (h + 1)] = (
                    _rope(t, cs, s1, s2, MLA_ROPE // 4) * qscale).astype(BF16)


def _odd_in(x, g, shift, scale, w, qg, kvg, wuq, wuk, wuv, tabs, need_q, qscale):
    bsz, n, d = x.shape
    tm = min(512, n)
    hw = LANES * MLA_HEADS
    row = lambda b, t: (b, t, 0)
    per_b = lambda b, t: (b, 0, 0)
    fixed = lambda b, t: (0, 0)
    tab = pl.BlockSpec((tm, LANES), lambda b, t: (t, 0))
    widths = [S5_WIDTH, hw, hw] + ([hw] if need_q else [])
    return pl.pallas_call(
        functools.partial(_odd_in_kernel, need_q=need_q, qscale=qscale),
        grid=(bsz, n // tm),
        in_specs=[pl.BlockSpec((1, tm, d), row), pl.BlockSpec((1, d), fixed),
                  pl.BlockSpec((1, 1, d), per_b), pl.BlockSpec((1, 1, d), per_b),
                  pl.BlockSpec(w.shape, fixed), pl.BlockSpec((1, MLA_Q_RANK), fixed),
                  pl.BlockSpec((1, MLA_KV_RANK), fixed), pl.BlockSpec(wuq.shape, fixed),
                  pl.BlockSpec(wuk.shape, fixed), pl.BlockSpec(wuv.shape, fixed), tab, tab, tab],
        out_specs=[pl.BlockSpec((1, tm, wd), row) for wd in widths],
        out_shape=[jax.ShapeDtypeStruct((bsz, n, wd), F32 if k == 0 else BF16) for k, wd in enumerate(widths)],
        compiler_params=_cparams(("parallel", "parallel"), 48),
        name="odd_in_q" if need_q else "odd_in",
    )(x, g.reshape(1, d), shift, scale, w, qg, kvg, wuq, wuk, wuv, *tabs)


def _hy_fwd_kernel(wc_ref, ws_ref, u_ref, cw_ref, cb_ref, ka_ref, kb_ref, ya_ref, yb_ref, ubf_ref,
                   *, conv, tf):
    f = pl.program_id(1)

    @pl.when(f == 0)
    def _():
        u = u_ref[0].astype(F32)
        if conv:
            u = _short_conv(u, cw_ref[...], cb_ref[...])
        ubf_ref[...] = u.astype(BF16)

    ub = ubf_ref[...]
    a = _dot(wc_ref[...], ub)
    b = _dot(ws_ref[...], ub)
    ka, kb = ka_ref[...], kb_ref[...]
    first = (lax.broadcasted_iota(I32, a.shape, 0) + f * tf) == 0
    ya_ref[0] = jnp.where(first, a * ka, a * ka - b * kb).astype(BF16)
    yb_ref[0] = jnp.where(first, b * kb, a * kb + b * ka).astype(BF16)


def _hy_fwd(wc, ws, u, col, cw, cb, ka, kb, conv):
    bsz, n, _ = u.shape
    c = A_WIDTH
    tf = min(1024, n)
    return pl.pallas_call(
        functools.partial(_hy_fwd_kernel, conv=conv, tf=tf),
        grid=(bsz, n // tf),
        in_specs=[pl.BlockSpec((tf, n), lambda b, f: (f, 0)), pl.BlockSpec((tf, n), lambda b, f: (f, 0)),
                  pl.BlockSpec((1, n, c), lambda b, f: (b, 0, col)),
                  pl.BlockSpec((3, c), lambda b, f: (0, 0)), pl.BlockSpec((1, c), lambda b, f: (0, 0)),
                  pl.BlockSpec((tf, c), lambda b, f: (f, 0)), pl.BlockSpec((tf, c), lambda b, f: (f, 0))],
        out_specs=[pl.BlockSpec((1, tf, c), lambda b, f: (b, f, 0))] * 2,
        out_shape=[jax.ShapeDtypeStruct((bsz, n, c), BF16)] * 2,
        scratch_shapes=[pltpu.VMEM((n, c), BF16)],
        compiler_params=_cparams(("parallel", "arbitrary"), 56),
        name="hyena_fwd",
    )(wc, ws, u, cw, cb, ka, kb)


def _hy_inv_kernel(ci_ref, si_ref, ya_ref, yb_ref, a_ref, g_ref, cwa_ref, cba_ref, cwg_ref, cbg_ref,
                   bias_ref, o_ref, a_sc, g_sc, *, conv_a, tt):
    t = pl.program_id(1)

    @pl.when(t == 0)
    def _():
        a = a_ref[0].astype(F32)
        if conv_a:
            a = _short_conv(a, cwa_ref[...], cba_ref[...])
        a_sc[...] = a
        g_sc[...] = _short_conv(g_ref[0].astype(F32), cwg_ref[...], cbg_ref[...])

    y = _dot(ci_ref[...], ya_ref[0]) + _dot(si_ref[...], yb_ref[0])
    rows = pl.ds(pl.multiple_of(t * tt, tt), tt)
    y = y + a_sc[rows, :] * bias_ref[...]
    o_ref[0] = (g_sc[rows, :] * y).astype(BF16)


def _hy_inv(ci, si, ya, yb, a, a_col, g, g_col, cwa, cba, cwg, cbg, bias, conv_a):
    bsz, n, c = ya.shape
    tt = min(512, n)
    whole = lambda b, t: (b, 0, 0)
    fixed = lambda b, t: (0, 0)
    return pl.pallas_call(
        functools.partial(_hy_inv_kernel, conv_a=conv_a, tt=tt),
        grid=(bsz, n // tt),
        in_specs=[pl.BlockSpec((tt, n), lambda b, t: (t, 0)), pl.BlockSpec((tt, n), lambda b, t: (t, 0)),
                  pl.BlockSpec((1, n, c), whole), pl.BlockSpec((1, n, c), whole),
                  pl.BlockSpec((1, n, c), lambda b, t: (b, 0, a_col)),
                  pl.BlockSpec((1, n, c), lambda b, t: (b, 0, g_col)),
                  pl.BlockSpec((3, c), fixed), pl.BlockSpec((1, c), fixed),
                  pl.BlockSpec((3, c), fixed), pl.BlockSpec((1, c), fixed), pl.BlockSpec((1, c), fixed)],
        out_specs=pl.BlockSpec((1, tt, c), lambda b, t: (b, t, 0)),
        out_shape=jax.ShapeDtypeStruct((bsz, n, c), BF16),
        scratch_shapes=[pltpu.VMEM((n, c), F32), pltpu.VMEM((n, c), F32)],
        compiler_params=_cparams(("parallel", "arbitrary"), 56),
        name="hyena_inv",
    )(ci, si, ya, yb, a, g, cwa, cba, cwg, cbg, bias)


def _dft_mats(n):
    big = 2 * n
    f = jnp.arange(n, dtype=I32)[:, None]
    s = jnp.arange(n, dtype=I32)[None, :]
    step = 64 if n % 64 == 0 else 1
    angle = lambda fr: ((fr[:, None] * s) % big).astype(F32) * (2.0 * math.pi / big)
    a_hi = angle(jnp.arange(n // step, dtype=I32) * step)
    a_lo = angle(jnp.arange(step, dtype=I32))
    c1, s1, c2, s2 = jnp.cos(a_hi)[:, None], jnp.sin(a_hi)[:, None], jnp.cos(a_lo)[None], jnp.sin(a_lo)[None]
    cos = (c1 * c2 - s1 * s2).reshape(n, n)
    sin = (s1 * c2 + c1 * s2).reshape(n, n)
    alt = jnp.where(jnp.arange(n) % 2 == 0, 1.0, -1.0).astype(F32)
    wc = cos
    ws = jnp.where(f == 0, alt[None, :], -sin)
    ci = jnp.where(s == 0, 1.0 / big, (2.0 / big) * cos)
    si = jnp.where(s == 0, alt[:, None] / big, -(2.0 / big) * sin)
    return (wc, ws), (wc.astype(BF16), ws.astype(BF16), ci.astype(BF16), si.astype(BF16))


def _spectrum_kernel(wc_ref, ws_ref, sh_ref, sl_ref, dh_ref, dl_ref, nyq_ref, nrm_ref, ka_ref, kb_ref, *, tf):
    ch, cl = _split_bf16(wc_ref[...])
    sh, sl = _split_bf16(ws_ref[...])
    a = _dot3(ch, cl, sh_ref[...], sl_ref[...])
    b = _dot3(sh, sl, dh_ref[...], dl_ref[...])
    first = (lax.broadcasted_iota(I32, a.shape, 0) + pl.program_id(1) * tf) == 0
    ka_ref[...] = a * nrm_ref[...]
    kb_ref[...] = jnp.where(first, nyq_ref[...], b) * nrm_ref[...]


def _filter_spectrum(tables, fwd, bwd):
    wc, ws = tables
    n, c = fwd.shape
    bwd = bwd.at[0].set(0.0)
    nrm = lax.rsqrt(jnp.sum(fwd * fwd, axis=0) + jnp.sum(bwd * bwd, axis=0) + EPS).reshape(1, c)
    s, d = fwd + bwd, fwd - bwd
    alt = jnp.where(jnp.arange(n) % 2 == 0, 1.0, -1.0).astype(F32)
    nyq = jnp.sum(alt[:, None] * s, axis=0).reshape(1, c)
    sh, sl = _split_bf16(s)
    dh, dl = _split_bf16(d)
    tf = min(256, n)
    tc = A_WIDTH
    tab = pl.BlockSpec((tf, n), lambda j, f: (f, 0))
    tap = pl.BlockSpec((n, tc), lambda j, f: (0, j))
    vec = pl.BlockSpec((1, tc), lambda j, f: (0, j))
    out = pl.BlockSpec((tf, tc), lambda j, f: (f, j))
    return pl.pallas_call(
        functools.partial(_spectrum_kernel, tf=tf),
        grid=(c // tc, n // tf),
        in_specs=[tab, tab, tap, tap, tap, tap, vec, vec],
        out_specs=[out, out],
        out_shape=[jax.ShapeDtypeStruct((n, c), F32)] * 2,
        compiler_params=_cparams(("parallel", "parallel"), 48),
        name="hyena_filter_spectrum",
    )(wc, ws, sh, sl, dh, dl, nyq, nrm)


def _hyena_filter_spectrum(n, tables, w1, b1, w2, b2, w3, freq):
    t = jnp.arange(n, dtype=F32)
    t_norm = t / max(n - 1, 1)
    bands = jnp.linspace(1e-4, HY_BANDS - 1, HY_BANDS, dtype=F32)
    ang = (2.0 * math.pi / n) * t[:, None] * bands
    z = jnp.concatenate([t_norm[:, None], jnp.cos(ang), jnp.sin(ang)], axis=-1)
    hdn = jnp.sin(freq * (jnp.dot(z, w1, precision=HIGHEST) + b1))
    hdn = jnp.sin(freq * (jnp.dot(hdn, w2, precision=HIGHEST) + b2))
    taps = jnp.dot(hdn, w3, precision=HIGHEST).reshape(n, 2, HY_ORDER, A_WIDTH)
    deltas = jnp.linspace(HY_MIN_DECAY, HY_MAX_DECAY, A_WIDTH, dtype=F32)
    taps = taps * jnp.exp(-t_norm[:, None] * deltas)[:, None, None, :]
    width = HY_ORDER * A_WIDTH
    ka, kb = _filter_spectrum(tables, taps[:, 0].reshape(n, width), taps[:, 1].reshape(n, width))
    return ka.reshape(n, HY_ORDER, A_WIDTH), kb.reshape(n, HY_ORDER, A_WIDTH)


def _hyena(hy, cw, cb, filt, fbias):
    n = hy.shape[1]
    c = A_WIDTH
    tables, (wc, ws, ci, si) = _dft_mats(n)
    ka, kb = _hyena_filter_spectrum(n, tables, *filt)
    cws = [cw[:, i * c:(i + 1) * c] for i in range(3)]
    cbs = [cb[i * c:(i + 1) * c].reshape(1, c) for i in range(3)]
    ya, yb = _hy_fwd(wc, ws, hy, 2, cws[2], cbs[2], ka[:, 0], kb[:, 0], True)
    z = _hy_inv(ci, si, ya, yb, hy, 2, hy, 0, cws[2], cbs[2], cws[0], cbs[0], fbias[0:1], True)
    ya, yb = _hy_fwd(wc, ws, z, 0, cws[2], cbs[2], ka[:, 1], kb[:, 1], False)
    return _hy_inv(ci, si, ya, yb, z, 0, hy, 1, cws[2], cbs[2], cws[1], cbs[1], fbias[1:2], False)


def _attn_kernel(*refs, nseg, hps, kps):
    q_ref = refs[0]
    k_refs = refs[1:1 + nseg]
    v_refs = refs[1 + nseg:1 + 2 * nseg]
    o_ref = refs[1 + 2 * nseg]
    low = lax.broadcasted_iota(I32, (q_ref.shape[1], LANES), 1) < HEAD_DIM
    for pair in range(hps // 2):
        accs = []
        for half in range(2):
            i = 2 * pair + half
            kv = (i * kps) // hps
            q = q_ref[0, :, LANES * i: LANES * (i + 1)]
            ss = [_dot_nt(q, k[0, :, LANES * kv: LANES * (kv + 1)]) for k in k_refs]
            m = ss[0].max(axis=-1, keepdims=True)
            for s in ss[1:]:
                m = jnp.maximum(m, s.max(axis=-1, keepdims=True))
            acc = None
            for s, v in zip(ss, v_refs):
                p = jnp.exp2((s - m).astype(BF16))
                o = _dot(p, v[0, :, LANES * kv: LANES * (kv + 1)])
                acc = o if acc is None else acc + o
            accs.append(acc)
        num = jnp.where(low, accs[0], pltpu.roll(accs[1], HEAD_DIM, 1))
        den = jnp.where(low, pltpu.roll(accs[0], HEAD_DIM, 1), accs[1])
        o_ref[0, :, LANES * pair: LANES * (pair + 1)] = (num / den).astype(BF16)


def _attention(q, ks, vs, hps, kps):
    bsz, lq, qw = q.shape
    nh = qw // LANES
    tq = min(512, lq)
    nseg = len(ks)
    kv_spec = lambda a: pl.BlockSpec((1, a.shape[1], kps * LANES), lambda b, g, t: (b, 0, g))
    return pl.pallas_call(
        functools.partial(_attn_kernel, nseg=nseg, hps=hps, kps=kps),
        grid=(bsz, nh // hps, lq // tq),
        in_specs=[pl.BlockSpec((1, tq, hps * LANES), lambda b, g, t: (b, t, g))]
        + [kv_spec(a) for a in ks] + [kv_spec(a) for a in vs],
        out_specs=pl.BlockSpec((1, tq, hps * HEAD_DIM), lambda b, g, t: (b, t, g)),
        out_shape=jax.ShapeDtypeStruct((bsz, lq, nh * HEAD_DIM), BF16),
        compiler_params=_cparams(("parallel", "parallel", "parallel"), 56),
        name="attention",
    )(q, *ks, *vs)


def _onehot_bf16(cond):
    return jnp.where(cond, 1.0, 0.0).astype(BF16)


def _s5_kernel(uc_ref, ul_ref, w_ref, a_ref, of_ref, or_ref, y_ref, xs_sc, yi_sc, s_sc, x_sc, ys_sc,
               *, bb, ncc, ncl):
    lc, gh, gt = S5_CHUNK, S5_GROUP, S5_TILE_GROUPS
    nc = ncc + ncl
    cw = lc * gh
    hb = gh.bit_length() - 1
    lb = LANES.bit_length() - 1

    for b in range(bb):
        for j in range(lc):
            p, half = divmod(j, 2)
            lanes = slice(half * LANES, (half + 1) * LANES)
            xs_sc[p, b * nc: b * nc + ncc, lanes] = uc_ref[b, pl.ds(j, ncc, stride=lc), :].astype(BF16)
            xs_sc[p, b * nc + ncc: (b + 1) * nc, lanes] = ul_ref[b, pl.ds(j, ncl, stride=lc), :].astype(BF16)

    r_i = lax.broadcasted_iota(I32, (2 * LANES, cw), 0)
    c_i = lax.broadcasted_iota(I32, (2 * LANES, cw), 1)
    same_h = (r_i & (gh - 1)) == (c_i & (gh - 1))
    r_grp = (r_i & (LANES - 1)) >> hb
    c_off = (c_i >> hb) - (r_i >> lb)
    for pr in range(gt // 2):
        rs = []
        for side in range(2):
            q = 2 * pr + side
            u = None
            for p in range(lc // 2):
                sel = _onehot_bf16(same_h & (r_grp == q) & (c_off == 2 * p))
                d = _dot(xs_sc[p], sel)
                u = d if u is None else u + d
            r = _dot(u.astype(BF16), w_ref[q])
            yi_sc[q] = r[:, :cw]
            rs.append(r)
        for kind in range(4):
            lo = cw + kind * LANES
            s_sc[kind, pr] = rs[0][:, lo:lo + LANES] + rs[1][:, lo:lo + LANES]

    a = a_ref[0]
    npair = gt // 2

    def advance(st, c, d):
        rws = pl.ds(c, bb, stride=nc)
        new = []
        for t in range(npair):
            xr, xi = st[2 * t], st[2 * t + 1]
            ar = a[2 * d:2 * d + 1, t * LANES:(t + 1) * LANES]
            ai = a[2 * d + 1:2 * d + 2, t * LANES:(t + 1) * LANES]
            x_sc[2 * d, t, rws, :] = xr
            x_sc[2 * d + 1, t, rws, :] = xi
            new.append(ar * xr - ai * xi + s_sc[2 * d, t, rws, :])
            new.append(ar * xi + ai * xr + s_sc[2 * d + 1, t, rws, :])
        return tuple(new)

    zero = tuple(jnp.zeros((bb, LANES), F32) for _ in range(2 * npair))
    lax.fori_loop(0, nc, lambda c, st: advance(st, c, 0), zero, unroll=2)
    st = lax.fori_loop(0, ncc, lambda i, st: advance(st, ncc - 1 - i, 1), zero, unroll=2)
    lax.fori_loop(0, ncl, lambda i, st: advance(st, nc - 1 - i, 1), st, unroll=2)

    def latent(ref, *idx):
        return jnp.concatenate([ref[idx + (slice(b * nc + ncc, (b + 1) * nc), slice(None))]
                                for b in range(bb)], axis=0)

    for q in range(gt):
        y = latent(yi_sc, q)
        for d, o_ref in ((0, of_ref), (1, or_ref)):
            state = jnp.concatenate([latent(x_sc, 2 * d, q // 2), latent(x_sc, 2 * d + 1, q // 2)], axis=1)
            y = y + _dot(state.astype(BF16), o_ref[q])
        for kt in range(cw // LANES):
            ys_sc[kt, :, q * LANES:(q + 1) * LANES] = y[:, kt * LANES:(kt + 1) * LANES].astype(BF16)

    r2 = lax.broadcasted_iota(I32, (gt * LANES, LANES), 0)
    c2 = lax.broadcasted_iota(I32, (gt * LANES, LANES), 1)
    keep = ((r2 & (gh - 1)) == (c2 & (gh - 1))) & ((r2 >> lb) == (c2 >> hb))
    r_k = (r2 & (LANES - 1)) >> hb
    for k in range(0, lc, 2):
        kt, kk = divmod(k, gt)
        sel = jnp.concatenate([_onehot_bf16(keep & (r_k == kk)), _onehot_bf16(keep & (r_k == kk + 1))], axis=1)
        z = _dot(ys_sc[kt], sel)
        for b in range(bb):
            y_ref[b, pl.ds(k, ncl, stride=lc), :] = z[b * ncl:(b + 1) * ncl, :LANES]
            y_ref[b, pl.ds(k + 1, ncl, stride=lc), :] = z[b * ncl:(b + 1) * ncl, LANES:]


def _s5_scan(u_c, u_l, w, a_tab, of, orr):
    bsz, ncx, c = u_c.shape
    n = u_l.shape[1]
    lc, gt = S5_CHUNK, S5_TILE_GROUPS
    ncc, ncl = ncx // lc, n // lc
    bb = min(4, bsz)
    rows = bb * (ncc + ncl)
    cw = lc * S5_GROUP
    return pl.pallas_call(
        functools.partial(_s5_kernel, bb=bb, ncc=ncc, ncl=ncl),
        grid=(c // LANES, bsz // bb),
        in_specs=[pl.BlockSpec((bb, ncx, LANES), lambda t, i: (i, 0, t)),
                  pl.BlockSpec((bb, n, LANES), lambda t, i: (i, 0, t)),
                  pl.BlockSpec((gt,) + w.shape[1:], lambda t, i: (t, 0, 0)),
                  pl.BlockSpec((1,) + a_tab.shape[1:], lambda t, i: (t, 0, 0)),
                  pl.BlockSpec((gt,) + of.shape[1:], lambda t, i: (t, 0, 0)),
                  pl.BlockSpec((gt,) + orr.shape[1:], lambda t, i: (t, 0, 0))],
        out_specs=pl.BlockSpec((bb, n, LANES), lambda t, i: (i, 0, t)),
        out_shape=jax.ShapeDtypeStruct((bsz, n, c), F32),
        scratch_shapes=[pltpu.VMEM((lc // 2, rows, 2 * LANES), BF16), pltpu.VMEM((gt, rows, cw), F32),
                        pltpu.VMEM((4, gt // 2, rows, LANES), F32), pltpu.VMEM((4, gt // 2, rows, LANES), F32),
                        pltpu.VMEM((cw // LANES, bb * ncl, gt * LANES), BF16)],
        compiler_params=_cparams(("parallel", "parallel"), 56),
        name="s5_scan",
    )(u_c, u_l, w, a_tab, of, orr)


def _s5_matrices(a_re, a_im, log_dt, b_re, b_im, c_re, c_im, d_skip):
    lc, h, p, g = S5_CHUNK, S5_GROUP, S5_STATE, S5_GROUPS
    dt = jnp.exp(log_dt)[..., None]
    lam_re, lam_im = a_re * dt, a_im * dt
    mag = jnp.exp(lam_re)
    er, ei = mag * jnp.cos(lam_im) - 1.0, mag * jnp.sin(lam_im)
    den = a_re * a_re + a_im * a_im
    co_re = (er * a_re + ei * a_im) / den
    co_im = (ei * a_re - er * a_im) / den
    bb_re = co_re[..., None] * b_re - co_im[..., None] * b_im
    bb_im = co_re[..., None] * b_im + co_im[..., None] * b_re
    m = jnp.arange(lc + 1, dtype=F32)[:, None, None, None]
    pw_mag = jnp.exp(m * lam_re[None])
    pw_re, pw_im = pw_mag * jnp.cos(m * lam_im[None]), pw_mag * jnp.sin(m * lam_im[None])
    ab_re = pw_re[..., None] * bb_re[None] - pw_im[..., None] * bb_im[None]
    ab_im = pw_re[..., None] * bb_im[None] + pw_im[..., None] * bb_re[None]
    kk = (jnp.einsum("dgop,mdgpi->mdgoi", c_re, ab_re[:lc], precision=HIGHEST)
          - jnp.einsum("dgop,mdgpi->mdgoi", c_im, ab_im[:lc], precision=HIGHEST))
    cw = lc * h

    def shifted_rows(x):
        period = 2 * cw + h
        z = jnp.tile(x, (1, 1, lc))[:, :, :lc * (period - h)]
        return jnp.swapaxes(z.reshape(g, h, lc, period - h)[..., cw:], 1, 2)

    resp = jnp.transpose(kk, (1, 2, 4, 0, 3))
    zeros = lambda w: jnp.zeros((g, h, w), F32)
    fwd = resp[0].reshape(g, h, cw)
    rev = resp[1][:, :, ::-1].reshape(g, h, cw)
    tf = shifted_rows(jnp.concatenate([zeros(cw), fwd, zeros(h)], axis=-1))
    tr = shifted_rows(jnp.concatenate([zeros(h), rev, zeros(cw)], axis=-1))
    lane = jnp.arange(cw, dtype=I32)[None, None, None, :]
    diag = (jnp.arange(lc, dtype=I32)[None, :, None, None] * h + jnp.arange(h, dtype=I32)[None, None, :, None])
    dsk = d_skip.reshape(g, 1, h, 1)
    tw = (tf + tr + jnp.where(lane == diag, dsk, 0.0)).reshape(g, cw, cw)
    idx_f = slice(lc - 1, None, -1)
    idx_r = slice(0, lc)

    odd = (jnp.arange(g) % 2 == 1)

    def place(t, axis):
        z = jnp.zeros_like(t)
        shape = [g] + [1] * (t.ndim - 1)
        return jnp.where(odd.reshape(shape), jnp.concatenate([z, t], axis=axis),
                         jnp.concatenate([t, z], axis=axis))

    def to_state(idx, d):
        re = jnp.transpose(ab_re[idx, d], (1, 0, 3, 2)).reshape(g, lc * h, p)
        im = jnp.transpose(ab_im[idx, d], (1, 0, 3, 2)).reshape(g, lc * h, p)
        return [place(re, 2), place(im, 2)]

    w = jnp.concatenate([tw] + to_state(idx_f, 0) + to_state(idx_r, 1), axis=-1)

    def from_state(idx, d):
        pr, pi = pw_re[idx, d], pw_im[idx, d]
        cr, cim = c_re[d], c_im[d]
        mr = cr[None] * pr[:, :, None, :] - cim[None] * pi[:, :, None, :]
        mi = cr[None] * pi[:, :, None, :] + cim[None] * pr[:, :, None, :]
        rows = lambda m: place(jnp.transpose(m, (1, 3, 0, 2)).reshape(g, p, lc * h), 1)
        return jnp.concatenate([rows(mr), rows(-mi)], axis=1)

    of = from_state(slice(1, lc + 1), 0)
    orr = from_state(slice(lc, 0, -1), 1)
    tiles = g // S5_TILE_GROUPS
    a_tab = jnp.stack([pw_re[lc, 0].reshape(tiles, -1), pw_im[lc, 0].reshape(tiles, -1),
                       pw_re[lc, 1].reshape(tiles, -1), pw_im[lc, 1].reshape(tiles, -1)], axis=1)
    return w.astype(BF16), a_tab.astype(F32), of.astype(BF16), orr.astype(BF16)


def _glu_kernel(y_ref, w_ref, b_ref, o_ref):
    y = y_ref[0]
    inner = 0.7978845608028654 * (y + 0.044715 * (y * y * y))
    gl = 0.5 * y * (1.0 + jnp.tanh(inner))
    o_ref[0] = (gl * _sigmoid(_dot(gl.astype(BF16), w_ref[...]) + b_ref[...])).astype(BF16)


def _glu(y, w, b):
    bsz, n, c = y.shape
    tm = min(512, n)
    return pl.pallas_call(
        _glu_kernel,
        grid=(bsz, n // tm),
        in_specs=[pl.BlockSpec((1, tm, c), lambda i, t: (i, t, 0)),
                  pl.BlockSpec((c, c), lambda i, t: (0, 0)), pl.BlockSpec((1, c), lambda i, t: (0, 0))],
        out_specs=pl.BlockSpec((1, tm, c), lambda i, t: (i, t, 0)),
        out_shape=jax.ShapeDtypeStruct((bsz, n, c), BF16),
        compiler_params=_cparams(("parallel", "parallel"), 32),
        name="s5_glu",
    )(y, w, b.reshape(1, c))


def _s5_mixer(u_c, u_l, mats, glu_w, glu_b):
    return _glu(_s5_scan(u_c, u_l, *mats), glu_w, glu_b)


def _route(lg):
    lane = lax.broadcasted_iota(I32, lg.shape, 1).astype(F32)
    neg = jnp.float32(-1e30)
    far = jnp.float32(LANES)
    is_g = lane < N_GROUPS
    lgm = jnp.where(is_g, lg, neg)
    m = lgm.max(axis=-1, keepdims=True)
    gidx = jnp.where(lgm == m, lane, far).min(axis=-1, keepdims=True)
    gp = 1.0 / jnp.where(is_g, jnp.exp(lgm - m), 0.0).sum(axis=-1, keepdims=True)
    lo = N_GROUPS + EXPERTS_PER_GROUP * gidx
    is_e = (lane >= lo) & (lane < lo + EXPERTS_PER_GROUP)
    lem = jnp.where(is_e, lg, neg)
    l0 = lem.max(axis=-1, keepdims=True)
    i0 = jnp.where((lem == l0) & is_e, lane, far).min(axis=-1, keepdims=True)
    lem1 = jnp.where(lane == i0, neg, lem)
    l1 = lem1.max(axis=-1, keepdims=True)
    i1 = jnp.where((lem1 == l1) & is_e & (lane != i0), lane, far).min(axis=-1, keepdims=True)
    e1 = jnp.exp(l1 - l0)
    w0 = 1.0 / (1.0 + e1)
    w1 = e1 * w0
    out = jnp.where(lane == 0, i0 - N_GROUPS, 0.0)
    out = jnp.where(lane == 1, i1 - N_GROUPS, out)
    out = jnp.where(lane == 2, gp * w0, out)
    return jnp.where(lane == 3, gp * w1, out)


def _proj_res_kernel(a_ref, b_ref, wa_ref, wb_ref, x_ref, g1_ref, ng_ref, sh_ref, sc_ref, rh_ref, rl_ref,
                     xo_ref, h_ref, route_ref):
    for rows in _row_chunks(x_ref.shape[1]):
        o = _dot(a_ref[0, rows, :], wa_ref[...]) + _dot(b_ref[0, rows, :], wb_ref[...])
        x = x_ref[0, rows, :] + g1_ref[0] * o
        xo_ref[0, rows, :] = x
        h = _modnorm(x, ng_ref[...], sh_ref[0], sc_ref[0])
        _store_token_tiles(h_ref, rows.start, h)
        hh, hl = _split_bf16(h)
        route_ref[0, rows, :] = _route(_dot3(hh, hl, rh_ref[...], rl_ref[...]))


def _proj_res(a, b, wa, wb, x, g1, ng, shift, scale, rh, rl):
    bsz, n, d = x.shape
    tm = min(512, n)
    row = lambda i, t: (i, t, 0)
    per_b = lambda i, t: (i, 0, 0)
    fixed = lambda i, t: (0, 0)
    return pl.pallas_call(
        _proj_res_kernel,
        grid=(bsz, n // tm),
        in_specs=[pl.BlockSpec((1, tm, a.shape[2]), row), pl.BlockSpec((1, tm, b.shape[2]), row),
                  pl.BlockSpec(wa.shape, fixed), pl.BlockSpec(wb.shape, fixed),
                  pl.BlockSpec((1, tm, d), row), pl.BlockSpec((1, 1, d), per_b), pl.BlockSpec((1, d), fixed),
                  pl.BlockSpec((1, 1, d), per_b), pl.BlockSpec((1, 1, d), per_b),
                  pl.BlockSpec((d, LANES), fixed), pl.BlockSpec((d, LANES), fixed)],
        out_specs=[pl.BlockSpec((1, tm, d), row),
                   pl.BlockSpec((tm * SUBLANES, LANES), lambda i, t: (i * (n // tm) + t, 0)),
                   pl.BlockSpec((1, tm, LANES), row)],
        out_shape=[jax.ShapeDtypeStruct((bsz, n, d), F32),
                   jax.ShapeDtypeStruct((bsz * n * SUBLANES, LANES), F32),
                   jax.ShapeDtypeStruct((bsz, n, LANES), F32)],
        compiler_params=_cparams(("parallel", "parallel"), 40),
        name="proj_res",
    )(a, b, wa, wb, x, g1, ng.reshape(1, d), shift, scale, rh, rl)


def _token_copy(src, src_tok, dst, dst_tok, sem):
    rows = lambda t: pl.ds(pl.multiple_of(t * SUBLANES, SUBLANES), SUBLANES)
    return pltpu.make_async_copy(src.at[rows(src_tok)], dst.at[rows(dst_tok)], sem)


def _tile_copy(src, src_tok, dst, dst_tok, sem):
    n = MOE_TILE * SUBLANES
    rows = lambda t: pl.ds(pl.multiple_of(t * SUBLANES, SUBLANES), n)
    return pltpu.make_async_copy(src.at[rows(src_tok)], dst.at[rows(dst_tok)], sem)


def _experts_kernel(te_ref, ts_ref, tr_ref, tp_ref, win_ref, winn_ref, h_ref, wg_ref, wu_ref, wd_ref, ys_ref,
                    xbuf, obuf, gsem, ssem, *, n_assign):
    del te_ref
    i = pl.program_id(0)
    nt = pl.num_programs(0)
    slot = i % 2
    other = 1 - slot
    nxt = jnp.minimum(i + 1, nt - 1)
    within = lambda start: start & (MOE_TILE - 1)

    def start_gather(w_ref, off, s):
        def body(j, c):
            for k in range(2):
                r = 2 * j + k
                tok = lax.shift_right_logical(w_ref[0, 0, off + r], 1)
                _token_copy(h_ref, tok, xbuf, s * MOE_TILE + r, gsem.at[s]).start(priority=k)
            return c
        lax.fori_loop(0, MOE_TILE // 2, body, 0, unroll=8)

    @pl.when(i == 0)
    def _():
        start_gather(win_ref, within(ts_ref[0]), 0)

    @pl.when(i + 1 < nt)
    def _():
        start_gather(winn_ref, within(ts_ref[nxt]), other)

    base = slot * MOE_TILE
    _tile_copy(h_ref, 0, xbuf, base, gsem.at[slot]).wait()

    @pl.when(i >= 2)
    def _():
        _tile_copy(obuf, base, ys_ref, 0, ssem.at[slot]).wait()

    x = _load_token_tiles(xbuf, base, MOE_TILE).astype(BF16)
    gt = _dot(x, wg_ref[0, 0].astype(BF16))
    hid = (gt * _sigmoid(gt)) * _dot(x, wu_ref[0, 0].astype(BF16))
    _store_token_tiles(obuf, base, _dot(hid.astype(BF16), wd_ref[0, 0].astype(BF16)))

    off, real = within(ts_ref[i]), tr_ref[i]
    spare = n_assign + tp_ref[i] - real

    def scatter(j, c):
        for k in range(2):
            r = 2 * j + k
            dst = jnp.where(r < real, win_ref[0, 0, off + r], spare + r)
            _token_copy(obuf, base + r, ys_ref, dst, ssem.at[slot]).start(priority=k)
        return c
    lax.fori_loop(0, MOE_TILE // 2, scatter, 0, unroll=8)

    @pl.when(i == nt - 1)
    def _():
        _tile_copy(obuf, base, ys_ref, 0, ssem.at[slot]).wait()
        _tile_copy(obuf, other * MOE_TILE, ys_ref, 0, ssem.at[other]).wait()


def _experts(plan, h, n_assign, layer, wg, wu, wd):
    windows, tile_expert, tile_start, tile_real, tile_padbase = plan
    nt = tile_expert.shape[0]
    d = D_MODEL
    wmap = lambda i, te, ts, tr, tp: (layer, te[i], 0, 0)
    log_tile = MOE_TILE.bit_length() - 1
    smem = lambda f: pl.BlockSpec((1, 1, 2 * MOE_TILE), f, memory_space=pltpu.SMEM)
    tile_rows = MOE_TILE * SUBLANES
    return pl.pallas_call(
        functools.partial(_experts_kernel, n_assign=n_assign),
        grid_spec=pltpu.PrefetchScalarGridSpec(
            num_scalar_prefetch=4, grid=(nt,),
            in_specs=[smem(lambda i, te, ts, tr, tp: (lax.shift_right_logical(ts[i], log_tile), 0, 0)),
                      smem(lambda i, te, ts, tr, tp: (
                          lax.shift_right_logical(ts[jnp.minimum(i + 1, nt - 1)], log_tile), 0, 0)),
                      pl.BlockSpec(memory_space=pl.ANY),
                      pl.BlockSpec((1, 1, d, D_EXPERT), wmap), pl.BlockSpec((1, 1, d, D_EXPERT), wmap),
                      pl.BlockSpec((1, 1, D_EXPERT, d), wmap)],
            out_specs=pl.BlockSpec(memory_space=pl.ANY),
            scratch_shapes=[pltpu.VMEM((2 * tile_rows, LANES), F32), pltpu.VMEM((2 * tile_rows, LANES), F32),
                            pltpu.SemaphoreType.DMA((2,)), pltpu.SemaphoreType.DMA((2,))]),
        out_shape=jax.ShapeDtypeStruct((nt * tile_rows, LANES), F32),
        compiler_params=pltpu.CompilerParams(dimension_semantics=("arbitrary",), vmem_limit_bytes=40 * MIB,
                                             has_side_effects=True),
        name="moe_experts",
    )(tile_expert, tile_start, tile_real, tile_padbase, windows[:, None, :], windows[:, None, :], h, wg, wu, wd)


def _combine_kernel(route_ref, x_ref, g2_ref, fg_ref, ys_ref, o_ref, *, tc, final):
    rt = route_ref[0]
    g0, g1 = rt[:, 2:3], rt[:, 3:4]
    cols = [g0 * ys_ref[pl.ds(s, tc, stride=2 * SUBLANES), :]
            + g1 * ys_ref[pl.ds(SUBLANES + s, tc, stride=2 * SUBLANES), :] for s in range(SUBLANES)]
    x = x_ref[0] + g2_ref[0] * jnp.concatenate(cols, axis=-1)
    if final:
        x = _rms(x, fg_ref[...], x.shape[-1])
    o_ref[0] = x


def _combine(route, x, g2, fg, ys, tok_off, final):
    bsz, n, d = x.shape
    tc = min(256, n)
    nt = n // tc
    row = lambda i, t: (i, t, 0)
    blk = tok_off // tc
    return pl.pallas_call(
        functools.partial(_combine_kernel, tc=tc, final=final),
        grid=(bsz, nt),
        in_specs=[pl.BlockSpec((1, tc, LANES), row), pl.BlockSpec((1, tc, d), row),
                  pl.BlockSpec((1, 1, d), lambda i, t: (i, 0, 0)), pl.BlockSpec((1, d), lambda i, t: (0, 0)),
                  pl.BlockSpec((2 * tc * SUBLANES, LANES), lambda i, t: (blk + i * nt + t, 0))],
        out_specs=pl.BlockSpec((1, tc, d), row),
        out_shape=jax.ShapeDtypeStruct((bsz, n, d), F32),
        compiler_params=_cparams(("parallel", "parallel"), 32),
        name="moe_combine",
    )(route, x, g2, fg.reshape(1, d), ys)


def _moe_plan(eids):
    a = eids.shape[0]
    bits = max(a - 1, 1).bit_length()
    order = jnp.sort(eids * (1 << bits) + jnp.arange(a, dtype=I32), stable=False) & ((1 << bits) - 1)
    experts = jnp.arange(N_EXPERTS, dtype=I32)
    counts = (eids[:, None] == experts[None, :]).astype(I32).sum(axis=0)
    tiles_e = (counts + MOE_TILE - 1) // MOE_TILE
    tile_end = jnp.cumsum(tiles_e)
    first = jnp.cumsum(counts) - counts
    nt = a // MOE_TILE + N_EXPERTS
    tidx = jnp.arange(nt, dtype=I32)
    tile_expert = jnp.minimum((tidx[:, None] >= tile_end[None, :]).astype(I32).sum(axis=1), N_EXPERTS - 1)
    pick = lambda table: ((tile_expert[:, None] == experts[None, :]).astype(I32) * table[None, :]).sum(axis=1)
    tile_pos = (tidx - pick(tile_end - tiles_e)) * MOE_TILE
    used = tidx < tile_end[-1]
    tile_real = jnp.where(used, jnp.clip(pick(counts) - tile_pos, 0, MOE_TILE), 0)
    tile_start = jnp.where(used, pick(first) + tile_pos, 0)
    pads = MOE_TILE - tile_real
    tile_padbase = jnp.cumsum(pads) - pads
    rows = jnp.concatenate([order, jnp.zeros((MOE_TILE,), I32)]).reshape(-1, MOE_TILE)
    windows = jnp.concatenate([rows[:-1], rows[1:]], axis=1)
    return windows, tile_expert, tile_start, tile_real, tile_padbase


def _moe(streams, layer, wg, wu, wd, final_g, final):
    eids = jnp.concatenate([s[2][..., :2].reshape(-1) for s in streams]).astype(I32)
    h = streams[0][1] if len(streams) == 1 else jnp.concatenate([s[1] for s in streams], axis=0)
    ys = _experts(_moe_plan(eids), h, eids.shape[0], layer, wg, wu, wd)
    outs, off = [], 0
    for x, _, route, g2 in streams:
        outs.append(_combine(route, x, g2, final_g, ys, off, final))
        off += x.shape[0] * x.shape[1]
    return outs


def _rope_tables(n, half, offset, identity):
    cs = jnp.ones((n, LANES), F32)
    z = jnp.zeros((n, LANES), F32)
    if identity:
        return cs, z, z
    rows = (jnp.arange(n, dtype=I32) // GRID_W).astype(F32)
    cols = (jnp.arange(n, dtype=I32) % GRID_W).astype(F32)
    freqs = ROPE_THETA ** (-jnp.arange(half, dtype=F32) / half)
    ar, ac = rows[:, None] * freqs, cols[:, None] * freqs
    zero = jnp.zeros((n, half), F32)
    cos4 = jnp.concatenate([jnp.cos(ar), jnp.cos(ar), jnp.cos(ac), jnp.cos(ac)], axis=1)
    s1 = jnp.concatenate([-jnp.sin(ar), zero, -jnp.sin(ac), zero], axis=1)
    s2 = jnp.concatenate([zero, jnp.sin(ar), zero, jnp.sin(ac)], axis=1)
    place = lambda base, t: lax.dynamic_update_slice(base, t, (0, offset))
    return place(cs, cos4), place(z, s1), place(z, s2)


def _pad_heads(w, nheads, width, left=0):
    k = w.shape[0]
    w = w.reshape(k, nheads, width)
    w = jnp.pad(w, ((0, 0), (0, 0), (left, LANES - width - left)))
    return w.reshape(k, nheads * LANES)


def _lane_pad(v, left=0):
    return jnp.pad(v, (left, LANES - v.shape[0] - left)).reshape(1, LANES)


def kernel(x, c, ctx, c_ctx, w_mod, b_mod, norm_g, e_w_in, e_hy_conv_w, e_hy_conv_b, e_hy_w1, e_hy_b1, e_hy_w2, e_hy_b2, e_hy_w3, e_hy_freq, e_hy_fbias, e_qk_g, e_w_out, o_w_in, o_q_norm_g, o_w_uq, o_kv_norm_g, o_w_ukv, o_s5_a_re, o_s5_a_im, o_s5_log_dt, o_s5_b_re, o_s5_b_im, o_s5_c_re, o_s5_c_im, o_s5_d, o_glu_w, o_glu_b, o_w_out, moe_w_rg, moe_w_re, moe_w_gate, moe_w_up, moe_w_down, final_g):
    bsz, n, d = x.shape
    depth = w_mod.shape[0]
    xc = ctx
    ncx = ctx.shape[1]
    cc = jnp.zeros((bsz + 8, d), F32).at[:bsz].set(c).at[bsz].set(c_ctx)

    for i in range(depth):
        j = i // 2
        last = i == depth - 1
        mods = _modulation(cc, w_mod, b_mod, i)
        lat = [m.reshape(bsz, 1, d) for m in jnp.split(mods[:bsz], 6, axis=-1)]
        cxm = [jnp.broadcast_to(m.reshape(1, 1, d), (bsz, 1, d)) for m in jnp.split(mods[bsz], 6, axis=-1)]

        if i % 2 == 0:
            wq = _pad_heads(e_w_in[j][:, 3 * A_WIDTH:3 * A_WIDTH + B_Q_HEADS * HEAD_DIM], B_Q_HEADS, HEAD_DIM)
            k0 = 3 * A_WIDTH + B_Q_HEADS * HEAD_DIM
            wk = _pad_heads(e_w_in[j][:, k0:k0 + B_KV_HEADS * HEAD_DIM], B_KV_HEADS, HEAD_DIM)
            wv = _pad_heads(e_w_in[j][:, k0 + B_KV_HEADS * HEAD_DIM:], B_KV_HEADS, HEAD_DIM)
            w_in = jnp.concatenate([e_w_in[j][:, :3 * A_WIDTH], wq, wk, wv], axis=1).astype(BF16)
            qg, kg = _lane_pad(e_qk_g[j, 0]), _lane_pad(e_qk_g[j, 1])
            filt = (e_hy_w1[j], e_hy_b1[j], e_hy_w2[j], e_hy_b2[j], e_hy_w3[j], e_hy_freq[j])
            qscale = HEAD_DIM ** -0.5 * LOG2E
            hy_l, q_l, k_l, v_l = _even_in(x, norm_g[i, 0], lat[0], lat[1], w_in, qg, kg,
                                           _rope_tables(n, HEAD_DIM // 4, 0, False), qscale)
            hy_c, q_c, k_c, v_c = _even_in(xc, norm_g[i, 0], cxm[0], cxm[1], w_in, qg, kg,
                                           _rope_tables(ncx, HEAD_DIM // 4, 0, True), qscale)
            rep = B_Q_HEADS // B_KV_HEADS
            att_l = _attention(q_l, [k_c, k_l], [v_c, v_l], rep, 1)
            mix_l = (_hyena(hy_l, e_hy_conv_w[j], e_hy_conv_b[j], filt, e_hy_fbias[j]), att_l)
            if not last:
                att_c = _attention(q_c, [k_c], [v_c], rep, 1)
                mix_c = (_hyena(hy_c, e_hy_conv_w[j], e_hy_conv_b[j], filt, e_hy_fbias[j]), att_c)
            w_out = e_w_out[j]
            split = A_WIDTH
        else:
            w1 = o_w_in[j]
            c0, c1, c2 = MLA_Q_RANK, MLA_Q_RANK + MLA_KV_RANK, MLA_Q_RANK + MLA_KV_RANK + MLA_ROPE
            w_kr = jnp.pad(w1[:, c1:c2], ((0, 0), (MLA_NOPE, LANES - MLA_NOPE - MLA_ROPE)))
            w_in = jnp.concatenate([w1[:, :c1], w_kr, w1[:, c2:]], axis=1).astype(BF16)
            wuq = _pad_heads(o_w_uq[j], MLA_HEADS, MLA_NOPE + MLA_ROPE).astype(BF16)
            wkv = o_w_ukv[j].reshape(MLA_KV_RANK, MLA_HEADS, MLA_NOPE + MLA_V)
            wuk = _pad_heads(wkv[:, :, :MLA_NOPE].reshape(MLA_KV_RANK, -1), MLA_HEADS, MLA_NOPE).astype(BF16)
            wuv = _pad_heads(wkv[:, :, MLA_NOPE:].reshape(MLA_KV_RANK, -1), MLA_HEADS, MLA_V).astype(BF16)
            qg = o_q_norm_g[j].reshape(1, MLA_Q_RANK)
            kvg = o_kv_norm_g[j].reshape(1, MLA_KV_RANK)
            args = (w_in, qg, kvg, wuq, wuk, wuv)
            qscale = (MLA_NOPE + MLA_ROPE) ** -0.5 * LOG2E
            u_l, k_l, v_l, q_l = _odd_in(x, norm_g[i, 0], lat[0], lat[1], *args,
                                         _rope_tables(n, MLA_ROPE // 4, MLA_NOPE, False), True, qscale)
            outs_c = _odd_in(xc, norm_g[i, 0], cxm[0], cxm[1], *args,
                             _rope_tables(ncx, MLA_ROPE // 4, MLA_NOPE, True), not last, qscale)
            u_c, k_c, v_c = outs_c[:3]
            att_l = _attention(q_l, [k_c, k_l], [v_c, v_l], 4, 4)
            mats = _s5_matrices(o_s5_a_re[j], o_s5_a_im[j], o_s5_log_dt[j], o_s5_b_re[j], o_s5_b_im[j],
                                o_s5_c_re[j], o_s5_c_im[j], o_s5_d[j])
            glu_w = o_glu_w[j].astype(BF16)
            mix_l = (att_l, _s5_mixer(u_c, u_l, mats, glu_w, o_glu_b[j]))
            if not last:
                raise NotImplementedError("an odd layer that is not the last needs the context S5 readout")
            w_out = o_w_out[j]
            split = MLA_HEADS * MLA_V

        wa, wb = w_out[:split].astype(BF16), w_out[split:].astype(BF16)
        router = jnp.pad(jnp.concatenate([moe_w_rg[i], moe_w_re[i]], axis=1),
                         ((0, 0), (0, LANES - N_GROUPS - N_EXPERTS)))
        rh = router.astype(BF16)
        rl = (router - rh.astype(F32)).astype(BF16)

        x, h_l, route_l = _proj_res(*mix_l, wa, wb, x, lat[2], norm_g[i, 1], lat[3], lat[4], rh, rl)
        streams = [(x, h_l, route_l, lat[5])]
        if not last:
            xc, h_c, route_c = _proj_res(*mix_c, wa, wb, xc, cxm[2], norm_g[i, 1], cxm[3], cxm[4], rh, rl)
            streams.append((xc, h_c, route_c, cxm[5]))
        outs = _moe(streams, i, moe_w_gate, moe_w_up, moe_w_down, final_g, last)
        x = outs[0]
        if not last:
            xc = outs[1]
    return x
```
